```python
import math
import jax, jax.numpy as jnp
from jax import lax
import numpy as np

D_MODEL = 1024
BATCH = 2
SEQ = 8192
DEPTH = 1

PLE_DIM = 256
N_HEADS = 4
NOPE_DIM = 128
ROPE_DIM = 64
V_DIM = 128
QK_DIM = NOPE_DIM + ROPE_DIM
Q_LORA = 256
KV_LORA = 128
ATTN_WIDTH = N_HEADS * V_DIM
CONV_WIDTH = D_MODEL - ATTN_WIDTH
CONV_K = 3
ROPE_THETA = 10000.0
RMS_EPS = 1e-6
Q_BLOCK = 128
NEG_INF = -1e30
IN_WIDTHS = (Q_LORA, KV_LORA, ROPE_DIM, ATTN_WIDTH,
             CONV_WIDTH, CONV_WIDTH, CONV_WIDTH, CONV_WIDTH)
IN_TOTAL = Q_LORA + KV_LORA + ROPE_DIM + ATTN_WIDTH + 4 * CONV_WIDTH

kernel_name = "hymba_mla_shortconv_ple_block"


def rms_norm(x, g):
    xf = x.astype(jnp.float32)
    y = xf * lax.rsqrt(jnp.mean(xf * xf, axis=-1, keepdims=True) + RMS_EPS)
    return (y * g.astype(jnp.float32)).astype(x.dtype)


def rope_cos_sin(positions):
    inv_freq = 1.0 / (ROPE_THETA ** (jnp.arange(0, ROPE_DIM, 2, dtype=jnp.float32) / ROPE_DIM))
    ang = positions.astype(jnp.float32)[..., None] * inv_freq
    return jnp.cos(ang)[:, :, None, :], jnp.sin(ang)[:, :, None, :]


def apply_rope(t, cos, sin):
    tf = t.astype(jnp.float32)
    t1, t2 = tf[..., : ROPE_DIM // 2], tf[..., ROPE_DIM // 2:]
    out = jnp.concatenate([t1 * cos - t2 * sin, t2 * cos + t1 * sin], axis=-1)
    return out.astype(t.dtype)


def causal_block_attention(q, k, v):
    B, S, H, D = q.shape
    nb = S // Q_BLOCK
    scale = 1.0 / math.sqrt(D)
    q_blocks = q.reshape(B, nb, Q_BLOCK, H, D).transpose(1, 0, 2, 3, 4)
    k_pos = jnp.arange(S)

    def one_block(args):
        qb, bi = args
        s = jnp.einsum('bqhd,bkhd->bhqk', qb, k, preferred_element_type=jnp.float32) * scale
        q_pos = bi * Q_BLOCK + jnp.arange(Q_BLOCK)
        mask = k_pos[None, :] <= q_pos[:, None]
        s = jnp.where(mask[None, None], s, NEG_INF)
        pr = jax.nn.softmax(s, axis=-1).astype(v.dtype)
        return jnp.einsum('bhqk,bkhd->bqhd', pr, v)

    out = lax.map(one_block, (q_blocks, jnp.arange(nb)))
    return out.transpose(1, 0, 2, 3, 4).reshape(B, S, H, v.shape[-1])


def causal_depthwise_conv(u, w):
    C = u.shape[-1]
    return lax.conv_general_dilated(
        u, w[:, None, :].astype(u.dtype), window_strides=(1,),
        padding=[(CONV_K - 1, 0)], dimension_numbers=('NWC', 'WIO', 'NWC'),
        feature_group_count=C)


def hybrid_layer(x, p_i, cos, sin, g_in, w_in, g_cq, w_uq, g_ckv, w_ukv, g_q, g_k,
                 conv_w, g_oa, g_oc, w_o, w_pl, w_plg, g_pl):
    B, S, _ = x.shape
    h = rms_norm(x, g_in)
    proj = h @ w_in
    splits, acc = [], 0
    for wdt in IN_WIDTHS[:-1]:
        acc += wdt
        splits.append(acc)
    c_q, c_kv, k_pe, z_a, cb, cc, cx, z_c = jnp.split(proj, splits, axis=-1)

    q = (rms_norm(c_q, g_cq) @ w_uq).reshape(B, S, N_HEADS, QK_DIM)
    kv = (rms_norm(c_kv, g_ckv) @ w_ukv).reshape(B, S, N_HEADS, NOPE_DIM + V_DIM)
    k_nope, v = kv[..., :NOPE_DIM], kv[..., NOPE_DIM:]
    k = jnp.concatenate(
        [k_nope, jnp.broadcast_to(k_pe[:, :, None, :], (B, S, N_HEADS, ROPE_DIM))], axis=-1)
    q = rms_norm(q, g_q)
    k = rms_norm(k, g_k)
    q = jnp.concatenate([q[..., :NOPE_DIM], apply_rope(q[..., NOPE_DIM:], cos, sin)], axis=-1)
    k = jnp.concatenate([k[..., :NOPE_DIM], apply_rope(k[..., NOPE_DIM:], cos, sin)], axis=-1)
    o_attn = causal_block_attention(q, k, v).reshape(B, S, ATTN_WIDTH)
    y_attn = rms_norm(o_attn * jax.nn.silu(z_a), g_oa)

    u = causal_depthwise_conv(cc * cx, conv_w)
    y_conv = rms_norm(cb * u * jax.nn.silu(z_c), g_oc)

    x = x + jnp.concatenate([y_attn, y_conv], axis=-1) @ w_o

    gate = jax.nn.sigmoid(rms_norm(x, g_pl) @ w_plg)
    return x + gate * (p_i @ w_pl)


def setup_inputs(seed: int = 0) -> dict:
    key = jax.random.key(seed)
    ks = jax.random.split(key, 20)
    f32 = jnp.float32

    def nrm(k, shape, fan_in):
        return jax.random.normal(k, shape, f32) * (fan_in ** -0.5)

    def gain(k, n):
        return 1.0 + 0.02 * jax.random.normal(k, (DEPTH, n), f32)

    x = jax.random.normal(ks[0], (BATCH, SEQ, D_MODEL), f32)
    p = jax.random.normal(ks[1], (DEPTH, BATCH, SEQ, PLE_DIM), f32)
    positions = jnp.broadcast_to(jnp.arange(SEQ, dtype=jnp.int32)[None, :], (BATCH, SEQ))
    return {
        "x": x,
        "p": p,
        "positions": positions,
        "g_in": gain(ks[2], D_MODEL),
        "w_in": nrm(ks[3], (DEPTH, D_MODEL, IN_TOTAL), D_MODEL),
        "g_cq": gain(ks[4], Q_LORA),
        "w_uq": nrm(ks[5], (DEPTH, Q_LORA, N_HEADS * QK_DIM), Q_LORA),
        "g_ckv": gain(ks[6], KV_LORA),
        "w_ukv": nrm(ks[7], (DEPTH, KV_LORA, N_HEADS * (NOPE_DIM + V_DIM)), KV_LORA),
        "g_q": gain(ks[8], QK_DIM),
        "g_k": gain(ks[9], QK_DIM),
        "conv_w": nrm(ks[10], (DEPTH, CONV_K, CONV_WIDTH), CONV_K),
        "g_oa": gain(ks[11], ATTN_WIDTH),
        "g_oc": gain(ks[12], CONV_WIDTH),
        "w_o": nrm(ks[13], (DEPTH, D_MODEL, D_MODEL), D_MODEL),
        "w_pl": nrm(ks[14], (DEPTH, PLE_DIM, D_MODEL), PLE_DIM),
        "w_plg": nrm(ks[15], (DEPTH, D_MODEL, D_MODEL), D_MODEL),
        "g_pl": gain(ks[16], D_MODEL),
    }


def reference(x, p, positions, g_in, w_in, g_cq, w_uq, g_ckv, w_ukv, g_q, g_k,
              conv_w, g_oa, g_oc, w_o, w_pl, w_plg, g_pl):
    cos, sin = rope_cos_sin(positions)
    h = x
    for i in range(DEPTH):
        h = hybrid_layer(h, p[i], cos, sin, g_in[i], w_in[i], g_cq[i], w_uq[i],
                         g_ckv[i], w_ukv[i], g_q[i], g_k[i], conv_w[i], g_oa[i],
                         g_oc[i], w_o[i], w_pl[i], w_plg[i], g_pl[i])
    return h.astype(x.dtype)
```

```python
import functools
import math

import jax
import jax.numpy as jnp
import numpy as np
from jax import lax
from jax.experimental import pallas as pl
from jax.experimental.pallas import tpu as pltpu

D_MODEL = 1024
PLE_DIM = 256
N_HEADS = 4
NOPE_DIM = 128
ROPE_DIM = 64
HALF_ROPE = ROPE_DIM // 2
V_DIM = 128
QK_DIM = NOPE_DIM + ROPE_DIM
Q_LORA = 256
KV_LORA = 128
ATTN_WIDTH = N_HEADS * V_DIM
CONV_WIDTH = D_MODEL - ATTN_WIDTH
CONV_K = 3
ROPE_THETA = 10000.0
RMS_EPS = 1e-6
NEG_INF = -1e30

LANES = 128
SUBLANES = 8
HEAD_PAD = 2 * LANES
KPE_PAD = LANES
OFF_CQ = 0
OFF_CKV = OFF_CQ + Q_LORA
OFF_KPE = OFF_CKV + KV_LORA
OFF_ZA = OFF_KPE + KPE_PAD
OFF_CB = OFF_ZA + ATTN_WIDTH
OFF_CC = OFF_CB + CONV_WIDTH
OFF_CX = OFF_CC + CONV_WIDTH
OFF_ZC = OFF_CX + CONV_WIDTH
IN_PAD = OFF_ZC + CONV_WIDTH

PROJ_TILE = 512
ATTN_TILE = 512
VMEM_LIMIT = 48 * 1024 * 1024

BF16 = jnp.bfloat16
F32 = jnp.float32


def _rms_scale(v, width):
    return lax.rsqrt(jnp.sum(v * v, axis=-1, keepdims=True) * (1.0 / width) + RMS_EPS)


def _silu(z):
    return z / (1.0 + jnp.exp(-z))


def _dot(a, b):
    return jnp.dot(a, b, preferred_element_type=F32)


def _rope_tables(pos_ref, invf_ref):
    ang = pos_ref[...] * invf_ref[...]
    lane = lax.broadcasted_iota(jnp.int32, (1, LANES), 1)
    cosv = jnp.cos(ang)
    sinv = jnp.sin(ang)
    c = jnp.where(lane < ROPE_DIM, cosv, 0.0)
    s_lo = jnp.where(lane < HALF_ROPE, -sinv, 0.0)
    s_hi = jnp.where((lane >= HALF_ROPE) & (lane < ROPE_DIM), sinv, 0.0)
    return c, s_lo, s_hi


def _rope(t, tables):
    c, s_lo, s_hi = tables
    return (t * c + pltpu.roll(t, LANES - HALF_ROPE, 1) * s_lo
            + pltpu.roll(t, HALF_ROPE, 1) * s_hi)


def _proj_kernel(tiles_per_seq,
                 x_ref, pos_ref, invf_ref, g_in_ref, w_in_ref, g_cq_ref, w_uq_ref,
                 g_ckv_ref, w_ukv_ref, g_q_ref, g_kn_ref, g_kr_ref, conv_w_ref, g_oc_ref,
                 q_ref, k_ref, v_ref, ga_ref, yc_ref, carry_ref):
    tm = x_ref.shape[0]
    x = x_ref[...]
    h = (x * _rms_scale(x, D_MODEL) * g_in_ref[...]).astype(BF16)

    def proj(off, width):
        return _dot(h, w_in_ref[:, off:off + width])

    tables = _rope_tables(pos_ref, invf_ref)
    q_scale = 1.0 / math.sqrt(QK_DIM)

    c_q = proj(OFF_CQ, Q_LORA)
    cqn = (c_q * _rms_scale(c_q, Q_LORA) * g_cq_ref[...]).astype(BF16)
    qf = _dot(cqn, w_uq_ref[...])
    for hd in range(N_HEADS):
        qh = qf[:, hd * HEAD_PAD:(hd + 1) * HEAD_PAD]
        qn = qh * (_rms_scale(qh, QK_DIM) * q_scale) * g_q_ref[...]
        q_ref[:, hd * HEAD_PAD:hd * HEAD_PAD + LANES] = qn[:, :LANES].astype(BF16)
        q_ref[:, hd * HEAD_PAD + LANES:(hd + 1) * HEAD_PAD] = _rope(qn[:, LANES:], tables).astype(BF16)

    c_kv = proj(OFF_CKV, KV_LORA)
    ckvn = (c_kv * _rms_scale(c_kv, KV_LORA) * g_ckv_ref[...]).astype(BF16)
    kv = _dot(ckvn, w_ukv_ref[...])
    kpe = proj(OFF_KPE, KPE_PAD)
    ss_kpe = jnp.sum(kpe * kpe, axis=-1, keepdims=True)
    k_rot = _rope(kpe * g_kr_ref[...], tables)
    for hd in range(N_HEADS):
        kn = kv[:, hd * (NOPE_DIM + V_DIM):hd * (NOPE_DIM + V_DIM) + NOPE_DIM]
        vh = kv[:, hd * (NOPE_DIM + V_DIM) + NOPE_DIM:(hd + 1) * (NOPE_DIM + V_DIM)]
        ss = jnp.sum(kn * kn, axis=-1, keepdims=True) + ss_kpe
        rs = lax.rsqrt(ss * (1.0 / QK_DIM) + RMS_EPS)
        k_ref[:, hd * HEAD_PAD:hd * HEAD_PAD + LANES] = (kn * rs * g_kn_ref[...]).astype(BF16)
        k_ref[:, hd * HEAD_PAD + LANES:(hd + 1) * HEAD_PAD] = (k_rot * rs).astype(BF16)
        v_ref[:, hd * V_DIM:(hd + 1) * V_DIM] = vh.astype(BF16)

    ga_ref[...] = _silu(proj(OFF_ZA, ATTN_WIDTH)).astype(BF16)

    @pl.when(pl.program_id(0) % tiles_per_seq == 0)
    def _():
        carry_ref[0:SUBLANES, :] = jnp.zeros((SUBLANES, CONV_WIDTH), F32)

    cv = proj(OFF_CC, CONV_WIDTH) * proj(OFF_CX, CONV_WIDTH)
    carry_ref[SUBLANES:SUBLANES + tm, :] = cv
    u = (conv_w_ref[2:3, :] * cv
         + conv_w_ref[1:2, :] * carry_ref[SUBLANES - 1:SUBLANES - 1 + tm, :]
         + conv_w_ref[0:1, :] * carry_ref[SUBLANES - 2:SUBLANES - 2 + tm, :])
    carry_ref[0:SUBLANES, :] = cv[tm - SUBLANES:, :]
    yc = proj(OFF_CB, CONV_WIDTH) * u * _silu(proj(OFF_ZC, CONV_WIDTH))
    yc_ref[...] = (yc * _rms_scale(yc, CONV_WIDTH) * g_oc_ref[...]).astype(BF16)


def _attn_kernel(qi_ref, ki_ref, first_ref, last_ref,
                 q_ref, k_ref, v_ref, o_ref, m_ref, l_ref, acc_ref):
    step = pl.program_id(0)
    tq = q_ref.shape[0]
    tk = k_ref.shape[0]

    @pl.when(first_ref[step] == 1)
    def _():
        m_ref[...] = jnp.full(m_ref.shape, NEG_INF, F32)
        l_ref[...] = jnp.zeros(l_ref.shape, F32)
        acc_ref[...] = jnp.zeros(acc_ref.shape, F32)

    def update(masked):
        if masked:
            row = lax.broadcasted_iota(jnp.int32, (tq, tk), 0)
            col = lax.broadcasted_iota(jnp.int32, (tq, tk), 1)
            keep = col <= row
        for hd in range(N_HEADS):
            q = q_ref[:, hd * HEAD_PAD:(hd + 1) * HEAD_PAD]
            k = k_ref[:, hd * HEAD_PAD:(hd + 1) * HEAD_PAD]
            s = lax.dot_general(q, k, (((1,), (1,)), ((), ())), preferred_element_type=F32)
            if masked:
                s = jnp.where(keep, s, NEG_INF)
            m_prev = m_ref[hd]
            m_new = jnp.maximum(m_prev, jnp.max(s, axis=-1, keepdims=True))
            alpha = jnp.exp(m_prev - m_new)
            p = jnp.exp(s - m_new[:, :1])
            l_ref[hd] = alpha * l_ref[hd] + jnp.sum(p, axis=-1, keepdims=True)
            acc_ref[hd] = alpha * acc_ref[hd] + _dot(p.astype(BF16), v_ref[:, hd * V_DIM:(hd + 1) * V_DIM])
            m_ref[hd] = m_new

    is_last = last_ref[step] == 1

    @pl.when(jnp.logical_not(is_last))
    def _():
        update(False)

    @pl.when(is_last)
    def _():
        update(True)
        for hd in range(N_HEADS):
            o_ref[:, hd * V_DIM:(hd + 1) * V_DIM] = (acc_ref[hd] / l_ref[hd]).astype(o_ref.dtype)


def _out_kernel(o_ref, ga_ref, yc_ref, x_ref, p_ref, g_oa_ref, w_oa_ref, w_oc_ref,
                g_pl_ref, w_plg_ref, w_pl_ref, out_ref):
    ya = o_ref[...].astype(F32) * ga_ref[...].astype(F32)
    ya_n = (ya * _rms_scale(ya, ATTN_WIDTH) * g_oa_ref[...]).astype(BF16)
    x1 = x_ref[...] + _dot(ya_n, w_oa_ref[...]) + _dot(yc_ref[...], w_oc_ref[...])
    hn = (x1 * _rms_scale(x1, D_MODEL) * g_pl_ref[...]).astype(BF16)
    gate = 1.0 / (1.0 + jnp.exp(-_dot(hn, w_plg_ref[...])))
    out_ref[...] = x1 + gate * _dot(p_ref[...].astype(BF16), w_pl_ref[...])


def _row_spec(tile, width):
    return pl.BlockSpec((tile, width), lambda i: (i, 0))


def _const_spec(shape):
    return pl.BlockSpec(shape, lambda i: (0,) * len(shape))


def _triangle_schedule(batch, n_blk):
    qi, ki, first, last = [], [], [], []
    for b in range(batch):
        for i in range(n_blk):
            for j in range(i + 1):
                qi.append(b * n_blk + i)
                ki.append(b * n_blk + j)
                first.append(int(j == 0))
                last.append(int(j == i))
    return tuple(jnp.asarray(np.asarray(a, np.int32)) for a in (qi, ki, first, last))


def _layer(x2, p2, posb, invf, batch, seq, g_in, w_in, g_cq, w_uq, g_ckv, w_ukv, g_q, g_k,
           conv_w, g_oa, g_oc, w_o, w_pl, w_plg, g_pl):
    tokens = batch * seq
    tm = PROJ_TILE
    assert seq % tm == 0 and seq % ATTN_TILE == 0

    w_in_p = jnp.concatenate(
        [w_in[:, :OFF_KPE + ROPE_DIM], jnp.zeros((D_MODEL, KPE_PAD - ROPE_DIM), F32),
         w_in[:, OFF_KPE + ROPE_DIM:]], axis=1).astype(BF16)
    w_uq_p = jnp.pad(w_uq.reshape(Q_LORA, N_HEADS, QK_DIM),
                     ((0, 0), (0, 0), (0, HEAD_PAD - QK_DIM))).reshape(Q_LORA, N_HEADS * HEAD_PAD).astype(BF16)
    g_q_p = jnp.pad(g_q, (0, HEAD_PAD - QK_DIM)).reshape(1, HEAD_PAD)
    g_kn = g_k[:NOPE_DIM].reshape(1, NOPE_DIM)
    g_kr = jnp.pad(g_k[NOPE_DIM:], (0, KPE_PAD - ROPE_DIM)).reshape(1, KPE_PAD)

    cparams = pltpu.CompilerParams(dimension_semantics=("arbitrary",), vmem_limit_bytes=VMEM_LIMIT)

    q, k, v, ga, yc = pl.pallas_call(
        functools.partial(_proj_kernel, seq // tm),
        grid=(tokens // tm,),
        in_specs=[
            _row_spec(tm, D_MODEL), _row_spec(tm, LANES), _const_spec((1, LANES)),
            _const_spec((1, D_MODEL)), _const_spec((D_MODEL, IN_PAD)),
            _const_spec((1, Q_LORA)), _const_spec((Q_LORA, N_HEADS * HEAD_PAD)),
            _const_spec((1, KV_LORA)), _const_spec((KV_LORA, N_HEADS * (NOPE_DIM + V_DIM))),
            _const_spec((1, HEAD_PAD)), _const_spec((1, NOPE_DIM)), _const_spec((1, KPE_PAD)),
            _const_spec((CONV_K, CONV_WIDTH)), _const_spec((1, CONV_WIDTH)),
        ],
        out_specs=[
            _row_spec(tm, N_HEADS * HEAD_PAD), _row_spec(tm, N_HEADS * HEAD_PAD),
            _row_spec(tm, ATTN_WIDTH), _row_spec(tm, ATTN_WIDTH), _row_spec(tm, CONV_WIDTH),
        ],
        out_shape=[
            jax.ShapeDtypeStruct((tokens, N_HEADS * HEAD_PAD), BF16),
            jax.ShapeDtypeStruct((tokens, N_HEADS * HEAD_PAD), BF16),
            jax.ShapeDtypeStruct((tokens, ATTN_WIDTH), BF16),
            jax.ShapeDtypeStruct((tokens, ATTN_WIDTH), BF16),
            jax.ShapeDtypeStruct((tokens, CONV_WIDTH), BF16),
        ],
        scratch_shapes=[pltpu.VMEM((tm + SUBLANES, CONV_WIDTH), F32)],
        compiler_params=cparams,
        name="mla_conv_proj",
    )(x2, posb, invf, g_in.reshape(1, -1), w_in_p, g_cq.reshape(1, -1), w_uq_p,
      g_ckv.reshape(1, -1), w_ukv.astype(BF16), g_q_p, g_kn, g_kr, conv_w, g_oc.reshape(1, -1))

    ta = ATTN_TILE
    qi, ki, first, last = _triangle_schedule(batch, seq // ta)
    o = pl.pallas_call(
        _attn_kernel,
        grid_spec=pltpu.PrefetchScalarGridSpec(
            num_scalar_prefetch=4,
            grid=(int(qi.shape[0]),),
            in_specs=[
                pl.BlockSpec((ta, N_HEADS * HEAD_PAD), lambda s, qi, ki, f, l: (qi[s], 0)),
                pl.BlockSpec((ta, N_HEADS * HEAD_PAD), lambda s, qi, ki, f, l: (ki[s], 0)),
                pl.BlockSpec((ta, ATTN_WIDTH), lambda s, qi, ki, f, l: (ki[s], 0)),
            ],
            out_specs=pl.BlockSpec((ta, ATTN_WIDTH), lambda s, qi, ki, f, l: (qi[s], 0)),
            scratch_shapes=[pltpu.VMEM((N_HEADS, ta, LANES), F32),
                            pltpu.VMEM((N_HEADS, ta, LANES), F32),
                            pltpu.VMEM((N_HEADS, ta, V_DIM), F32)],
        ),
        out_shape=jax.ShapeDtypeStruct((tokens, ATTN_WIDTH), BF16),
        compiler_params=cparams,
        name="mla_flash_attn",
    )(qi, ki, first, last, q, k, v)

    w_o_b = w_o.astype(BF16)
    return pl.pallas_call(
        _out_kernel,
        grid=(tokens // tm,),
        in_specs=[
            _row_spec(tm, ATTN_WIDTH), _row_spec(tm, ATTN_WIDTH), _row_spec(tm, CONV_WIDTH),
            _row_spec(tm, D_MODEL), _row_spec(tm, PLE_DIM),
            _const_spec((1, ATTN_WIDTH)), _const_spec((ATTN_WIDTH, D_MODEL)),
            _const_spec((CONV_WIDTH, D_MODEL)), _const_spec((1, D_MODEL)),
            _const_spec((D_MODEL, D_MODEL)), _const_spec((PLE_DIM, D_MODEL)),
        ],
        out_specs=_row_spec(tm, D_MODEL),
        out_shape=jax.ShapeDtypeStruct((tokens, D_MODEL), F32),
        compiler_params=cparams,
        name="mla_conv_out",
    )(o, ga, yc, x2, p2, g_oa.reshape(1, -1), w_o_b[:ATTN_WIDTH], w_o_b[ATTN_WIDTH:],
      g_pl.reshape(1, -1), w_plg.astype(BF16), w_pl.astype(BF16))


def kernel(x, p, positions, g_in, w_in, g_cq, w_uq, g_ckv, w_ukv, g_q, g_k, conv_w, g_oa, g_oc,
           w_o, w_pl, w_plg, g_pl):
    batch, seq, d_model = x.shape
    depth = p.shape[0]
    tokens = batch * seq
    posb = jnp.broadcast_to(positions.astype(F32).reshape(tokens, 1), (tokens, LANES))
    inv_freq = 1.0 / (ROPE_THETA ** (jnp.arange(0, ROPE_DIM, 2, dtype=F32) / ROPE_DIM))
    invf = jnp.tile(inv_freq, LANES // HALF_ROPE).reshape(1, LANES)
    h = x.reshape(tokens, d_model)
    for i in range(depth):
        h = _layer(h, p[i].reshape(tokens, PLE_DIM), posb, invf, batch, seq,
                   g_in[i], w_in[i], g_cq[i], w_uq[i], g_ckv[i], w_ukv[i], g_q[i], g_k[i],
                   conv_w[i], g_oa[i], g_oc[i], w_o[i], w_pl[i], w_plg[i], g_pl[i])
    return h.reshape(batch, seq, d_model).astype(x.dtype)
```

```python
import functools
import math

import jax
import jax.numpy as jnp
import numpy as np
from jax import lax
from jax.experimental import pallas as pl
from jax.experimental.pallas import tpu as pltpu

D_MODEL = 1024
PLE_DIM = 256
N_HEADS = 4
NOPE_DIM = 128
ROPE_DIM = 64
HALF_ROPE = ROPE_DIM // 2
V_DIM = 128
QK_DIM = NOPE_DIM + ROPE_DIM
Q_LORA = 256
KV_LORA = 128
ATTN_WIDTH = N_HEADS * V_DIM
CONV_WIDTH = D_MODEL - ATTN_WIDTH
CONV_K = 3
ROPE_THETA = 10000.0
RMS_EPS = 1e-6
NEG_INF = -1e30

LANES = 128
SUBLANES = 8
HEAD_PAD = 2 * LANES
KPE_PAD = LANES
OFF_CQ = 0
OFF_CKV = OFF_CQ + Q_LORA
OFF_KPE = OFF_CKV + KV_LORA
OFF_ZA = OFF_KPE + KPE_PAD
OFF_CB = OFF_ZA + ATTN_WIDTH
OFF_CC = OFF_CB + CONV_WIDTH
OFF_CX = OFF_CC + CONV_WIDTH
OFF_ZC = OFF_CX + CONV_WIDTH
IN_PAD = OFF_ZC + CONV_WIDTH

PROJ_TILE = 512
ATTN_TILE = 512
VMEM_LIMIT = 48 * 1024 * 1024

BF16 = jnp.bfloat16
F32 = jnp.float32


def _rms_scale(v, width):
    return lax.rsqrt(jnp.sum(v * v, axis=-1, keepdims=True) * (1.0 / width) + RMS_EPS)


def _silu(z):
    return z / (1.0 + jnp.exp(-z))


def _dot(a, b):
    return jnp.dot(a, b, preferred_element_type=F32)


def _rope_tables(pos_ref, invf_ref):
    ang = pos_ref[...] * invf_ref[...]
    lane = lax.broadcasted_iota(jnp.int32, (1, LANES), 1)
    cosv = jnp.cos(ang)
    sinv = jnp.sin(ang)
    c = jnp.where(lane < ROPE_DIM, cosv, 0.0)
    s_lo = jnp.where(lane < HALF_ROPE, -sinv, 0.0)
    s_hi = jnp.where((lane >= HALF_ROPE) & (lane < ROPE_DIM), sinv, 0.0)
    return c, s_lo, s_hi


def _rope(t, tables):
    c, s_lo, s_hi = tables
    return (t * c + pltpu.roll(t, LANES - HALF_ROPE, 1) * s_lo
            + pltpu.roll(t, HALF_ROPE, 1) * s_hi)


def _proj_kernel(tiles_per_seq,
                 x_ref, pos_ref, invf_ref, g_in_ref, w_in_ref, g_cq_ref, w_uq_ref,
                 g_ckv_ref, w_ukv_ref, g_q_ref, g_kn_ref, g_kr_ref, conv_w_ref, g_oc_ref,
                 qt_ref, k_ref, vt_ref, ga_ref, yc_ref, carry_ref):
    tm = x_ref.shape[0]
    x = x_ref[...]
    h = (x * _rms_scale(x, D_MODEL) * g_in_ref[...]).astype(BF16)

    def proj(off, width):
        return _dot(h, w_in_ref[:, off:off + width])

    tables = _rope_tables(pos_ref, invf_ref)
    q_scale = math.log2(math.e) / math.sqrt(QK_DIM)

    c_q = proj(OFF_CQ, Q_LORA)
    cqn = (c_q * _rms_scale(c_q, Q_LORA) * g_cq_ref[...]).astype(BF16)
    qf = _dot(cqn, w_uq_ref[...])
    for hd in range(N_HEADS):
        qh = qf[:, hd * HEAD_PAD:(hd + 1) * HEAD_PAD]
        qn = qh * (_rms_scale(qh, QK_DIM) * q_scale) * g_q_ref[...]
        qt_ref[hd * HEAD_PAD:hd * HEAD_PAD + LANES, :] = qn[:, :LANES].T.astype(BF16)
        qt_ref[hd * HEAD_PAD + LANES:(hd + 1) * HEAD_PAD, :] = _rope(qn[:, LANES:], tables).T.astype(BF16)

    c_kv = proj(OFF_CKV, KV_LORA)
    ckvn = (c_kv * _rms_scale(c_kv, KV_LORA) * g_ckv_ref[...]).astype(BF16)
    kv = _dot(ckvn, w_ukv_ref[...])
    kpe = proj(OFF_KPE, KPE_PAD)
    ss_kpe = jnp.sum(kpe * kpe, axis=-1, keepdims=True)
    k_rot = _rope(kpe * g_kr_ref[...], tables)
    for hd in range(N_HEADS):
        kn = kv[:, hd * (NOPE_DIM + V_DIM):hd * (NOPE_DIM + V_DIM) + NOPE_DIM]
        vh = kv[:, hd * (NOPE_DIM + V_DIM) + NOPE_DIM:(hd + 1) * (NOPE_DIM + V_DIM)]
        ss = jnp.sum(kn * kn, axis=-1, keepdims=True) + ss_kpe
        rs = lax.rsqrt(ss * (1.0 / QK_DIM) + RMS_EPS)
        k_ref[:, hd * HEAD_PAD:hd * HEAD_PAD + LANES] = (kn * rs * g_kn_ref[...]).astype(BF16)
        k_ref[:, hd * HEAD_PAD + LANES:(hd + 1) * HEAD_PAD] = (k_rot * rs).astype(BF16)
        vt_ref[hd * V_DIM:(hd + 1) * V_DIM, :] = vh.T.astype(BF16)

    ga_ref[...] = _silu(proj(OFF_ZA, ATTN_WIDTH)).astype(BF16)

    @pl.when(pl.program_id(0) % tiles_per_seq == 0)
    def _():
        carry_ref[0:SUBLANES, :] = jnp.zeros((SUBLANES, CONV_WIDTH), F32)

    cv = proj(OFF_CC, CONV_WIDTH) * proj(OFF_CX, CONV_WIDTH)
    carry_ref[SUBLANES:SUBLANES + tm, :] = cv
    u = (conv_w_ref[2:3, :] * cv
         + conv_w_ref[1:2, :] * carry_ref[SUBLANES - 1:SUBLANES - 1 + tm, :]
         + conv_w_ref[0:1, :] * carry_ref[SUBLANES - 2:SUBLANES - 2 + tm, :])
    carry_ref[0:SUBLANES, :] = cv[tm - SUBLANES:, :]
    yc = proj(OFF_CB, CONV_WIDTH) * u * _silu(proj(OFF_ZC, CONV_WIDTH))
    yc_ref[...] = (yc * _rms_scale(yc, CONV_WIDTH) * g_oc_ref[...]).astype(BF16)


def _attn_kernel(qi_ref, ki_ref, first_ref, last_ref,
                 qt_ref, k_ref, vt_ref, o_ref, m_ref, l_ref, acc_ref):
    step = pl.program_id(0)
    tq = qt_ref.shape[1]
    tk = k_ref.shape[0]

    @pl.when(first_ref[step] == 1)
    def _():
        m_ref[...] = jnp.full(m_ref.shape, NEG_INF, F32)
        l_ref[...] = jnp.zeros(l_ref.shape, F32)
        acc_ref[...] = jnp.zeros(acc_ref.shape, F32)

    def update(masked):
        if masked:
            key = lax.broadcasted_iota(jnp.int32, (tk, tq), 0)
            qry = lax.broadcasted_iota(jnp.int32, (tk, tq), 1)
            keep = key <= qry
        def scores(hd):
            return _dot(k_ref[:, hd * HEAD_PAD:(hd + 1) * HEAD_PAD],
                        qt_ref[hd * HEAD_PAD:(hd + 1) * HEAD_PAD, :])

        pending = {0: scores(0)}
        for hd in range(N_HEADS):
            if hd + 1 < N_HEADS:
                pending[hd + 1] = scores(hd + 1)
            s = pending.pop(hd)
            if masked:
                s = jnp.where(keep, s, NEG_INF)
            m_prev = m_ref[hd]
            m_new = jnp.maximum(m_prev, jnp.max(s, axis=0, keepdims=True))
            alpha = jnp.exp2(m_prev - m_new)
            p = jnp.exp2(s - m_new)
            l_ref[hd] = alpha * l_ref[hd] + jnp.sum(p, axis=0, keepdims=True)
            pv = _dot(vt_ref[hd * V_DIM:(hd + 1) * V_DIM, :], p.astype(BF16))
            acc_ref[hd] = alpha * acc_ref[hd] + pv
            m_ref[hd] = m_new

    is_last = last_ref[step] == 1

    @pl.when(jnp.logical_not(is_last))
    def _():
        update(False)

    @pl.when(is_last)
    def _():
        update(True)
        for hd in range(N_HEADS):
            o_ref[:, hd * V_DIM:(hd + 1) * V_DIM] = (acc_ref[hd] / l_ref[hd]).T.astype(o_ref.dtype)


def _out_kernel(o_ref, ga_ref, yc_ref, x_ref, p_ref, g_oa_ref, w_oa_ref, w_oc_ref,
                g_pl_ref, w_plg_ref, w_pl_ref, out_ref):
    ya = o_ref[...].astype(F32) * ga_ref[...].astype(F32)
    ya_n = (ya * _rms_scale(ya, ATTN_WIDTH) * g_oa_ref[...]).astype(BF16)
    x1 = x_ref[...] + _dot(ya_n, w_oa_ref[...]) + _dot(yc_ref[...], w_oc_ref[...])
    hn = (x1 * _rms_scale(x1, D_MODEL) * g_pl_ref[...]).astype(BF16)
    gate = 1.0 / (1.0 + jnp.exp(-_dot(hn, w_plg_ref[...])))
    out_ref[...] = x1 + gate * _dot(p_ref[...].astype(BF16), w_pl_ref[...])


def _row_spec(tile, width):
    return pl.BlockSpec((tile, width), lambda i: (i, 0))


def _const_spec(shape):
    return pl.BlockSpec(shape, lambda i: (0,) * len(shape))


def _triangle_schedule(batch, n_blk):
    qi, ki, first, last = [], [], [], []
    for b in range(batch):
        for i in range(n_blk):
            for j in range(i + 1):
                qi.append(b * n_blk + i)
                ki.append(b * n_blk + j)
                first.append(int(j == 0))
                last.append(int(j == i))
    return tuple(jnp.asarray(np.asarray(a, np.int32)) for a in (qi, ki, first, last))


def _layer(x2, p2, posb, invf, batch, seq, g_in, w_in, g_cq, w_uq, g_ckv, w_ukv, g_q, g_k,
           conv_w, g_oa, g_oc, w_o, w_pl, w_plg, g_pl):
    tokens = batch * seq
    tm = PROJ_TILE
    assert seq % tm == 0 and seq % ATTN_TILE == 0

    w_in_p = jnp.concatenate(
        [w_in[:, :OFF_KPE + ROPE_DIM], jnp.zeros((D_MODEL, KPE_PAD - ROPE_DIM), F32),
         w_in[:, OFF_KPE + ROPE_DIM:]], axis=1).astype(BF16)
    w_uq_p = jnp.pad(w_uq.reshape(Q_LORA, N_HEADS, QK_DIM),
                     ((0, 0), (0, 0), (0, HEAD_PAD - QK_DIM))).reshape(Q_LORA, N_HEADS * HEAD_PAD).astype(BF16)
    g_q_p = jnp.pad(g_q, (0, HEAD_PAD - QK_DIM)).reshape(1, HEAD_PAD)
    g_kn = g_k[:NOPE_DIM].reshape(1, NOPE_DIM)
    g_kr = jnp.pad(g_k[NOPE_DIM:], (0, KPE_PAD - ROPE_DIM)).reshape(1, KPE_PAD)

    cparams = pltpu.CompilerParams(dimension_semantics=("arbitrary",), vmem_limit_bytes=VMEM_LIMIT)

    def col_spec(rows, tile):
        return pl.BlockSpec((rows, tile), lambda i: (0, i))

    qt, k, vt, ga, yc = pl.pallas_call(
        functools.partial(_proj_kernel, seq // tm),
        grid=(tokens // tm,),
        in_specs=[
            _row_spec(tm, D_MODEL), _row_spec(tm, LANES), _const_spec((1, LANES)),
            _const_spec((1, D_MODEL)), _const_spec((D_MODEL, IN_PAD)),
            _const_spec((1, Q_LORA)), _const_spec((Q_LORA, N_HEADS * HEAD_PAD)),
            _const_spec((1, KV_LORA)), _const_spec((KV_LORA, N_HEADS * (NOPE_DIM + V_DIM))),
            _const_spec((1, HEAD_PAD)), _const_spec((1, NOPE_DIM)), _const_spec((1, KPE_PAD)),
            _const_spec((CONV_K, CONV_WIDTH)), _const_spec((1, CONV_WIDTH)),
        ],
        out_specs=[
            col_spec(N_HEADS * HEAD_PAD, tm), _row_spec(tm, N_HEADS * HEAD_PAD),
            col_spec(ATTN_WIDTH, tm), _row_spec(tm, ATTN_WIDTH), _row_spec(tm, CONV_WIDTH),
        ],
        out_shape=[
            jax.ShapeDtypeStruct((N_HEADS * HEAD_PAD, tokens), BF16),
            jax.ShapeDtypeStruct((tokens, N_HEADS * HEAD_PAD), BF16),
            jax.ShapeDtypeStruct((ATTN_WIDTH, tokens), BF16),
            jax.ShapeDtypeStruct((tokens, ATTN_WIDTH), BF16),
            jax.ShapeDtypeStruct((tokens, CONV_WIDTH), BF16),
        ],
        scratch_shapes=[pltpu.VMEM((tm + SUBLANES, CONV_WIDTH), F32)],
        compiler_params=cparams,
        name="mla_conv_proj",
    )(x2, posb, invf, g_in.reshape(1, -1), w_in_p, g_cq.reshape(1, -1), w_uq_p,
      g_ckv.reshape(1, -1), w_ukv.astype(BF16), g_q_p, g_kn, g_kr, conv_w, g_oc.reshape(1, -1))

    ta = ATTN_TILE
    qi, ki, first, last = _triangle_schedule(batch, seq // ta)
    o = pl.pallas_call(
        _attn_kernel,
        grid_spec=pltpu.PrefetchScalarGridSpec(
            num_scalar_prefetch=4,
            grid=(int(qi.shape[0]),),
            in_specs=[
                pl.BlockSpec((N_HEADS * HEAD_PAD, ta), lambda s, qi, ki, f, l: (0, qi[s])),
                pl.BlockSpec((ta, N_HEADS * HEAD_PAD), lambda s, qi, ki, f, l: (ki[s], 0)),
                pl.BlockSpec((ATTN_WIDTH, ta), lambda s, qi, ki, f, l: (0, ki[s])),
            ],
            out_specs=pl.BlockSpec((ta, ATTN_WIDTH), lambda s, qi, ki, f, l: (qi[s], 0)),
            scratch_shapes=[pltpu.VMEM((N_HEADS, 1, ta), F32),
                            pltpu.VMEM((N_HEADS, 1, ta), F32),
                            pltpu.VMEM((N_HEADS, V_DIM, ta), F32)],
        ),
        out_shape=jax.ShapeDtypeStruct((tokens, ATTN_WIDTH), BF16),
        compiler_params=cparams,
        name="mla_flash_attn",
    )(qi, ki, first, last, qt, k, vt)

    w_o_b = w_o.astype(BF16)
    return pl.pallas_call(
        _out_kernel,
        grid=(tokens // tm,),
        in_specs=[
            _row_spec(tm, ATTN_WIDTH), _row_spec(tm, ATTN_WIDTH), _row_spec(tm, CONV_WIDTH),
            _row_spec(tm, D_MODEL), _row_spec(tm, PLE_DIM),
            _const_spec((1, ATTN_WIDTH)), _const_spec((ATTN_WIDTH, D_MODEL)),
            _const_spec((CONV_WIDTH, D_MODEL)), _const_spec((1, D_MODEL)),
            _const_spec((D_MODEL, D_MODEL)), _const_spec((PLE_DIM, D_MODEL)),
        ],
        out_specs=_row_spec(tm, D_MODEL),
        out_shape=jax.ShapeDtypeStruct((tokens, D_MODEL), F32),
        compiler_params=cparams,
        name="mla_conv_out",
    )(o, ga, yc, x2, p2, g_oa.reshape(1, -1), w_o_b[:ATTN_WIDTH], w_o_b[ATTN_WIDTH:],
      g_pl.reshape(1, -1), w_plg.astype(BF16), w_pl.astype(BF16))


def kernel(x, p, positions, g_in, w_in, g_cq, w_uq, g_ckv, w_ukv, g_q, g_k, conv_w, g_oa, g_oc,
           w_o, w_pl, w_plg, g_pl):
    batch, seq, d_model = x.shape
    depth = p.shape[0]
    tokens = batch * seq
    posb = jnp.broadcast_to(positions.astype(F32).reshape(tokens, 1), (tokens, LANES))
    inv_freq = 1.0 / (ROPE_THETA ** (jnp.arange(0, ROPE_DIM, 2, dtype=F32) / ROPE_DIM))
    invf = jnp.tile(inv_freq, LANES // HALF_ROPE).reshape(1, LANES)
    h = x.reshape(tokens, d_model)
    for i in range(depth):
        h = _layer(h, p[i].reshape(tokens, PLE_DIM), posb, invf, batch, seq,
                   g_in[i], w_in[i], g_cq[i], w_uq[i], g_ckv[i], w_ukv[i], g_q[i], g_k[i],
                   conv_w[i], g_oa[i], g_oc[i], w_o[i], w_pl[i], w_plg[i], g_pl[i])
    return h.reshape(batch, seq, d_model).astype(x.dtype)
```

```python
import functools
import math

import jax
import jax.numpy as jnp
import numpy as np
from jax import lax
from jax.experimental import pallas as pl
from jax.experimental.pallas import tpu as pltpu

D_MODEL = 1024
PLE_DIM = 256
N_HEADS = 4
NOPE_DIM = 128
ROPE_DIM = 64
HALF_ROPE = ROPE_DIM // 2
V_DIM = 128
QK_DIM = NOPE_DIM + ROPE_DIM
Q_LORA = 256
KV_LORA = 128
ATTN_WIDTH = N_HEADS * V_DIM
CONV_WIDTH = D_MODEL - ATTN_WIDTH
CONV_K = 3
ROPE_THETA = 10000.0
RMS_EPS = 1e-6
NEG_INF = -1e30

LANES = 128
SUBLANES = 8
HEAD_PAD = 2 * LANES
KPE_PAD = LANES
OFF_CQ = 0
OFF_CKV = OFF_CQ + Q_LORA
OFF_KPE = OFF_CKV + KV_LORA
OFF_ZA = OFF_KPE + KPE_PAD
OFF_CB = OFF_ZA + ATTN_WIDTH
OFF_CC = OFF_CB + CONV_WIDTH
OFF_CX = OFF_CC + CONV_WIDTH
OFF_ZC = OFF_CX + CONV_WIDTH
IN_PAD = OFF_ZC + CONV_WIDTH

PROJ_TILE = 512
ATTN_TILE = 512
VMEM_LIMIT = 48 * 1024 * 1024

BF16 = jnp.bfloat16
F32 = jnp.float32


def _rms_scale(v, width):
    return lax.rsqrt(jnp.sum(v * v, axis=-1, keepdims=True) * (1.0 / width) + RMS_EPS)


def _silu(z):
    return z / (1.0 + jnp.exp(-z))


def _dot(a, b):
    return jnp.dot(a, b, preferred_element_type=F32)


def _rope_tables(pos_ref, invf_ref):
    ang = pos_ref[...] * invf_ref[...]
    lane = lax.broadcasted_iota(jnp.int32, (1, LANES), 1)
    cosv = jnp.cos(ang)
    sinv = jnp.sin(ang)
    c = jnp.where(lane < ROPE_DIM, cosv, 0.0)
    s_lo = jnp.where(lane < HALF_ROPE, -sinv, 0.0)
    s_hi = jnp.where((lane >= HALF_ROPE) & (lane < ROPE_DIM), sinv, 0.0)
    return c, s_lo, s_hi


def _rope(t, tables):
    c, s_lo, s_hi = tables
    return (t * c + pltpu.roll(t, LANES - HALF_ROPE, 1) * s_lo
            + pltpu.roll(t, HALF_ROPE, 1) * s_hi)


def _proj_kernel(tiles_per_seq,
                 x_ref, pos_ref, invf_ref, g_in_ref, w_in_ref, g_cq_ref, w_uq_ref,
                 g_ckv_ref, w_ukv_ref, g_q_ref, g_kn_ref, g_kr_ref, conv_w_ref, g_oc_ref,
                 qt_ref, k_ref, vt_ref, ga_ref, yc_ref, carry_ref):
    tm = x_ref.shape[0]
    x = x_ref[...]
    h = (x * _rms_scale(x, D_MODEL) * g_in_ref[...]).astype(BF16)

    def proj(off, width):
        return _dot(h, w_in_ref[:, off:off + width])

    tables = _rope_tables(pos_ref, invf_ref)
    q_scale = math.log2(math.e) / math.sqrt(QK_DIM)

    c_q = proj(OFF_CQ, Q_LORA)
    cqn = (c_q * _rms_scale(c_q, Q_LORA) * g_cq_ref[...]).astype(BF16)
    qf = _dot(cqn, w_uq_ref[...])
    for hd in range(N_HEADS):
        qh = qf[:, hd * HEAD_PAD:(hd + 1) * HEAD_PAD]
        qn = qh * (_rms_scale(qh, QK_DIM) * q_scale) * g_q_ref[...]
        qt_ref[hd * HEAD_PAD:hd * HEAD_PAD + LANES, :] = qn[:, :LANES].T.astype(BF16)
        qt_ref[hd * HEAD_PAD + LANES:(hd + 1) * HEAD_PAD, :] = _rope(qn[:, LANES:], tables).T.astype(BF16)

    c_kv = proj(OFF_CKV, KV_LORA)
    ckvn = (c_kv * _rms_scale(c_kv, KV_LORA) * g_ckv_ref[...]).astype(BF16)
    kv = _dot(ckvn, w_ukv_ref[...])
    kpe = proj(OFF_KPE, KPE_PAD)
    ss_kpe = jnp.sum(kpe * kpe, axis=-1, keepdims=True)
    k_rot = _rope(kpe * g_kr_ref[...], tables)
    for hd in range(N_HEADS):
        kn = kv[:, hd * (NOPE_DIM + V_DIM):hd * (NOPE_DIM + V_DIM) + NOPE_DIM]
        vh = kv[:, hd * (NOPE_DIM + V_DIM) + NOPE_DIM:(hd + 1) * (NOPE_DIM + V_DIM)]
        ss = jnp.sum(kn * kn, axis=-1, keepdims=True) + ss_kpe
        rs = lax.rsqrt(ss * (1.0 / QK_DIM) + RMS_EPS)
        k_ref[:, hd * HEAD_PAD:hd * HEAD_PAD + LANES] = (kn * rs * g_kn_ref[...]).astype(BF16)
        k_ref[:, hd * HEAD_PAD + LANES:(hd + 1) * HEAD_PAD] = (k_rot * rs).astype(BF16)
        vt_ref[hd * V_DIM:(hd + 1) * V_DIM, :] = vh.T.astype(BF16)

    ga_ref[...] = _silu(proj(OFF_ZA, ATTN_WIDTH)).astype(BF16)

    @pl.when(pl.program_id(0) % tiles_per_seq == 0)
    def _():
        carry_ref[0:SUBLANES, :] = jnp.zeros((SUBLANES, CONV_WIDTH), F32)

    cv = proj(OFF_CC, CONV_WIDTH) * proj(OFF_CX, CONV_WIDTH)
    carry_ref[SUBLANES:SUBLANES + tm, :] = cv
    u = (conv_w_ref[2:3, :] * cv
         + conv_w_ref[1:2, :] * carry_ref[SUBLANES - 1:SUBLANES - 1 + tm, :]
         + conv_w_ref[0:1, :] * carry_ref[SUBLANES - 2:SUBLANES - 2 + tm, :])
    carry_ref[0:SUBLANES, :] = cv[tm - SUBLANES:, :]
    yc = proj(OFF_CB, CONV_WIDTH) * u * _silu(proj(OFF_ZC, CONV_WIDTH))
    yc_ref[...] = (yc * _rms_scale(yc, CONV_WIDTH) * g_oc_ref[...]).astype(BF16)


def _attn_kernel(qn_ref, kn_ref, kc_ref, qc_ref, diag_n_ref, first_c_ref, last_c_ref,
                 qt_ref, k_ref, vt_ref, o_ref, s_ref, mx_ref, m_ref, l_ref, acc_ref):
    step = pl.program_id(0)
    tq = qt_ref.shape[1]
    tk = k_ref.shape[0]

    @pl.when(step == 0)
    def _():
        s_ref[...] = jnp.zeros(s_ref.shape, F32)
        mx_ref[...] = jnp.zeros(mx_ref.shape, F32)

    @pl.when((first_c_ref[step] == 1) | (step == 0))
    def _():
        m_ref[...] = jnp.full(m_ref.shape, NEG_INF, F32)
        l_ref[...] = jnp.zeros(l_ref.shape, F32)
        acc_ref[...] = jnp.zeros(acc_ref.shape, F32)

    def body(masked):
        if masked:
            key = lax.broadcasted_iota(jnp.int32, (tk, tq), 0)
            qry = lax.broadcasted_iota(jnp.int32, (tk, tq), 1)
            keep = key <= qry

        def scores(hd):
            s = _dot(k_ref[:, hd * HEAD_PAD:(hd + 1) * HEAD_PAD],
                     qt_ref[hd * HEAD_PAD:(hd + 1) * HEAD_PAD, :])
            if masked:
                s = jnp.where(keep, s, NEG_INF)
            s_ref[hd] = s
            mx_ref[hd] = jnp.max(s, axis=0, keepdims=True)

        def softmax_pv(hd):
            m_prev = m_ref[hd]
            m_new = jnp.maximum(m_prev, mx_ref[hd])
            alpha = jnp.exp2(m_prev - m_new)
            p = jnp.exp2(s_ref[hd] - m_new)
            l_ref[hd] = alpha * l_ref[hd] + jnp.sum(p, axis=0, keepdims=True)
            pv = _dot(vt_ref[hd * V_DIM:(hd + 1) * V_DIM, :], p.astype(BF16))
            acc_ref[hd] = alpha * acc_ref[hd] + pv
            m_ref[hd] = m_new

        for hd in range(N_HEADS):
            softmax_pv(hd)
            scores(hd)

    is_diag = diag_n_ref[step] == 1

    @pl.when(jnp.logical_not(is_diag))
    def _():
        body(False)

    @pl.when(is_diag)
    def _():
        body(True)

    @pl.when(last_c_ref[step] == 1)
    def _():
        for hd in range(N_HEADS):
            o_ref[:, hd * V_DIM:(hd + 1) * V_DIM] = (acc_ref[hd] / l_ref[hd]).T.astype(o_ref.dtype)


def _out_kernel(o_ref, ga_ref, yc_ref, x_ref, p_ref, g_oa_ref, w_oa_ref, w_oc_ref,
                g_pl_ref, w_plg_ref, w_pl_ref, out_ref):
    ya = o_ref[...].astype(F32) * ga_ref[...].astype(F32)
    ya_n = (ya * _rms_scale(ya, ATTN_WIDTH) * g_oa_ref[...]).astype(BF16)
    x1 = x_ref[...] + _dot(ya_n, w_oa_ref[...]) + _dot(yc_ref[...], w_oc_ref[...])
    hn = (x1 * _rms_scale(x1, D_MODEL) * g_pl_ref[...]).astype(BF16)
    gate = 1.0 / (1.0 + jnp.exp(-_dot(hn, w_plg_ref[...])))
    out_ref[...] = x1 + gate * _dot(p_ref[...].astype(BF16), w_pl_ref[...])


def _row_spec(tile, width):
    return pl.BlockSpec((tile, width), lambda i: (i, 0))


def _const_spec(shape):
    return pl.BlockSpec(shape, lambda i: (0,) * len(shape))


def _triangle_schedule(batch, n_blk):
    pairs = [(b * n_blk + i, b * n_blk + j, j == 0, j == i)
             for b in range(batch) for i in range(n_blk) for j in range(i + 1)]
    nxt = pairs + [pairs[-1]]
    cur = [pairs[0]] + pairs
    cols = ([p[0] for p in nxt], [p[1] for p in nxt], [p[1] for p in cur], [p[0] for p in cur],
            [int(p[3]) for p in nxt],
            [0] + [int(p[2]) for p in pairs], [0] + [int(p[3]) for p in pairs])
    return tuple(jnp.asarray(np.asarray(c, np.int32)) for c in cols)


def _layer(x2, p2, posb, invf, batch, seq, g_in, w_in, g_cq, w_uq, g_ckv, w_ukv, g_q, g_k,
           conv_w, g_oa, g_oc, w_o, w_pl, w_plg, g_pl):
    tokens = batch * seq
    tm = PROJ_TILE
    assert seq % tm == 0 and seq % ATTN_TILE == 0

    w_in_p = jnp.concatenate(
        [w_in[:, :OFF_KPE + ROPE_DIM], jnp.zeros((D_MODEL, KPE_PAD - ROPE_DIM), F32),
         w_in[:, OFF_KPE + ROPE_DIM:]], axis=1).astype(BF16)
    w_uq_p = jnp.pad(w_uq.reshape(Q_LORA, N_HEADS, QK_DIM),
                     ((0, 0), (0, 0), (0, HEAD_PAD - QK_DIM))).reshape(Q_LORA, N_HEADS * HEAD_PAD).astype(BF16)
    g_q_p = jnp.pad(g_q, (0, HEAD_PAD - QK_DIM)).reshape(1, HEAD_PAD)
    g_kn = g_k[:NOPE_DIM].reshape(1, NOPE_DIM)
    g_kr = jnp.pad(g_k[NOPE_DIM:], (0, KPE_PAD - ROPE_DIM)).reshape(1, KPE_PAD)

    cparams = pltpu.CompilerParams(dimension_semantics=("arbitrary",), vmem_limit_bytes=VMEM_LIMIT)

    def col_spec(rows, tile):
        return pl.BlockSpec((rows, tile), lambda i: (0, i))

    qt, k, vt, ga, yc = pl.pallas_call(
        functools.partial(_proj_kernel, seq // tm),
        grid=(tokens // tm,),
        in_specs=[
            _row_spec(tm, D_MODEL), _row_spec(tm, LANES), _const_spec((1, LANES)),
            _const_spec((1, D_MODEL)), _const_spec((D_MODEL, IN_PAD)),
            _const_spec((1, Q_LORA)), _const_spec((Q_LORA, N_HEADS * HEAD_PAD)),
            _const_spec((1, KV_LORA)), _const_spec((KV_LORA, N_HEADS * (NOPE_DIM + V_DIM))),
            _const_spec((1, HEAD_PAD)), _const_spec((1, NOPE_DIM)), _const_spec((1, KPE_PAD)),
            _const_spec((CONV_K, CONV_WIDTH)), _const_spec((1, CONV_WIDTH)),
        ],
        out_specs=[
            col_spec(N_HEADS * HEAD_PAD, tm), _row_spec(tm, N_HEADS * HEAD_PAD),
            col_spec(ATTN_WIDTH, tm), _row_spec(tm, ATTN_WIDTH), _row_spec(tm, CONV_WIDTH),
        ],
        out_shape=[
            jax.ShapeDtypeStruct((N_HEADS * HEAD_PAD, tokens), BF16),
            jax.ShapeDtypeStruct((tokens, N_HEADS * HEAD_PAD), BF16),
            jax.ShapeDtypeStruct((ATTN_WIDTH, tokens), BF16),
            jax.ShapeDtypeStruct((tokens, ATTN_WIDTH), BF16),
            jax.ShapeDtypeStruct((tokens, CONV_WIDTH), BF16),
        ],
        scratch_shapes=[pltpu.VMEM((tm + SUBLANES, CONV_WIDTH), F32)],
        compiler_params=cparams,
        name="mla_conv_proj",
    )(x2, posb, invf, g_in.reshape(1, -1), w_in_p, g_cq.reshape(1, -1), w_uq_p,
      g_ckv.reshape(1, -1), w_ukv.astype(BF16), g_q_p, g_kn, g_kr, conv_w, g_oc.reshape(1, -1))

    ta = ATTN_TILE
    sched = _triangle_schedule(batch, seq // ta)
    o = pl.pallas_call(
        _attn_kernel,
        grid_spec=pltpu.PrefetchScalarGridSpec(
            num_scalar_prefetch=len(sched),
            grid=(int(sched[0].shape[0]),),
            in_specs=[
                pl.BlockSpec((N_HEADS * HEAD_PAD, ta), lambda s, qn, kn, kc, qc, *_: (0, qn[s])),
                pl.BlockSpec((ta, N_HEADS * HEAD_PAD), lambda s, qn, kn, kc, qc, *_: (kn[s], 0)),
                pl.BlockSpec((ATTN_WIDTH, ta), lambda s, qn, kn, kc, qc, *_: (0, kc[s])),
            ],
            out_specs=pl.BlockSpec((ta, ATTN_WIDTH), lambda s, qn, kn, kc, qc, *_: (qc[s], 0)),
            scratch_shapes=[pltpu.VMEM((N_HEADS, ta, ta), F32),
                            pltpu.VMEM((N_HEADS, 1, ta), F32),
                            pltpu.VMEM((N_HEADS, 1, ta), F32),
                            pltpu.VMEM((N_HEADS, 1, ta), F32),
                            pltpu.VMEM((N_HEADS, V_DIM, ta), F32)],
        ),
        out_shape=jax.ShapeDtypeStruct((tokens, ATTN_WIDTH), BF16),
        compiler_params=cparams,
        name="mla_flash_attn",
    )(*sched, qt, k, vt)

    w_o_b = w_o.astype(BF16)
    return pl.pallas_call(
        _out_kernel,
        grid=(tokens // tm,),
        in_specs=[
            _row_spec(tm, ATTN_WIDTH), _row_spec(tm, ATTN_WIDTH), _row_spec(tm, CONV_WIDTH),
            _row_spec(tm, D_MODEL), _row_spec(tm, PLE_DIM),
            _const_spec((1, ATTN_WIDTH)), _const_spec((ATTN_WIDTH, D_MODEL)),
            _const_spec((CONV_WIDTH, D_MODEL)), _const_spec((1, D_MODEL)),
            _const_spec((D_MODEL, D_MODEL)), _const_spec((PLE_DIM, D_MODEL)),
        ],
        out_specs=_row_spec(tm, D_MODEL),
        out_shape=jax.ShapeDtypeStruct((tokens, D_MODEL), F32),
        compiler_params=cparams,
        name="mla_conv_out",
    )(o, ga, yc, x2, p2, g_oa.reshape(1, -1), w_o_b[:ATTN_WIDTH], w_o_b[ATTN_WIDTH:],
      g_pl.reshape(1, -1), w_plg.astype(BF16), w_pl.astype(BF16))


def kernel(x, p, positions, g_in, w_in, g_cq, w_uq, g_ckv, w_ukv, g_q, g_k, conv_w, g_oa, g_oc,
           w_o, w_pl, w_plg, g_pl):
    batch, seq, d_model = x.shape
    depth = p.shape[0]
    tokens = batch * seq
    posb = jnp.broadcast_to(positions.astype(F32).reshape(tokens, 1), (tokens, LANES))
    inv_freq = 1.0 / (ROPE_THETA ** (jnp.arange(0, ROPE_DIM, 2, dtype=F32) / ROPE_DIM))
    invf = jnp.tile(inv_freq, LANES // HALF_ROPE).reshape(1, LANES)
    h = x.reshape(tokens, d_model)
    for i in range(depth):
        h = _layer(h, p[i].reshape(tokens, PLE_DIM), posb, invf, batch, seq,
                   g_in[i], w_in[i], g_cq[i], w_uq[i], g_ckv[i], w_ukv[i], g_q[i], g_k[i],
                   conv_w[i], g_oa[i], g_oc[i], w_o[i], w_pl[i], w_plg[i], g_pl[i])
    return h.reshape(batch, seq, d_model).astype(x.dtype)
```

```python
import functools
import math

import jax
import jax.numpy as jnp
import numpy as np
from jax import lax
from jax.experimental import pallas as pl
from jax.experimental.pallas import tpu as pltpu

D_MODEL = 1024
PLE_DIM = 256
N_HEADS = 4
NOPE_DIM = 128
ROPE_DIM = 64
HALF_ROPE = ROPE_DIM // 2
V_DIM = 128
QK_DIM = NOPE_DIM + ROPE_DIM
Q_LORA = 256
KV_LORA = 128
ATTN_WIDTH = N_HEADS * V_DIM
CONV_WIDTH = D_MODEL - ATTN_WIDTH
CONV_K = 3
ROPE_THETA = 10000.0
RMS_EPS = 1e-6
NEG_INF = -1e30

LANES = 128
SUBLANES = 8
HEAD_PAD = 2 * LANES
KPE_PAD = LANES
TOKENS_PER_ROW = LANES // HALF_ROPE
OFF_CQ = 0
OFF_CKV = OFF_CQ + Q_LORA
OFF_KPE = OFF_CKV + KV_LORA
OFF_ZA = OFF_KPE + KPE_PAD
OFF_CB = OFF_ZA + ATTN_WIDTH
OFF_CC = OFF_CB + CONV_WIDTH
OFF_CX = OFF_CC + CONV_WIDTH
OFF_ZC = OFF_CX + CONV_WIDTH
IN_PAD = OFF_ZC + CONV_WIDTH

PROJ_TILE = 1024
PROJ_ROWS = 256
ATTN_TILE = 512
VMEM_LIMIT = 48 * 1024 * 1024

BF16 = jnp.bfloat16
F32 = jnp.float32


def _rms_scale(v, width):
    return lax.rsqrt(jnp.sum(v * v, axis=-1, keepdims=True) * (1.0 / width) + RMS_EPS)


def _silu(z):
    return z / (1.0 + jnp.exp(-z))


def _dot(a, b):
    return jnp.dot(a, b, preferred_element_type=F32)


def _rope_tables(posc, invf):
    ang = posc * invf
    cosc = jnp.cos(ang)
    sinc = jnp.sin(ang)
    lane = lax.broadcasted_iota(jnp.int32, (1, LANES), 1)
    lo = lane < HALF_ROPE
    mid = (lane >= HALF_ROPE) & (lane < ROPE_DIM)
    quarters = []
    for a in range(TOKENS_PER_ROW):
        c = pltpu.roll(cosc, LANES - HALF_ROPE * a, 1) if a else cosc
        s = pltpu.roll(sinc, LANES - HALF_ROPE * a, 1) if a else sinc
        quarters.append((jnp.where(lo, c, jnp.where(mid, pltpu.roll(c, HALF_ROPE, 1), 0.0)),
                         jnp.where(lo, -s, jnp.where(mid, pltpu.roll(s, HALF_ROPE, 1), 0.0))))
    return quarters, lo


def _rope(t, c, s, lo):
    swapped = jnp.where(lo, pltpu.roll(t, LANES - HALF_ROPE, 1), pltpu.roll(t, HALF_ROPE, 1))
    return t * c + swapped * s


def _proj_kernel(tiles_per_seq,
                 x_ref, posc_ref, invf_ref, g_in_ref, w_in_ref, g_cq_ref, w_uq_ref,
                 g_ckv_ref, w_ukv_ref, g_q_ref, g_kn_ref, g_kr_ref, conv_w_ref, g_oc_ref,
                 qt_ref, k_ref, vt_ref, ga_ref, yc_ref, carry_ref):
    tm = x_ref.shape[0]
    quarter_rows = tm // TOKENS_PER_ROW
    tables = []
    q_scale = math.log2(math.e) / math.sqrt(QK_DIM)

    @pl.when(pl.program_id(0) % tiles_per_seq == 0)
    def _():
        carry_ref[0:SUBLANES, :] = jnp.zeros((SUBLANES, CONV_WIDTH), F32)

    def rows_block(r0, nr):
        x = x_ref[r0:r0 + nr, :]
        h = (x * _rms_scale(x, D_MODEL) * g_in_ref[...]).astype(BF16)

        def proj(off, width):
            return _dot(h, w_in_ref[:, off:off + width])

        lat = proj(OFF_CQ, OFF_ZA - OFF_CQ)
        c_q = lat[:, OFF_CQ:OFF_CKV]
        c_kv = lat[:, OFF_CKV:OFF_KPE]
        kpe = lat[:, OFF_KPE:OFF_ZA]
        cv = proj(OFF_CC, CONV_WIDTH) * proj(OFF_CX, CONV_WIDTH)
        cqn = (c_q * _rms_scale(c_q, Q_LORA) * g_cq_ref[...]).astype(BF16)
        qf = _dot(cqn, w_uq_ref[...])
        cb = proj(OFF_CB, CONV_WIDTH)
        ckvn = (c_kv * _rms_scale(c_kv, KV_LORA) * g_ckv_ref[...]).astype(BF16)
        kv = _dot(ckvn, w_ukv_ref[...])
        z_c = proj(OFF_ZC, CONV_WIDTH)
        z_a = proj(OFF_ZA, ATTN_WIDTH)

        if not tables:
            tables.extend(_rope_tables(posc_ref[...], invf_ref[...]))
        quarters, lo = tables
        qa = r0 // quarter_rows
        cos_t = jnp.concatenate([quarters[qa + i][0] for i in range(nr // quarter_rows)], axis=0)
        sin_t = jnp.concatenate([quarters[qa + i][1] for i in range(nr // quarter_rows)], axis=0)

        for hd in range(N_HEADS):
            qh = qf[:, hd * HEAD_PAD:(hd + 1) * HEAD_PAD]
            qn = qh * (_rms_scale(qh, QK_DIM) * q_scale) * g_q_ref[...]
            qt_ref[hd * HEAD_PAD:hd * HEAD_PAD + LANES, r0:r0 + nr] = qn[:, :LANES].T.astype(BF16)
            qt_ref[hd * HEAD_PAD + LANES:(hd + 1) * HEAD_PAD, r0:r0 + nr] = (
                _rope(qn[:, LANES:], cos_t, sin_t, lo).T.astype(BF16))

        ss_kpe = jnp.sum(kpe * kpe, axis=-1, keepdims=True)
        k_rot = _rope(kpe * g_kr_ref[...], cos_t, sin_t, lo)
        for hd in range(N_HEADS):
            kn = kv[:, hd * (NOPE_DIM + V_DIM):hd * (NOPE_DIM + V_DIM) + NOPE_DIM]
            vh = kv[:, hd * (NOPE_DIM + V_DIM) + NOPE_DIM:(hd + 1) * (NOPE_DIM + V_DIM)]
            ss = jnp.sum(kn * kn, axis=-1, keepdims=True) + ss_kpe
            rs = lax.rsqrt(ss * (1.0 / QK_DIM) + RMS_EPS)
            k_ref[r0:r0 + nr, hd * HEAD_PAD:hd * HEAD_PAD + LANES] = (kn * rs * g_kn_ref[...]).astype(BF16)
            k_ref[r0:r0 + nr, hd * HEAD_PAD + LANES:(hd + 1) * HEAD_PAD] = (k_rot * rs).astype(BF16)
            vt_ref[hd * V_DIM:(hd + 1) * V_DIM, r0:r0 + nr] = vh.T.astype(BF16)

        base = SUBLANES + r0
        carry_ref[base:base + nr, :] = cv
        u = (conv_w_ref[2:3, :] * cv
             + conv_w_ref[1:2, :] * carry_ref[base - 1:base - 1 + nr, :]
             + conv_w_ref[0:1, :] * carry_ref[base - 2:base - 2 + nr, :])
        yc = cb * u * _silu(z_c)
        yc_ref[r0:r0 + nr, :] = (yc * _rms_scale(yc, CONV_WIDTH) * g_oc_ref[...]).astype(BF16)

        ga_ref[r0:r0 + nr, :] = _silu(z_a).astype(BF16)

    for blk in range(tm // PROJ_ROWS):
        rows_block(blk * PROJ_ROWS, PROJ_ROWS)
    carry_ref[0:SUBLANES, :] = carry_ref[tm:tm + SUBLANES, :]


def _attn_kernel(qn_ref, kn_ref, kc_ref, qc_ref, diag_n_ref, first_c_ref, last_c_ref,
                 qt_ref, k_ref, vt_ref, o_ref, s_ref, mx_ref, m_ref, l_ref, acc_ref):
    step = pl.program_id(0)
    tq = qt_ref.shape[1]
    tk = k_ref.shape[0]

    @pl.when(step == 0)
    def _():
        s_ref[...] = jnp.zeros(s_ref.shape, F32)
        mx_ref[...] = jnp.zeros(mx_ref.shape, F32)

    @pl.when((first_c_ref[step] == 1) | (step == 0))
    def _():
        m_ref[...] = jnp.full(m_ref.shape, NEG_INF, F32)
        l_ref[...] = jnp.zeros(l_ref.shape, F32)
        acc_ref[...] = jnp.zeros(acc_ref.shape, F32)

    def body(masked):
        if masked:
            key = lax.broadcasted_iota(jnp.int32, (tk, tq), 0)
            qry = lax.broadcasted_iota(jnp.int32, (tk, tq), 1)
            keep = key <= qry

        def scores(hd):
            s = _dot(k_ref[:, hd * HEAD_PAD:(hd + 1) * HEAD_PAD],
                     qt_ref[hd * HEAD_PAD:(hd + 1) * HEAD_PAD, :])
            if masked:
                s = jnp.where(keep, s, NEG_INF)
            s_ref[hd] = s
            mx_ref[hd] = jnp.max(s, axis=0, keepdims=True)

        def softmax_pv(hd):
            m_prev = m_ref[hd]
            m_new = jnp.maximum(m_prev, mx_ref[hd])
            alpha = jnp.exp2(m_prev - m_new)
            p = jnp.exp2(s_ref[hd] - m_new)
            l_ref[hd] = alpha * l_ref[hd] + jnp.sum(p, axis=0, keepdims=True)
            pv = _dot(vt_ref[hd * V_DIM:(hd + 1) * V_DIM, :], p.astype(BF16))
            acc_ref[hd] = alpha * acc_ref[hd] + pv
            m_ref[hd] = m_new

        for hd in range(N_HEADS):
            softmax_pv(hd)
            scores(hd)

    is_diag = diag_n_ref[step] == 1

    @pl.when(jnp.logical_not(is_diag))
    def _():
        body(False)

    @pl.when(is_diag)
    def _():
        body(True)

    @pl.when(last_c_ref[step] == 1)
    def _():
        for hd in range(N_HEADS):
            o_ref[:, hd * V_DIM:(hd + 1) * V_DIM] = (acc_ref[hd] / l_ref[hd]).T.astype(o_ref.dtype)


def _out_kernel(o_ref, ga_ref, yc_ref, x_ref, p_ref, g_oa_ref, w_oa_ref, w_oc_ref,
                g_pl_ref, w_plg_ref, w_pl_ref, out_ref):
    ya = o_ref[...].astype(F32) * ga_ref[...].astype(F32)
    ya_n = (ya * _rms_scale(ya, ATTN_WIDTH) * g_oa_ref[...]).astype(BF16)
    x1 = x_ref[...] + _dot(ya_n, w_oa_ref[...]) + _dot(yc_ref[...], w_oc_ref[...])
    hn = (x1 * _rms_scale(x1, D_MODEL) * g_pl_ref[...]).astype(BF16)
    gate = 1.0 / (1.0 + jnp.exp(-_dot(hn, w_plg_ref[...])))
    out_ref[...] = x1 + gate * _dot(p_ref[...].astype(BF16), w_pl_ref[...])


def _row_spec(tile, width):
    return pl.BlockSpec((tile, width), lambda i: (i, 0))


def _const_spec(shape):
    return pl.BlockSpec(shape, lambda i: (0,) * len(shape))


def _triangle_schedule(batch, n_blk):
    pairs = [(b * n_blk + i, b * n_blk + j, j == 0, j == i)
             for b in range(batch) for i in range(n_blk) for j in range(i + 1)]
    nxt = pairs + [pairs[-1]]
    cur = [pairs[0]] + pairs
    cols = ([p[0] for p in nxt], [p[1] for p in nxt], [p[1] for p in cur], [p[0] for p in cur],
            [int(p[3]) for p in nxt],
            [0] + [int(p[2]) for p in pairs], [0] + [int(p[3]) for p in pairs])
    return tuple(jnp.asarray(np.asarray(c, np.int32)) for c in cols)


def _layer(x2, p2, posc, invf, batch, seq, g_in, w_in, g_cq, w_uq, g_ckv, w_ukv, g_q, g_k,
           conv_w, g_oa, g_oc, w_o, w_pl, w_plg, g_pl):
    tokens = batch * seq
    tm = PROJ_TILE
    assert seq % tm == 0 and seq % ATTN_TILE == 0

    w_in_p = jnp.concatenate(
        [w_in[:, :OFF_KPE + ROPE_DIM], jnp.zeros((D_MODEL, KPE_PAD - ROPE_DIM), F32),
         w_in[:, OFF_KPE + ROPE_DIM:]], axis=1).astype(BF16)
    w_uq_p = jnp.pad(w_uq.reshape(Q_LORA, N_HEADS, QK_DIM),
                     ((0, 0), (0, 0), (0, HEAD_PAD - QK_DIM))).reshape(Q_LORA, N_HEADS * HEAD_PAD).astype(BF16)
    g_q_p = jnp.pad(g_q, (0, HEAD_PAD - QK_DIM)).reshape(1, HEAD_PAD)
    g_kn = g_k[:NOPE_DIM].reshape(1, NOPE_DIM)
    g_kr = jnp.pad(g_k[NOPE_DIM:], (0, KPE_PAD - ROPE_DIM)).reshape(1, KPE_PAD)

    cparams = pltpu.CompilerParams(dimension_semantics=("arbitrary",), vmem_limit_bytes=VMEM_LIMIT)

    def col_spec(rows, tile):
        return pl.BlockSpec((rows, tile), lambda i: (0, i))

    qt, k, vt, ga, yc = pl.pallas_call(
        functools.partial(_proj_kernel, seq // tm),
        grid=(tokens // tm,),
        in_specs=[
            _row_spec(tm, D_MODEL), _row_spec(tm // TOKENS_PER_ROW, LANES), _const_spec((1, LANES)),
            _const_spec((1, D_MODEL)), _const_spec((D_MODEL, IN_PAD)),
            _const_spec((1, Q_LORA)), _const_spec((Q_LORA, N_HEADS * HEAD_PAD)),
            _const_spec((1, KV_LORA)), _const_spec((KV_LORA, N_HEADS * (NOPE_DIM + V_DIM))),
            _const_spec((1, HEAD_PAD)), _const_spec((1, NOPE_DIM)), _const_spec((1, KPE_PAD)),
            _const_spec((CONV_K, CONV_WIDTH)), _const_spec((1, CONV_WIDTH)),
        ],
        out_specs=[
            col_spec(N_HEADS * HEAD_PAD, tm), _row_spec(tm, N_HEADS * HEAD_PAD),
            col_spec(ATTN_WIDTH, tm), _row_spec(tm, ATTN_WIDTH), _row_spec(tm, CONV_WIDTH),
        ],
        out_shape=[
            jax.ShapeDtypeStruct((N_HEADS * HEAD_PAD, tokens), BF16),
            jax.ShapeDtypeStruct((tokens, N_HEADS * HEAD_PAD), BF16),
            jax.ShapeDtypeStruct((ATTN_WIDTH, tokens), BF16),
            jax.ShapeDtypeStruct((tokens, ATTN_WIDTH), BF16),
            jax.ShapeDtypeStruct((tokens, CONV_WIDTH), BF16),
        ],
        scratch_shapes=[pltpu.VMEM((tm + SUBLANES, CONV_WIDTH), F32)],
        compiler_params=cparams,
        name="mla_conv_proj",
    )(x2, posc, invf, g_in.reshape(1, -1), w_in_p, g_cq.reshape(1, -1), w_uq_p,
      g_ckv.reshape(1, -1), w_ukv.astype(BF16), g_q_p, g_kn, g_kr, conv_w, g_oc.reshape(1, -1))

    ta = ATTN_TILE
    sched = _triangle_schedule(batch, seq // ta)
    o = pl.pallas_call(
        _attn_kernel,
        grid_spec=pltpu.PrefetchScalarGridSpec(
            num_scalar_prefetch=len(sched),
            grid=(int(sched[0].shape[0]),),
            in_specs=[
                pl.BlockSpec((N_HEADS * HEAD_PAD, ta), lambda s, qn, kn, kc, qc, *_: (0, qn[s])),
                pl.BlockSpec((ta, N_HEADS * HEAD_PAD), lambda s, qn, kn, kc, qc, *_: (kn[s], 0)),
                pl.BlockSpec((ATTN_WIDTH, ta), lambda s, qn, kn, kc, qc, *_: (0, kc[s])),
            ],
            out_specs=pl.BlockSpec((ta, ATTN_WIDTH), lambda s, qn, kn, kc, qc, *_: (qc[s], 0)),
            scratch_shapes=[pltpu.VMEM((N_HEADS, ta, ta), F32),
                            pltpu.VMEM((N_HEADS, 1, ta), F32),
                            pltpu.VMEM((N_HEADS, 1, ta), F32),
                            pltpu.VMEM((N_HEADS, 1, ta), F32),
                            pltpu.VMEM((N_HEADS, V_DIM, ta), F32)],
        ),
        out_shape=jax.ShapeDtypeStruct((tokens, ATTN_WIDTH), BF16),
        compiler_params=cparams,
        name="mla_flash_attn",
    )(*sched, qt, k, vt)

    w_o_b = w_o.astype(BF16)
    return pl.pallas_call(
        _out_kernel,
        grid=(tokens // tm,),
        in_specs=[
            _row_spec(tm, ATTN_WIDTH), _row_spec(tm, ATTN_WIDTH), _row_spec(tm, CONV_WIDTH),
            _row_spec(tm, D_MODEL), _row_spec(tm, PLE_DIM),
            _const_spec((1, ATTN_WIDTH)), _const_spec((ATTN_WIDTH, D_MODEL)),
            _const_spec((CONV_WIDTH, D_MODEL)), _const_spec((1, D_MODEL)),
            _const_spec((D_MODEL, D_MODEL)), _const_spec((PLE_DIM, D_MODEL)),
        ],
        out_specs=_row_spec(tm, D_MODEL),
        out_shape=jax.ShapeDtypeStruct((tokens, D_MODEL), F32),
        compiler_params=cparams,
        name="mla_conv_out",
    )(o, ga, yc, x2, p2, g_oa.reshape(1, -1), w_o_b[:ATTN_WIDTH], w_o_b[ATTN_WIDTH:],
      g_pl.reshape(1, -1), w_plg.astype(BF16), w_pl.astype(BF16))


def kernel(x, p, positions, g_in, w_in, g_cq, w_uq, g_ckv, w_ukv, g_q, g_k, conv_w, g_oa, g_oc,
           w_o, w_pl, w_plg, g_pl):
    batch, seq, d_model = x.shape
    depth = p.shape[0]
    tokens = batch * seq
    rows = PROJ_TILE // TOKENS_PER_ROW
    posc = jnp.repeat(positions.astype(F32).reshape(tokens // PROJ_TILE, TOKENS_PER_ROW, rows)
                      .transpose(0, 2, 1), HALF_ROPE, axis=-1).reshape(tokens // TOKENS_PER_ROW, LANES)
    inv_freq = 1.0 / (ROPE_THETA ** (jnp.arange(0, ROPE_DIM, 2, dtype=F32) / ROPE_DIM))
    invf = jnp.tile(inv_freq, LANES // HALF_ROPE).reshape(1, LANES)
    h = x.reshape(tokens, d_model)
    for i in range(depth):
        h = _layer(h, p[i].reshape(tokens, PLE_DIM), posc, invf, batch, seq,
                   g_in[i], w_in[i], g_cq[i], w_uq[i], g_ckv[i], w_ukv[i], g_q[i], g_k[i],
                   conv_w[i], g_oa[i], g_oc[i], w_o[i], w_pl[i], w_plg[i], g_pl[i])
    return h.reshape(batch, seq, d_model).astype(x.dtype)
```

```python
import functools
import math

import jax
import jax.numpy as jnp
import numpy as np
from jax import lax
from jax.experimental import pallas as pl
from jax.experimental.pallas import tpu as pltpu

D_MODEL = 1024
PLE_DIM = 256
N_HEADS = 4
NOPE_DIM = 128
ROPE_DIM = 64
HALF_ROPE = ROPE_DIM // 2
V_DIM = 128
QK_DIM = NOPE_DIM + ROPE_DIM
Q_LORA = 256
KV_LORA = 128
ATTN_WIDTH = N_HEADS * V_DIM
CONV_WIDTH = D_MODEL - ATTN_WIDTH
CONV_K = 3
ROPE_THETA = 10000.0
RMS_EPS = 1e-6
NEG_INF = -1e30

LANES = 128
SUBLANES = 8
HEAD_PAD = 2 * LANES
KPE_PAD = LANES
TOKENS_PER_ROW = LANES // HALF_ROPE
OFF_CQ = 0
OFF_CKV = OFF_CQ + Q_LORA
OFF_KPE = OFF_CKV + KV_LORA
OFF_ZA = OFF_KPE + KPE_PAD
OFF_CB = OFF_ZA + ATTN_WIDTH
OFF_CC = OFF_CB + CONV_WIDTH
OFF_CX = OFF_CC + CONV_WIDTH
OFF_ZC = OFF_CX + CONV_WIDTH
IN_PAD = OFF_ZC + CONV_WIDTH

PROJ_TILE = 1024
PROJ_ROWS = 256
ATTN_TILE = 1024
ATTN_Q_CHUNK = 512
VMEM_LIMIT = 48 * 1024 * 1024

BF16 = jnp.bfloat16
F32 = jnp.float32


def _rms_scale(v, width):
    return lax.rsqrt(jnp.sum(v * v, axis=-1, keepdims=True) * (1.0 / width) + RMS_EPS)


def _silu(z):
    return z / (1.0 + jnp.exp(-z))


def _dot(a, b):
    return jnp.dot(a, b, preferred_element_type=F32)


def _rope_tables(posc, invf):
    ang = posc * invf
    cosc = jnp.cos(ang)
    sinc = jnp.sin(ang)
    lane = lax.broadcasted_iota(jnp.int32, (1, LANES), 1)
    lo = lane < HALF_ROPE
    mid = (lane >= HALF_ROPE) & (lane < ROPE_DIM)
    quarters = []
    for a in range(TOKENS_PER_ROW):
        c = pltpu.roll(cosc, LANES - HALF_ROPE * a, 1) if a else cosc
        s = pltpu.roll(sinc, LANES - HALF_ROPE * a, 1) if a else sinc
        quarters.append((jnp.where(lo, c, jnp.where(mid, pltpu.roll(c, HALF_ROPE, 1), 0.0)),
                         jnp.where(lo, -s, jnp.where(mid, pltpu.roll(s, HALF_ROPE, 1), 0.0))))
    return quarters, lo


def _rope(t, c, s, lo):
    swapped = jnp.where(lo, pltpu.roll(t, LANES - HALF_ROPE, 1), pltpu.roll(t, HALF_ROPE, 1))
    return t * c + swapped * s


def _proj_kernel(tiles_per_seq,
                 x_ref, posc_ref, invf_ref, g_in_ref, w_in_ref, g_cq_ref, w_uq_ref,
                 g_ckv_ref, w_ukv_ref, g_q_ref, g_kn_ref, g_kr_ref, conv_w_ref, g_oc_ref,
                 qt_ref, k_ref, vt_ref, ga_ref, yc_ref, carry_ref):
    tm = x_ref.shape[0]
    quarter_rows = tm // TOKENS_PER_ROW
    tables = []
    q_scale = math.log2(math.e) / math.sqrt(QK_DIM)

    @pl.when(pl.program_id(0) % tiles_per_seq == 0)
    def _():
        carry_ref[0:SUBLANES, :] = jnp.zeros((SUBLANES, CONV_WIDTH), F32)

    def rows_block(r0, nr):
        x = x_ref[r0:r0 + nr, :]
        h = (x * _rms_scale(x, D_MODEL) * g_in_ref[...]).astype(BF16)

        def proj(off, width):
            return _dot(h, w_in_ref[:, off:off + width])

        lat = proj(OFF_CQ, OFF_ZA - OFF_CQ)
        c_q = lat[:, OFF_CQ:OFF_CKV]
        c_kv = lat[:, OFF_CKV:OFF_KPE]
        kpe = lat[:, OFF_KPE:OFF_ZA]
        cv = proj(OFF_CC, CONV_WIDTH) * proj(OFF_CX, CONV_WIDTH)
        cqn = (c_q * _rms_scale(c_q, Q_LORA) * g_cq_ref[...]).astype(BF16)
        qf = _dot(cqn, w_uq_ref[...])
        cb = proj(OFF_CB, CONV_WIDTH)
        ckvn = (c_kv * _rms_scale(c_kv, KV_LORA) * g_ckv_ref[...]).astype(BF16)
        kv = _dot(ckvn, w_ukv_ref[...])
        z_c = proj(OFF_ZC, CONV_WIDTH)
        z_a = proj(OFF_ZA, ATTN_WIDTH)

        if not tables:
            tables.extend(_rope_tables(posc_ref[...], invf_ref[...]))
        quarters, lo = tables
        qa = r0 // quarter_rows
        cos_t = jnp.concatenate([quarters[qa + i][0] for i in range(nr // quarter_rows)], axis=0)
        sin_t = jnp.concatenate([quarters[qa + i][1] for i in range(nr // quarter_rows)], axis=0)

        for hd in range(N_HEADS):
            qh = qf[:, hd * HEAD_PAD:(hd + 1) * HEAD_PAD]
            qn = qh * (_rms_scale(qh, QK_DIM) * q_scale) * g_q_ref[...]
            qt_ref[hd * HEAD_PAD:hd * HEAD_PAD + LANES, r0:r0 + nr] = qn[:, :LANES].T.astype(BF16)
            qt_ref[hd * HEAD_PAD + LANES:(hd + 1) * HEAD_PAD, r0:r0 + nr] = (
                _rope(qn[:, LANES:], cos_t, sin_t, lo).T.astype(BF16))

        ss_kpe = jnp.sum(kpe * kpe, axis=-1, keepdims=True)
        k_rot = _rope(kpe * g_kr_ref[...], cos_t, sin_t, lo)
        for hd in range(N_HEADS):
            kn = kv[:, hd * (NOPE_DIM + V_DIM):hd * (NOPE_DIM + V_DIM) + NOPE_DIM]
            vh = kv[:, hd * (NOPE_DIM + V_DIM) + NOPE_DIM:(hd + 1) * (NOPE_DIM + V_DIM)]
            ss = jnp.sum(kn * kn, axis=-1, keepdims=True) + ss_kpe
            rs = lax.rsqrt(ss * (1.0 / QK_DIM) + RMS_EPS)
            k_ref[r0:r0 + nr, hd * HEAD_PAD:hd * HEAD_PAD + LANES] = (kn * rs * g_kn_ref[...]).astype(BF16)
            k_ref[r0:r0 + nr, hd * HEAD_PAD + LANES:(hd + 1) * HEAD_PAD] = (k_rot * rs).astype(BF16)
            vt_ref[hd * V_DIM:(hd + 1) * V_DIM, r0:r0 + nr] = vh.T.astype(BF16)

        base = SUBLANES + r0
        carry_ref[base:base + nr, :] = cv
        u = (conv_w_ref[2:3, :] * cv
             + conv_w_ref[1:2, :] * carry_ref[base - 1:base - 1 + nr, :]
             + conv_w_ref[0:1, :] * carry_ref[base - 2:base - 2 + nr, :])
        yc = cb * u * _silu(z_c)
        yc_ref[r0:r0 + nr, :] = (yc * _rms_scale(yc, CONV_WIDTH) * g_oc_ref[...]).astype(BF16)

        ga_ref[r0:r0 + nr, :] = _silu(z_a).astype(BF16)

    for blk in range(tm // PROJ_ROWS):
        rows_block(blk * PROJ_ROWS, PROJ_ROWS)
    carry_ref[0:SUBLANES, :] = carry_ref[tm:tm + SUBLANES, :]


def _attn_kernel(qn_ref, kn_ref, kc_ref, qc_ref, diag_n_ref, first_c_ref, last_c_ref,
                 qt_ref, k_ref, vt_ref, o_ref, s_ref, mx_ref, m_ref, l_ref, acc_ref):
    step = pl.program_id(0)
    tq = qt_ref.shape[1]
    tk = k_ref.shape[0]

    @pl.when(step == 0)
    def _():
        s_ref[...] = jnp.zeros(s_ref.shape, F32)
        mx_ref[...] = jnp.zeros(mx_ref.shape, F32)

    @pl.when((first_c_ref[step] == 1) | (step == 0))
    def _():
        m_ref[...] = jnp.full(m_ref.shape, NEG_INF, F32)
        l_ref[...] = jnp.zeros(l_ref.shape, F32)
        acc_ref[...] = jnp.zeros(acc_ref.shape, F32)

    nchunk = tq // ATTN_Q_CHUNK
    half = tk // 2

    def body(masked):
        if masked:
            assert nchunk == 2 and half == ATTN_Q_CHUNK
            tri = (lax.broadcasted_iota(jnp.int32, (half, ATTN_Q_CHUNK), 0)
                   <= lax.broadcasted_iota(jnp.int32, (half, ATTN_Q_CHUNK), 1))

        def scores(hd, c):
            cols = slice(c * ATTN_Q_CHUNK, (c + 1) * ATTN_Q_CHUNK)
            qt = qt_ref[hd * HEAD_PAD:(hd + 1) * HEAD_PAD, cols]
            if not masked:
                s = _dot(k_ref[:, hd * HEAD_PAD:(hd + 1) * HEAD_PAD], qt)
                s_ref[hd, :, cols] = s
                mx_ref[hd, :, cols] = jnp.max(s, axis=0, keepdims=True)
            elif c == 0:
                s = jnp.where(tri, _dot(k_ref[0:half, hd * HEAD_PAD:(hd + 1) * HEAD_PAD], qt), NEG_INF)
                s_ref[hd, 0:half, cols] = s
                s_ref[hd, half:tk, cols] = jnp.full((half, ATTN_Q_CHUNK), NEG_INF, F32)
                mx_ref[hd, :, cols] = jnp.max(s, axis=0, keepdims=True)
            else:
                s = _dot(k_ref[:, hd * HEAD_PAD:(hd + 1) * HEAD_PAD], qt)
                top = s[0:half]
                bot = jnp.where(tri, s[half:tk], NEG_INF)
                s_ref[hd, 0:half, cols] = top
                s_ref[hd, half:tk, cols] = bot
                mx_ref[hd, :, cols] = jnp.maximum(jnp.max(top, axis=0, keepdims=True),
                                                  jnp.max(bot, axis=0, keepdims=True))

        def softmax_pv(hd, c):
            cols = slice(c * ATTN_Q_CHUNK, (c + 1) * ATTN_Q_CHUNK)
            m_prev = m_ref[hd, :, cols]
            m_new = jnp.maximum(m_prev, mx_ref[hd, :, cols])
            alpha = jnp.exp2(m_prev - m_new)
            p = jnp.exp2(s_ref[hd, :, cols] - m_new)
            l_ref[hd, :, cols] = alpha * l_ref[hd, :, cols] + jnp.sum(p, axis=0, keepdims=True)
            pv = _dot(vt_ref[hd * V_DIM:(hd + 1) * V_DIM, :], p.astype(BF16))
            acc_ref[hd, :, cols] = alpha * acc_ref[hd, :, cols] + pv
            m_ref[hd, :, cols] = m_new

        for hd in range(N_HEADS):
            for c in range(nchunk):
                softmax_pv(hd, c)
                scores(hd, c)

    is_diag = diag_n_ref[step] == 1

    @pl.when(jnp.logical_not(is_diag))
    def _():
        body(False)

    @pl.when(is_diag)
    def _():
        body(True)

    @pl.when(last_c_ref[step] == 1)
    def _():
        for hd in range(N_HEADS):
            o_ref[:, hd * V_DIM:(hd + 1) * V_DIM] = (acc_ref[hd] / l_ref[hd]).T.astype(o_ref.dtype)


def _out_kernel(o_ref, ga_ref, yc_ref, x_ref, p_ref, g_oa_ref, w_oa_ref, w_oc_ref,
                g_pl_ref, w_plg_ref, w_pl_ref, out_ref):
    ya = o_ref[...].astype(F32) * ga_ref[...].astype(F32)
    ya_n = (ya * _rms_scale(ya, ATTN_WIDTH) * g_oa_ref[...]).astype(BF16)
    x1 = x_ref[...] + _dot(ya_n, w_oa_ref[...]) + _dot(yc_ref[...], w_oc_ref[...])
    hn = (x1 * _rms_scale(x1, D_MODEL) * g_pl_ref[...]).astype(BF16)
    gate = 1.0 / (1.0 + jnp.exp(-_dot(hn, w_plg_ref[...])))
    out_ref[...] = x1 + gate * _dot(p_ref[...].astype(BF16), w_pl_ref[...])


def _row_spec(tile, width):
    return pl.BlockSpec((tile, width), lambda i: (i, 0))


def _const_spec(shape):
    return pl.BlockSpec(shape, lambda i: (0,) * len(shape))


def _triangle_schedule(batch, n_blk):
    pairs = [(b * n_blk + i, b * n_blk + j, j == 0, j == i)
             for b in range(batch) for i in range(n_blk) for j in range(i + 1)]
    nxt = pairs + [pairs[-1]]
    cur = [pairs[0]] + pairs
    cols = ([p[0] for p in nxt], [p[1] for p in nxt], [p[1] for p in cur], [p[0] for p in cur],
            [int(p[3]) for p in nxt],
            [0] + [int(p[2]) for p in pairs], [0] + [int(p[3]) for p in pairs])
    return tuple(jnp.asarray(np.asarray(c, np.int32)) for c in cols)


def _layer(x2, p2, posc, invf, batch, seq, g_in, w_in, g_cq, w_uq, g_ckv, w_ukv, g_q, g_k,
           conv_w, g_oa, g_oc, w_o, w_pl, w_plg, g_pl):
    tokens = batch * seq
    tm = PROJ_TILE
    assert seq % tm == 0 and seq % ATTN_TILE == 0

    w_in_p = jnp.concatenate(
        [w_in[:, :OFF_KPE + ROPE_DIM], jnp.zeros((D_MODEL, KPE_PAD - ROPE_DIM), F32),
         w_in[:, OFF_KPE + ROPE_DIM:]], axis=1).astype(BF16)
    w_uq_p = jnp.pad(w_uq.reshape(Q_LORA, N_HEADS, QK_DIM),
                     ((0, 0), (0, 0), (0, HEAD_PAD - QK_DIM))).reshape(Q_LORA, N_HEADS * HEAD_PAD).astype(BF16)
    g_q_p = jnp.pad(g_q, (0, HEAD_PAD - QK_DIM)).reshape(1, HEAD_PAD)
    g_kn = g_k[:NOPE_DIM].reshape(1, NOPE_DIM)
    g_kr = jnp.pad(g_k[NOPE_DIM:], (0, KPE_PAD - ROPE_DIM)).reshape(1, KPE_PAD)

    cparams = pltpu.CompilerParams(dimension_semantics=("arbitrary",), vmem_limit_bytes=VMEM_LIMIT)

    def col_spec(rows, tile):
        return pl.BlockSpec((rows, tile), lambda i: (0, i))

    qt, k, vt, ga, yc = pl.pallas_call(
        functools.partial(_proj_kernel, seq // tm),
        grid=(tokens // tm,),
        in_specs=[
            _row_spec(tm, D_MODEL), _row_spec(tm // TOKENS_PER_ROW, LANES), _const_spec((1, LANES)),
            _const_spec((1, D_MODEL)), _const_spec((D_MODEL, IN_PAD)),
            _const_spec((1, Q_LORA)), _const_spec((Q_LORA, N_HEADS * HEAD_PAD)),
            _const_spec((1, KV_LORA)), _const_spec((KV_LORA, N_HEADS * (NOPE_DIM + V_DIM))),
            _const_spec((1, HEAD_PAD)), _const_spec((1, NOPE_DIM)), _const_spec((1, KPE_PAD)),
            _const_spec((CONV_K, CONV_WIDTH)), _const_spec((1, CONV_WIDTH)),
        ],
        out_specs=[
            col_spec(N_HEADS * HEAD_PAD, tm), _row_spec(tm, N_HEADS * HEAD_PAD),
            col_spec(ATTN_WIDTH, tm), _row_spec(tm, ATTN_WIDTH), _row_spec(tm, CONV_WIDTH),
        ],
        out_shape=[
            jax.ShapeDtypeStruct((N_HEADS * HEAD_PAD, tokens), BF16),
            jax.ShapeDtypeStruct((tokens, N_HEADS * HEAD_PAD), BF16),
            jax.ShapeDtypeStruct((ATTN_WIDTH, tokens), BF16),
            jax.ShapeDtypeStruct((tokens, ATTN_WIDTH), BF16),
            jax.ShapeDtypeStruct((tokens, CONV_WIDTH), BF16),
        ],
        scratch_shapes=[pltpu.VMEM((tm + SUBLANES, CONV_WIDTH), F32)],
        compiler_params=cparams,
        name="mla_conv_proj",
    )(x2, posc, invf, g_in.reshape(1, -1), w_in_p, g_cq.reshape(1, -1), w_uq_p,
      g_ckv.reshape(1, -1), w_ukv.astype(BF16), g_q_p, g_kn, g_kr, conv_w, g_oc.reshape(1, -1))

    ta = ATTN_TILE
    sched = _triangle_schedule(batch, seq // ta)
    o = pl.pallas_call(
        _attn_kernel,
        grid_spec=pltpu.PrefetchScalarGridSpec(
            num_scalar_prefetch=len(sched),
            grid=(int(sched[0].shape[0]),),
            in_specs=[
                pl.BlockSpec((N_HEADS * HEAD_PAD, ta), lambda s, qn, kn, kc, qc, *_: (0, qn[s])),
                pl.BlockSpec((ta, N_HEADS * HEAD_PAD), lambda s, qn, kn, kc, qc, *_: (kn[s], 0)),
                pl.BlockSpec((ATTN_WIDTH, ta), lambda s, qn, kn, kc, qc, *_: (0, kc[s])),
            ],
            out_specs=pl.BlockSpec((ta, ATTN_WIDTH), lambda s, qn, kn, kc, qc, *_: (qc[s], 0)),
            scratch_shapes=[pltpu.VMEM((N_HEADS, ta, ta), F32),
                            pltpu.VMEM((N_HEADS, 1, ta), F32),
                            pltpu.VMEM((N_HEADS, 1, ta), F32),
                            pltpu.VMEM((N_HEADS, 1, ta), F32),
                            pltpu.VMEM((N_HEADS, V_DIM, ta), F32)],
        ),
        out_shape=jax.ShapeDtypeStruct((tokens, ATTN_WIDTH), BF16),
        compiler_params=cparams,
        name="mla_flash_attn",
    )(*sched, qt, k, vt)

    w_o_b = w_o.astype(BF16)
    return pl.pallas_call(
        _out_kernel,
        grid=(tokens // tm,),
        in_specs=[
            _row_spec(tm, ATTN_WIDTH), _row_spec(tm, ATTN_WIDTH), _row_spec(tm, CONV_WIDTH),
            _row_spec(tm, D_MODEL), _row_spec(tm, PLE_DIM),
            _const_spec((1, ATTN_WIDTH)), _const_spec((ATTN_WIDTH, D_MODEL)),
            _const_spec((CONV_WIDTH, D_MODEL)), _const_spec((1, D_MODEL)),
            _const_spec((D_MODEL, D_MODEL)), _const_spec((PLE_DIM, D_MODEL)),
        ],
        out_specs=_row_spec(tm, D_MODEL),
        out_shape=jax.ShapeDtypeStruct((tokens, D_MODEL), F32),
        compiler_params=cparams,
        name="mla_conv_out",
    )(o, ga, yc, x2, p2, g_oa.reshape(1, -1), w_o_b[:ATTN_WIDTH], w_o_b[ATTN_WIDTH:],
      g_pl.reshape(1, -1), w_plg.astype(BF16), w_pl.astype(BF16))


def kernel(x, p, positions, g_in, w_in, g_cq, w_uq, g_ckv, w_ukv, g_q, g_k, conv_w, g_oa, g_oc,
           w_o, w_pl, w_plg, g_pl):
    batch, seq, d_model = x.shape
    depth = p.shape[0]
    tokens = batch * seq
    rows = PROJ_TILE // TOKENS_PER_ROW
    posc = jnp.repeat(positions.astype(F32).reshape(tokens // PROJ_TILE, TOKENS_PER_ROW, rows)
                      .transpose(0, 2, 1), HALF_ROPE, axis=-1).reshape(tokens // TOKENS_PER_ROW, LANES)
    inv_freq = 1.0 / (ROPE_THETA ** (jnp.arange(0, ROPE_DIM, 2, dtype=F32) / ROPE_DIM))
    invf = jnp.tile(inv_freq, LANES // HALF_ROPE).reshape(1, LANES)
    h = x.reshape(tokens, d_model)
    for i in range(depth):
        h = _layer(h, p[i].reshape(tokens, PLE_DIM), posc, invf, batch, seq,
                   g_in[i], w_in[i], g_cq[i], w_uq[i], g_ckv[i], w_ukv[i], g_q[i], g_k[i],
                   conv_w[i], g_oa[i], g_oc[i], w_o[i], w_pl[i], w_plg[i], g_pl[i])
    return h.reshape(batch, seq, d_model).astype(x.dtype)
```

```python
import functools
import math

import jax
import jax.numpy as jnp
import numpy as np
from jax import lax
from jax.experimental import pallas as pl
from jax.experimental.pallas import tpu as pltpu

D_MODEL = 1024
PLE_DIM = 256
N_HEADS = 4
NOPE_DIM = 128
ROPE_DIM = 64
HALF_ROPE = ROPE_DIM // 2
V_DIM = 128
QK_DIM = NOPE_DIM + ROPE_DIM
Q_LORA = 256
KV_LORA = 128
ATTN_WIDTH = N_HEADS * V_DIM
CONV_WIDTH = D_MODEL - ATTN_WIDTH
CONV_K = 3
ROPE_THETA = 10000.0
RMS_EPS = 1e-6
NEG_INF = -1e30

LANES = 128
SUBLANES = 8
HEAD_PAD = 2 * LANES
KPE_PAD = LANES
TOKENS_PER_ROW = LANES // HALF_ROPE
OFF_CQ = 0
OFF_CKV = OFF_CQ + Q_LORA
OFF_KPE = OFF_CKV + KV_LORA
LAT_SRC = OFF_KPE + ROPE_DIM
LAT_PAD = OFF_KPE + KPE_PAD
OFF_ZA = 0
OFF_CB = OFF_ZA + ATTN_WIDTH
OFF_CC = OFF_CB + CONV_WIDTH
OFF_CX = OFF_CC + CONV_WIDTH
OFF_ZC = OFF_CX + CONV_WIDTH
WIDE = OFF_ZC + CONV_WIDTH

PROJ_TILE = 1024
PROJ_ROWS = 256
OUT_ROWS = 256
ATTN_TILE = 1024
ATTN_Q_CHUNK = 512
VMEM_LIMIT = 48 * 1024 * 1024

BF16 = jnp.bfloat16
F32 = jnp.float32


def _rms_scale(v, width):
    return lax.rsqrt(jnp.sum(v * v, axis=-1, keepdims=True) * (1.0 / width) + RMS_EPS)


def _silu(z):
    return z / (1.0 + jnp.exp(-z))


def _dot(a, b):
    return jnp.dot(a, b, preferred_element_type=F32)


def _rope_tables(posc, invf):
    ang = posc * invf
    cosc = jnp.cos(ang)
    sinc = jnp.sin(ang)
    lane = lax.broadcasted_iota(jnp.int32, (1, LANES), 1)
    lo = lane < HALF_ROPE
    mid = (lane >= HALF_ROPE) & (lane < ROPE_DIM)
    quarters = []
    for a in range(TOKENS_PER_ROW):
        c = pltpu.roll(cosc, LANES - HALF_ROPE * a, 1) if a else cosc
        s = pltpu.roll(sinc, LANES - HALF_ROPE * a, 1) if a else sinc
        quarters.append((jnp.where(lo, c, jnp.where(mid, pltpu.roll(c, HALF_ROPE, 1), 0.0)),
                         jnp.where(lo, -s, jnp.where(mid, pltpu.roll(s, HALF_ROPE, 1), 0.0))))
    return quarters, lo


def _rope(t, c, s, lo):
    swapped = jnp.where(lo, pltpu.roll(t, LANES - HALF_ROPE, 1), pltpu.roll(t, HALF_ROPE, 1))
    return t * c + swapped * s


def _proj_kernel(tiles_per_seq,
                 x_ref, posc_ref, invf_ref, g_in_ref, w_lat_ref, w_in_ref, g_cq_ref, w_uq_ref,
                 g_ckv_ref, w_ukv_ref, g_q_ref, g_kn_ref, g_kr_ref, conv_w_ref, g_oc_ref,
                 qt_ref, k_ref, vt_ref, ga_ref, yc_ref, carry_ref):
    tm = x_ref.shape[0]
    quarter_rows = tm // TOKENS_PER_ROW
    tables = []
    q_scale = math.log2(math.e) / math.sqrt(QK_DIM)

    @pl.when(pl.program_id(0) % tiles_per_seq == 0)
    def _():
        carry_ref[0:SUBLANES, :] = jnp.zeros((SUBLANES, CONV_WIDTH), F32)

    def rows_block(r0, nr):
        x = x_ref[r0:r0 + nr, :]
        h = (x * _rms_scale(x, D_MODEL) * g_in_ref[...]).astype(BF16)

        def proj(off, width):
            return _dot(h, w_in_ref[:, off:off + width])

        lat = _dot(h, w_lat_ref[...])
        c_q = lat[:, OFF_CQ:OFF_CKV]
        c_kv = lat[:, OFF_CKV:OFF_KPE]
        kpe = lat[:, OFF_KPE:LAT_PAD]
        cv = proj(OFF_CC, CONV_WIDTH) * proj(OFF_CX, CONV_WIDTH)
        cqn = (c_q * _rms_scale(c_q, Q_LORA) * g_cq_ref[...]).astype(BF16)
        qf = _dot(cqn, w_uq_ref[...])
        cb = proj(OFF_CB, CONV_WIDTH)
        ckvn = (c_kv * _rms_scale(c_kv, KV_LORA) * g_ckv_ref[...]).astype(BF16)
        kv = _dot(ckvn, w_ukv_ref[...])
        z_c = proj(OFF_ZC, CONV_WIDTH)
        z_a = proj(OFF_ZA, ATTN_WIDTH)

        if not tables:
            tables.extend(_rope_tables(posc_ref[...], invf_ref[...]))
        quarters, lo = tables
        qa = r0 // quarter_rows
        cos_t = jnp.concatenate([quarters[qa + i][0] for i in range(nr // quarter_rows)], axis=0)
        sin_t = jnp.concatenate([quarters[qa + i][1] for i in range(nr // quarter_rows)], axis=0)

        for hd in range(N_HEADS):
            qh = qf[:, hd * HEAD_PAD:(hd + 1) * HEAD_PAD]
            qn = qh * (_rms_scale(qh, QK_DIM) * q_scale) * g_q_ref[...]
            qt_ref[hd * HEAD_PAD:hd * HEAD_PAD + LANES, r0:r0 + nr] = qn[:, :LANES].T.astype(BF16)
            qt_ref[hd * HEAD_PAD + LANES:(hd + 1) * HEAD_PAD, r0:r0 + nr] = (
                _rope(qn[:, LANES:], cos_t, sin_t, lo).T.astype(BF16))

        ss_kpe = jnp.sum(kpe * kpe, axis=-1, keepdims=True)
        k_rot = _rope(kpe * g_kr_ref[...], cos_t, sin_t, lo)
        for hd in range(N_HEADS):
            kn = kv[:, hd * (NOPE_DIM + V_DIM):hd * (NOPE_DIM + V_DIM) + NOPE_DIM]
            vh = kv[:, hd * (NOPE_DIM + V_DIM) + NOPE_DIM:(hd + 1) * (NOPE_DIM + V_DIM)]
            ss = jnp.sum(kn * kn, axis=-1, keepdims=True) + ss_kpe
            rs = lax.rsqrt(ss * (1.0 / QK_DIM) + RMS_EPS)
            k_ref[r0:r0 + nr, hd * HEAD_PAD:hd * HEAD_PAD + LANES] = (kn * rs * g_kn_ref[...]).astype(BF16)
            k_ref[r0:r0 + nr, hd * HEAD_PAD + LANES:(hd + 1) * HEAD_PAD] = (k_rot * rs).astype(BF16)
            vt_ref[hd * V_DIM:(hd + 1) * V_DIM, r0:r0 + nr] = vh.T.astype(BF16)

        base = SUBLANES + r0
        carry_ref[base:base + nr, :] = cv
        u = (conv_w_ref[2:3, :] * cv
             + conv_w_ref[1:2, :] * carry_ref[base - 1:base - 1 + nr, :]
             + conv_w_ref[0:1, :] * carry_ref[base - 2:base - 2 + nr, :])
        yc = cb * u * _silu(z_c)
        yc_ref[r0:r0 + nr, :] = (yc * _rms_scale(yc, CONV_WIDTH) * g_oc_ref[...]).astype(BF16)

        ga_ref[r0:r0 + nr, :] = _silu(z_a).astype(BF16)

    for blk in range(tm // PROJ_ROWS):
        rows_block(blk * PROJ_ROWS, PROJ_ROWS)
    carry_ref[0:SUBLANES, :] = carry_ref[tm:tm + SUBLANES, :]


def _attn_kernel(qn_ref, kn_ref, kc_ref, qc_ref, diag_n_ref, first_c_ref, last_c_ref,
                 qt_ref, k_ref, vt_ref, o_ref, s_ref, mx_ref, m_ref, l_ref, acc_ref):
    step = pl.program_id(0)
    tq = qt_ref.shape[1]
    tk = k_ref.shape[0]

    @pl.when(step == 0)
    def _():
        s_ref[...] = jnp.zeros(s_ref.shape, F32)
        mx_ref[...] = jnp.zeros(mx_ref.shape, F32)

    @pl.when((first_c_ref[step] == 1) | (step == 0))
    def _():
        m_ref[...] = jnp.full(m_ref.shape, NEG_INF, F32)
        l_ref[...] = jnp.zeros(l_ref.shape, F32)
        acc_ref[...] = jnp.zeros(acc_ref.shape, F32)

    nchunk = tq // ATTN_Q_CHUNK
    half = tk // 2

    def body(masked):
        if masked:
            assert nchunk == 2 and half == ATTN_Q_CHUNK
            tri = (lax.broadcasted_iota(jnp.int32, (half, ATTN_Q_CHUNK), 0)
                   <= lax.broadcasted_iota(jnp.int32, (half, ATTN_Q_CHUNK), 1))

        def scores(hd, c):
            cols = slice(c * ATTN_Q_CHUNK, (c + 1) * ATTN_Q_CHUNK)
            qt = qt_ref[hd * HEAD_PAD:(hd + 1) * HEAD_PAD, cols]
            if not masked:
                s = _dot(k_ref[:, hd * HEAD_PAD:(hd + 1) * HEAD_PAD], qt)
                s_ref[hd, :, cols] = s
                mx_ref[hd, :, cols] = jnp.max(s, axis=0, keepdims=True)
            elif c == 0:
                s = jnp.where(tri, _dot(k_ref[0:half, hd * HEAD_PAD:(hd + 1) * HEAD_PAD], qt), NEG_INF)
                s_ref[hd, 0:half, cols] = s
                s_ref[hd, half:tk, cols] = jnp.full((half, ATTN_Q_CHUNK), NEG_INF, F32)
                mx_ref[hd, :, cols] = jnp.max(s, axis=0, keepdims=True)
            else:
                s = _dot(k_ref[:, hd * HEAD_PAD:(hd + 1) * HEAD_PAD], qt)
                top = s[0:half]
                bot = jnp.where(tri, s[half:tk], NEG_INF)
                s_ref[hd, 0:half, cols] = top
                s_ref[hd, half:tk, cols] = bot
                mx_ref[hd, :, cols] = jnp.maximum(jnp.max(top, axis=0, keepdims=True),
                                                  jnp.max(bot, axis=0, keepdims=True))

        def softmax_pv(hd, c):
            cols = slice(c * ATTN_Q_CHUNK, (c + 1) * ATTN_Q_CHUNK)
            m_prev = m_ref[hd, :, cols]
            m_new = jnp.maximum(m_prev, mx_ref[hd, :, cols])
            alpha = jnp.exp2(m_prev - m_new)
            p = jnp.exp2(s_ref[hd, :, cols] - m_new)
            l_ref[hd, :, cols] = alpha * l_ref[hd, :, cols] + jnp.sum(p, axis=0, keepdims=True)
            pv = _dot(vt_ref[hd * V_DIM:(hd + 1) * V_DIM, :], p.astype(BF16))
            acc_ref[hd, :, cols] = alpha * acc_ref[hd, :, cols] + pv
            m_ref[hd, :, cols] = m_new

        for hd in range(N_HEADS):
            for c in range(nchunk):
                softmax_pv(hd, c)
                scores(hd, c)

    is_diag = diag_n_ref[step] == 1

    @pl.when(jnp.logical_not(is_diag))
    def _():
        body(False)

    @pl.when(is_diag)
    def _():
        body(True)

    @pl.when(last_c_ref[step] == 1)
    def _():
        for hd in range(N_HEADS):
            o_ref[:, hd * V_DIM:(hd + 1) * V_DIM] = (acc_ref[hd] / l_ref[hd]).T.astype(o_ref.dtype)


def _out_kernel(o_ref, ga_ref, yc_ref, x_ref, p_ref, g_oa_ref, w_o_ref, g_pl_ref, w_plg_ref, w_pl_ref,
                out_ref):
    tm = x_ref.shape[0]

    def residual(r0):
        rows = slice(r0, r0 + OUT_ROWS)
        ya = o_ref[rows, :].astype(F32) * ga_ref[rows, :].astype(F32)
        ya_n = (ya * _rms_scale(ya, ATTN_WIDTH) * g_oa_ref[...]).astype(BF16)
        x1 = (x_ref[rows, :] + _dot(ya_n, w_o_ref[:ATTN_WIDTH, :])
              + _dot(yc_ref[rows, :], w_o_ref[ATTN_WIDTH:, :]))
        ple = _dot(p_ref[rows, :].astype(BF16), w_pl_ref[...])
        hn = (x1 * _rms_scale(x1, D_MODEL) * g_pl_ref[...]).astype(BF16)
        return r0, x1, ple, hn

    def gated(r0, x1, ple, hn):
        gate = 1.0 / (1.0 + jnp.exp(-_dot(hn, w_plg_ref[...])))
        out_ref[r0:r0 + OUT_ROWS, :] = x1 + gate * ple

    prev = None
    for r0 in range(0, tm, OUT_ROWS):
        cur = residual(r0)
        if prev is not None:
            gated(*prev)
        prev = cur
    gated(*prev)


def _row_spec(tile, width):
    return pl.BlockSpec((tile, width), lambda i: (i, 0))


def _const_spec(shape):
    return pl.BlockSpec(shape, lambda i: (0,) * len(shape))


def _triangle_schedule(batch, n_blk):
    pairs = [(b * n_blk + i, b * n_blk + j, j == 0, j == i)
             for b in range(batch) for i in range(n_blk) for j in range(i + 1)]
    nxt = pairs + [pairs[-1]]
    cur = [pairs[0]] + pairs
    cols = ([p[0] for p in nxt], [p[1] for p in nxt], [p[1] for p in cur], [p[0] for p in cur],
            [int(p[3]) for p in nxt],
            [0] + [int(p[2]) for p in pairs], [0] + [int(p[3]) for p in pairs])
    return tuple(jnp.asarray(np.asarray(c, np.int32)) for c in cols)


def _layer(x2, p2, posc, invf, batch, seq, g_in, w_in, g_cq, w_uq, g_ckv, w_ukv, g_q, g_k,
           conv_w, g_oa, g_oc, w_o, w_pl, w_plg, g_pl):
    tokens = batch * seq
    tm = PROJ_TILE
    assert seq % tm == 0 and seq % ATTN_TILE == 0

    w_lat = jnp.pad(w_in[:, :LAT_SRC].astype(BF16), ((0, 0), (0, LAT_PAD - LAT_SRC)))
    w_wide = w_in[:, LAT_SRC:].astype(BF16)
    w_uq_p = jnp.pad(w_uq.reshape(Q_LORA, N_HEADS, QK_DIM),
                     ((0, 0), (0, 0), (0, HEAD_PAD - QK_DIM))).reshape(Q_LORA, N_HEADS * HEAD_PAD).astype(BF16)
    g_q_p = jnp.pad(g_q, (0, HEAD_PAD - QK_DIM)).reshape(1, HEAD_PAD)
    g_kn = g_k[:NOPE_DIM].reshape(1, NOPE_DIM)
    g_kr = jnp.pad(g_k[NOPE_DIM:], (0, KPE_PAD - ROPE_DIM)).reshape(1, KPE_PAD)

    cparams = pltpu.CompilerParams(dimension_semantics=("arbitrary",), vmem_limit_bytes=VMEM_LIMIT)

    def col_spec(rows, tile):
        return pl.BlockSpec((rows, tile), lambda i: (0, i))

    qt, k, vt, ga, yc = pl.pallas_call(
        functools.partial(_proj_kernel, seq // tm),
        grid=(tokens // tm,),
        in_specs=[
            _row_spec(tm, D_MODEL), _row_spec(tm // TOKENS_PER_ROW, LANES), _const_spec((1, LANES)),
            _const_spec((1, D_MODEL)), _const_spec((D_MODEL, LAT_PAD)), _const_spec((D_MODEL, WIDE)),
            _const_spec((1, Q_LORA)), _const_spec((Q_LORA, N_HEADS * HEAD_PAD)),
            _const_spec((1, KV_LORA)), _const_spec((KV_LORA, N_HEADS * (NOPE_DIM + V_DIM))),
            _const_spec((1, HEAD_PAD)), _const_spec((1, NOPE_DIM)), _const_spec((1, KPE_PAD)),
            _const_spec((CONV_K, CONV_WIDTH)), _const_spec((1, CONV_WIDTH)),
        ],
        out_specs=[
            col_spec(N_HEADS * HEAD_PAD, tm), _row_spec(tm, N_HEADS * HEAD_PAD),
            col_spec(ATTN_WIDTH, tm), _row_spec(tm, ATTN_WIDTH), _row_spec(tm, CONV_WIDTH),
        ],
        out_shape=[
            jax.ShapeDtypeStruct((N_HEADS * HEAD_PAD, tokens), BF16),
            jax.ShapeDtypeStruct((tokens, N_HEADS * HEAD_PAD), BF16),
            jax.ShapeDtypeStruct((ATTN_WIDTH, tokens), BF16),
            jax.ShapeDtypeStruct((tokens, ATTN_WIDTH), BF16),
            jax.ShapeDtypeStruct((tokens, CONV_WIDTH), BF16),
        ],
        scratch_shapes=[pltpu.VMEM((tm + SUBLANES, CONV_WIDTH), F32)],
        compiler_params=cparams,
        name="mla_conv_proj",
    )(x2, posc, invf, g_in.reshape(1, -1), w_lat, w_wide, g_cq.reshape(1, -1), w_uq_p,
      g_ckv.reshape(1, -1), w_ukv.astype(BF16), g_q_p, g_kn, g_kr, conv_w, g_oc.reshape(1, -1))

    ta = ATTN_TILE
    sched = _triangle_schedule(batch, seq // ta)
    o = pl.pallas_call(
        _attn_kernel,
        grid_spec=pltpu.PrefetchScalarGridSpec(
            num_scalar_prefetch=len(sched),
            grid=(int(sched[0].shape[0]),),
            in_specs=[
                pl.BlockSpec((N_HEADS * HEAD_PAD, ta), lambda s, qn, kn, kc, qc, *_: (0, qn[s])),
                pl.BlockSpec((ta, N_HEADS * HEAD_PAD), lambda s, qn, kn, kc, qc, *_: (kn[s], 0)),
                pl.BlockSpec((ATTN_WIDTH, ta), lambda s, qn, kn, kc, qc, *_: (0, kc[s])),
            ],
            out_specs=pl.BlockSpec((ta, ATTN_WIDTH), lambda s, qn, kn, kc, qc, *_: (qc[s], 0)),
            scratch_shapes=[pltpu.VMEM((N_HEADS, ta, ta), F32),
                            pltpu.VMEM((N_HEADS, 1, ta), F32),
                            pltpu.VMEM((N_HEADS, 1, ta), F32),
                            pltpu.VMEM((N_HEADS, 1, ta), F32),
                            pltpu.VMEM((N_HEADS, V_DIM, ta), F32)],
        ),
        out_shape=jax.ShapeDtypeStruct((tokens, ATTN_WIDTH), BF16),
        compiler_params=cparams,
        name="mla_flash_attn",
    )(*sched, qt, k, vt)

    return pl.pallas_call(
        _out_kernel,
        grid=(tokens // tm,),
        in_specs=[
            _row_spec(tm, ATTN_WIDTH), _row_spec(tm, ATTN_WIDTH), _row_spec(tm, CONV_WIDTH),
            _row_spec(tm, D_MODEL), _row_spec(tm, PLE_DIM),
            _const_spec((1, ATTN_WIDTH)), _const_spec((D_MODEL, D_MODEL)), _const_spec((1, D_MODEL)),
            _const_spec((D_MODEL, D_MODEL)), _const_spec((PLE_DIM, D_MODEL)),
        ],
        out_specs=_row_spec(tm, D_MODEL),
        out_shape=jax.ShapeDtypeStruct((tokens, D_MODEL), F32),
        compiler_params=cparams,
        name="mla_conv_out",
    )(o, ga, yc, x2, p2, g_oa.reshape(1, -1), w_o.astype(BF16),
      g_pl.reshape(1, -1), w_plg.astype(BF16), w_pl.astype(BF16))


def kernel(x, p, positions, g_in, w_in, g_cq, w_uq, g_ckv, w_ukv, g_q, g_k, conv_w, g_oa, g_oc,
           w_o, w_pl, w_plg, g_pl):
    batch, seq, d_model = x.shape
    depth = p.shape[0]
    tokens = batch * seq
    rows = PROJ_TILE // TOKENS_PER_ROW
    posc = jnp.repeat(positions.astype(F32).reshape(tokens // PROJ_TILE, TOKENS_PER_ROW, rows)
                      .transpose(0, 2, 1), HALF_ROPE, axis=-1).reshape(tokens // TOKENS_PER_ROW, LANES)
    inv_freq = 1.0 / (ROPE_THETA ** (jnp.arange(0, ROPE_DIM, 2, dtype=F32) / ROPE_DIM))
    invf = jnp.tile(inv_freq, LANES // HALF_ROPE).reshape(1, LANES)
    h = x.reshape(tokens, d_model)
    for i in range(depth):
        h = _layer(h, p[i].reshape(tokens, PLE_DIM), posc, invf, batch, seq,
                   g_in[i], w_in[i], g_cq[i], w_uq[i], g_ckv[i], w_ukv[i], g_q[i], g_k[i],
                   conv_w[i], g_oa[i], g_oc[i], w_o[i], w_pl[i], w_plg[i], g_pl[i])
    return h.reshape(batch, seq, d_model).astype(x.dtype)
```

```python
import functools
import math

import jax
import jax.numpy as jnp
import numpy as np
from jax import lax
from jax.experimental import pallas as pl
from jax.experimental.pallas import tpu as pltpu

D_MODEL = 1024
PLE_DIM = 256
N_HEADS = 4
NOPE_DIM = 128
ROPE_DIM = 64
HALF_ROPE = ROPE_DIM // 2
V_DIM = 128
QK_DIM = NOPE_DIM + ROPE_DIM
Q_LORA = 256
KV_LORA = 128
ATTN_WIDTH = N_HEADS * V_DIM
CONV_WIDTH = D_MODEL - ATTN_WIDTH
CONV_K = 3
ROPE_THETA = 10000.0
RMS_EPS = 1e-6
NEG_INF = -1e30

LANES = 128
SUBLANES = 8
HEAD_PAD = 2 * LANES
KPE_PAD = LANES
BF16_ROWS = 16
V_ROWS = V_DIM + BF16_ROWS
TOKENS_PER_ROW = LANES // HALF_ROPE
OFF_CQ = 0
OFF_CKV = OFF_CQ + Q_LORA
OFF_KPE = OFF_CKV + KV_LORA
LAT_SRC = OFF_KPE + ROPE_DIM
LAT_PAD = OFF_KPE + KPE_PAD
OFF_ZA = 0
OFF_CB = OFF_ZA + ATTN_WIDTH
OFF_CC = OFF_CB + CONV_WIDTH
OFF_CX = OFF_CC + CONV_WIDTH
OFF_ZC = OFF_CX + CONV_WIDTH
WIDE = OFF_ZC + CONV_WIDTH

PROJ_TILE = 1024
PROJ_ROWS = 256
OUT_ROWS = 256
ATTN_TILE = 1024
ATTN_Q_CHUNK = 512
VMEM_LIMIT = 48 * 1024 * 1024

BF16 = jnp.bfloat16
F32 = jnp.float32


def _rms_scale(v, width):
    return lax.rsqrt(jnp.sum(v * v, axis=-1, keepdims=True) * (1.0 / width) + RMS_EPS)


def _silu(z):
    return z / (1.0 + jnp.exp(-z))


def _dot(a, b):
    return jnp.dot(a, b, preferred_element_type=F32)


def _rope_tables(posc, invf):
    ang = posc * invf
    cosc = jnp.cos(ang)
    sinc = jnp.sin(ang)
    lane = lax.broadcasted_iota(jnp.int32, (1, LANES), 1)
    lo = lane < HALF_ROPE
    mid = (lane >= HALF_ROPE) & (lane < ROPE_DIM)
    quarters = []
    for a in range(TOKENS_PER_ROW):
        c = pltpu.roll(cosc, LANES - HALF_ROPE * a, 1) if a else cosc
        s = pltpu.roll(sinc, LANES - HALF_ROPE * a, 1) if a else sinc
        quarters.append((jnp.where(lo, c, jnp.where(mid, pltpu.roll(c, HALF_ROPE, 1), 0.0)),
                         jnp.where(lo, -s, jnp.where(mid, pltpu.roll(s, HALF_ROPE, 1), 0.0))))
    return quarters, lo


def _rope(t, c, s, lo):
    swapped = jnp.where(lo, pltpu.roll(t, LANES - HALF_ROPE, 1), pltpu.roll(t, HALF_ROPE, 1))
    return t * c + swapped * s


def _proj_kernel(tiles_per_seq,
                 x_ref, posc_ref, invf_ref, g_in_ref, w_lat_ref, w_in_ref, g_cq_ref, w_uq_ref,
                 g_ckv_ref, w_ukv_ref, g_q_ref, g_kn_ref, g_kr_ref, conv_w_ref, g_oc_ref,
                 qt_ref, k_ref, vt_ref, ga_ref, yc_ref, carry_ref):
    tm = x_ref.shape[0]
    quarter_rows = tm // TOKENS_PER_ROW
    tables = []
    q_scale = math.log2(math.e) / math.sqrt(QK_DIM)

    @pl.when(pl.program_id(0) % tiles_per_seq == 0)
    def _():
        carry_ref[0:SUBLANES, :] = jnp.zeros((SUBLANES, CONV_WIDTH), F32)

    def rows_block(r0, nr):
        x = x_ref[r0:r0 + nr, :]
        h = (x * _rms_scale(x, D_MODEL) * g_in_ref[...]).astype(BF16)

        def proj(off, width):
            return _dot(h, w_in_ref[:, off:off + width])

        lat = _dot(h, w_lat_ref[...])
        c_q = lat[:, OFF_CQ:OFF_CKV]
        c_kv = lat[:, OFF_CKV:OFF_KPE]
        kpe = lat[:, OFF_KPE:LAT_PAD]
        cv = proj(OFF_CC, CONV_WIDTH) * proj(OFF_CX, CONV_WIDTH)
        cqn = (c_q * _rms_scale(c_q, Q_LORA) * g_cq_ref[...]).astype(BF16)
        qf = _dot(cqn, w_uq_ref[...])
        cb = proj(OFF_CB, CONV_WIDTH)
        ckvn = (c_kv * _rms_scale(c_kv, KV_LORA) * g_ckv_ref[...]).astype(BF16)
        kv = _dot(ckvn, w_ukv_ref[...])
        z_c = proj(OFF_ZC, CONV_WIDTH)
        z_a = proj(OFF_ZA, ATTN_WIDTH)

        if not tables:
            tables.extend(_rope_tables(posc_ref[...], invf_ref[...]))
        quarters, lo = tables
        qa = r0 // quarter_rows
        cos_t = jnp.concatenate([quarters[qa + i][0] for i in range(nr // quarter_rows)], axis=0)
        sin_t = jnp.concatenate([quarters[qa + i][1] for i in range(nr // quarter_rows)], axis=0)

        for hd in range(N_HEADS):
            qh = qf[:, hd * HEAD_PAD:(hd + 1) * HEAD_PAD]
            qn = qh * (_rms_scale(qh, QK_DIM) * q_scale) * g_q_ref[...]
            qt_ref[hd * HEAD_PAD:hd * HEAD_PAD + LANES, r0:r0 + nr] = qn[:, :LANES].T.astype(BF16)
            qt_ref[hd * HEAD_PAD + LANES:(hd + 1) * HEAD_PAD, r0:r0 + nr] = (
                _rope(qn[:, LANES:], cos_t, sin_t, lo).T.astype(BF16))

        ss_kpe = jnp.sum(kpe * kpe, axis=-1, keepdims=True)
        k_rot = _rope(kpe * g_kr_ref[...], cos_t, sin_t, lo)
        for hd in range(N_HEADS):
            kn = kv[:, hd * (NOPE_DIM + V_DIM):hd * (NOPE_DIM + V_DIM) + NOPE_DIM]
            vh = kv[:, hd * (NOPE_DIM + V_DIM) + NOPE_DIM:(hd + 1) * (NOPE_DIM + V_DIM)]
            ss = jnp.sum(kn * kn, axis=-1, keepdims=True) + ss_kpe
            rs = lax.rsqrt(ss * (1.0 / QK_DIM) + RMS_EPS)
            k_ref[r0:r0 + nr, hd * HEAD_PAD:hd * HEAD_PAD + LANES] = (kn * rs * g_kn_ref[...]).astype(BF16)
            k_ref[r0:r0 + nr, hd * HEAD_PAD + LANES:(hd + 1) * HEAD_PAD] = (k_rot * rs).astype(BF16)
            vt_ref[hd * V_ROWS:hd * V_ROWS + V_DIM, r0:r0 + nr] = vh.T.astype(BF16)
            vt_ref[hd * V_ROWS + V_DIM:(hd + 1) * V_ROWS, r0:r0 + nr] = jnp.ones((BF16_ROWS, nr), BF16)

        base = SUBLANES + r0
        carry_ref[base:base + nr, :] = cv
        u = (conv_w_ref[2:3, :] * cv
             + conv_w_ref[1:2, :] * carry_ref[base - 1:base - 1 + nr, :]
             + conv_w_ref[0:1, :] * carry_ref[base - 2:base - 2 + nr, :])
        yc = cb * u * _silu(z_c)
        yc_ref[r0:r0 + nr, :] = (yc * _rms_scale(yc, CONV_WIDTH) * g_oc_ref[...]).astype(BF16)

        ga_ref[r0:r0 + nr, :] = _silu(z_a).astype(BF16)

    for blk in range(tm // PROJ_ROWS):
        rows_block(blk * PROJ_ROWS, PROJ_ROWS)
    carry_ref[0:SUBLANES, :] = carry_ref[tm:tm + SUBLANES, :]


def _attn_kernel(qn_ref, kn_ref, kc_ref, qc_ref, diag_n_ref, first_c_ref, last_c_ref,
                 qt_ref, k_ref, vt_ref, o_ref, s_ref, mx_ref, m_ref, acc_ref):
    step = pl.program_id(0)
    tq = qt_ref.shape[1]
    tk = k_ref.shape[0]

    @pl.when(step == 0)
    def _():
        s_ref[...] = jnp.zeros(s_ref.shape, F32)
        mx_ref[...] = jnp.zeros(mx_ref.shape, F32)

    @pl.when((first_c_ref[step] == 1) | (step == 0))
    def _():
        m_ref[...] = jnp.full(m_ref.shape, NEG_INF, F32)
        acc_ref[...] = jnp.zeros(acc_ref.shape, F32)

    nchunk = tq // ATTN_Q_CHUNK
    half = tk // 2

    def body(masked):
        if masked:
            assert nchunk == 2 and half == ATTN_Q_CHUNK
            tri = (lax.broadcasted_iota(jnp.int32, (half, ATTN_Q_CHUNK), 0)
                   <= lax.broadcasted_iota(jnp.int32, (half, ATTN_Q_CHUNK), 1))

        def scores(hd, c):
            cols = slice(c * ATTN_Q_CHUNK, (c + 1) * ATTN_Q_CHUNK)
            qt = qt_ref[hd * HEAD_PAD:(hd + 1) * HEAD_PAD, cols]
            if not masked:
                s = _dot(k_ref[:, hd * HEAD_PAD:(hd + 1) * HEAD_PAD], qt)
                s_ref[hd, :, cols] = s
                mx_ref[hd, :, cols] = jnp.max(s, axis=0, keepdims=True)
            elif c == 0:
                s = jnp.where(tri, _dot(k_ref[0:half, hd * HEAD_PAD:(hd + 1) * HEAD_PAD], qt), NEG_INF)
                s_ref[hd, 0:half, cols] = s
                s_ref[hd, half:tk, cols] = jnp.full((half, ATTN_Q_CHUNK), NEG_INF, F32)
                mx_ref[hd, :, cols] = jnp.max(s, axis=0, keepdims=True)
            else:
                s = _dot(k_ref[:, hd * HEAD_PAD:(hd + 1) * HEAD_PAD], qt)
                top = s[0:half]
                bot = jnp.where(tri, s[half:tk], NEG_INF)
                s_ref[hd, 0:half, cols] = top
                s_ref[hd, half:tk, cols] = bot
                mx_ref[hd, :, cols] = jnp.maximum(jnp.max(top, axis=0, keepdims=True),
                                                  jnp.max(bot, axis=0, keepdims=True))

        def softmax_pv(hd, c):
            cols = slice(c * ATTN_Q_CHUNK, (c + 1) * ATTN_Q_CHUNK)
            m_prev = m_ref[hd, :, cols]
            m_new = jnp.maximum(m_prev, mx_ref[hd, :, cols])
            alpha = jnp.exp2(m_prev - m_new)
            p = jnp.exp2((s_ref[hd, :, cols] - m_new).astype(BF16))
            pv = _dot(vt_ref[hd * V_ROWS:(hd + 1) * V_ROWS, :], p)
            acc_ref[hd, :, cols] = alpha * acc_ref[hd, :, cols] + pv
            m_ref[hd, :, cols] = m_new

        for hd in range(N_HEADS):
            for c in range(nchunk):
                softmax_pv(hd, c)
                scores(hd, c)

    is_diag = diag_n_ref[step] == 1

    @pl.when(jnp.logical_not(is_diag))
    def _():
        body(False)

    @pl.when(is_diag)
    def _():
        body(True)

    @pl.when(last_c_ref[step] == 1)
    def _():
        for hd in range(N_HEADS):
            acc = acc_ref[hd]
            o_ref[:, hd * V_DIM:(hd + 1) * V_DIM] = (
                acc[:V_DIM] / acc[V_DIM:V_DIM + 1]).T.astype(o_ref.dtype)


def _out_kernel(o_ref, ga_ref, yc_ref, x_ref, p_ref, g_oa_ref, w_o_ref, g_pl_ref, w_plg_ref, w_pl_ref,
                out_ref):
    tm = x_ref.shape[0]

    def residual(r0):
        rows = slice(r0, r0 + OUT_ROWS)
        ya = o_ref[rows, :].astype(F32) * ga_ref[rows, :].astype(F32)
        ya_n = (ya * _rms_scale(ya, ATTN_WIDTH) * g_oa_ref[...]).astype(BF16)
        x1 = (x_ref[rows, :] + _dot(ya_n, w_o_ref[:ATTN_WIDTH, :])
              + _dot(yc_ref[rows, :], w_o_ref[ATTN_WIDTH:, :]))
        ple = _dot(p_ref[rows, :].astype(BF16), w_pl_ref[...])
        hn = (x1 * _rms_scale(x1, D_MODEL) * g_pl_ref[...]).astype(BF16)
        return r0, x1, ple, hn

    def gated(r0, x1, ple, hn):
        gate = 1.0 / (1.0 + jnp.exp(-_dot(hn, w_plg_ref[...])))
        out_ref[r0:r0 + OUT_ROWS, :] = x1 + gate * ple

    prev = None
    for r0 in range(0, tm, OUT_ROWS):
        cur = residual(r0)
        if prev is not None:
            gated(*prev)
        prev = cur
    gated(*prev)


def _row_spec(tile, width):
    return pl.BlockSpec((tile, width), lambda i: (i, 0))


def _const_spec(shape):
    return pl.BlockSpec(shape, lambda i: (0,) * len(shape))


def _triangle_schedule(batch, n_blk):
    pairs = [(b * n_blk + i, b * n_blk + j, j == 0, j == i)
             for b in range(batch) for i in range(n_blk) for j in range(i + 1)]
    nxt = pairs + [pairs[-1]]
    cur = [pairs[0]] + pairs
    cols = ([p[0] for p in nxt], [p[1] for p in nxt], [p[1] for p in cur], [p[0] for p in cur],
            [int(p[3]) for p in nxt],
            [0] + [int(p[2]) for p in pairs], [0] + [int(p[3]) for p in pairs])
    return tuple(jnp.asarray(np.asarray(c, np.int32)) for c in cols)


def _layer(x2, p2, posc, invf, batch, seq, g_in, w_in, g_cq, w_uq, g_ckv, w_ukv, g_q, g_k,
           conv_w, g_oa, g_oc, w_o, w_pl, w_plg, g_pl):
    tokens = batch * seq
    tm = PROJ_TILE
    assert seq % tm == 0 and seq % ATTN_TILE == 0

    w_lat = jnp.pad(w_in[:, :LAT_SRC].astype(BF16), ((0, 0), (0, LAT_PAD - LAT_SRC)))
    w_wide = w_in[:, LAT_SRC:].astype(BF16)
    w_uq_p = jnp.pad(w_uq.reshape(Q_LORA, N_HEADS, QK_DIM),
                     ((0, 0), (0, 0), (0, HEAD_PAD - QK_DIM))).reshape(Q_LORA, N_HEADS * HEAD_PAD).astype(BF16)
    g_q_p = jnp.pad(g_q, (0, HEAD_PAD - QK_DIM)).reshape(1, HEAD_PAD)
    g_kn = g_k[:NOPE_DIM].reshape(1, NOPE_DIM)
    g_kr = jnp.pad(g_k[NOPE_DIM:], (0, KPE_PAD - ROPE_DIM)).reshape(1, KPE_PAD)

    cparams = pltpu.CompilerParams(dimension_semantics=("arbitrary",), vmem_limit_bytes=VMEM_LIMIT)

    def col_spec(rows, tile):
        return pl.BlockSpec((rows, tile), lambda i: (0, i))

    qt, k, vt, ga, yc = pl.pallas_call(
        functools.partial(_proj_kernel, seq // tm),
        grid=(tokens // tm,),
        in_specs=[
            _row_spec(tm, D_MODEL), _row_spec(tm // TOKENS_PER_ROW, LANES), _const_spec((1, LANES)),
            _const_spec((1, D_MODEL)), _const_spec((D_MODEL, LAT_PAD)), _const_spec((D_MODEL, WIDE)),
            _const_spec((1, Q_LORA)), _const_spec((Q_LORA, N_HEADS * HEAD_PAD)),
            _const_spec((1, KV_LORA)), _const_spec((KV_LORA, N_HEADS * (NOPE_DIM + V_DIM))),
            _const_spec((1, HEAD_PAD)), _const_spec((1, NOPE_DIM)), _const_spec((1, KPE_PAD)),
            _const_spec((CONV_K, CONV_WIDTH)), _const_spec((1, CONV_WIDTH)),
        ],
        out_specs=[
            col_spec(N_HEADS * HEAD_PAD, tm), _row_spec(tm, N_HEADS * HEAD_PAD),
            col_spec(N_HEADS * V_ROWS, tm), _row_spec(tm, ATTN_WIDTH), _row_spec(tm, CONV_WIDTH),
        ],
        out_shape=[
            jax.ShapeDtypeStruct((N_HEADS * HEAD_PAD, tokens), BF16),
            jax.ShapeDtypeStruct((tokens, N_HEADS * HEAD_PAD), BF16),
            jax.ShapeDtypeStruct((N_HEADS * V_ROWS, tokens), BF16),
            jax.ShapeDtypeStruct((tokens, ATTN_WIDTH), BF16),
            jax.ShapeDtypeStruct((tokens, CONV_WIDTH), BF16),
        ],
        scratch_shapes=[pltpu.VMEM((tm + SUBLANES, CONV_WIDTH), F32)],
        compiler_params=cparams,
        name="mla_conv_proj",
    )(x2, posc, invf, g_in.reshape(1, -1), w_lat, w_wide, g_cq.reshape(1, -1), w_uq_p,
      g_ckv.reshape(1, -1), w_ukv.astype(BF16), g_q_p, g_kn, g_kr, conv_w, g_oc.reshape(1, -1))

    ta = ATTN_TILE
    sched = _triangle_schedule(batch, seq // ta)
    o = pl.pallas_call(
        _attn_kernel,
        grid_spec=pltpu.PrefetchScalarGridSpec(
            num_scalar_prefetch=len(sched),
            grid=(int(sched[0].shape[0]),),
            in_specs=[
                pl.BlockSpec((N_HEADS * HEAD_PAD, ta), lambda s, qn, kn, kc, qc, *_: (0, qn[s])),
                pl.BlockSpec((ta, N_HEADS * HEAD_PAD), lambda s, qn, kn, kc, qc, *_: (kn[s], 0)),
                pl.BlockSpec((N_HEADS * V_ROWS, ta), lambda s, qn, kn, kc, qc, *_: (0, kc[s])),
            ],
            out_specs=pl.BlockSpec((ta, ATTN_WIDTH), lambda s, qn, kn, kc, qc, *_: (qc[s], 0)),
            scratch_shapes=[pltpu.VMEM((N_HEADS, ta, ta), F32),
                            pltpu.VMEM((N_HEADS, 1, ta), F32),
                            pltpu.VMEM((N_HEADS, 1, ta), F32),
                            pltpu.VMEM((N_HEADS, V_ROWS, ta), F32)],
        ),
        out_shape=jax.ShapeDtypeStruct((tokens, ATTN_WIDTH), BF16),
        compiler_params=cparams,
        name="mla_flash_attn",
    )(*sched, qt, k, vt)

    return pl.pallas_call(
        _out_kernel,
        grid=(tokens // tm,),
        in_specs=[
            _row_spec(tm, ATTN_WIDTH), _row_spec(tm, ATTN_WIDTH), _row_spec(tm, CONV_WIDTH),
            _row_spec(tm, D_MODEL), _row_spec(tm, PLE_DIM),
            _const_spec((1, ATTN_WIDTH)), _const_spec((D_MODEL, D_MODEL)), _const_spec((1, D_MODEL)),
            _const_spec((D_MODEL, D_MODEL)), _const_spec((PLE_DIM, D_MODEL)),
        ],
        out_specs=_row_spec(tm, D_MODEL),
        out_shape=jax.ShapeDtypeStruct((tokens, D_MODEL), F32),
        compiler_params=cparams,
        name="mla_conv_out",
    )(o, ga, yc, x2, p2, g_oa.reshape(1, -1), w_o.astype(BF16),
      g_pl.reshape(1, -1), w_plg.astype(BF16), w_pl.astype(BF16))


def kernel(x, p, positions, g_in, w_in, g_cq, w_uq, g_ckv, w_ukv, g_q, g_k, conv_w, g_oa, g_oc,
           w_o, w_pl, w_plg, g_pl):
    batch, seq, d_model = x.shape
    depth = p.shape[0]
    tokens = batch * seq
    rows = PROJ_TILE // TOKENS_PER_ROW
    posc = jnp.repeat(positions.astype(F32).reshape(tokens // PROJ_TILE, TOKENS_PER_ROW, rows)
                      .transpose(0, 2, 1), HALF_ROPE, axis=-1).reshape(tokens // TOKENS_PER_ROW, LANES)
    inv_freq = 1.0 / (ROPE_THETA ** (jnp.arange(0, ROPE_DIM, 2, dtype=F32) / ROPE_DIM))
    invf = jnp.tile(inv_freq, LANES // HALF_ROPE).reshape(1, LANES)
    h = x.reshape(tokens, d_model)
    for i in range(depth):
        h = _layer(h, p[i].reshape(tokens, PLE_DIM), posc, invf, batch, seq,
                   g_in[i], w_in[i], g_cq[i], w_uq[i], g_ckv[i], w_ukv[i], g_q[i], g_k[i],
                   conv_w[i], g_oa[i], g_oc[i], w_o[i], w_pl[i], w_plg[i], g_pl[i])
    return h.reshape(batch, seq, d_model).astype(x.dtype)
```

```python
import functools
import math

import jax
import jax.numpy as jnp
import numpy as np
from jax import lax
from jax.experimental import pallas as pl
from jax.experimental.pallas import tpu as pltpu

D_MODEL = 1024
PLE_DIM = 256
N_HEADS = 4
NOPE_DIM = 128
ROPE_DIM = 64
HALF_ROPE = ROPE_DIM // 2
V_DIM = 128
QK_DIM = NOPE_DIM + ROPE_DIM
Q_LORA = 256
KV_LORA = 128
ATTN_WIDTH = N_HEADS * V_DIM
CONV_WIDTH = D_MODEL - ATTN_WIDTH
CONV_K = 3
ROPE_THETA = 10000.0
RMS_EPS = 1e-6
NEG_INF = -1e30

LANES = 128
SUBLANES = 8
HEAD_PAD = 2 * LANES
KPE_PAD = LANES
BF16_ROWS = 16
V_ROWS = V_DIM + BF16_ROWS
TOKENS_PER_ROW = LANES // HALF_ROPE
OFF_CQ = 0
OFF_CKV = OFF_CQ + Q_LORA
OFF_KPE = OFF_CKV + KV_LORA
LAT_SRC = OFF_KPE + ROPE_DIM
LAT_PAD = OFF_KPE + KPE_PAD
OFF_ZA = 0
OFF_CB = OFF_ZA + ATTN_WIDTH
OFF_CC = OFF_CB + CONV_WIDTH
OFF_CX = OFF_CC + CONV_WIDTH
OFF_ZC = OFF_CX + CONV_WIDTH
WIDE = OFF_ZC + CONV_WIDTH

PROJ_TILE = 1024
PROJ_ROWS = 256
OUT_ROWS = 256
PREP_STEPS = 8
ATTN_TILE = 1024
ATTN_Q_CHUNK = 512
VMEM_LIMIT = 48 * 1024 * 1024

BF16 = jnp.bfloat16
F32 = jnp.float32


def _rms_scale(v, width):
    return lax.rsqrt(jnp.sum(v * v, axis=-1, keepdims=True) * (1.0 / width) + RMS_EPS)


def _silu(z):
    return z / (1.0 + jnp.exp(-z))


def _dot(a, b):
    return jnp.dot(a, b, preferred_element_type=F32)


def _rope_tables(posc, invf):
    ang = posc * invf
    cosc = jnp.cos(ang)
    sinc = jnp.sin(ang)
    lane = lax.broadcasted_iota(jnp.int32, (1, LANES), 1)
    lo = lane < HALF_ROPE
    mid = (lane >= HALF_ROPE) & (lane < ROPE_DIM)
    quarters = []
    for a in range(TOKENS_PER_ROW):
        c = pltpu.roll(cosc, LANES - HALF_ROPE * a, 1) if a else cosc
        s = pltpu.roll(sinc, LANES - HALF_ROPE * a, 1) if a else sinc
        quarters.append((jnp.where(lo, c, jnp.where(mid, pltpu.roll(c, HALF_ROPE, 1), 0.0)),
                         jnp.where(lo, -s, jnp.where(mid, pltpu.roll(s, HALF_ROPE, 1), 0.0))))
    return quarters, lo


def _rope(t, c, s, lo):
    swapped = jnp.where(lo, pltpu.roll(t, LANES - HALF_ROPE, 1), pltpu.roll(t, HALF_ROPE, 1))
    return t * c + swapped * s


def _proj_kernel(tiles_per_seq,
                 x_ref, posc_ref, invf_ref, w_lat_ref, w_in_ref, w_uq_ref, w_ukv_ref,
                 g_q_ref, g_kn_ref, g_kr_ref, conv_w_ref,
                 qt_ref, k_ref, vt_ref, ga_ref, yc_ref, carry_ref):
    tm = x_ref.shape[0]
    quarter_rows = tm // TOKENS_PER_ROW
    tables = []
    q_scale = math.log2(math.e) / math.sqrt(QK_DIM)

    @pl.when(pl.program_id(0) % tiles_per_seq == 0)
    def _():
        carry_ref[0:SUBLANES, :] = jnp.zeros((SUBLANES, CONV_WIDTH), F32)

    def rows_block(r0, nr):
        x = x_ref[r0:r0 + nr, :]
        h = (x * _rms_scale(x, D_MODEL)).astype(BF16)

        def proj(off, width):
            return _dot(h, w_in_ref[:, off:off + width])

        lat = _dot(h, w_lat_ref[...])
        c_q = lat[:, OFF_CQ:OFF_CKV]
        c_kv = lat[:, OFF_CKV:OFF_KPE]
        kpe = lat[:, OFF_KPE:LAT_PAD]
        cv = proj(OFF_CC, CONV_WIDTH) * proj(OFF_CX, CONV_WIDTH)
        cqn = (c_q * _rms_scale(c_q, Q_LORA)).astype(BF16)
        qf = _dot(cqn, w_uq_ref[...])
        cb = proj(OFF_CB, CONV_WIDTH)
        ckvn = (c_kv * _rms_scale(c_kv, KV_LORA)).astype(BF16)
        kv = _dot(ckvn, w_ukv_ref[...])
        z_c = proj(OFF_ZC, CONV_WIDTH)
        z_a = proj(OFF_ZA, ATTN_WIDTH)

        if not tables:
            tables.extend(_rope_tables(posc_ref[...], invf_ref[...]))
        quarters, lo = tables
        qa = r0 // quarter_rows
        cos_t = jnp.concatenate([quarters[qa + i][0] for i in range(nr // quarter_rows)], axis=0)
        sin_t = jnp.concatenate([quarters[qa + i][1] for i in range(nr // quarter_rows)], axis=0)

        for hd in range(N_HEADS):
            qh = qf[:, hd * HEAD_PAD:(hd + 1) * HEAD_PAD]
            qn = qh * (_rms_scale(qh, QK_DIM) * q_scale) * g_q_ref[...]
            qt_ref[hd * HEAD_PAD:hd * HEAD_PAD + LANES, r0:r0 + nr] = qn[:, :LANES].T.astype(BF16)
            qt_ref[hd * HEAD_PAD + LANES:(hd + 1) * HEAD_PAD, r0:r0 + nr] = (
                _rope(qn[:, LANES:], cos_t, sin_t, lo).T.astype(BF16))

        ss_kpe = jnp.sum(kpe * kpe, axis=-1, keepdims=True)
        k_rot = _rope(kpe * g_kr_ref[...], cos_t, sin_t, lo)
        for hd in range(N_HEADS):
            kn = kv[:, hd * (NOPE_DIM + V_DIM):hd * (NOPE_DIM + V_DIM) + NOPE_DIM]
            vh = kv[:, hd * (NOPE_DIM + V_DIM) + NOPE_DIM:(hd + 1) * (NOPE_DIM + V_DIM)]
            ss = jnp.sum(kn * kn, axis=-1, keepdims=True) + ss_kpe
            rs = lax.rsqrt(ss * (1.0 / QK_DIM) + RMS_EPS)
            k_ref[r0:r0 + nr, hd * HEAD_PAD:hd * HEAD_PAD + LANES] = (kn * rs * g_kn_ref[...]).astype(BF16)
            k_ref[r0:r0 + nr, hd * HEAD_PAD + LANES:(hd + 1) * HEAD_PAD] = (k_rot * rs).astype(BF16)
            vt_ref[hd * V_ROWS:hd * V_ROWS + V_DIM, r0:r0 + nr] = vh.T.astype(BF16)
            vt_ref[hd * V_ROWS + V_DIM:(hd + 1) * V_ROWS, r0:r0 + nr] = jnp.ones((BF16_ROWS, nr), BF16)

        base = SUBLANES + r0
        carry_ref[base:base + nr, :] = cv
        u = (conv_w_ref[2:3, :] * cv
             + conv_w_ref[1:2, :] * carry_ref[base - 1:base - 1 + nr, :]
             + conv_w_ref[0:1, :] * carry_ref[base - 2:base - 2 + nr, :])
        yc = cb * u * _silu(z_c)
        yc_ref[r0:r0 + nr, :] = (yc * _rms_scale(yc, CONV_WIDTH)).astype(BF16)

        ga_ref[r0:r0 + nr, :] = _silu(z_a).astype(BF16)

    for blk in range(tm // PROJ_ROWS):
        rows_block(blk * PROJ_ROWS, PROJ_ROWS)
    carry_ref[0:SUBLANES, :] = carry_ref[tm:tm + SUBLANES, :]


def _attn_kernel(qn_ref, kn_ref, kc_ref, qc_ref, diag_n_ref, first_c_ref, last_c_ref,
                 qt_ref, k_ref, vt_ref, o_ref, s_ref, mx_ref, m_ref, acc_ref):
    step = pl.program_id(0)
    tq = qt_ref.shape[1]
    tk = k_ref.shape[0]

    @pl.when(step == 0)
    def _():
        s_ref[...] = jnp.zeros(s_ref.shape, F32)
        mx_ref[...] = jnp.zeros(mx_ref.shape, F32)

    @pl.when((first_c_ref[step] == 1) | (step == 0))
    def _():
        m_ref[...] = jnp.full(m_ref.shape, NEG_INF, F32)
        acc_ref[...] = jnp.zeros(acc_ref.shape, F32)

    nchunk = tq // ATTN_Q_CHUNK
    half = tk // 2

    def body(masked):
        if masked:
            assert nchunk == 2 and half == ATTN_Q_CHUNK
            tri = (lax.broadcasted_iota(jnp.int32, (half, ATTN_Q_CHUNK), 0)
                   <= lax.broadcasted_iota(jnp.int32, (half, ATTN_Q_CHUNK), 1))

        def scores(hd, c):
            cols = slice(c * ATTN_Q_CHUNK, (c + 1) * ATTN_Q_CHUNK)
            qt = qt_ref[hd * HEAD_PAD:(hd + 1) * HEAD_PAD, cols]
            if not masked:
                s = _dot(k_ref[:, hd * HEAD_PAD:(hd + 1) * HEAD_PAD], qt)
                s_ref[hd, :, cols] = s
                mx_ref[hd, :, cols] = jnp.max(s, axis=0, keepdims=True)
            elif c == 0:
                s = jnp.where(tri, _dot(k_ref[0:half, hd * HEAD_PAD:(hd + 1) * HEAD_PAD], qt), NEG_INF)
                s_ref[hd, 0:half, cols] = s
                s_ref[hd, half:tk, cols] = jnp.full((half, ATTN_Q_CHUNK), NEG_INF, F32)
                mx_ref[hd, :, cols] = jnp.max(s, axis=0, keepdims=True)
            else:
                s = _dot(k_ref[:, hd * HEAD_PAD:(hd + 1) * HEAD_PAD], qt)
                top = s[0:half]
                bot = jnp.where(tri, s[half:tk], NEG_INF)
                s_ref[hd, 0:half, cols] = top
                s_ref[hd, half:tk, cols] = bot
                mx_ref[hd, :, cols] = jnp.maximum(jnp.max(top, axis=0, keepdims=True),
                                                  jnp.max(bot, axis=0, keepdims=True))

        def softmax_pv(hd, c):
            cols = slice(c * ATTN_Q_CHUNK, (c + 1) * ATTN_Q_CHUNK)
            m_prev = m_ref[hd, :, cols]
            m_new = jnp.maximum(m_prev, mx_ref[hd, :, cols])
            alpha = jnp.exp2(m_prev - m_new)
            p = jnp.exp2((s_ref[hd, :, cols] - m_new).astype(BF16))
            pv = _dot(vt_ref[hd * V_ROWS:(hd + 1) * V_ROWS, :], p)
            acc_ref[hd, :, cols] = alpha * acc_ref[hd, :, cols] + pv
            m_ref[hd, :, cols] = m_new

        for hd in range(N_HEADS):
            for c in range(nchunk):
                softmax_pv(hd, c)
                scores(hd, c)

    is_diag = diag_n_ref[step] == 1

    @pl.when(jnp.logical_not(is_diag))
    def _():
        body(False)

    @pl.when(is_diag)
    def _():
        body(True)

    @pl.when(last_c_ref[step] == 1)
    def _():
        for hd in range(N_HEADS):
            acc = acc_ref[hd]
            o_ref[:, hd * V_DIM:(hd + 1) * V_DIM] = (
                acc[:V_DIM] / acc[V_DIM:V_DIM + 1]).T.astype(o_ref.dtype)


def _out_kernel(o_ref, ga_ref, yc_ref, x_ref, p_ref, w_o_ref, w_plg_ref, w_pl_ref, out_ref):
    tm = x_ref.shape[0]

    def residual(r0):
        rows = slice(r0, r0 + OUT_ROWS)
        ya = o_ref[rows, :].astype(F32) * ga_ref[rows, :].astype(F32)
        ya_n = (ya * _rms_scale(ya, ATTN_WIDTH)).astype(BF16)
        x1 = (x_ref[rows, :] + _dot(ya_n, w_o_ref[:ATTN_WIDTH, :])
              + _dot(yc_ref[rows, :], w_o_ref[ATTN_WIDTH:, :]))
        ple = _dot(p_ref[rows, :].astype(BF16), w_pl_ref[...])
        hn = (x1 * _rms_scale(x1, D_MODEL)).astype(BF16)
        return r0, x1, ple, hn

    def gated(r0, x1, ple, hn):
        gate = 1.0 / (1.0 + jnp.exp(-_dot(hn, w_plg_ref[...])))
        out_ref[r0:r0 + OUT_ROWS, :] = x1 + gate * ple

    prev = None
    for r0 in range(0, tm, OUT_ROWS):
        cur = residual(r0)
        if prev is not None:
            gated(*prev)
        prev = cur
    gated(*prev)


def _row_spec(tile, width):
    return pl.BlockSpec((tile, width), lambda i: (i, 0))


def _const_spec(shape):
    return pl.BlockSpec(shape, lambda i: (0,) * len(shape))


def _triangle_schedule(batch, n_blk):
    pairs = [(b * n_blk + i, b * n_blk + j, j == 0, j == i)
             for b in range(batch) for i in range(n_blk) for j in range(i + 1)]
    nxt = pairs + [pairs[-1]]
    cur = [pairs[0]] + pairs
    cols = ([p[0] for p in nxt], [p[1] for p in nxt], [p[1] for p in cur], [p[0] for p in cur],
            [int(p[3]) for p in nxt],
            [0] + [int(p[2]) for p in pairs], [0] + [int(p[3]) for p in pairs])
    return tuple(jnp.asarray(np.asarray(c, np.int32)) for c in cols)


def _prep_kernel(w_in_ref, g_in_ref, w_uq_ref, g_cq_ref, w_ukv_ref, g_ckv_ref, w_o_ref, g_o_ref,
                 w_plg_ref, g_pl_ref, w_pl_ref,
                 lat_ref, wide_ref, uq_ref, ukv_ref, o_ref, plg_ref, pl_ref):
    w_in = w_in_ref[...] * g_in_ref[...]
    lat_ref[:, :LAT_SRC] = w_in[:, :LAT_SRC].astype(BF16)
    lat_ref[:, LAT_SRC:] = jnp.zeros((lat_ref.shape[0], LAT_PAD - LAT_SRC), BF16)
    wide_ref[...] = w_in[:, LAT_SRC:].astype(BF16)
    w_uq = w_uq_ref[...] * g_cq_ref[...]
    for hd in range(N_HEADS):
        uq_ref[:, hd * HEAD_PAD:hd * HEAD_PAD + QK_DIM] = w_uq[:, hd * QK_DIM:(hd + 1) * QK_DIM].astype(BF16)
        uq_ref[:, hd * HEAD_PAD + QK_DIM:(hd + 1) * HEAD_PAD] = jnp.zeros(
            (uq_ref.shape[0], HEAD_PAD - QK_DIM), BF16)
    ukv_ref[...] = (w_ukv_ref[...] * g_ckv_ref[...]).astype(BF16)
    o_ref[...] = (w_o_ref[...] * g_o_ref[...]).astype(BF16)
    plg_ref[...] = (w_plg_ref[...] * g_pl_ref[...]).astype(BF16)
    pl_ref[...] = w_pl_ref[...].astype(BF16)


def _prepare_weights(w_in, g_in, w_uq, g_cq, w_ukv, g_ckv, w_o, g_o, w_plg, g_pl, w_pl):
    def blk(rows, width):
        return pl.BlockSpec((rows // PREP_STEPS, width), lambda i: (i, 0))

    def weight_and_gain(w):
        return [blk(w.shape[0], w.shape[1]), blk(w.shape[0], 1)]

    kv_width = N_HEADS * (NOPE_DIM + V_DIM)
    return pl.pallas_call(
        _prep_kernel,
        grid=(PREP_STEPS,),
        in_specs=(weight_and_gain(w_in) + weight_and_gain(w_uq) + weight_and_gain(w_ukv)
                  + weight_and_gain(w_o) + weight_and_gain(w_plg) + [blk(PLE_DIM, D_MODEL)]),
        out_specs=[blk(D_MODEL, LAT_PAD), blk(D_MODEL, WIDE), blk(Q_LORA, N_HEADS * HEAD_PAD),
                   blk(KV_LORA, kv_width), blk(D_MODEL, D_MODEL), blk(D_MODEL, D_MODEL),
                   blk(PLE_DIM, D_MODEL)],
        out_shape=[jax.ShapeDtypeStruct(shape, BF16) for shape in (
            (D_MODEL, LAT_PAD), (D_MODEL, WIDE), (Q_LORA, N_HEADS * HEAD_PAD), (KV_LORA, kv_width),
            (D_MODEL, D_MODEL), (D_MODEL, D_MODEL), (PLE_DIM, D_MODEL))],
        compiler_params=pltpu.CompilerParams(dimension_semantics=("arbitrary",),
                                             vmem_limit_bytes=VMEM_LIMIT),
        name="mla_conv_weight_prep",
    )(w_in, g_in.reshape(-1, 1), w_uq, g_cq.reshape(-1, 1), w_ukv, g_ckv.reshape(-1, 1),
      w_o, g_o.reshape(-1, 1), w_plg, g_pl.reshape(-1, 1), w_pl)


def _layer(x2, p2, posc, invf, batch, seq, g_in, w_in, g_cq, w_uq, g_ckv, w_ukv, g_q, g_k,
           conv_w, g_oa, g_oc, w_o, w_pl, w_plg, g_pl):
    tokens = batch * seq
    tm = PROJ_TILE
    assert seq % tm == 0 and seq % ATTN_TILE == 0

    w_lat, w_wide, w_uq_p, w_ukv_b, w_o_b, w_plg_b, w_pl_b = _prepare_weights(
        w_in, g_in, w_uq, g_cq, w_ukv, g_ckv, w_o, jnp.concatenate([g_oa, g_oc]), w_plg, g_pl, w_pl)
    g_q_p = jnp.pad(g_q, (0, HEAD_PAD - QK_DIM)).reshape(1, HEAD_PAD)
    g_kn = g_k[:NOPE_DIM].reshape(1, NOPE_DIM)
    g_kr = jnp.pad(g_k[NOPE_DIM:], (0, KPE_PAD - ROPE_DIM)).reshape(1, KPE_PAD)

    cparams = pltpu.CompilerParams(dimension_semantics=("arbitrary",), vmem_limit_bytes=VMEM_LIMIT)

    def col_spec(rows, tile):
        return pl.BlockSpec((rows, tile), lambda i: (0, i))

    qt, k, vt, ga, yc = pl.pallas_call(
        functools.partial(_proj_kernel, seq // tm),
        grid=(tokens // tm,),
        in_specs=[
            _row_spec(tm, D_MODEL), _row_spec(tm // TOKENS_PER_ROW, LANES), _const_spec((1, LANES)),
            _const_spec((D_MODEL, LAT_PAD)), _const_spec((D_MODEL, WIDE)),
            _const_spec((Q_LORA, N_HEADS * HEAD_PAD)),
            _const_spec((KV_LORA, N_HEADS * (NOPE_DIM + V_DIM))),
            _const_spec((1, HEAD_PAD)), _const_spec((1, NOPE_DIM)), _const_spec((1, KPE_PAD)),
            _const_spec((CONV_K, CONV_WIDTH)),
        ],
        out_specs=[
            col_spec(N_HEADS * HEAD_PAD, tm), _row_spec(tm, N_HEADS * HEAD_PAD),
            col_spec(N_HEADS * V_ROWS, tm), _row_spec(tm, ATTN_WIDTH), _row_spec(tm, CONV_WIDTH),
        ],
        out_shape=[
            jax.ShapeDtypeStruct((N_HEADS * HEAD_PAD, tokens), BF16),
            jax.ShapeDtypeStruct((tokens, N_HEADS * HEAD_PAD), BF16),
            jax.ShapeDtypeStruct((N_HEADS * V_ROWS, tokens), BF16),
            jax.ShapeDtypeStruct((tokens, ATTN_WIDTH), BF16),
            jax.ShapeDtypeStruct((tokens, CONV_WIDTH), BF16),
        ],
        scratch_shapes=[pltpu.VMEM((tm + SUBLANES, CONV_WIDTH), F32)],
        compiler_params=cparams,
        name="mla_conv_proj",
    )(x2, posc, invf, w_lat, w_wide, w_uq_p, w_ukv_b, g_q_p, g_kn, g_kr, conv_w)

    ta = ATTN_TILE
    sched = _triangle_schedule(batch, seq // ta)
    o = pl.pallas_call(
        _attn_kernel,
        grid_spec=pltpu.PrefetchScalarGridSpec(
            num_scalar_prefetch=len(sched),
            grid=(int(sched[0].shape[0]),),
            in_specs=[
                pl.BlockSpec((N_HEADS * HEAD_PAD, ta), lambda s, qn, kn, kc, qc, *_: (0, qn[s])),
                pl.BlockSpec((ta, N_HEADS * HEAD_PAD), lambda s, qn, kn, kc, qc, *_: (kn[s], 0)),
                pl.BlockSpec((N_HEADS * V_ROWS, ta), lambda s, qn, kn, kc, qc, *_: (0, kc[s])),
            ],
            out_specs=pl.BlockSpec((ta, ATTN_WIDTH), lambda s, qn, kn, kc, qc, *_: (qc[s], 0)),
            scratch_shapes=[pltpu.VMEM((N_HEADS, ta, ta), F32),
                            pltpu.VMEM((N_HEADS, 1, ta), F32),
                            pltpu.VMEM((N_HEADS, 1, ta), F32),
                            pltpu.VMEM((N_HEADS, V_ROWS, ta), F32)],
        ),
        out_shape=jax.ShapeDtypeStruct((tokens, ATTN_WIDTH), BF16),
        compiler_params=cparams,
        name="mla_flash_attn",
    )(*sched, qt, k, vt)

    return pl.pallas_call(
        _out_kernel,
        grid=(tokens // tm,),
        in_specs=[
            _row_spec(tm, ATTN_WIDTH), _row_spec(tm, ATTN_WIDTH), _row_spec(tm, CONV_WIDTH),
            _row_spec(tm, D_MODEL), _row_spec(tm, PLE_DIM),
            _const_spec((D_MODEL, D_MODEL)), _const_spec((D_MODEL, D_MODEL)), _const_spec((PLE_DIM, D_MODEL)),
        ],
        out_specs=_row_spec(tm, D_MODEL),
        out_shape=jax.ShapeDtypeStruct((tokens, D_MODEL), F32),
        compiler_params=cparams,
        name="mla_conv_out",
    )(o, ga, yc, x2, p2, w_o_b, w_plg_b, w_pl_b)


def kernel(x, p, positions, g_in, w_in, g_cq, w_uq, g_ckv, w_ukv, g_q, g_k, conv_w, g_oa, g_oc,
           w_o, w_pl, w_plg, g_pl):
    batch, seq, d_model = x.shape
    depth = p.shape[0]
    tokens = batch * seq
    rows = PROJ_TILE // TOKENS_PER_ROW
    posc = jnp.repeat(positions.astype(F32).reshape(tokens // PROJ_TILE, TOKENS_PER_ROW, rows)
                      .transpose(0, 2, 1), HALF_ROPE, axis=-1).reshape(tokens // TOKENS_PER_ROW, LANES)
    inv_freq = 1.0 / (ROPE_THETA ** (jnp.arange(0, ROPE_DIM, 2, dtype=F32) / ROPE_DIM))
    invf = jnp.tile(inv_freq, LANES // HALF_ROPE).reshape(1, LANES)
    h = x.reshape(tokens, d_model)
    for i in range(depth):
        h = _layer(h, p[i].reshape(tokens, PLE_DIM), posc, invf, batch, seq,
                   g_in[i], w_in[i], g_cq[i], w_uq[i], g_ckv[i], w_ukv[i], g_q[i], g_k[i],
                   conv_w[i], g_oa[i], g_oc[i], w_o[i], w_pl[i], w_plg[i], g_pl[i])
    return h.reshape(batch, seq, d_model).astype(x.dtype)
```

```python
import functools
import math

import jax
import jax.numpy as jnp
import numpy as np
from jax import lax
from jax.experimental import pallas as pl
from jax.experimental.pallas import tpu as pltpu

D_MODEL = 1024
PLE_DIM = 256
N_HEADS = 4
NOPE_DIM = 128
ROPE_DIM = 64
HALF_ROPE = ROPE_DIM // 2
V_DIM = 128
QK_DIM = NOPE_DIM + ROPE_DIM
Q_LORA = 256
KV_LORA = 128
ATTN_WIDTH = N_HEADS * V_DIM
CONV_WIDTH = D_MODEL - ATTN_WIDTH
CONV_K = 3
ROPE_THETA = 10000.0
RMS_EPS = 1e-6
NEG_INF = -1e30

LANES = 128
SUBLANES = 8
HEAD_PAD = 2 * LANES
KPE_PAD = LANES
BF16_ROWS = 16
V_ROWS = V_DIM + BF16_ROWS
TOKENS_PER_ROW = LANES // HALF_ROPE
OFF_CQ = 0
OFF_CKV = OFF_CQ + Q_LORA
OFF_KPE = OFF_CKV + KV_LORA
LAT_SRC = OFF_KPE + ROPE_DIM
LAT_PAD = OFF_KPE + KPE_PAD
OFF_ZA = 0
OFF_CB = OFF_ZA + ATTN_WIDTH
OFF_CC = OFF_CB + CONV_WIDTH
OFF_CX = OFF_CC + CONV_WIDTH
OFF_ZC = OFF_CX + CONV_WIDTH
WIDE = OFF_ZC + CONV_WIDTH

PROJ_TILE = 1024
PROJ_ROWS = 256
OUT_ROWS = 256
PREP_STEPS = 8
PREP_COLS = 512
ATTN_TILE = 1024
ATTN_Q_CHUNK = 512
VMEM_LIMIT = 48 * 1024 * 1024

BF16 = jnp.bfloat16
F32 = jnp.float32


def _rms_scale(v, width):
    return lax.rsqrt(jnp.sum(v * v, axis=-1, keepdims=True) * (1.0 / width) + RMS_EPS)


def _silu(z):
    return z / (1.0 + jnp.exp(-z))


def _dot(a, b):
    return jnp.dot(a, b, preferred_element_type=F32)


def _rope_tables(posc, invf):
    ang = posc * invf
    cosc = jnp.cos(ang)
    sinc = jnp.sin(ang)
    lane = lax.broadcasted_iota(jnp.int32, (1, LANES), 1)
    lo = lane < HALF_ROPE
    mid = (lane >= HALF_ROPE) & (lane < ROPE_DIM)
    quarters = []
    for a in range(TOKENS_PER_ROW):
        c = pltpu.roll(cosc, LANES - HALF_ROPE * a, 1) if a else cosc
        s = pltpu.roll(sinc, LANES - HALF_ROPE * a, 1) if a else sinc
        quarters.append((jnp.where(lo, c, jnp.where(mid, pltpu.roll(c, HALF_ROPE, 1), 0.0)),
                         jnp.where(lo, -s, jnp.where(mid, pltpu.roll(s, HALF_ROPE, 1), 0.0))))
    return quarters, lo


def _rope(t, c, s, lo):
    swapped = jnp.where(lo, pltpu.roll(t, LANES - HALF_ROPE, 1), pltpu.roll(t, HALF_ROPE, 1))
    return t * c + swapped * s


def _proj_kernel(tiles_per_seq,
                 x_ref, posc_ref, invf_ref, w_lat_ref, w_in_ref, w_uq_ref, w_ukv_ref,
                 g_q_ref, g_kn_ref, g_kr_ref, conv_w_ref,
                 qt_ref, k_ref, vt_ref, ga_ref, yc_ref, carry_ref):
    tm = x_ref.shape[0]
    quarter_rows = tm // TOKENS_PER_ROW
    tables = []
    q_scale = math.log2(math.e) / math.sqrt(QK_DIM)

    @pl.when(pl.program_id(0) % tiles_per_seq == 0)
    def _():
        carry_ref[0:SUBLANES, :] = jnp.zeros((SUBLANES, CONV_WIDTH), F32)

    def rows_block(r0, nr):
        x = x_ref[r0:r0 + nr, :]
        h = (x * _rms_scale(x, D_MODEL)).astype(BF16)

        def proj(off, width):
            return _dot(h, w_in_ref[:, off:off + width])

        lat = _dot(h, w_lat_ref[...])
        c_q = lat[:, OFF_CQ:OFF_CKV]
        c_kv = lat[:, OFF_CKV:OFF_KPE]
        kpe = lat[:, OFF_KPE:LAT_PAD]
        cv = proj(OFF_CC, CONV_WIDTH) * proj(OFF_CX, CONV_WIDTH)
        cqn = (c_q * _rms_scale(c_q, Q_LORA)).astype(BF16)
        qf = _dot(cqn, w_uq_ref[...])
        cb = proj(OFF_CB, CONV_WIDTH)
        ckvn = (c_kv * _rms_scale(c_kv, KV_LORA)).astype(BF16)
        kv = _dot(ckvn, w_ukv_ref[...])
        z_c = proj(OFF_ZC, CONV_WIDTH)
        z_a = proj(OFF_ZA, ATTN_WIDTH)

        if not tables:
            tables.extend(_rope_tables(posc_ref[...], invf_ref[...]))
        quarters, lo = tables
        qa = r0 // quarter_rows
        cos_t = jnp.concatenate([quarters[qa + i][0] for i in range(nr // quarter_rows)], axis=0)
        sin_t = jnp.concatenate([quarters[qa + i][1] for i in range(nr // quarter_rows)], axis=0)

        for hd in range(N_HEADS):
            qh = qf[:, hd * HEAD_PAD:(hd + 1) * HEAD_PAD]
            qn = qh * (_rms_scale(qh, QK_DIM) * q_scale) * g_q_ref[...]
            qt_ref[hd * HEAD_PAD:hd * HEAD_PAD + LANES, r0:r0 + nr] = qn[:, :LANES].T.astype(BF16)
            qt_ref[hd * HEAD_PAD + LANES:(hd + 1) * HEAD_PAD, r0:r0 + nr] = (
                _rope(qn[:, LANES:], cos_t, sin_t, lo).T.astype(BF16))

        ss_kpe = jnp.sum(kpe * kpe, axis=-1, keepdims=True)
        k_rot = _rope(kpe * g_kr_ref[...], cos_t, sin_t, lo)
        for hd in range(N_HEADS):
            kn = kv[:, hd * (NOPE_DIM + V_DIM):hd * (NOPE_DIM + V_DIM) + NOPE_DIM]
            vh = kv[:, hd * (NOPE_DIM + V_DIM) + NOPE_DIM:(hd + 1) * (NOPE_DIM + V_DIM)]
            ss = jnp.sum(kn * kn, axis=-1, keepdims=True) + ss_kpe
            rs = lax.rsqrt(ss * (1.0 / QK_DIM) + RMS_EPS)
            k_ref[r0:r0 + nr, hd * HEAD_PAD:hd * HEAD_PAD + LANES] = (kn * rs * g_kn_ref[...]).astype(BF16)
            k_ref[r0:r0 + nr, hd * HEAD_PAD + LANES:(hd + 1) * HEAD_PAD] = (k_rot * rs).astype(BF16)
            vt_ref[hd * V_ROWS:hd * V_ROWS + V_DIM, r0:r0 + nr] = vh.T.astype(BF16)
            vt_ref[hd * V_ROWS + V_DIM:(hd + 1) * V_ROWS, r0:r0 + nr] = jnp.ones((BF16_ROWS, nr), BF16)

        base = SUBLANES + r0
        carry_ref[base:base + nr, :] = cv
        u = (conv_w_ref[2:3, :] * cv
             + conv_w_ref[1:2, :] * carry_ref[base - 1:base - 1 + nr, :]
             + conv_w_ref[0:1, :] * carry_ref[base - 2:base - 2 + nr, :])
        yc = cb * u * _silu(z_c)
        yc_ref[r0:r0 + nr, :] = (yc * _rms_scale(yc, CONV_WIDTH)).astype(BF16)

        ga_ref[r0:r0 + nr, :] = _silu(z_a).astype(BF16)

    for blk in range(tm // PROJ_ROWS):
        rows_block(blk * PROJ_ROWS, PROJ_ROWS)
    carry_ref[0:SUBLANES, :] = carry_ref[tm:tm + SUBLANES, :]


def _attn_kernel(qn_ref, kn_ref, kc_ref, qc_ref, diag_n_ref, first_c_ref, last_c_ref,
                 qt_ref, k_ref, vt_ref, o_ref, s_ref, mx_ref, m_ref, acc_ref):
    step = pl.program_id(0)
    tq = qt_ref.shape[1]
    tk = k_ref.shape[0]

    @pl.when(step == 0)
    def _():
        s_ref[...] = jnp.zeros(s_ref.shape, F32)
        mx_ref[...] = jnp.zeros(mx_ref.shape, F32)

    @pl.when((first_c_ref[step] == 1) | (step == 0))
    def _():
        m_ref[...] = jnp.full(m_ref.shape, NEG_INF, F32)
        acc_ref[...] = jnp.zeros(acc_ref.shape, F32)

    nchunk = tq // ATTN_Q_CHUNK
    half = tk // 2

    def body(masked):
        if masked:
            assert nchunk == 2 and half == ATTN_Q_CHUNK
            tri = (lax.broadcasted_iota(jnp.int32, (half, ATTN_Q_CHUNK), 0)
                   <= lax.broadcasted_iota(jnp.int32, (half, ATTN_Q_CHUNK), 1))

        def scores(hd, c):
            cols = slice(c * ATTN_Q_CHUNK, (c + 1) * ATTN_Q_CHUNK)
            qt = qt_ref[hd * HEAD_PAD:(hd + 1) * HEAD_PAD, cols]
            if not masked:
                s = _dot(k_ref[:, hd * HEAD_PAD:(hd + 1) * HEAD_PAD], qt)
                s_ref[hd, :, cols] = s
                mx_ref[hd, :, cols] = jnp.max(s, axis=0, keepdims=True)
            elif c == 0:
                s = jnp.where(tri, _dot(k_ref[0:half, hd * HEAD_PAD:(hd + 1) * HEAD_PAD], qt), NEG_INF)
                s_ref[hd, 0:half, cols] = s
                s_ref[hd, half:tk, cols] = jnp.full((half, ATTN_Q_CHUNK), NEG_INF, F32)
                mx_ref[hd, :, cols] = jnp.max(s, axis=0, keepdims=True)
            else:
                s = _dot(k_ref[:, hd * HEAD_PAD:(hd + 1) * HEAD_PAD], qt)
                top = s[0:half]
                bot = jnp.where(tri, s[half:tk], NEG_INF)
                s_ref[hd, 0:half, cols] = top
                s_ref[hd, half:tk, cols] = bot
                mx_ref[hd, :, cols] = jnp.maximum(jnp.max(top, axis=0, keepdims=True),
                                                  jnp.max(bot, axis=0, keepdims=True))

        def softmax_pv(hd, c):
            cols = slice(c * ATTN_Q_CHUNK, (c + 1) * ATTN_Q_CHUNK)
            m_prev = m_ref[hd, :, cols]
            m_new = jnp.maximum(m_prev, mx_ref[hd, :, cols])
            alpha = jnp.exp2(m_prev - m_new)
            p = jnp.exp2((s_ref[hd, :, cols] - m_new).astype(BF16))
            pv = _dot(vt_ref[hd * V_ROWS:(hd + 1) * V_ROWS, :], p)
            acc_ref[hd, :, cols] = alpha * acc_ref[hd, :, cols] + pv
            m_ref[hd, :, cols] = m_new

        for hd in range(N_HEADS):
            for c in range(nchunk):
                softmax_pv(hd, c)
                scores(hd, c)

    is_diag = diag_n_ref[step] == 1

    @pl.when(jnp.logical_not(is_diag))
    def _():
        body(False)

    @pl.when(is_diag)
    def _():
        body(True)

    @pl.when(last_c_ref[step] == 1)
    def _():
        for hd in range(N_HEADS):
            acc = acc_ref[hd]
            o_ref[:, hd * V_DIM:(hd + 1) * V_DIM] = (
                acc[:V_DIM] / acc[V_DIM:V_DIM + 1]).T.astype(o_ref.dtype)


def _out_kernel(o_ref, ga_ref, yc_ref, x_ref, p_ref, w_o_ref, w_plg_ref, w_pl_ref, out_ref):
    tm = x_ref.shape[0]

    def residual(r0):
        rows = slice(r0, r0 + OUT_ROWS)
        ya = o_ref[rows, :].astype(F32) * ga_ref[rows, :].astype(F32)
        ya_n = (ya * _rms_scale(ya, ATTN_WIDTH)).astype(BF16)
        x1 = (x_ref[rows, :] + _dot(ya_n, w_o_ref[:ATTN_WIDTH, :])
              + _dot(yc_ref[rows, :], w_o_ref[ATTN_WIDTH:, :]))
        ple = _dot(p_ref[rows, :].astype(BF16), w_pl_ref[...])
        hn = (x1 * _rms_scale(x1, D_MODEL)).astype(BF16)
        return r0, x1, ple, hn

    def gated(r0, x1, ple, hn):
        gate = 1.0 / (1.0 + jnp.exp(-_dot(hn, w_plg_ref[...])))
        out_ref[r0:r0 + OUT_ROWS, :] = x1 + gate * ple

    prev = None
    for r0 in range(0, tm, OUT_ROWS):
        cur = residual(r0)
        if prev is not None:
            gated(*prev)
        prev = cur
    gated(*prev)


def _row_spec(tile, width):
    return pl.BlockSpec((tile, width), lambda i: (i, 0))


def _const_spec(shape):
    return pl.BlockSpec(shape, lambda i: (0,) * len(shape))


def _triangle_schedule(batch, n_blk):
    pairs = [(b * n_blk + i, b * n_blk + j, j == 0, j == i)
             for b in range(batch) for i in range(n_blk) for j in range(i + 1)]
    nxt = pairs + [pairs[-1]]
    cur = [pairs[0]] + pairs
    cols = ([p[0] for p in nxt], [p[1] for p in nxt], [p[1] for p in cur], [p[0] for p in cur],
            [int(p[3]) for p in nxt],
            [0] + [int(p[2]) for p in pairs], [0] + [int(p[3]) for p in pairs])
    return tuple(jnp.asarray(np.asarray(c, np.int32)) for c in cols)


def _prep_in_kernel(wt_lat_ref, wt_ref, g_ref, lat_ref, wide_ref):
    g = g_ref[...]

    @pl.when(pl.program_id(0) == 0)
    def _():
        pad = jnp.zeros((LAT_PAD - LAT_SRC, D_MODEL), F32)
        lat_ref[...] = jnp.concatenate([wt_lat_ref[...] * g, pad], axis=0).T.astype(BF16)

    wide_ref[...] = (wt_ref[...] * g).T.astype(BF16)


def _prep_kernel(w_uq_ref, g_cq_ref, w_ukv_ref, g_ckv_ref, w_o_ref, g_o_ref, w_plg_ref, g_pl_ref, w_pl_ref,
                 uq_ref, ukv_ref, o_ref, plg_ref, pl_ref):
    w_uq = w_uq_ref[...] * g_cq_ref[...]
    for hd in range(N_HEADS):
        uq_ref[:, hd * HEAD_PAD:hd * HEAD_PAD + QK_DIM] = w_uq[:, hd * QK_DIM:(hd + 1) * QK_DIM].astype(BF16)
        uq_ref[:, hd * HEAD_PAD + QK_DIM:(hd + 1) * HEAD_PAD] = jnp.zeros(
            (uq_ref.shape[0], HEAD_PAD - QK_DIM), BF16)
    ukv_ref[...] = (w_ukv_ref[...] * g_ckv_ref[...]).astype(BF16)
    o_ref[...] = (w_o_ref[...] * g_o_ref[...]).astype(BF16)
    plg_ref[...] = (w_plg_ref[...] * g_pl_ref[...]).astype(BF16)
    pl_ref[...] = w_pl_ref[...].astype(BF16)


def _prepare_weights(w_in, g_in, w_uq, g_cq, w_ukv, g_ckv, w_o, g_o, w_plg, g_pl, w_pl):
    cparams = pltpu.CompilerParams(dimension_semantics=("arbitrary",), vmem_limit_bytes=VMEM_LIMIT)

    def rows_at(nrows, start):
        return pl.BlockSpec((pl.Element(nrows), pl.Element(D_MODEL)), start)

    w_lat, w_wide = pl.pallas_call(
        _prep_in_kernel,
        grid=(WIDE // PREP_COLS,),
        in_specs=[rows_at(LAT_SRC, lambda j: (0, 0)),
                  rows_at(PREP_COLS, lambda j: (pl.multiple_of(LAT_SRC + j * PREP_COLS, SUBLANES), 0)),
                  _const_spec((1, D_MODEL))],
        out_specs=[_const_spec((D_MODEL, LAT_PAD)), pl.BlockSpec((D_MODEL, PREP_COLS), lambda j: (0, j))],
        out_shape=[jax.ShapeDtypeStruct((D_MODEL, LAT_PAD), BF16),
                   jax.ShapeDtypeStruct((D_MODEL, WIDE), BF16)],
        compiler_params=cparams,
        name="mla_conv_w_in_prep",
    )(w_in.T, w_in.T, g_in.reshape(1, -1))

    def blk(rows, width):
        return pl.BlockSpec((rows // PREP_STEPS, width), lambda i: (i, 0))

    def weight_and_gain(w):
        return [blk(w.shape[0], w.shape[1]), blk(w.shape[0], 1)]

    kv_width = N_HEADS * (NOPE_DIM + V_DIM)
    rest = pl.pallas_call(
        _prep_kernel,
        grid=(PREP_STEPS,),
        in_specs=(weight_and_gain(w_uq) + weight_and_gain(w_ukv) + weight_and_gain(w_o)
                  + weight_and_gain(w_plg) + [blk(PLE_DIM, D_MODEL)]),
        out_specs=[blk(Q_LORA, N_HEADS * HEAD_PAD), blk(KV_LORA, kv_width), blk(D_MODEL, D_MODEL),
                   blk(D_MODEL, D_MODEL), blk(PLE_DIM, D_MODEL)],
        out_shape=[jax.ShapeDtypeStruct(shape, BF16) for shape in (
            (Q_LORA, N_HEADS * HEAD_PAD), (KV_LORA, kv_width), (D_MODEL, D_MODEL), (D_MODEL, D_MODEL),
            (PLE_DIM, D_MODEL))],
        compiler_params=cparams,
        name="mla_conv_weight_prep",
    )(w_uq, g_cq.reshape(-1, 1), w_ukv, g_ckv.reshape(-1, 1), w_o, g_o.reshape(-1, 1),
      w_plg, g_pl.reshape(-1, 1), w_pl)
    return (w_lat, w_wide, *rest)


def _layer(x2, p2, posc, invf, batch, seq, g_in, w_in, g_cq, w_uq, g_ckv, w_ukv, g_q, g_k,
           conv_w, g_oa, g_oc, w_o, w_pl, w_plg, g_pl):
    tokens = batch * seq
    tm = PROJ_TILE
    assert seq % tm == 0 and seq % ATTN_TILE == 0

    w_lat, w_wide, w_uq_p, w_ukv_b, w_o_b, w_plg_b, w_pl_b = _prepare_weights(
        w_in, g_in, w_uq, g_cq, w_ukv, g_ckv, w_o, jnp.concatenate([g_oa, g_oc]), w_plg, g_pl, w_pl)
    g_q_p = jnp.pad(g_q, (0, HEAD_PAD - QK_DIM)).reshape(1, HEAD_PAD)
    g_kn = g_k[:NOPE_DIM].reshape(1, NOPE_DIM)
    g_kr = jnp.pad(g_k[NOPE_DIM:], (0, KPE_PAD - ROPE_DIM)).reshape(1, KPE_PAD)

    cparams = pltpu.CompilerParams(dimension_semantics=("arbitrary",), vmem_limit_bytes=VMEM_LIMIT)

    def col_spec(rows, tile):
        return pl.BlockSpec((rows, tile), lambda i: (0, i))

    qt, k, vt, ga, yc = pl.pallas_call(
        functools.partial(_proj_kernel, seq // tm),
        grid=(tokens // tm,),
        in_specs=[
            _row_spec(tm, D_MODEL), _row_spec(tm // TOKENS_PER_ROW, LANES), _const_spec((1, LANES)),
            _const_spec((D_MODEL, LAT_PAD)), _const_spec((D_MODEL, WIDE)),
            _const_spec((Q_LORA, N_HEADS * HEAD_PAD)),
            _const_spec((KV_LORA, N_HEADS * (NOPE_DIM + V_DIM))),
            _const_spec((1, HEAD_PAD)), _const_spec((1, NOPE_DIM)), _const_spec((1, KPE_PAD)),
            _const_spec((CONV_K, CONV_WIDTH)),
        ],
        out_specs=[
            col_spec(N_HEADS * HEAD_PAD, tm), _row_spec(tm, N_HEADS * HEAD_PAD),
            col_spec(N_HEADS * V_ROWS, tm), _row_spec(tm, ATTN_WIDTH), _row_spec(tm, CONV_WIDTH),
        ],
        out_shape=[
            jax.ShapeDtypeStruct((N_HEADS * HEAD_PAD, tokens), BF16),
            jax.ShapeDtypeStruct((tokens, N_HEADS * HEAD_PAD), BF16),
            jax.ShapeDtypeStruct((N_HEADS * V_ROWS, tokens), BF16),
            jax.ShapeDtypeStruct((tokens, ATTN_WIDTH), BF16),
            jax.ShapeDtypeStruct((tokens, CONV_WIDTH), BF16),
        ],
        scratch_shapes=[pltpu.VMEM((tm + SUBLANES, CONV_WIDTH), F32)],
        compiler_params=cparams,
        name="mla_conv_proj",
    )(x2, posc, invf, w_lat, w_wide, w_uq_p, w_ukv_b, g_q_p, g_kn, g_kr, conv_w)

    ta = ATTN_TILE
    sched = _triangle_schedule(batch, seq // ta)
    o = pl.pallas_call(
        _attn_kernel,
        grid_spec=pltpu.PrefetchScalarGridSpec(
            num_scalar_prefetch=len(sched),
            grid=(int(sched[0].shape[0]),),
            in_specs=[
                pl.BlockSpec((N_HEADS * HEAD_PAD, ta), lambda s, qn, kn, kc, qc, *_: (0, qn[s])),
                pl.BlockSpec((ta, N_HEADS * HEAD_PAD), lambda s, qn, kn, kc, qc, *_: (kn[s], 0)),
                pl.BlockSpec((N_HEADS * V_ROWS, ta), lambda s, qn, kn, kc, qc, *_: (0, kc[s])),
            ],
            out_specs=pl.BlockSpec((ta, ATTN_WIDTH), lambda s, qn, kn, kc, qc, *_: (qc[s], 0)),
            scratch_shapes=[pltpu.VMEM((N_HEADS, ta, ta), F32),
                            pltpu.VMEM((N_HEADS, 1, ta), F32),
                            pltpu.VMEM((N_HEADS, 1, ta), F32),
                            pltpu.VMEM((N_HEADS, V_ROWS, ta), F32)],
        ),
        out_shape=jax.ShapeDtypeStruct((tokens, ATTN_WIDTH), BF16),
        compiler_params=cparams,
        name="mla_flash_attn",
    )(*sched, qt, k, vt)

    return pl.pallas_call(
        _out_kernel,
        grid=(tokens // tm,),
        in_specs=[
            _row_spec(tm, ATTN_WIDTH), _row_spec(tm, ATTN_WIDTH), _row_spec(tm, CONV_WIDTH),
            _row_spec(tm, D_MODEL), _row_spec(tm, PLE_DIM),
            _const_spec((D_MODEL, D_MODEL)), _const_spec((D_MODEL, D_MODEL)), _const_spec((PLE_DIM, D_MODEL)),
        ],
        out_specs=_row_spec(tm, D_MODEL),
        out_shape=jax.ShapeDtypeStruct((tokens, D_MODEL), F32),
        compiler_params=cparams,
        name="mla_conv_out",
    )(o, ga, yc, x2, p2, w_o_b, w_plg_b, w_pl_b)


def kernel(x, p, positions, g_in, w_in, g_cq, w_uq, g_ckv, w_ukv, g_q, g_k, conv_w, g_oa, g_oc,
           w_o, w_pl, w_plg, g_pl):
    batch, seq, d_model = x.shape
    depth = p.shape[0]
    tokens = batch * seq
    rows = PROJ_TILE // TOKENS_PER_ROW
    posc = jnp.repeat(positions.astype(F32).reshape(tokens // PROJ_TILE, TOKENS_PER_ROW, rows)
                      .transpose(0, 2, 1), HALF_ROPE, axis=-1).reshape(tokens // TOKENS_PER_ROW, LANES)
    inv_freq = 1.0 / (ROPE_THETA ** (jnp.arange(0, ROPE_DIM, 2, dtype=F32) / ROPE_DIM))
    invf = jnp.tile(inv_freq, LANES // HALF_ROPE).reshape(1, LANES)
    h = x.reshape(tokens, d_model)
    for i in range(depth):
        h = _layer(h, p[i].reshape(tokens, PLE_DIM), posc, invf, batch, seq,
                   g_in[i], w_in[i], g_cq[i], w_uq[i], g_ckv[i], w_ukv[i], g_q[i], g_k[i],
                   conv_w[i], g_oa[i], g_oc[i], w_o[i], w_pl[i], w_plg[i], g_pl[i])
    return h.reshape(batch, seq, d_model).astype(x.dtype)
```

```python
import functools
import math

import jax
import jax.numpy as jnp
import numpy as np
from jax import lax
from jax.experimental import pallas as pl
from jax.experimental.pallas import tpu as pltpu

D_MODEL = 1024
PLE_DIM = 256
N_HEADS = 4
NOPE_DIM = 128
ROPE_DIM = 64
HALF_ROPE = ROPE_DIM // 2
V_DIM = 128
QK_DIM = NOPE_DIM + ROPE_DIM
Q_LORA = 256
KV_LORA = 128
ATTN_WIDTH = N_HEADS * V_DIM
CONV_WIDTH = D_MODEL - ATTN_WIDTH
CONV_K = 3
ROPE_THETA = 10000.0
RMS_EPS = 1e-6
NEG_INF = -1e30

LANES = 128
SUBLANES = 8
HEAD_PAD = 2 * LANES
KPE_PAD = LANES
BF16_ROWS = 16
V_ROWS = V_DIM + BF16_ROWS
TOKENS_PER_ROW = LANES // HALF_ROPE
OFF_CQ = 0
OFF_CKV = OFF_CQ + Q_LORA
OFF_KPE = OFF_CKV + KV_LORA
LAT_SRC = OFF_KPE + ROPE_DIM
LAT_PAD = OFF_KPE + KPE_PAD
OFF_ZA = 0
OFF_CB = OFF_ZA + ATTN_WIDTH
OFF_CC = OFF_CB + CONV_WIDTH
OFF_CX = OFF_CC + CONV_WIDTH
OFF_ZC = OFF_CX + CONV_WIDTH
WIDE = OFF_ZC + CONV_WIDTH

PROJ_TILE = 1024
PROJ_ROWS = 256
OUT_ROWS = 256
PREP_COLS = 256
ATTN_TILE = 1024
ATTN_Q_CHUNK = 512
VMEM_LIMIT = 48 * 1024 * 1024

BF16 = jnp.bfloat16
F32 = jnp.float32


def _rms_scale(v, width):
    return lax.rsqrt(jnp.sum(v * v, axis=-1, keepdims=True) * (1.0 / width) + RMS_EPS)


def _silu(z):
    return z / (1.0 + jnp.exp(-z))


def _dot(a, b):
    return jnp.dot(a, b, preferred_element_type=F32)


def _rope_tables(posc, invf):
    ang = posc * invf
    cosc = jnp.cos(ang)
    sinc = jnp.sin(ang)
    lane = lax.broadcasted_iota(jnp.int32, (1, LANES), 1)
    lo = lane < HALF_ROPE
    mid = (lane >= HALF_ROPE) & (lane < ROPE_DIM)
    quarters = []
    for a in range(TOKENS_PER_ROW):
        c = pltpu.roll(cosc, LANES - HALF_ROPE * a, 1) if a else cosc
        s = pltpu.roll(sinc, LANES - HALF_ROPE * a, 1) if a else sinc
        quarters.append((jnp.where(lo, c, jnp.where(mid, pltpu.roll(c, HALF_ROPE, 1), 0.0)),
                         jnp.where(lo, -s, jnp.where(mid, pltpu.roll(s, HALF_ROPE, 1), 0.0))))
    return quarters, lo


def _rope(t, c, s, lo):
    swapped = jnp.where(lo, pltpu.roll(t, LANES - HALF_ROPE, 1), pltpu.roll(t, HALF_ROPE, 1))
    return t * c + swapped * s


def _proj_kernel(tiles_per_seq,
                 x_ref, posc_ref, invf_ref, w_lat_ref, w_in_ref, g_cq_ref, w_uq_ref,
                 g_ckv_ref, w_ukv_ref, g_q_ref, g_kn_ref, g_kr_ref, conv_w_ref, g_oc_ref,
                 qt_ref, k_ref, vt_ref, ga_ref, yc_ref, carry_ref):
    tm = x_ref.shape[0]
    quarter_rows = tm // TOKENS_PER_ROW
    tables = []
    q_scale = math.log2(math.e) / math.sqrt(QK_DIM)

    @pl.when(pl.program_id(0) % tiles_per_seq == 0)
    def _():
        carry_ref[0:SUBLANES, :] = jnp.zeros((SUBLANES, CONV_WIDTH), F32)

    def rows_block(r0, nr):
        x = x_ref[r0:r0 + nr, :]
        h = (x * _rms_scale(x, D_MODEL)).astype(BF16)

        def proj(off, width):
            return _dot(h, w_in_ref[:, off:off + width])

        lat = _dot(h, w_lat_ref[...])
        c_q = lat[:, OFF_CQ:OFF_CKV]
        c_kv = lat[:, OFF_CKV:OFF_KPE]
        kpe = lat[:, OFF_KPE:LAT_PAD]
        cv = proj(OFF_CC, CONV_WIDTH) * proj(OFF_CX, CONV_WIDTH)
        cqn = (c_q * _rms_scale(c_q, Q_LORA) * g_cq_ref[...]).astype(BF16)
        qf = _dot(cqn, w_uq_ref[...])
        cb = proj(OFF_CB, CONV_WIDTH)
        ckvn = (c_kv * _rms_scale(c_kv, KV_LORA) * g_ckv_ref[...]).astype(BF16)
        kv = _dot(ckvn, w_ukv_ref[...])
        z_c = proj(OFF_ZC, CONV_WIDTH)
        z_a = proj(OFF_ZA, ATTN_WIDTH)

        if not tables:
            tables.extend(_rope_tables(posc_ref[...], invf_ref[...]))
        quarters, lo = tables
        qa = r0 // quarter_rows
        cos_t = jnp.concatenate([quarters[qa + i][0] for i in range(nr // quarter_rows)], axis=0)
        sin_t = jnp.concatenate([quarters[qa + i][1] for i in range(nr // quarter_rows)], axis=0)

        for hd in range(N_HEADS):
            qh = qf[:, hd * HEAD_PAD:(hd + 1) * HEAD_PAD]
            qn = qh * (_rms_scale(qh, QK_DIM) * q_scale) * g_q_ref[...]
            qt_ref[hd * HEAD_PAD:hd * HEAD_PAD + LANES, r0:r0 + nr] = qn[:, :LANES].T.astype(BF16)
            qt_ref[hd * HEAD_PAD + LANES:(hd + 1) * HEAD_PAD, r0:r0 + nr] = (
                _rope(qn[:, LANES:], cos_t, sin_t, lo).T.astype(BF16))

        ss_kpe = jnp.sum(kpe * kpe, axis=-1, keepdims=True)
        k_rot = _rope(kpe * g_kr_ref[...], cos_t, sin_t, lo)
        for hd in range(N_HEADS):
            kn = kv[:, hd * (NOPE_DIM + V_DIM):hd * (NOPE_DIM + V_DIM) + NOPE_DIM]
            vh = kv[:, hd * (NOPE_DIM + V_DIM) + NOPE_DIM:(hd + 1) * (NOPE_DIM + V_DIM)]
            ss = jnp.sum(kn * kn, axis=-1, keepdims=True) + ss_kpe
            rs = lax.rsqrt(ss * (1.0 / QK_DIM) + RMS_EPS)
            k_ref[r0:r0 + nr, hd * HEAD_PAD:hd * HEAD_PAD + LANES] = (kn * rs * g_kn_ref[...]).astype(BF16)
            k_ref[r0:r0 + nr, hd * HEAD_PAD + LANES:(hd + 1) * HEAD_PAD] = (k_rot * rs).astype(BF16)
            vt_ref[hd * V_ROWS:hd * V_ROWS + V_DIM, r0:r0 + nr] = vh.T.astype(BF16)
            vt_ref[hd * V_ROWS + V_DIM:(hd + 1) * V_ROWS, r0:r0 + nr] = jnp.ones((BF16_ROWS, nr), BF16)

        base = SUBLANES + r0
        carry_ref[base:base + nr, :] = cv
        u = (conv_w_ref[2:3, :] * cv
             + conv_w_ref[1:2, :] * carry_ref[base - 1:base - 1 + nr, :]
             + conv_w_ref[0:1, :] * carry_ref[base - 2:base - 2 + nr, :])
        yc = cb * u * _silu(z_c)
        yc_ref[r0:r0 + nr, :] = (yc * _rms_scale(yc, CONV_WIDTH) * g_oc_ref[...]).astype(BF16)

        ga_ref[r0:r0 + nr, :] = _silu(z_a).astype(BF16)

    for blk in range(tm // PROJ_ROWS):
        rows_block(blk * PROJ_ROWS, PROJ_ROWS)
    carry_ref[0:SUBLANES, :] = carry_ref[tm:tm + SUBLANES, :]


def _attn_kernel(qn_ref, kn_ref, kc_ref, qc_ref, diag_n_ref, first_c_ref, last_c_ref,
                 qt_ref, k_ref, vt_ref, o_ref, s_ref, mx_ref, m_ref, acc_ref):
    step = pl.program_id(0)
    tq = qt_ref.shape[1]
    tk = k_ref.shape[0]

    @pl.when(step == 0)
    def _():
        s_ref[...] = jnp.zeros(s_ref.shape, F32)
        mx_ref[...] = jnp.zeros(mx_ref.shape, F32)

    @pl.when((first_c_ref[step] == 1) | (step == 0))
    def _():
        m_ref[...] = jnp.full(m_ref.shape, NEG_INF, F32)
        acc_ref[...] = jnp.zeros(acc_ref.shape, F32)

    nchunk = tq // ATTN_Q_CHUNK
    half = tk // 2

    def body(masked):
        if masked:
            assert nchunk == 2 and half == ATTN_Q_CHUNK
            tri = (lax.broadcasted_iota(jnp.int32, (half, ATTN_Q_CHUNK), 0)
                   <= lax.broadcasted_iota(jnp.int32, (half, ATTN_Q_CHUNK), 1))

        def scores(hd, c):
            cols = slice(c * ATTN_Q_CHUNK, (c + 1) * ATTN_Q_CHUNK)
            qt = qt_ref[hd * HEAD_PAD:(hd + 1) * HEAD_PAD, cols]
            if not masked:
                s = _dot(k_ref[:, hd * HEAD_PAD:(hd + 1) * HEAD_PAD], qt)
                s_ref[hd, :, cols] = s
                mx_ref[hd, :, cols] = jnp.max(s, axis=0, keepdims=True)
            elif c == 0:
                s = jnp.where(tri, _dot(k_ref[0:half, hd * HEAD_PAD:(hd + 1) * HEAD_PAD], qt), NEG_INF)
                s_ref[hd, 0:half, cols] = s
                s_ref[hd, half:tk, cols] = jnp.full((half, ATTN_Q_CHUNK), NEG_INF, F32)
                mx_ref[hd, :, cols] = jnp.max(s, axis=0, keepdims=True)
            else:
                s = _dot(k_ref[:, hd * HEAD_PAD:(hd + 1) * HEAD_PAD], qt)
                top = s[0:half]
                bot = jnp.where(tri, s[half:tk], NEG_INF)
                s_ref[hd, 0:half, cols] = top
                s_ref[hd, half:tk, cols] = bot
                mx_ref[hd, :, cols] = jnp.maximum(jnp.max(top, axis=0, keepdims=True),
                                                  jnp.max(bot, axis=0, keepdims=True))

        def softmax_pv(hd, c):
            cols = slice(c * ATTN_Q_CHUNK, (c + 1) * ATTN_Q_CHUNK)
            m_prev = m_ref[hd, :, cols]
            m_new = jnp.maximum(m_prev, mx_ref[hd, :, cols])
            alpha = jnp.exp2(m_prev - m_new)
            p = jnp.exp2((s_ref[hd, :, cols] - m_new).astype(BF16))
            pv = _dot(vt_ref[hd * V_ROWS:(hd + 1) * V_ROWS, :], p)
            acc_ref[hd, :, cols] = alpha * acc_ref[hd, :, cols] + pv
            m_ref[hd, :, cols] = m_new

        for hd in range(N_HEADS):
            for c in range(nchunk):
                softmax_pv(hd, c)
                scores(hd, c)

    is_diag = diag_n_ref[step] == 1

    @pl.when(jnp.logical_not(is_diag))
    def _():
        body(False)

    @pl.when(is_diag)
    def _():
        body(True)

    @pl.when(last_c_ref[step] == 1)
    def _():
        for hd in range(N_HEADS):
            acc = acc_ref[hd]
            o_ref[:, hd * V_DIM:(hd + 1) * V_DIM] = (
                acc[:V_DIM] / acc[V_DIM:V_DIM + 1]).T.astype(o_ref.dtype)


def _out_kernel(o_ref, ga_ref, yc_ref, x_ref, p_ref, g_oa_ref, w_o_ref, g_pl_ref, w_plg_ref, w_pl_ref,
                out_ref):
    tm = x_ref.shape[0]

    def residual(r0):
        rows = slice(r0, r0 + OUT_ROWS)
        ya = o_ref[rows, :].astype(F32) * ga_ref[rows, :].astype(F32)
        ya_n = (ya * _rms_scale(ya, ATTN_WIDTH) * g_oa_ref[...]).astype(BF16)
        x1 = (x_ref[rows, :] + _dot(ya_n, w_o_ref[:ATTN_WIDTH, :])
              + _dot(yc_ref[rows, :], w_o_ref[ATTN_WIDTH:, :]))
        ple = _dot(p_ref[rows, :].astype(BF16), w_pl_ref[...])
        hn = (x1 * _rms_scale(x1, D_MODEL) * g_pl_ref[...]).astype(BF16)
        return r0, x1, ple, hn

    def gated(r0, x1, ple, hn):
        gate = 1.0 / (1.0 + jnp.exp(-_dot(hn, w_plg_ref[...])))
        out_ref[r0:r0 + OUT_ROWS, :] = x1 + gate * ple

    prev = None
    for r0 in range(0, tm, OUT_ROWS):
        cur = residual(r0)
        if prev is not None:
            gated(*prev)
        prev = cur
    gated(*prev)


def _row_spec(tile, width):
    return pl.BlockSpec((tile, width), lambda i: (i, 0))


def _const_spec(shape):
    return pl.BlockSpec(shape, lambda i: (0,) * len(shape))


def _triangle_schedule(batch, n_blk):
    pairs = [(b * n_blk + i, b * n_blk + j, j == 0, j == i)
             for b in range(batch) for i in range(n_blk) for j in range(i + 1)]
    nxt = pairs + [pairs[-1]]
    cur = [pairs[0]] + pairs
    cols = ([p[0] for p in nxt], [p[1] for p in nxt], [p[1] for p in cur], [p[0] for p in cur],
            [int(p[3]) for p in nxt],
            [0] + [int(p[2]) for p in pairs], [0] + [int(p[3]) for p in pairs])
    return tuple(jnp.asarray(np.asarray(c, np.int32)) for c in cols)


def _prep_in_kernel(wt_lat_ref, wt_ref, g_ref, lat_ref, wide_ref):
    g = g_ref[...]

    @pl.when(pl.program_id(0) == 0)
    def _():
        pad = jnp.zeros((LAT_PAD - LAT_SRC, D_MODEL), F32)
        lat_ref[...] = jnp.concatenate([wt_lat_ref[...] * g, pad], axis=0).T.astype(BF16)

    wide_ref[...] = (wt_ref[...] * g).T.astype(BF16)


def _prepare_w_in(w_in, g_in):
    def rows_at(nrows, start):
        return pl.BlockSpec((pl.Element(nrows), pl.Element(D_MODEL)), start)

    return pl.pallas_call(
        _prep_in_kernel,
        grid=(WIDE // PREP_COLS,),
        in_specs=[rows_at(LAT_SRC, lambda j: (0, 0)),
                  rows_at(PREP_COLS, lambda j: (pl.multiple_of(LAT_SRC + j * PREP_COLS, SUBLANES), 0)),
                  _const_spec((1, D_MODEL))],
        out_specs=[_const_spec((D_MODEL, LAT_PAD)), pl.BlockSpec((D_MODEL, PREP_COLS), lambda j: (0, j))],
        out_shape=[jax.ShapeDtypeStruct((D_MODEL, LAT_PAD), BF16),
                   jax.ShapeDtypeStruct((D_MODEL, WIDE), BF16)],
        compiler_params=pltpu.CompilerParams(dimension_semantics=("arbitrary",),
                                             vmem_limit_bytes=VMEM_LIMIT),
        name="mla_conv_w_in_prep",
    )(w_in.T, w_in.T, g_in.reshape(1, -1))


def _layer(x2, p2, posc, invf, batch, seq, g_in, w_in, g_cq, w_uq, g_ckv, w_ukv, g_q, g_k,
           conv_w, g_oa, g_oc, w_o, w_pl, w_plg, g_pl):
    tokens = batch * seq
    tm = PROJ_TILE
    assert seq % tm == 0 and seq % ATTN_TILE == 0

    w_lat, w_wide = _prepare_w_in(w_in, g_in)
    w_uq_p = jnp.pad(w_uq.reshape(Q_LORA, N_HEADS, QK_DIM),
                     ((0, 0), (0, 0), (0, HEAD_PAD - QK_DIM))).reshape(Q_LORA, N_HEADS * HEAD_PAD).astype(BF16)
    g_q_p = jnp.pad(g_q, (0, HEAD_PAD - QK_DIM)).reshape(1, HEAD_PAD)
    g_kn = g_k[:NOPE_DIM].reshape(1, NOPE_DIM)
    g_kr = jnp.pad(g_k[NOPE_DIM:], (0, KPE_PAD - ROPE_DIM)).reshape(1, KPE_PAD)

    cparams = pltpu.CompilerParams(dimension_semantics=("arbitrary",), vmem_limit_bytes=VMEM_LIMIT)

    def col_spec(rows, tile):
        return pl.BlockSpec((rows, tile), lambda i: (0, i))

    qt, k, vt, ga, yc = pl.pallas_call(
        functools.partial(_proj_kernel, seq // tm),
        grid=(tokens // tm,),
        in_specs=[
            _row_spec(tm, D_MODEL), _row_spec(tm // TOKENS_PER_ROW, LANES), _const_spec((1, LANES)),
            _const_spec((D_MODEL, LAT_PAD)), _const_spec((D_MODEL, WIDE)),
            _const_spec((1, Q_LORA)), _const_spec((Q_LORA, N_HEADS * HEAD_PAD)),
            _const_spec((1, KV_LORA)), _const_spec((KV_LORA, N_HEADS * (NOPE_DIM + V_DIM))),
            _const_spec((1, HEAD_PAD)), _const_spec((1, NOPE_DIM)), _const_spec((1, KPE_PAD)),
            _const_spec((CONV_K, CONV_WIDTH)), _const_spec((1, CONV_WIDTH)),
        ],
        out_specs=[
            col_spec(N_HEADS * HEAD_PAD, tm), _row_spec(tm, N_HEADS * HEAD_PAD),
            col_spec(N_HEADS * V_ROWS, tm), _row_spec(tm, ATTN_WIDTH), _row_spec(tm, CONV_WIDTH),
        ],
        out_shape=[
            jax.ShapeDtypeStruct((N_HEADS * HEAD_PAD, tokens), BF16),
            jax.ShapeDtypeStruct((tokens, N_HEADS * HEAD_PAD), BF16),
            jax.ShapeDtypeStruct((N_HEADS * V_ROWS, tokens), BF16),
            jax.ShapeDtypeStruct((tokens, ATTN_WIDTH), BF16),
            jax.ShapeDtypeStruct((tokens, CONV_WIDTH), BF16),
        ],
        scratch_shapes=[pltpu.VMEM((tm + SUBLANES, CONV_WIDTH), F32)],
        compiler_params=cparams,
        name="mla_conv_proj",
    )(x2, posc, invf, w_lat, w_wide, g_cq.reshape(1, -1), w_uq_p,
      g_ckv.reshape(1, -1), w_ukv.astype(BF16), g_q_p, g_kn, g_kr, conv_w, g_oc.reshape(1, -1))

    ta = ATTN_TILE
    sched = _triangle_schedule(batch, seq // ta)
    o = pl.pallas_call(
        _attn_kernel,
        grid_spec=pltpu.PrefetchScalarGridSpec(
            num_scalar_prefetch=len(sched),
            grid=(int(sched[0].shape[0]),),
            in_specs=[
                pl.BlockSpec((N_HEADS * HEAD_PAD, ta), lambda s, qn, kn, kc, qc, *_: (0, qn[s])),
                pl.BlockSpec((ta, N_HEADS * HEAD_PAD), lambda s, qn, kn, kc, qc, *_: (kn[s], 0)),
                pl.BlockSpec((N_HEADS * V_ROWS, ta), lambda s, qn, kn, kc, qc, *_: (0, kc[s])),
            ],
            out_specs=pl.BlockSpec((ta, ATTN_WIDTH), lambda s, qn, kn, kc, qc, *_: (qc[s], 0)),
            scratch_shapes=[pltpu.VMEM((N_HEADS, ta, ta), F32),
                            pltpu.VMEM((N_HEADS, 1, ta), F32),
                            pltpu.VMEM((N_HEADS, 1, ta), F32),
                            pltpu.VMEM((N_HEADS, V_ROWS, ta), F32)],
        ),
        out_shape=jax.ShapeDtypeStruct((tokens, ATTN_WIDTH), BF16),
        compiler_params=cparams,
        name="mla_flash_attn",
    )(*sched, qt, k, vt)

    return pl.pallas_call(
        _out_kernel,
        grid=(tokens // tm,),
        in_specs=[
            _row_spec(tm, ATTN_WIDTH), _row_spec(tm, ATTN_WIDTH), _row_spec(tm, CONV_WIDTH),
            _row_spec(tm, D_MODEL), _row_spec(tm, PLE_DIM),
            _const_spec((1, ATTN_WIDTH)), _const_spec((D_MODEL, D_MODEL)), _const_spec((1, D_MODEL)),
            _const_spec((D_MODEL, D_MODEL)), _const_spec((PLE_DIM, D_MODEL)),
        ],
        out_specs=_row_spec(tm, D_MODEL),
        out_shape=jax.ShapeDtypeStruct((tokens, D_MODEL), F32),
        compiler_params=cparams,
        name="mla_conv_out",
    )(o, ga, yc, x2, p2, g_oa.reshape(1, -1), w_o.astype(BF16),
      g_pl.reshape(1, -1), w_plg.astype(BF16), w_pl.astype(BF16))


def kernel(x, p, positions, g_in, w_in, g_cq, w_uq, g_ckv, w_ukv, g_q, g_k, conv_w, g_oa, g_oc,
           w_o, w_pl, w_plg, g_pl):
    batch, seq, d_model = x.shape
    depth = p.shape[0]
    tokens = batch * seq
    rows = PROJ_TILE // TOKENS_PER_ROW
    posc = jnp.repeat(positions.astype(F32).reshape(tokens // PROJ_TILE, TOKENS_PER_ROW, rows)
                      .transpose(0, 2, 1), HALF_ROPE, axis=-1).reshape(tokens // TOKENS_PER_ROW, LANES)
    inv_freq = 1.0 / (ROPE_THETA ** (jnp.arange(0, ROPE_DIM, 2, dtype=F32) / ROPE_DIM))
    invf = jnp.tile(inv_freq, LANES // HALF_ROPE).reshape(1, LANES)
    h = x.reshape(tokens, d_model)
    for i in range(depth):
        h = _layer(h, p[i].reshape(tokens, PLE_DIM), posc, invf, batch, seq,
                   g_in[i], w_in[i], g_cq[i], w_uq[i], g_ckv[i], w_ukv[i], g_q[i], g_k[i],
                   conv_w[i], g_oa[i], g_oc[i], w_o[i], w_pl[i], w_plg[i], g_pl[i])
    return h.reshape(batch, seq, d_model).astype(x.dtype)
```

```python
import functools
import math

import jax
import jax.numpy as jnp
import numpy as np
from jax import lax
from jax.experimental import pallas as pl
from jax.experimental.pallas import tpu as pltpu

D_MODEL = 1024
PLE_DIM = 256
N_HEADS = 4
NOPE_DIM = 128
ROPE_DIM = 64
HALF_ROPE = ROPE_DIM // 2
V_DIM = 128
QK_DIM = NOPE_DIM + ROPE_DIM
Q_LORA = 256
KV_LORA = 128
ATTN_WIDTH = N_HEADS * V_DIM
CONV_WIDTH = D_MODEL - ATTN_WIDTH
CONV_K = 3
ROPE_THETA = 10000.0
RMS_EPS = 1e-6
NEG_INF = -1e30

LANES = 128
SUBLANES = 8
HEAD_PAD = 2 * LANES
KPE_PAD = LANES
BF16_ROWS = 16
V_ROWS = V_DIM + BF16_ROWS
TOKENS_PER_ROW = LANES // HALF_ROPE
OFF_CQ = 0
OFF_CKV = OFF_CQ + Q_LORA
OFF_KPE = OFF_CKV + KV_LORA
LAT_SRC = OFF_KPE + ROPE_DIM
LAT_PAD = OFF_KPE + KPE_PAD
OFF_ZA = 0
OFF_CB = OFF_ZA + ATTN_WIDTH
OFF_CC = OFF_CB + CONV_WIDTH
OFF_CX = OFF_CC + CONV_WIDTH
OFF_ZC = OFF_CX + CONV_WIDTH
WIDE = OFF_ZC + CONV_WIDTH

PROJ_TILE = 1024
PROJ_ROWS = 256
OUT_ROWS = 256
PREP_COLS = 512
ATTN_TILE = 1024
ATTN_Q_CHUNK = 512
VMEM_LIMIT = 48 * 1024 * 1024

BF16 = jnp.bfloat16
F32 = jnp.float32


def _rms_scale(v, width):
    return lax.rsqrt(jnp.sum(v * v, axis=-1, keepdims=True) * (1.0 / width) + RMS_EPS)


def _silu(z):
    h = 0.5 * z
    return h + h * jnp.tanh(h)


def _dot(a, b):
    return jnp.dot(a, b, preferred_element_type=F32)


def _rope_tables(posc, invf):
    ang = posc * invf
    cosc = jnp.cos(ang)
    sinc = jnp.sin(ang)
    lane = lax.broadcasted_iota(jnp.int32, (1, LANES), 1)
    lo = lane < HALF_ROPE
    mid = (lane >= HALF_ROPE) & (lane < ROPE_DIM)
    quarters = []
    for a in range(TOKENS_PER_ROW):
        c = pltpu.roll(cosc, LANES - HALF_ROPE * a, 1) if a else cosc
        s = pltpu.roll(sinc, LANES - HALF_ROPE * a, 1) if a else sinc
        quarters.append((jnp.where(lo, c, jnp.where(mid, pltpu.roll(c, HALF_ROPE, 1), 0.0)),
                         jnp.where(lo, -s, jnp.where(mid, pltpu.roll(s, HALF_ROPE, 1), 0.0))))
    return quarters, lo


def _rope(t, c, s, lo):
    swapped = jnp.where(lo, pltpu.roll(t, LANES - HALF_ROPE, 1), pltpu.roll(t, HALF_ROPE, 1))
    return t * c + swapped * s


def _proj_kernel(tiles_per_seq,
                 x_ref, posc_ref, invf_ref, w_lat_ref, w_in_ref, g_cq_ref, w_uq_ref,
                 g_ckv_ref, w_ukv_ref, g_q_ref, g_kn_ref, g_kr_ref, conv_w_ref, g_oc_ref,
                 qt_ref, k_ref, vt_ref, ga_ref, yc_ref, carry_ref):
    tm = x_ref.shape[0]
    quarter_rows = tm // TOKENS_PER_ROW
    tables = []
    q_scale = math.log2(math.e) / math.sqrt(QK_DIM)

    @pl.when(pl.program_id(0) % tiles_per_seq == 0)
    def _():
        carry_ref[0:SUBLANES, :] = jnp.zeros((SUBLANES, CONV_WIDTH), F32)

    def rows_block(r0, nr):
        x = x_ref[r0:r0 + nr, :]
        h = (x * _rms_scale(x, D_MODEL)).astype(BF16)

        def proj(off, width):
            return _dot(h, w_in_ref[:, off:off + width])

        lat = _dot(h, w_lat_ref[...])
        c_q = lat[:, OFF_CQ:OFF_CKV]
        c_kv = lat[:, OFF_CKV:OFF_KPE]
        kpe = lat[:, OFF_KPE:LAT_PAD]
        cv = proj(OFF_CC, CONV_WIDTH) * proj(OFF_CX, CONV_WIDTH)
        cqn = (c_q * _rms_scale(c_q, Q_LORA) * g_cq_ref[...]).astype(BF16)
        qf = _dot(cqn, w_uq_ref[...])
        cb = proj(OFF_CB, CONV_WIDTH)
        ckvn = (c_kv * _rms_scale(c_kv, KV_LORA) * g_ckv_ref[...]).astype(BF16)
        kv = _dot(ckvn, w_ukv_ref[...])
        z_c = proj(OFF_ZC, CONV_WIDTH)
        z_a = proj(OFF_ZA, ATTN_WIDTH)

        if not tables:
            tables.extend(_rope_tables(posc_ref[...], invf_ref[...]))
        quarters, lo = tables
        qa = r0 // quarter_rows
        cos_t = jnp.concatenate([quarters[qa + i][0] for i in range(nr // quarter_rows)], axis=0)
        sin_t = jnp.concatenate([quarters[qa + i][1] for i in range(nr // quarter_rows)], axis=0)

        for hd in range(N_HEADS):
            qh = qf[:, hd * HEAD_PAD:(hd + 1) * HEAD_PAD]
            qn = qh * (_rms_scale(qh, QK_DIM) * q_scale) * g_q_ref[...]
            qt_ref[hd * HEAD_PAD:hd * HEAD_PAD + LANES, r0:r0 + nr] = qn[:, :LANES].T.astype(BF16)
            qt_ref[hd * HEAD_PAD + LANES:(hd + 1) * HEAD_PAD, r0:r0 + nr] = (
                _rope(qn[:, LANES:], cos_t, sin_t, lo).T.astype(BF16))

        ss_kpe = jnp.sum(kpe * kpe, axis=-1, keepdims=True)
        k_rot = _rope(kpe * g_kr_ref[...], cos_t, sin_t, lo)
        for hd in range(N_HEADS):
            kn = kv[:, hd * (NOPE_DIM + V_DIM):hd * (NOPE_DIM + V_DIM) + NOPE_DIM]
            vh = kv[:, hd * (NOPE_DIM + V_DIM) + NOPE_DIM:(hd + 1) * (NOPE_DIM + V_DIM)]
            ss = jnp.sum(kn * kn, axis=-1, keepdims=True) + ss_kpe
            rs = lax.rsqrt(ss * (1.0 / QK_DIM) + RMS_EPS)
            k_ref[r0:r0 + nr, hd * HEAD_PAD:hd * HEAD_PAD + LANES] = (kn * rs * g_kn_ref[...]).astype(BF16)
            k_ref[r0:r0 + nr, hd * HEAD_PAD + LANES:(hd + 1) * HEAD_PAD] = (k_rot * rs).astype(BF16)
            vt_ref[hd * V_ROWS:hd * V_ROWS + V_DIM, r0:r0 + nr] = vh.T.astype(BF16)
            vt_ref[hd * V_ROWS + V_DIM:(hd + 1) * V_ROWS, r0:r0 + nr] = jnp.ones((BF16_ROWS, nr), BF16)

        base = SUBLANES + r0
        carry_ref[base:base + nr, :] = cv
        u = (conv_w_ref[2:3, :] * cv
             + conv_w_ref[1:2, :] * carry_ref[base - 1:base - 1 + nr, :]
             + conv_w_ref[0:1, :] * carry_ref[base - 2:base - 2 + nr, :])
        yc = cb * u * _silu(z_c)
        yc_ref[r0:r0 + nr, :] = (yc * _rms_scale(yc, CONV_WIDTH) * g_oc_ref[...]).astype(BF16)

        ga_ref[r0:r0 + nr, :] = _silu(z_a).astype(BF16)

    for blk in range(tm // PROJ_ROWS):
        rows_block(blk * PROJ_ROWS, PROJ_ROWS)
    carry_ref[0:SUBLANES, :] = carry_ref[tm:tm + SUBLANES, :]


def _attn_kernel(qn_ref, kn_ref, kc_ref, qc_ref, diag_n_ref, first_c_ref, last_c_ref,
                 qt_ref, k_ref, vt_ref, o_ref, s_ref, mx_ref, m_ref, acc_ref):
    step = pl.program_id(0)
    tq = qt_ref.shape[1]
    tk = k_ref.shape[0]

    @pl.when(step == 0)
    def _():
        s_ref[...] = jnp.zeros(s_ref.shape, F32)
        mx_ref[...] = jnp.zeros(mx_ref.shape, F32)

    @pl.when((first_c_ref[step] == 1) | (step == 0))
    def _():
        m_ref[...] = jnp.full(m_ref.shape, NEG_INF, F32)
        acc_ref[...] = jnp.zeros(acc_ref.shape, F32)

    nchunk = tq // ATTN_Q_CHUNK
    half = tk // 2

    def body(masked):
        if masked:
            assert nchunk == 2 and half == ATTN_Q_CHUNK
            tri = (lax.broadcasted_iota(jnp.int32, (half, ATTN_Q_CHUNK), 0)
                   <= lax.broadcasted_iota(jnp.int32, (half, ATTN_Q_CHUNK), 1))

        def scores(hd, c):
            cols = slice(c * ATTN_Q_CHUNK, (c + 1) * ATTN_Q_CHUNK)
            qt = qt_ref[hd * HEAD_PAD:(hd + 1) * HEAD_PAD, cols]
            if not masked:
                s = _dot(k_ref[:, hd * HEAD_PAD:(hd + 1) * HEAD_PAD], qt)
                s_ref[hd, :, cols] = s
                mx_ref[hd, :, cols] = jnp.max(s, axis=0, keepdims=True)
            elif c == 0:
                s = jnp.where(tri, _dot(k_ref[0:half, hd * HEAD_PAD:(hd + 1) * HEAD_PAD], qt), NEG_INF)
                s_ref[hd, 0:half, cols] = s
                s_ref[hd, half:tk, cols] = jnp.full((half, ATTN_Q_CHUNK), NEG_INF, F32)
                mx_ref[hd, :, cols] = jnp.max(s, axis=0, keepdims=True)
            else:
                s = _dot(k_ref[:, hd * HEAD_PAD:(hd + 1) * HEAD_PAD], qt)
                top = s[0:half]
                bot = jnp.where(tri, s[half:tk], NEG_INF)
                s_ref[hd, 0:half, cols] = top
                s_ref[hd, half:tk, cols] = bot
                mx_ref[hd, :, cols] = jnp.maximum(jnp.max(top, axis=0, keepdims=True),
                                                  jnp.max(bot, axis=0, keepdims=True))

        def softmax_pv(hd, c):
            cols = slice(c * ATTN_Q_CHUNK, (c + 1) * ATTN_Q_CHUNK)
            m_prev = m_ref[hd, :, cols]
            m_new = jnp.maximum(m_prev, mx_ref[hd, :, cols])
            alpha = jnp.exp2(m_prev - m_new)
            p = jnp.exp2((s_ref[hd, :, cols] - m_new).astype(BF16))
            pv = _dot(vt_ref[hd * V_ROWS:(hd + 1) * V_ROWS, :], p)
            acc_ref[hd, :, cols] = alpha * acc_ref[hd, :, cols] + pv
            m_ref[hd, :, cols] = m_new

        for hd in range(N_HEADS):
            for c in range(nchunk):
                softmax_pv(hd, c)
                scores(hd, c)

    is_diag = diag_n_ref[step] == 1

    @pl.when(jnp.logical_not(is_diag))
    def _():
        body(False)

    @pl.when(is_diag)
    def _():
        body(True)

    @pl.when(last_c_ref[step] == 1)
    def _():
        for hd in range(N_HEADS):
            acc = acc_ref[hd]
            o_ref[:, hd * V_DIM:(hd + 1) * V_DIM] = (
                acc[:V_DIM] / acc[V_DIM:V_DIM + 1]).T.astype(o_ref.dtype)


def _out_kernel(o_ref, ga_ref, yc_ref, x_ref, p_ref, g_oa_ref, w_o_ref, g_pl_ref, w_plg_ref, w_pl_ref,
                out_ref):
    tm = x_ref.shape[0]

    def residual(r0):
        rows = slice(r0, r0 + OUT_ROWS)
        ya = o_ref[rows, :].astype(F32) * ga_ref[rows, :].astype(F32)
        ya_n = (ya * _rms_scale(ya, ATTN_WIDTH) * g_oa_ref[...]).astype(BF16)
        x1 = (x_ref[rows, :] + _dot(ya_n, w_o_ref[:ATTN_WIDTH, :])
              + _dot(yc_ref[rows, :], w_o_ref[ATTN_WIDTH:, :]))
        ple = _dot(p_ref[rows, :].astype(BF16), w_pl_ref[...])
        hn = (x1 * _rms_scale(x1, D_MODEL) * g_pl_ref[...]).astype(BF16)
        return r0, x1, ple, hn

    def gated(r0, x1, ple, hn):
        gate = 1.0 / (1.0 + jnp.exp(-_dot(hn, w_plg_ref[...])))
        out_ref[r0:r0 + OUT_ROWS, :] = x1 + gate * ple

    prev = None
    for r0 in range(0, tm, OUT_ROWS):
        cur = residual(r0)
        if prev is not None:
            gated(*prev)
        prev = cur
    gated(*prev)


def _row_spec(tile, width):
    return pl.BlockSpec((tile, width), lambda i: (i, 0))


def _const_spec(shape):
    return pl.BlockSpec(shape, lambda i: (0,) * len(shape))


def _triangle_schedule(batch, n_blk):
    pairs = [(b * n_blk + i, b * n_blk + j, j == 0, j == i)
             for b in range(batch) for i in range(n_blk) for j in range(i + 1)]
    nxt = pairs + [pairs[-1]]
    cur = [pairs[0]] + pairs
    cols = ([p[0] for p in nxt], [p[1] for p in nxt], [p[1] for p in cur], [p[0] for p in cur],
            [int(p[3]) for p in nxt],
            [0] + [int(p[2]) for p in pairs], [0] + [int(p[3]) for p in pairs])
    return tuple(jnp.asarray(np.asarray(c, np.int32)) for c in cols)


def _prep_in_kernel(wt_lat_ref, wt_ref, g_ref, lat_ref, wide_ref):
    g = g_ref[...]

    @pl.when(pl.program_id(0) == 0)
    def _():
        pad = jnp.zeros((LAT_PAD - LAT_SRC, D_MODEL), F32)
        lat_ref[...] = jnp.concatenate([wt_lat_ref[...] * g, pad], axis=0).T.astype(BF16)

    wide_ref[...] = (wt_ref[...] * g).T.astype(BF16)


def _prepare_w_in(w_in, g_in):
    def rows_at(nrows, start):
        return pl.BlockSpec((pl.Element(nrows), pl.Element(D_MODEL)), start)

    return pl.pallas_call(
        _prep_in_kernel,
        grid=(WIDE // PREP_COLS,),
        in_specs=[rows_at(LAT_SRC, lambda j: (0, 0)),
                  rows_at(PREP_COLS, lambda j: (pl.multiple_of(LAT_SRC + j * PREP_COLS, SUBLANES), 0)),
                  _const_spec((1, D_MODEL))],
        out_specs=[_const_spec((D_MODEL, LAT_PAD)), pl.BlockSpec((D_MODEL, PREP_COLS), lambda j: (0, j))],
        out_shape=[jax.ShapeDtypeStruct((D_MODEL, LAT_PAD), BF16),
                   jax.ShapeDtypeStruct((D_MODEL, WIDE), BF16)],
        compiler_params=pltpu.CompilerParams(dimension_semantics=("arbitrary",),
                                             vmem_limit_bytes=VMEM_LIMIT),
        name="mla_conv_w_in_prep",
    )(w_in.T, w_in.T, g_in.reshape(1, -1))


def _layer(x2, p2, posc, invf, batch, seq, g_in, w_in, g_cq, w_uq, g_ckv, w_ukv, g_q, g_k,
           conv_w, g_oa, g_oc, w_o, w_pl, w_plg, g_pl):
    tokens = batch * seq
    tm = PROJ_TILE
    assert seq % tm == 0 and seq % ATTN_TILE == 0

    w_lat, w_wide = _prepare_w_in(w_in, g_in)
    w_uq_p = jnp.pad(w_uq.reshape(Q_LORA, N_HEADS, QK_DIM),
                     ((0, 0), (0, 0), (0, HEAD_PAD - QK_DIM))).reshape(Q_LORA, N_HEADS * HEAD_PAD).astype(BF16)
    g_q_p = jnp.pad(g_q, (0, HEAD_PAD - QK_DIM)).reshape(1, HEAD_PAD)
    g_kn = g_k[:NOPE_DIM].reshape(1, NOPE_DIM)
    g_kr = jnp.pad(g_k[NOPE_DIM:], (0, KPE_PAD - ROPE_DIM)).reshape(1, KPE_PAD)

    cparams = pltpu.CompilerParams(dimension_semantics=("arbitrary",), vmem_limit_bytes=VMEM_LIMIT)

    def col_spec(rows, tile):
        return pl.BlockSpec((rows, tile), lambda i: (0, i))

    qt, k, vt, ga, yc = pl.pallas_call(
        functools.partial(_proj_kernel, seq // tm),
        grid=(tokens // tm,),
        in_specs=[
            _row_spec(tm, D_MODEL), _row_spec(tm // TOKENS_PER_ROW, LANES), _const_spec((1, LANES)),
            _const_spec((D_MODEL, LAT_PAD)), _const_spec((D_MODEL, WIDE)),
            _const_spec((1, Q_LORA)), _const_spec((Q_LORA, N_HEADS * HEAD_PAD)),
            _const_spec((1, KV_LORA)), _const_spec((KV_LORA, N_HEADS * (NOPE_DIM + V_DIM))),
            _const_spec((1, HEAD_PAD)), _const_spec((1, NOPE_DIM)), _const_spec((1, KPE_PAD)),
            _const_spec((CONV_K, CONV_WIDTH)), _const_spec((1, CONV_WIDTH)),
        ],
        out_specs=[
            col_spec(N_HEADS * HEAD_PAD, tm), _row_spec(tm, N_HEADS * HEAD_PAD),
            col_spec(N_HEADS * V_ROWS, tm), _row_spec(tm, ATTN_WIDTH), _row_spec(tm, CONV_WIDTH),
        ],
        out_shape=[
            jax.ShapeDtypeStruct((N_HEADS * HEAD_PAD, tokens), BF16),
            jax.ShapeDtypeStruct((tokens, N_HEADS * HEAD_PAD), BF16),
            jax.ShapeDtypeStruct((N_HEADS * V_ROWS, tokens), BF16),
            jax.ShapeDtypeStruct((tokens, ATTN_WIDTH), BF16),
            jax.ShapeDtypeStruct((tokens, CONV_WIDTH), BF16),
        ],
        scratch_shapes=[pltpu.VMEM((tm + SUBLANES, CONV_WIDTH), F32)],
        compiler_params=cparams,
        name="mla_conv_proj",
    )(x2, posc, invf, w_lat, w_wide, g_cq.reshape(1, -1), w_uq_p,
      g_ckv.reshape(1, -1), w_ukv.astype(BF16), g_q_p, g_kn, g_kr, conv_w, g_oc.reshape(1, -1))

    ta = ATTN_TILE
    sched = _triangle_schedule(batch, seq // ta)
    o = pl.pallas_call(
        _attn_kernel,
        grid_spec=pltpu.PrefetchScalarGridSpec(
            num_scalar_prefetch=len(sched),
            grid=(int(sched[0].shape[0]),),
            in_specs=[
                pl.BlockSpec((N_HEADS * HEAD_PAD, ta), lambda s, qn, kn, kc, qc, *_: (0, qn[s])),
                pl.BlockSpec((ta, N_HEADS * HEAD_PAD), lambda s, qn, kn, kc, qc, *_: (kn[s], 0)),
                pl.BlockSpec((N_HEADS * V_ROWS, ta), lambda s, qn, kn, kc, qc, *_: (0, kc[s])),
            ],
            out_specs=pl.BlockSpec((ta, ATTN_WIDTH), lambda s, qn, kn, kc, qc, *_: (qc[s], 0)),
            scratch_shapes=[pltpu.VMEM((N_HEADS, ta, ta), F32),
                            pltpu.VMEM((N_HEADS, 1, ta), F32),
                            pltpu.VMEM((N_HEADS, 1, ta), F32),
                            pltpu.VMEM((N_HEADS, V_ROWS, ta), F32)],
        ),
        out_shape=jax.ShapeDtypeStruct((tokens, ATTN_WIDTH), BF16),
        compiler_params=cparams,
        name="mla_flash_attn",
    )(*sched, qt, k, vt)

    return pl.pallas_call(
        _out_kernel,
        grid=(tokens // tm,),
        in_specs=[
            _row_spec(tm, ATTN_WIDTH), _row_spec(tm, ATTN_WIDTH), _row_spec(tm, CONV_WIDTH),
            _row_spec(tm, D_MODEL), _row_spec(tm, PLE_DIM),
            _const_spec((1, ATTN_WIDTH)), _const_spec((D_MODEL, D_MODEL)), _const_spec((1, D_MODEL)),
            _const_spec((D_MODEL, D_MODEL)), _const_spec((PLE_DIM, D_MODEL)),
        ],
        out_specs=_row_spec(tm, D_MODEL),
        out_shape=jax.ShapeDtypeStruct((tokens, D_MODEL), F32),
        compiler_params=cparams,
        name="mla_conv_out",
    )(o, ga, yc, x2, p2, g_oa.reshape(1, -1), w_o.astype(BF16),
      g_pl.reshape(1, -1), w_plg.astype(BF16), w_pl.astype(BF16))


def kernel(x, p, positions, g_in, w_in, g_cq, w_uq, g_ckv, w_ukv, g_q, g_k, conv_w, g_oa, g_oc,
           w_o, w_pl, w_plg, g_pl):
    batch, seq, d_model = x.shape
    depth = p.shape[0]
    tokens = batch * seq
    rows = PROJ_TILE // TOKENS_PER_ROW
    posc = jnp.repeat(positions.astype(F32).reshape(tokens // PROJ_TILE, TOKENS_PER_ROW, rows)
                      .transpose(0, 2, 1), HALF_ROPE, axis=-1).reshape(tokens // TOKENS_PER_ROW, LANES)
    inv_freq = 1.0 / (ROPE_THETA ** (jnp.arange(0, ROPE_DIM, 2, dtype=F32) / ROPE_DIM))
    invf = jnp.tile(inv_freq, LANES // HALF_ROPE).reshape(1, LANES)
    h = x.reshape(tokens, d_model)
    for i in range(depth):
        h = _layer(h, p[i].reshape(tokens, PLE_DIM), posc, invf, batch, seq,
                   g_in[i], w_in[i], g_cq[i], w_uq[i], g_ckv[i], w_ukv[i], g_q[i], g_k[i],
                   conv_w[i], g_oa[i], g_oc[i], w_o[i], w_pl[i], w_plg[i], g_pl[i])
    return h.reshape(batch, seq, d_model).astype(x.dtype)
```

```python
import functools
import math

import jax
import jax.numpy as jnp
import numpy as np
from jax import lax
from jax.experimental import pallas as pl
from jax.experimental.pallas import tpu as pltpu

D_MODEL = 1024
PLE_DIM = 256
N_HEADS = 4
NOPE_DIM = 128
ROPE_DIM = 64
HALF_ROPE = ROPE_DIM // 2
V_DIM = 128
QK_DIM = NOPE_DIM + ROPE_DIM
Q_LORA = 256
KV_LORA = 128
ATTN_WIDTH = N_HEADS * V_DIM
CONV_WIDTH = D_MODEL - ATTN_WIDTH
CONV_K = 3
ROPE_THETA = 10000.0
RMS_EPS = 1e-6
NEG_INF = -1e30

LANES = 128
SUBLANES = 8
HEAD_PAD = 2 * LANES
KPE_PAD = LANES
BF16_ROWS = 16
V_ROWS = V_DIM + BF16_ROWS
TOKENS_PER_ROW = LANES // HALF_ROPE
OFF_CQ = 0
OFF_CKV = OFF_CQ + Q_LORA
OFF_KPE = OFF_CKV + KV_LORA
LAT_SRC = OFF_KPE + ROPE_DIM
LAT_PAD = OFF_KPE + KPE_PAD
OFF_ZA = 0
OFF_CB = OFF_ZA + ATTN_WIDTH
OFF_CC = OFF_CB + CONV_WIDTH
OFF_CX = OFF_CC + CONV_WIDTH
OFF_ZC = OFF_CX + CONV_WIDTH
WIDE = OFF_ZC + CONV_WIDTH

PROJ_TILE = 1024
PROJ_ROWS = 256
OUT_ROWS = 256
PREP_COLS = 512
ATTN_TILE = 1024
ATTN_Q_CHUNK = 256
VMEM_LIMIT = 48 * 1024 * 1024

BF16 = jnp.bfloat16
F32 = jnp.float32


def _rms_scale(v, width):
    return lax.rsqrt(jnp.sum(v * v, axis=-1, keepdims=True) * (1.0 / width) + RMS_EPS)


def _silu(z):
    h = 0.5 * z
    return h + h * jnp.tanh(h)


def _dot(a, b):
    return jnp.dot(a, b, preferred_element_type=F32)


def _rope_tables(posc, invf):
    ang = posc * invf
    cosc = jnp.cos(ang)
    sinc = jnp.sin(ang)
    lane = lax.broadcasted_iota(jnp.int32, (1, LANES), 1)
    lo = lane < HALF_ROPE
    mid = (lane >= HALF_ROPE) & (lane < ROPE_DIM)
    quarters = []
    for a in range(TOKENS_PER_ROW):
        c = pltpu.roll(cosc, LANES - HALF_ROPE * a, 1) if a else cosc
        s = pltpu.roll(sinc, LANES - HALF_ROPE * a, 1) if a else sinc
        quarters.append((jnp.where(lo, c, jnp.where(mid, pltpu.roll(c, HALF_ROPE, 1), 0.0)),
                         jnp.where(lo, -s, jnp.where(mid, pltpu.roll(s, HALF_ROPE, 1), 0.0))))
    return quarters, lo


def _rope(t, c, s, lo):
    swapped = jnp.where(lo, pltpu.roll(t, LANES - HALF_ROPE, 1), pltpu.roll(t, HALF_ROPE, 1))
    return t * c + swapped * s


def _proj_kernel(tiles_per_seq,
                 x_ref, posc_ref, invf_ref, w_lat_ref, w_in_ref, g_cq_ref, w_uq_ref,
                 g_ckv_ref, w_ukv_ref, g_q_ref, g_kn_ref, g_kr_ref, conv_w_ref, g_oc_ref,
                 qt_ref, k_ref, vt_ref, ga_ref, yc_ref, carry_ref):
    tm = x_ref.shape[0]
    quarter_rows = tm // TOKENS_PER_ROW
    tables = []
    q_scale = math.log2(math.e) / math.sqrt(QK_DIM)

    @pl.when(pl.program_id(0) % tiles_per_seq == 0)
    def _():
        carry_ref[0:SUBLANES, :] = jnp.zeros((SUBLANES, CONV_WIDTH), F32)

    def rows_block(r0, nr):
        x = x_ref[r0:r0 + nr, :]
        h = (x * _rms_scale(x, D_MODEL)).astype(BF16)

        def proj(off, width):
            return _dot(h, w_in_ref[:, off:off + width])

        lat = _dot(h, w_lat_ref[...])
        c_q = lat[:, OFF_CQ:OFF_CKV]
        c_kv = lat[:, OFF_CKV:OFF_KPE]
        kpe = lat[:, OFF_KPE:LAT_PAD]
        cv = proj(OFF_CC, CONV_WIDTH) * proj(OFF_CX, CONV_WIDTH)
        cqn = (c_q * _rms_scale(c_q, Q_LORA) * g_cq_ref[...]).astype(BF16)
        qf = _dot(cqn, w_uq_ref[...])
        cb = proj(OFF_CB, CONV_WIDTH)
        ckvn = (c_kv * _rms_scale(c_kv, KV_LORA) * g_ckv_ref[...]).astype(BF16)
        kv = _dot(ckvn, w_ukv_ref[...])
        z_c = proj(OFF_ZC, CONV_WIDTH)
        z_a = proj(OFF_ZA, ATTN_WIDTH)

        if not tables:
            tables.extend(_rope_tables(posc_ref[...], invf_ref[...]))
        quarters, lo = tables
        qa = r0 // quarter_rows
        cos_t = jnp.concatenate([quarters[qa + i][0] for i in range(nr // quarter_rows)], axis=0)
        sin_t = jnp.concatenate([quarters[qa + i][1] for i in range(nr // quarter_rows)], axis=0)

        for hd in range(N_HEADS):
            qh = qf[:, hd * HEAD_PAD:(hd + 1) * HEAD_PAD]
            qn = qh * (_rms_scale(qh, QK_DIM) * q_scale) * g_q_ref[...]
            qt_ref[hd * HEAD_PAD:hd * HEAD_PAD + LANES, r0:r0 + nr] = qn[:, :LANES].T.astype(BF16)
            qt_ref[hd * HEAD_PAD + LANES:(hd + 1) * HEAD_PAD, r0:r0 + nr] = (
                _rope(qn[:, LANES:], cos_t, sin_t, lo).T.astype(BF16))

        ss_kpe = jnp.sum(kpe * kpe, axis=-1, keepdims=True)
        k_rot = _rope(kpe * g_kr_ref[...], cos_t, sin_t, lo)
        for hd in range(N_HEADS):
            kn = kv[:, hd * (NOPE_DIM + V_DIM):hd * (NOPE_DIM + V_DIM) + NOPE_DIM]
            vh = kv[:, hd * (NOPE_DIM + V_DIM) + NOPE_DIM:(hd + 1) * (NOPE_DIM + V_DIM)]
            ss = jnp.sum(kn * kn, axis=-1, keepdims=True) + ss_kpe
            rs = lax.rsqrt(ss * (1.0 / QK_DIM) + RMS_EPS)
            k_ref[r0:r0 + nr, hd * HEAD_PAD:hd * HEAD_PAD + LANES] = (kn * rs * g_kn_ref[...]).astype(BF16)
            k_ref[r0:r0 + nr, hd * HEAD_PAD + LANES:(hd + 1) * HEAD_PAD] = (k_rot * rs).astype(BF16)
            vt_ref[hd * V_ROWS:hd * V_ROWS + V_DIM, r0:r0 + nr] = vh.T.astype(BF16)
            vt_ref[hd * V_ROWS + V_DIM:(hd + 1) * V_ROWS, r0:r0 + nr] = jnp.ones((BF16_ROWS, nr), BF16)

        base = SUBLANES + r0
        carry_ref[base:base + nr, :] = cv
        u = (conv_w_ref[2:3, :] * cv
             + conv_w_ref[1:2, :] * carry_ref[base - 1:base - 1 + nr, :]
             + conv_w_ref[0:1, :] * carry_ref[base - 2:base - 2 + nr, :])
        yc = cb * u * _silu(z_c)
        yc_ref[r0:r0 + nr, :] = (yc * _rms_scale(yc, CONV_WIDTH) * g_oc_ref[...]).astype(BF16)

        ga_ref[r0:r0 + nr, :] = _silu(z_a).astype(BF16)

    for blk in range(tm // PROJ_ROWS):
        rows_block(blk * PROJ_ROWS, PROJ_ROWS)
    carry_ref[0:SUBLANES, :] = carry_ref[tm:tm + SUBLANES, :]


def _attn_kernel(qn_ref, kn_ref, kc_ref, qc_ref, diag_n_ref, first_c_ref, last_c_ref,
                 qt_ref, k_ref, vt_ref, o_ref, s_ref, mx_ref, m_ref, acc_ref):
    step = pl.program_id(0)
    tq = qt_ref.shape[1]
    tk = k_ref.shape[0]

    @pl.when(step == 0)
    def _():
        s_ref[...] = jnp.zeros(s_ref.shape, F32)
        mx_ref[...] = jnp.zeros(mx_ref.shape, F32)

    @pl.when((first_c_ref[step] == 1) | (step == 0))
    def _():
        m_ref[...] = jnp.full(m_ref.shape, NEG_INF, F32)
        acc_ref[...] = jnp.zeros(acc_ref.shape, F32)

    half = tk // 2

    def body(masked):
        chunk = half if masked else ATTN_Q_CHUNK
        nchunk = tq // chunk
        if masked:
            assert nchunk == 2
            tri = (lax.broadcasted_iota(jnp.int32, (half, chunk), 0)
                   <= lax.broadcasted_iota(jnp.int32, (half, chunk), 1))

        def scores(hd, c):
            cols = slice(c * chunk, (c + 1) * chunk)
            qt = qt_ref[hd * HEAD_PAD:(hd + 1) * HEAD_PAD, cols]
            if not masked:
                s = _dot(k_ref[:, hd * HEAD_PAD:(hd + 1) * HEAD_PAD], qt)
                s_ref[hd, :, cols] = s
                mx_ref[hd, :, cols] = jnp.max(s, axis=0, keepdims=True)
            elif c == 0:
                s = jnp.where(tri, _dot(k_ref[0:half, hd * HEAD_PAD:(hd + 1) * HEAD_PAD], qt), NEG_INF)
                s_ref[hd, 0:half, cols] = s
                s_ref[hd, half:tk, cols] = jnp.full((half, chunk), NEG_INF, F32)
                mx_ref[hd, :, cols] = jnp.max(s, axis=0, keepdims=True)
            else:
                s = _dot(k_ref[:, hd * HEAD_PAD:(hd + 1) * HEAD_PAD], qt)
                top = s[0:half]
                bot = jnp.where(tri, s[half:tk], NEG_INF)
                s_ref[hd, 0:half, cols] = top
                s_ref[hd, half:tk, cols] = bot
                mx_ref[hd, :, cols] = jnp.maximum(jnp.max(top, axis=0, keepdims=True),
                                                  jnp.max(bot, axis=0, keepdims=True))

        def softmax_pv(hd, c):
            cols = slice(c * chunk, (c + 1) * chunk)
            m_prev = m_ref[hd, :, cols]
            m_new = jnp.maximum(m_prev, mx_ref[hd, :, cols])
            alpha = jnp.exp2(m_prev - m_new)
            p = jnp.exp2((s_ref[hd, :, cols] - m_new).astype(BF16))
            pv = _dot(vt_ref[hd * V_ROWS:(hd + 1) * V_ROWS, :], p)
            acc_ref[hd, :, cols] = alpha * acc_ref[hd, :, cols] + pv
            m_ref[hd, :, cols] = m_new

        for hd in range(N_HEADS):
            for c in range(nchunk):
                softmax_pv(hd, c)
                scores(hd, c)

    is_diag = diag_n_ref[step] == 1

    @pl.when(jnp.logical_not(is_diag))
    def _():
        body(False)

    @pl.when(is_diag)
    def _():
        body(True)

    @pl.when(last_c_ref[step] == 1)
    def _():
        for hd in range(N_HEADS):
            acc = acc_ref[hd]
            o_ref[:, hd * V_DIM:(hd + 1) * V_DIM] = (
                acc[:V_DIM] / acc[V_DIM:V_DIM + 1]).T.astype(o_ref.dtype)


def _out_kernel(o_ref, ga_ref, yc_ref, x_ref, p_ref, g_oa_ref, w_o_ref, g_pl_ref, w_plg_ref, w_pl_ref,
                out_ref):
    tm = x_ref.shape[0]

    def residual(r0):
        rows = slice(r0, r0 + OUT_ROWS)
        ya = o_ref[rows, :].astype(F32) * ga_ref[rows, :].astype(F32)
        ya_n = (ya * _rms_scale(ya, ATTN_WIDTH) * g_oa_ref[...]).astype(BF16)
        x1 = (x_ref[rows, :] + _dot(ya_n, w_o_ref[:ATTN_WIDTH, :])
              + _dot(yc_ref[rows, :], w_o_ref[ATTN_WIDTH:, :]))
        ple = _dot(p_ref[rows, :].astype(BF16), w_pl_ref[...])
        hn = (x1 * _rms_scale(x1, D_MODEL) * g_pl_ref[...]).astype(BF16)
        return r0, x1, ple, hn

    def gated(r0, x1, ple, hn):
        gate = 1.0 / (1.0 + jnp.exp(-_dot(hn, w_plg_ref[...])))
        out_ref[r0:r0 + OUT_ROWS, :] = x1 + gate * ple

    prev = None
    for r0 in range(0, tm, OUT_ROWS):
        cur = residual(r0)
        if prev is not None:
            gated(*prev)
        prev = cur
    gated(*prev)


def _row_spec(tile, width):
    return pl.BlockSpec((tile, width), lambda i: (i, 0))


def _const_spec(shape):
    return pl.BlockSpec(shape, lambda i: (0,) * len(shape))


def _triangle_schedule(batch, n_blk):
    pairs = [(b * n_blk + i, b * n_blk + j, j == 0, j == i)
             for b in range(batch) for i in range(n_blk) for j in range(i + 1)]
    nxt = pairs + [pairs[-1]]
    cur = [pairs[0]] + pairs
    cols = ([p[0] for p in nxt], [p[1] for p in nxt], [p[1] for p in cur], [p[0] for p in cur],
            [int(p[3]) for p in nxt],
            [0] + [int(p[2]) for p in pairs], [0] + [int(p[3]) for p in pairs])
    return tuple(jnp.asarray(np.asarray(c, np.int32)) for c in cols)


def _prep_in_kernel(wt_lat_ref, wt_ref, g_ref, lat_ref, wide_ref):
    g = g_ref[...]

    @pl.when(pl.program_id(0) == 0)
    def _():
        pad = jnp.zeros((LAT_PAD - LAT_SRC, D_MODEL), F32)
        lat_ref[...] = jnp.concatenate([wt_lat_ref[...] * g, pad], axis=0).T.astype(BF16)

    wide_ref[...] = (wt_ref[...] * g).T.astype(BF16)


def _prepare_w_in(w_in, g_in):
    def rows_at(nrows, start):
        return pl.BlockSpec((pl.Element(nrows), pl.Element(D_MODEL)), start)

    return pl.pallas_call(
        _prep_in_kernel,
        grid=(WIDE // PREP_COLS,),
        in_specs=[rows_at(LAT_SRC, lambda j: (0, 0)),
                  rows_at(PREP_COLS, lambda j: (pl.multiple_of(LAT_SRC + j * PREP_COLS, SUBLANES), 0)),
                  _const_spec((1, D_MODEL))],
        out_specs=[_const_spec((D_MODEL, LAT_PAD)), pl.BlockSpec((D_MODEL, PREP_COLS), lambda j: (0, j))],
        out_shape=[jax.ShapeDtypeStruct((D_MODEL, LAT_PAD), BF16),
                   jax.ShapeDtypeStruct((D_MODEL, WIDE), BF16)],
        compiler_params=pltpu.CompilerParams(dimension_semantics=("arbitrary",),
                                             vmem_limit_bytes=VMEM_LIMIT),
        name="mla_conv_w_in_prep",
    )(w_in.T, w_in.T, g_in.reshape(1, -1))


def _layer(x2, p2, posc, invf, batch, seq, g_in, w_in, g_cq, w_uq, g_ckv, w_ukv, g_q, g_k,
           conv_w, g_oa, g_oc, w_o, w_pl, w_plg, g_pl):
    tokens = batch * seq
    tm = PROJ_TILE
    assert seq % tm == 0 and seq % ATTN_TILE == 0

    w_lat, w_wide = _prepare_w_in(w_in, g_in)
    w_uq_p = jnp.pad(w_uq.reshape(Q_LORA, N_HEADS, QK_DIM),
                     ((0, 0), (0, 0), (0, HEAD_PAD - QK_DIM))).reshape(Q_LORA, N_HEADS * HEAD_PAD).astype(BF16)
    g_q_p = jnp.pad(g_q, (0, HEAD_PAD - QK_DIM)).reshape(1, HEAD_PAD)
    g_kn = g_k[:NOPE_DIM].reshape(1, NOPE_DIM)
    g_kr = jnp.pad(g_k[NOPE_DIM:], (0, KPE_PAD - ROPE_DIM)).reshape(1, KPE_PAD)

    cparams = pltpu.CompilerParams(dimension_semantics=("arbitrary",), vmem_limit_bytes=VMEM_LIMIT)

    def col_spec(rows, tile):
        return pl.BlockSpec((rows, tile), lambda i: (0, i))

    qt, k, vt, ga, yc = pl.pallas_call(
        functools.partial(_proj_kernel, seq // tm),
        grid=(tokens // tm,),
        in_specs=[
            _row_spec(tm, D_MODEL), _row_spec(tm // TOKENS_PER_ROW, LANES), _const_spec((1, LANES)),
            _const_spec((D_MODEL, LAT_PAD)), _const_spec((D_MODEL, WIDE)),
            _const_spec((1, Q_LORA)), _const_spec((Q_LORA, N_HEADS * HEAD_PAD)),
            _const_spec((1, KV_LORA)), _const_spec((KV_LORA, N_HEADS * (NOPE_DIM + V_DIM))),
            _const_spec((1, HEAD_PAD)), _const_spec((1, NOPE_DIM)), _const_spec((1, KPE_PAD)),
            _const_spec((CONV_K, CONV_WIDTH)), _const_spec((1, CONV_WIDTH)),
        ],
        out_specs=[
            col_spec(N_HEADS * HEAD_PAD, tm), _row_spec(tm, N_HEADS * HEAD_PAD),
            col_spec(N_HEADS * V_ROWS, tm), _row_spec(tm, ATTN_WIDTH), _row_spec(tm, CONV_WIDTH),
        ],
        out_shape=[
            jax.ShapeDtypeStruct((N_HEADS * HEAD_PAD, tokens), BF16),
            jax.ShapeDtypeStruct((tokens, N_HEADS * HEAD_PAD), BF16),
            jax.ShapeDtypeStruct((N_HEADS * V_ROWS, tokens), BF16),
            jax.ShapeDtypeStruct((tokens, ATTN_WIDTH), BF16),
            jax.ShapeDtypeStruct((tokens, CONV_WIDTH), BF16),
        ],
        scratch_shapes=[pltpu.VMEM((tm + SUBLANES, CONV_WIDTH), F32)],
        compiler_params=cparams,
        name="mla_conv_proj",
    )(x2, posc, invf, w_lat, w_wide, g_cq.reshape(1, -1), w_uq_p,
      g_ckv.reshape(1, -1), w_ukv.astype(BF16), g_q_p, g_kn, g_kr, conv_w, g_oc.reshape(1, -1))

    ta = ATTN_TILE
    sched = _triangle_schedule(batch, seq // ta)
    o = pl.pallas_call(
        _attn_kernel,
        grid_spec=pltpu.PrefetchScalarGridSpec(
            num_scalar_prefetch=len(sched),
            grid=(int(sched[0].shape[0]),),
            in_specs=[
                pl.BlockSpec((N_HEADS * HEAD_PAD, ta), lambda s, qn, kn, kc, qc, *_: (0, qn[s])),
                pl.BlockSpec((ta, N_HEADS * HEAD_PAD), lambda s, qn, kn, kc, qc, *_: (kn[s], 0)),
                pl.BlockSpec((N_HEADS * V_ROWS, ta), lambda s, qn, kn, kc, qc, *_: (0, kc[s])),
            ],
            out_specs=pl.BlockSpec((ta, ATTN_WIDTH), lambda s, qn, kn, kc, qc, *_: (qc[s], 0)),
            scratch_shapes=[pltpu.VMEM((N_HEADS, ta, ta), F32),
                            pltpu.VMEM((N_HEADS, 1, ta), F32),
                            pltpu.VMEM((N_HEADS, 1, ta), F32),
                            pltpu.VMEM((N_HEADS, V_ROWS, ta), F32)],
        ),
        out_shape=jax.ShapeDtypeStruct((tokens, ATTN_WIDTH), BF16),
        compiler_params=cparams,
        name="mla_flash_attn",
    )(*sched, qt, k, vt)

    return pl.pallas_call(
        _out_kernel,
        grid=(tokens // tm,),
        in_specs=[
            _row_spec(tm, ATTN_WIDTH), _row_spec(tm, ATTN_WIDTH), _row_spec(tm, CONV_WIDTH),
            _row_spec(tm, D_MODEL), _row_spec(tm, PLE_DIM),
            _const_spec((1, ATTN_WIDTH)), _const_spec((D_MODEL, D_MODEL)), _const_spec((1, D_MODEL)),
            _const_spec((D_MODEL, D_MODEL)), _const_spec((PLE_DIM, D_MODEL)),
        ],
        out_specs=_row_spec(tm, D_MODEL),
        out_shape=jax.ShapeDtypeStruct((tokens, D_MODEL), F32),
        compiler_params=cparams,
        name="mla_conv_out",
    )(o, ga, yc, x2, p2, g_oa.reshape(1, -1), w_o.astype(BF16),
      g_pl.reshape(1, -1), w_plg.astype(BF16), w_pl.astype(BF16))


def kernel(x, p, positions, g_in, w_in, g_cq, w_uq, g_ckv, w_ukv, g_q, g_k, conv_w, g_oa, g_oc,
           w_o, w_pl, w_plg, g_pl):
    batch, seq, d_model = x.shape
    depth = p.shape[0]
    tokens = batch * seq
    rows = PROJ_TILE // TOKENS_PER_ROW
    posc = jnp.repeat(positions.astype(F32).reshape(tokens // PROJ_TILE, TOKENS_PER_ROW, rows)
                      .transpose(0, 2, 1), HALF_ROPE, axis=-1).reshape(tokens // TOKENS_PER_ROW, LANES)
    inv_freq = 1.0 / (ROPE_THETA ** (jnp.arange(0, ROPE_DIM, 2, dtype=F32) / ROPE_DIM))
    invf = jnp.tile(inv_freq, LANES // HALF_ROPE).reshape(1, LANES)
    h = x.reshape(tokens, d_model)
    for i in range(depth):
        h = _layer(h, p[i].reshape(tokens, PLE_DIM), posc, invf, batch, seq,
                   g_in[i], w_in[i], g_cq[i], w_uq[i], g_ckv[i], w_ukv[i], g_q[i], g_k[i],
                   conv_w[i], g_oa[i], g_oc[i], w_o[i], w_pl[i], w_plg[i], g_pl[i])
    return h.reshape(batch, seq, d_model).astype(x.dtype)
```

```python
import functools
import math

import jax
import jax.numpy as jnp
import numpy as np
from jax import lax
from jax.experimental import pallas as pl
from jax.experimental.pallas import tpu as pltpu

D_MODEL = 1024
PLE_DIM = 256
N_HEADS = 4
NOPE_DIM = 128
ROPE_DIM = 64
HALF_ROPE = ROPE_DIM // 2
V_DIM = 128
QK_DIM = NOPE_DIM + ROPE_DIM
Q_LORA = 256
KV_LORA = 128
ATTN_WIDTH = N_HEADS * V_DIM
CONV_WIDTH = D_MODEL - ATTN_WIDTH
CONV_K = 3
ROPE_THETA = 10000.0
RMS_EPS = 1e-6
NEG_INF = -1e30

LANES = 128
SUBLANES = 8
HEAD_PAD = 2 * LANES
KPE_PAD = LANES
BF16_ROWS = 16
V_ROWS = V_DIM + BF16_ROWS
TOKENS_PER_ROW = LANES // HALF_ROPE
OFF_CQ = 0
OFF_CKV = OFF_CQ + Q_LORA
OFF_KPE = OFF_CKV + KV_LORA
LAT_SRC = OFF_KPE + ROPE_DIM
LAT_PAD = OFF_KPE + KPE_PAD
OFF_ZA = 0
OFF_CB = OFF_ZA + ATTN_WIDTH
OFF_CC = OFF_CB + CONV_WIDTH
OFF_CX = OFF_CC + CONV_WIDTH
OFF_ZC = OFF_CX + CONV_WIDTH
WIDE = OFF_ZC + CONV_WIDTH

PROJ_TILE = 1024
PROJ_ROWS = 256
OUT_ROWS = 256
PREP_COLS = 512
ATTN_TILE = 1024
ATTN_Q_CHUNK = 256
VMEM_LIMIT = 48 * 1024 * 1024

BF16 = jnp.bfloat16
F32 = jnp.float32


def _rms_scale(v, width):
    return lax.rsqrt(jnp.sum(v * v, axis=-1, keepdims=True) * (1.0 / width) + RMS_EPS)


def _silu(z):
    h = 0.5 * z
    return h + h * jnp.tanh(h)


def _dot(a, b):
    return jnp.dot(a, b, preferred_element_type=F32)


def _rope_tables(posc, invf):
    ang = posc * invf
    cosc = jnp.cos(ang)
    sinc = jnp.sin(ang)
    lane = lax.broadcasted_iota(jnp.int32, (1, LANES), 1)
    lo = lane < HALF_ROPE
    mid = (lane >= HALF_ROPE) & (lane < ROPE_DIM)
    quarters = []
    for a in range(TOKENS_PER_ROW):
        c = pltpu.roll(cosc, LANES - HALF_ROPE * a, 1) if a else cosc
        s = pltpu.roll(sinc, LANES - HALF_ROPE * a, 1) if a else sinc
        quarters.append((jnp.where(lo, c, jnp.where(mid, pltpu.roll(c, HALF_ROPE, 1), 0.0)),
                         jnp.where(lo, -s, jnp.where(mid, pltpu.roll(s, HALF_ROPE, 1), 0.0))))
    return quarters, lo


def _rope(t, c, s, lo):
    swapped = jnp.where(lo, pltpu.roll(t, LANES - HALF_ROPE, 1), pltpu.roll(t, HALF_ROPE, 1))
    return t * c + swapped * s


def _proj_kernel(tiles_per_seq,
                 x_ref, posc_ref, invf_ref, w_lat_ref, w_in_ref, g_cq_ref, w_uq_ref,
                 g_ckv_ref, w_ukv_ref, g_q_ref, g_kn_ref, g_kr_ref, conv_w_ref, g_oc_ref,
                 qt_ref, k_ref, vt_ref, ga_ref, yc_ref, carry_ref):
    tm = x_ref.shape[0]
    quarter_rows = tm // TOKENS_PER_ROW
    tables = []
    q_scale = math.log2(math.e) / math.sqrt(QK_DIM)

    @pl.when(pl.program_id(0) % tiles_per_seq == 0)
    def _():
        carry_ref[0:SUBLANES, :] = jnp.zeros((SUBLANES, CONV_WIDTH), F32)

    def rows_block(r0, nr):
        x = x_ref[r0:r0 + nr, :]
        h = (x * _rms_scale(x, D_MODEL)).astype(BF16)

        def proj(off, width):
            return _dot(h, w_in_ref[:, off:off + width])

        lat = _dot(h, w_lat_ref[...])
        c_q = lat[:, OFF_CQ:OFF_CKV]
        c_kv = lat[:, OFF_CKV:OFF_KPE]
        kpe = lat[:, OFF_KPE:LAT_PAD]
        cv = proj(OFF_CC, CONV_WIDTH) * proj(OFF_CX, CONV_WIDTH)
        cqn = (c_q * _rms_scale(c_q, Q_LORA) * g_cq_ref[...]).astype(BF16)
        qf = _dot(cqn, w_uq_ref[...])
        cb = proj(OFF_CB, CONV_WIDTH)
        ckvn = (c_kv * _rms_scale(c_kv, KV_LORA) * g_ckv_ref[...]).astype(BF16)
        kv = _dot(ckvn, w_ukv_ref[...])
        z_c = proj(OFF_ZC, CONV_WIDTH)
        z_a = proj(OFF_ZA, ATTN_WIDTH)

        if not tables:
            tables.extend(_rope_tables(posc_ref[...], invf_ref[...]))
        quarters, lo = tables
        qa = r0 // quarter_rows
        cos_t = jnp.concatenate([quarters[qa + i][0] for i in range(nr // quarter_rows)], axis=0)
        sin_t = jnp.concatenate([quarters[qa + i][1] for i in range(nr // quarter_rows)], axis=0)

        for hd in range(N_HEADS):
            qh = qf[:, hd * HEAD_PAD:(hd + 1) * HEAD_PAD]
            qn = qh * (_rms_scale(qh, QK_DIM) * q_scale) * g_q_ref[...]
            qt_ref[hd * HEAD_PAD:hd * HEAD_PAD + LANES, r0:r0 + nr] = qn[:, :LANES].astype(BF16).T
            qt_ref[hd * HEAD_PAD + LANES:(hd + 1) * HEAD_PAD, r0:r0 + nr] = (
                _rope(qn[:, LANES:], cos_t, sin_t, lo).astype(BF16).T)

        ss_kpe = jnp.sum(kpe * kpe, axis=-1, keepdims=True)
        k_rot = _rope(kpe * g_kr_ref[...], cos_t, sin_t, lo)
        for hd in range(N_HEADS):
            kn = kv[:, hd * (NOPE_DIM + V_DIM):hd * (NOPE_DIM + V_DIM) + NOPE_DIM]
            vh = kv[:, hd * (NOPE_DIM + V_DIM) + NOPE_DIM:(hd + 1) * (NOPE_DIM + V_DIM)]
            ss = jnp.sum(kn * kn, axis=-1, keepdims=True) + ss_kpe
            rs = lax.rsqrt(ss * (1.0 / QK_DIM) + RMS_EPS)
            k_ref[r0:r0 + nr, hd * HEAD_PAD:hd * HEAD_PAD + LANES] = (kn * rs * g_kn_ref[...]).astype(BF16)
            k_ref[r0:r0 + nr, hd * HEAD_PAD + LANES:(hd + 1) * HEAD_PAD] = (k_rot * rs).astype(BF16)
            vt_ref[hd * V_ROWS:hd * V_ROWS + V_DIM, r0:r0 + nr] = vh.astype(BF16).T
            vt_ref[hd * V_ROWS + V_DIM:(hd + 1) * V_ROWS, r0:r0 + nr] = jnp.ones((BF16_ROWS, nr), BF16)

        base = SUBLANES + r0
        prev = carry_ref[base - SUBLANES:base, :]
        carry_ref[base + nr - SUBLANES:base + nr, :] = cv[nr - SUBLANES:, :]
        row = lax.broadcasted_iota(jnp.int32, (SUBLANES, CONV_WIDTH), 0)

        def delayed(d):
            sh = pltpu.roll(cv, d, 0)
            head = jnp.where(row < d, pltpu.roll(prev, d, 0), sh[0:SUBLANES])
            return jnp.concatenate([head, sh[SUBLANES:]], axis=0)

        u = (conv_w_ref[2:3, :] * cv + conv_w_ref[1:2, :] * delayed(1) + conv_w_ref[0:1, :] * delayed(2))
        yc = cb * u * _silu(z_c)
        yc_ref[r0:r0 + nr, :] = (yc * _rms_scale(yc, CONV_WIDTH) * g_oc_ref[...]).astype(BF16)

        ga_ref[r0:r0 + nr, :] = _silu(z_a).astype(BF16)

    for blk in range(tm // PROJ_ROWS):
        rows_block(blk * PROJ_ROWS, PROJ_ROWS)
    carry_ref[0:SUBLANES, :] = carry_ref[tm:tm + SUBLANES, :]


def _attn_kernel(qn_ref, kn_ref, kc_ref, qc_ref, diag_n_ref, first_c_ref, last_c_ref,
                 qt_ref, k_ref, vt_ref, o_ref, s_ref, mx_ref, m_ref, acc_ref):
    step = pl.program_id(0)
    tq = qt_ref.shape[1]
    tk = k_ref.shape[0]

    @pl.when(step == 0)
    def _():
        s_ref[...] = jnp.zeros(s_ref.shape, F32)
        mx_ref[...] = jnp.zeros(mx_ref.shape, F32)

    @pl.when((first_c_ref[step] == 1) | (step == 0))
    def _():
        m_ref[...] = jnp.full(m_ref.shape, NEG_INF, F32)
        acc_ref[...] = jnp.zeros(acc_ref.shape, F32)

    half = tk // 2

    def body(masked):
        chunk = half if masked else ATTN_Q_CHUNK
        nchunk = tq // chunk
        if masked:
            assert nchunk == 2
            tri = (lax.broadcasted_iota(jnp.int32, (half, chunk), 0)
                   <= lax.broadcasted_iota(jnp.int32, (half, chunk), 1))

        def scores(hd, c):
            cols = slice(c * chunk, (c + 1) * chunk)
            qt = qt_ref[hd * HEAD_PAD:(hd + 1) * HEAD_PAD, cols]
            if not masked:
                s = _dot(k_ref[:, hd * HEAD_PAD:(hd + 1) * HEAD_PAD], qt)
                s_ref[hd, :, cols] = s
                mx_ref[hd, :, cols] = jnp.max(s, axis=0, keepdims=True)
            elif c == 0:
                s = jnp.where(tri, _dot(k_ref[0:half, hd * HEAD_PAD:(hd + 1) * HEAD_PAD], qt), NEG_INF)
                s_ref[hd, 0:half, cols] = s
                s_ref[hd, half:tk, cols] = jnp.full((half, chunk), NEG_INF, F32)
                mx_ref[hd, :, cols] = jnp.max(s, axis=0, keepdims=True)
            else:
                s = _dot(k_ref[:, hd * HEAD_PAD:(hd + 1) * HEAD_PAD], qt)
                top = s[0:half]
                bot = jnp.where(tri, s[half:tk], NEG_INF)
                s_ref[hd, 0:half, cols] = top
                s_ref[hd, half:tk, cols] = bot
                mx_ref[hd, :, cols] = jnp.maximum(jnp.max(top, axis=0, keepdims=True),
                                                  jnp.max(bot, axis=0, keepdims=True))

        def softmax_pv(hd, c):
            cols = slice(c * chunk, (c + 1) * chunk)
            m_prev = m_ref[hd, :, cols]
            m_new = jnp.maximum(m_prev, mx_ref[hd, :, cols])
            alpha = jnp.exp2(m_prev - m_new)
            p = jnp.exp2((s_ref[hd, :, cols] - m_new).astype(BF16))
            pv = _dot(vt_ref[hd * V_ROWS:(hd + 1) * V_ROWS, :], p)
            acc_ref[hd, :, cols] = alpha * acc_ref[hd, :, cols] + pv
            m_ref[hd, :, cols] = m_new

        for hd in range(N_HEADS):
            for c in range(nchunk):
                softmax_pv(hd, c)
                scores(hd, c)

    is_diag = diag_n_ref[step] == 1

    @pl.when(jnp.logical_not(is_diag))
    def _():
        body(False)

    @pl.when(is_diag)
    def _():
        body(True)

    @pl.when(last_c_ref[step] == 1)
    def _():
        for hd in range(N_HEADS):
            acc = acc_ref[hd]
            o_ref[:, hd * V_DIM:(hd + 1) * V_DIM] = (
                acc[:V_DIM] / acc[V_DIM:V_DIM + 1]).T.astype(o_ref.dtype)


def _out_kernel(o_ref, ga_ref, yc_ref, x_ref, p_ref, g_oa_ref, w_o_ref, g_pl_ref, w_plg_ref, w_pl_ref,
                out_ref):
    tm = x_ref.shape[0]

    def residual(r0):
        rows = slice(r0, r0 + OUT_ROWS)
        ya = o_ref[rows, :].astype(F32) * ga_ref[rows, :].astype(F32)
        ya_n = (ya * _rms_scale(ya, ATTN_WIDTH) * g_oa_ref[...]).astype(BF16)
        x1 = (x_ref[rows, :] + _dot(ya_n, w_o_ref[:ATTN_WIDTH, :])
              + _dot(yc_ref[rows, :], w_o_ref[ATTN_WIDTH:, :]))
        ple = _dot(p_ref[rows, :].astype(BF16), w_pl_ref[...])
        hn = (x1 * _rms_scale(x1, D_MODEL) * g_pl_ref[...]).astype(BF16)
        return r0, x1, ple, hn

    def gated(r0, x1, ple, hn):
        gate = 1.0 / (1.0 + jnp.exp(-_dot(hn, w_plg_ref[...])))
        out_ref[r0:r0 + OUT_ROWS, :] = x1 + gate * ple

    prev = None
    for r0 in range(0, tm, OUT_ROWS):
        cur = residual(r0)
        if prev is not None:
            gated(*prev)
        prev = cur
    gated(*prev)


def _row_spec(tile, width):
    return pl.BlockSpec((tile, width), lambda i: (i, 0))


def _const_spec(shape):
    return pl.BlockSpec(shape, lambda i: (0,) * len(shape))


def _triangle_schedule(batch, n_blk):
    pairs = [(b * n_blk + i, b * n_blk + j, j == 0, j == i)
             for b in range(batch) for i in range(n_blk) for j in range(i + 1)]
    nxt = pairs + [pairs[-1]]
    cur = [pairs[0]] + pairs
    cols = ([p[0] for p in nxt], [p[1] for p in nxt], [p[1] for p in cur], [p[0] for p in cur],
            [int(p[3]) for p in nxt],
            [0] + [int(p[2]) for p in pairs], [0] + [int(p[3]) for p in pairs])
    return tuple(jnp.asarray(np.asarray(c, np.int32)) for c in cols)


def _prep_in_kernel(wt_lat_ref, wt_ref, g_ref, lat_ref, wide_ref):
    g = g_ref[...]

    @pl.when(pl.program_id(0) == 0)
    def _():
        pad = jnp.zeros((LAT_PAD - LAT_SRC, D_MODEL), F32)
        lat_ref[...] = jnp.concatenate([wt_lat_ref[...] * g, pad], axis=0).T.astype(BF16)

    wide_ref[...] = (wt_ref[...] * g).T.astype(BF16)


def _prepare_w_in(w_in, g_in):
    def rows_at(nrows, start):
        return pl.BlockSpec((pl.Element(nrows), pl.Element(D_MODEL)), start)

    return pl.pallas_call(
        _prep_in_kernel,
        grid=(WIDE // PREP_COLS,),
        in_specs=[rows_at(LAT_SRC, lambda j: (0, 0)),
                  rows_at(PREP_COLS, lambda j: (pl.multiple_of(LAT_SRC + j * PREP_COLS, SUBLANES), 0)),
                  _const_spec((1, D_MODEL))],
        out_specs=[_const_spec((D_MODEL, LAT_PAD)), pl.BlockSpec((D_MODEL, PREP_COLS), lambda j: (0, j))],
        out_shape=[jax.ShapeDtypeStruct((D_MODEL, LAT_PAD), BF16),
                   jax.ShapeDtypeStruct((D_MODEL, WIDE), BF16)],
        compiler_params=pltpu.CompilerParams(dimension_semantics=("arbitrary",),
                                             vmem_limit_bytes=VMEM_LIMIT),
        name="mla_conv_w_in_prep",
    )(w_in.T, w_in.T, g_in.reshape(1, -1))


def _layer(x2, p2, posc, invf, batch, seq, g_in, w_in, g_cq, w_uq, g_ckv, w_ukv, g_q, g_k,
           conv_w, g_oa, g_oc, w_o, w_pl, w_plg, g_pl):
    tokens = batch * seq
    tm = PROJ_TILE
    assert seq % tm == 0 and seq % ATTN_TILE == 0

    w_lat, w_wide = _prepare_w_in(w_in, g_in)
    w_uq_p = jnp.pad(w_uq.reshape(Q_LORA, N_HEADS, QK_DIM),
                     ((0, 0), (0, 0), (0, HEAD_PAD - QK_DIM))).reshape(Q_LORA, N_HEADS * HEAD_PAD).astype(BF16)
    g_q_p = jnp.pad(g_q, (0, HEAD_PAD - QK_DIM)).reshape(1, HEAD_PAD)
    g_kn = g_k[:NOPE_DIM].reshape(1, NOPE_DIM)
    g_kr = jnp.pad(g_k[NOPE_DIM:], (0, KPE_PAD - ROPE_DIM)).reshape(1, KPE_PAD)

    cparams = pltpu.CompilerParams(dimension_semantics=("arbitrary",), vmem_limit_bytes=VMEM_LIMIT)

    def col_spec(rows, tile):
        return pl.BlockSpec((rows, tile), lambda i: (0, i))

    qt, k, vt, ga, yc = pl.pallas_call(
        functools.partial(_proj_kernel, seq // tm),
        grid=(tokens // tm,),
        in_specs=[
            _row_spec(tm, D_MODEL), _row_spec(tm // TOKENS_PER_ROW, LANES), _const_spec((1, LANES)),
            _const_spec((D_MODEL, LAT_PAD)), _const_spec((D_MODEL, WIDE)),
            _const_spec((1, Q_LORA)), _const_spec((Q_LORA, N_HEADS * HEAD_PAD)),
            _const_spec((1, KV_LORA)), _const_spec((KV_LORA, N_HEADS * (NOPE_DIM + V_DIM))),
            _const_spec((1, HEAD_PAD)), _const_spec((1, NOPE_DIM)), _const_spec((1, KPE_PAD)),
            _const_spec((CONV_K, CONV_WIDTH)), _const_spec((1, CONV_WIDTH)),
        ],
        out_specs=[
            col_spec(N_HEADS * HEAD_PAD, tm), _row_spec(tm, N_HEADS * HEAD_PAD),
            col_spec(N_HEADS * V_ROWS, tm), _row_spec(tm, ATTN_WIDTH), _row_spec(tm, CONV_WIDTH),
        ],
        out_shape=[
            jax.ShapeDtypeStruct((N_HEADS * HEAD_PAD, tokens), BF16),
            jax.ShapeDtypeStruct((tokens, N_HEADS * HEAD_PAD), BF16),
            jax.ShapeDtypeStruct((N_HEADS * V_ROWS, tokens), BF16),
            jax.ShapeDtypeStruct((tokens, ATTN_WIDTH), BF16),
            jax.ShapeDtypeStruct((tokens, CONV_WIDTH), BF16),
        ],
        scratch_shapes=[pltpu.VMEM((tm + SUBLANES, CONV_WIDTH), F32)],
        compiler_params=cparams,
        name="mla_conv_proj",
    )(x2, posc, invf, w_lat, w_wide, g_cq.reshape(1, -1), w_uq_p,
      g_ckv.reshape(1, -1), w_ukv.astype(BF16), g_q_p, g_kn, g_kr, conv_w, g_oc.reshape(1, -1))

    ta = ATTN_TILE
    sched = _triangle_schedule(batch, seq // ta)
    o = pl.pallas_call(
        _attn_kernel,
        grid_spec=pltpu.PrefetchScalarGridSpec(
            num_scalar_prefetch=len(sched),
            grid=(int(sched[0].shape[0]),),
            in_specs=[
                pl.BlockSpec((N_HEADS * HEAD_PAD, ta), lambda s, qn, kn, kc, qc, *_: (0, qn[s])),
                pl.BlockSpec((ta, N_HEADS * HEAD_PAD), lambda s, qn, kn, kc, qc, *_: (kn[s], 0)),
                pl.BlockSpec((N_HEADS * V_ROWS, ta), lambda s, qn, kn, kc, qc, *_: (0, kc[s])),
            ],
            out_specs=pl.BlockSpec((ta, ATTN_WIDTH), lambda s, qn, kn, kc, qc, *_: (qc[s], 0)),
            scratch_shapes=[pltpu.VMEM((N_HEADS, ta, ta), F32),
                            pltpu.VMEM((N_HEADS, 1, ta), F32),
                            pltpu.VMEM((N_HEADS, 1, ta), F32),
                            pltpu.VMEM((N_HEADS, V_ROWS, ta), F32)],
        ),
        out_shape=jax.ShapeDtypeStruct((tokens, ATTN_WIDTH), BF16),
        compiler_params=cparams,
        name="mla_flash_attn",
    )(*sched, qt, k, vt)

    return pl.pallas_call(
        _out_kernel,
        grid=(tokens // tm,),
        in_specs=[
            _row_spec(tm, ATTN_WIDTH), _row_spec(tm, ATTN_WIDTH), _row_spec(tm, CONV_WIDTH),
            _row_spec(tm, D_MODEL), _row_spec(tm, PLE_DIM),
            _const_spec((1, ATTN_WIDTH)), _const_spec((D_MODEL, D_MODEL)), _const_spec((1, D_MODEL)),
            _const_spec((D_MODEL, D_MODEL)), _const_spec((PLE_DIM, D_MODEL)),
        ],
        out_specs=_row_spec(tm, D_MODEL),
        out_shape=jax.ShapeDtypeStruct((tokens, D_MODEL), F32),
        compiler_params=cparams,
        name="mla_conv_out",
    )(o, ga, yc, x2, p2, g_oa.reshape(1, -1), w_o.astype(BF16),
      g_pl.reshape(1, -1), w_plg.astype(BF16), w_pl.astype(BF16))


def kernel(x, p, positions, g_in, w_in, g_cq, w_uq, g_ckv, w_ukv, g_q, g_k, conv_w, g_oa, g_oc,
           w_o, w_pl, w_plg, g_pl):
    batch, seq, d_model = x.shape
    depth = p.shape[0]
    tokens = batch * seq
    rows = PROJ_TILE // TOKENS_PER_ROW
    posc = jnp.repeat(positions.astype(F32).reshape(tokens // PROJ_TILE, TOKENS_PER_ROW, rows)
                      .transpose(0, 2, 1), HALF_ROPE, axis=-1).reshape(tokens // TOKENS_PER_ROW, LANES)
    inv_freq = 1.0 / (ROPE_THETA ** (jnp.arange(0, ROPE_DIM, 2, dtype=F32) / ROPE_DIM))
    invf = jnp.tile(inv_freq, LANES // HALF_ROPE).reshape(1, LANES)
    h = x.reshape(tokens, d_model)
    for i in range(depth):
        h = _layer(h, p[i].reshape(tokens, PLE_DIM), posc, invf, batch, seq,
                   g_in[i], w_in[i], g_cq[i], w_uq[i], g_ckv[i], w_ukv[i], g_q[i], g_k[i],
                   conv_w[i], g_oa[i], g_oc[i], w_o[i], w_pl[i], w_plg[i], g_pl[i])
    return h.reshape(batch, seq, d_model).astype(x.dtype)
```

```python
import functools
import math

import jax
import jax.numpy as jnp
import numpy as np
from jax import lax
from jax.experimental import pallas as pl
from jax.experimental.pallas import tpu as pltpu

D_MODEL = 1024
PLE_DIM = 256
N_HEADS = 4
NOPE_DIM = 128
ROPE_DIM = 64
HALF_ROPE = ROPE_DIM // 2
V_DIM = 128
QK_DIM = NOPE_DIM + ROPE_DIM
Q_LORA = 256
KV_LORA = 128
ATTN_WIDTH = N_HEADS * V_DIM
CONV_WIDTH = D_MODEL - ATTN_WIDTH
CONV_K = 3
ROPE_THETA = 10000.0
RMS_EPS = 1e-6
NEG_INF = -1e30

LANES = 128
SUBLANES = 8
HEAD_PAD = 2 * LANES
KPE_PAD = LANES
BF16_ROWS = 16
V_ROWS = V_DIM + BF16_ROWS
TOKENS_PER_ROW = LANES // HALF_ROPE
OFF_CQ = 0
OFF_CKV = OFF_CQ + Q_LORA
OFF_KPE = OFF_CKV + KV_LORA
LAT_SRC = OFF_KPE + ROPE_DIM
LAT_PAD = OFF_KPE + KPE_PAD
OFF_ZA = 0
OFF_CB = OFF_ZA + ATTN_WIDTH
OFF_CC = OFF_CB + CONV_WIDTH
OFF_CX = OFF_CC + CONV_WIDTH
OFF_ZC = OFF_CX + CONV_WIDTH
WIDE = OFF_ZC + CONV_WIDTH

PROJ_TILE = 1024
PROJ_ROWS = 256
OUT_ROWS = 256
PREP_COLS = 512
ATTN_TILE = 1024
ATTN_Q_CHUNK = 256
VMEM_LIMIT = 48 * 1024 * 1024

BF16 = jnp.bfloat16
F32 = jnp.float32


def _rms_scale(v, width):
    return lax.rsqrt(jnp.sum(v * v, axis=-1, keepdims=True) * (1.0 / width) + RMS_EPS)


def _silu(z):
    h = 0.5 * z
    return h + h * jnp.tanh(h)


def _dot(a, b):
    return jnp.dot(a, b, preferred_element_type=F32)


def _rope_tables(posc, invf):
    ang = posc * invf
    cosc = jnp.cos(ang)
    sinc = jnp.sin(ang)
    lane = lax.broadcasted_iota(jnp.int32, (1, LANES), 1)
    lo = lane < HALF_ROPE
    mid = (lane >= HALF_ROPE) & (lane < ROPE_DIM)
    quarters = []
    for a in range(TOKENS_PER_ROW):
        c = pltpu.roll(cosc, LANES - HALF_ROPE * a, 1) if a else cosc
        s = pltpu.roll(sinc, LANES - HALF_ROPE * a, 1) if a else sinc
        quarters.append((jnp.where(lo, c, jnp.where(mid, pltpu.roll(c, HALF_ROPE, 1), 0.0)),
                         jnp.where(lo, -s, jnp.where(mid, pltpu.roll(s, HALF_ROPE, 1), 0.0))))
    return quarters, lo


def _rope(t, c, s, lo):
    swapped = jnp.where(lo, pltpu.roll(t, LANES - HALF_ROPE, 1), pltpu.roll(t, HALF_ROPE, 1))
    return t * c + swapped * s


def _proj_kernel(tiles_per_seq,
                 x_ref, posc_ref, invf_ref, w_lat_ref, w_in_ref, g_cq_ref, w_uq_ref,
                 g_ckv_ref, w_ukv_ref, g_q_ref, g_kn_ref, g_kr_ref, conv_w_ref, g_oc_ref,
                 qt_ref, k_ref, vt_ref, ga_ref, yc_ref, carry_ref):
    tm = x_ref.shape[0]
    quarter_rows = tm // TOKENS_PER_ROW
    tables = []
    q_scale = math.log2(math.e) / math.sqrt(QK_DIM)

    @pl.when(pl.program_id(0) % tiles_per_seq == 0)
    def _():
        carry_ref[0:SUBLANES, :] = jnp.zeros((SUBLANES, CONV_WIDTH), F32)

    def rows_block(r0, nr):
        x = x_ref[r0:r0 + nr, :]
        h = (x * _rms_scale(x, D_MODEL)).astype(BF16)

        def proj(off, width):
            return _dot(h, w_in_ref[:, off:off + width])

        lat = _dot(h, w_lat_ref[...])
        c_q = lat[:, OFF_CQ:OFF_CKV]
        c_kv = lat[:, OFF_CKV:OFF_KPE]
        kpe = lat[:, OFF_KPE:LAT_PAD]
        cv = proj(OFF_CC, CONV_WIDTH) * proj(OFF_CX, CONV_WIDTH)
        cqn = (c_q * _rms_scale(c_q, Q_LORA) * g_cq_ref[...]).astype(BF16)
        qf = _dot(cqn, w_uq_ref[...])
        cb = proj(OFF_CB, CONV_WIDTH)
        ckvn = (c_kv * _rms_scale(c_kv, KV_LORA) * g_ckv_ref[...]).astype(BF16)
        kv = _dot(ckvn, w_ukv_ref[...])
        z_c = proj(OFF_ZC, CONV_WIDTH)
        z_a = proj(OFF_ZA, ATTN_WIDTH)

        if not tables:
            tables.extend(_rope_tables(posc_ref[...], invf_ref[...]))
        quarters, lo = tables

        def table_rows(which):
            pieces, r = [], r0
            while r < r0 + nr:
                qa, off = divmod(r, quarter_rows)
                n = min(quarter_rows - off, r0 + nr - r)
                pieces.append(quarters[qa][which][off:off + n])
                r += n
            return pieces[0] if len(pieces) == 1 else jnp.concatenate(pieces, axis=0)

        cos_t, sin_t = table_rows(0), table_rows(1)

        for hd in range(N_HEADS):
            qh = qf[:, hd * HEAD_PAD:(hd + 1) * HEAD_PAD]
            qn = qh * (_rms_scale(qh, QK_DIM) * q_scale) * g_q_ref[...]
            qt_ref[hd * HEAD_PAD:hd * HEAD_PAD + LANES, r0:r0 + nr] = qn[:, :LANES].astype(BF16).T
            qt_ref[hd * HEAD_PAD + LANES:(hd + 1) * HEAD_PAD, r0:r0 + nr] = (
                _rope(qn[:, LANES:], cos_t, sin_t, lo).astype(BF16).T)

        ss_kpe = jnp.sum(kpe * kpe, axis=-1, keepdims=True)
        k_rot = _rope(kpe * g_kr_ref[...], cos_t, sin_t, lo)
        for hd in range(N_HEADS):
            kn = kv[:, hd * (NOPE_DIM + V_DIM):hd * (NOPE_DIM + V_DIM) + NOPE_DIM]
            vh = kv[:, hd * (NOPE_DIM + V_DIM) + NOPE_DIM:(hd + 1) * (NOPE_DIM + V_DIM)]
            ss = jnp.sum(kn * kn, axis=-1, keepdims=True) + ss_kpe
            rs = lax.rsqrt(ss * (1.0 / QK_DIM) + RMS_EPS)
            k_ref[r0:r0 + nr, hd * HEAD_PAD:hd * HEAD_PAD + LANES] = (kn * rs * g_kn_ref[...]).astype(BF16)
            k_ref[r0:r0 + nr, hd * HEAD_PAD + LANES:(hd + 1) * HEAD_PAD] = (k_rot * rs).astype(BF16)
            vt_ref[hd * V_ROWS:hd * V_ROWS + V_DIM, r0:r0 + nr] = vh.astype(BF16).T
            vt_ref[hd * V_ROWS + V_DIM:(hd + 1) * V_ROWS, r0:r0 + nr] = jnp.ones((BF16_ROWS, nr), BF16)

        base = SUBLANES + r0
        prev = carry_ref[base - SUBLANES:base, :]
        carry_ref[base + nr - SUBLANES:base + nr, :] = cv[nr - SUBLANES:, :]
        row = lax.broadcasted_iota(jnp.int32, (SUBLANES, CONV_WIDTH), 0)

        def delayed(d):
            sh = pltpu.roll(cv, d, 0)
            head = jnp.where(row < d, pltpu.roll(prev, d, 0), sh[0:SUBLANES])
            return jnp.concatenate([head, sh[SUBLANES:]], axis=0)

        u = (conv_w_ref[2:3, :] * cv + conv_w_ref[1:2, :] * delayed(1) + conv_w_ref[0:1, :] * delayed(2))
        yc = cb * u * _silu(z_c)
        yc_ref[r0:r0 + nr, :] = (yc * _rms_scale(yc, CONV_WIDTH) * g_oc_ref[...]).astype(BF16)

        ga_ref[r0:r0 + nr, :] = _silu(z_a).astype(BF16)

    for blk in range(tm // PROJ_ROWS):
        rows_block(blk * PROJ_ROWS, PROJ_ROWS)
    carry_ref[0:SUBLANES, :] = carry_ref[tm:tm + SUBLANES, :]


def _attn_kernel(qn_ref, kn_ref, kc_ref, qc_ref, diag_n_ref, first_c_ref, last_c_ref,
                 qt_ref, k_ref, vt_ref, o_ref, s_ref, mx_ref, m_ref, acc_ref):
    step = pl.program_id(0)
    tq = qt_ref.shape[1]
    tk = k_ref.shape[0]

    @pl.when(step == 0)
    def _():
        s_ref[...] = jnp.zeros(s_ref.shape, F32)
        mx_ref[...] = jnp.zeros(mx_ref.shape, F32)

    @pl.when((first_c_ref[step] == 1) | (step == 0))
    def _():
        m_ref[...] = jnp.full(m_ref.shape, NEG_INF, F32)
        acc_ref[...] = jnp.zeros(acc_ref.shape, F32)

    half = tk // 2

    def body(masked):
        chunk = half if masked else ATTN_Q_CHUNK
        nchunk = tq // chunk
        if masked:
            assert nchunk == 2
            tri = (lax.broadcasted_iota(jnp.int32, (half, chunk), 0)
                   <= lax.broadcasted_iota(jnp.int32, (half, chunk), 1))

        def scores(hd, c):
            cols = slice(c * chunk, (c + 1) * chunk)
            qt = qt_ref[hd * HEAD_PAD:(hd + 1) * HEAD_PAD, cols]
            if not masked:
                s = _dot(k_ref[:, hd * HEAD_PAD:(hd + 1) * HEAD_PAD], qt)
                s_ref[hd, :, cols] = s
                mx_ref[hd, :, cols] = jnp.max(s, axis=0, keepdims=True)
            elif c == 0:
                s = jnp.where(tri, _dot(k_ref[0:half, hd * HEAD_PAD:(hd + 1) * HEAD_PAD], qt), NEG_INF)
                s_ref[hd, 0:half, cols] = s
                s_ref[hd, half:tk, cols] = jnp.full((half, chunk), NEG_INF, F32)
                mx_ref[hd, :, cols] = jnp.max(s, axis=0, keepdims=True)
            else:
                s = _dot(k_ref[:, hd * HEAD_PAD:(hd + 1) * HEAD_PAD], qt)
                top = s[0:half]
                bot = jnp.where(tri, s[half:tk], NEG_INF)
                s_ref[hd, 0:half, cols] = top
                s_ref[hd, half:tk, cols] = bot
                mx_ref[hd, :, cols] = jnp.maximum(jnp.max(top, axis=0, keepdims=True),
                                                  jnp.max(bot, axis=0, keepdims=True))

        def softmax_pv(hd, c):
            cols = slice(c * chunk, (c + 1) * chunk)
            m_prev = m_ref[hd, :, cols]
            m_new = jnp.maximum(m_prev, mx_ref[hd, :, cols])
            alpha = jnp.exp2(m_prev - m_new)
            p = jnp.exp2((s_ref[hd, :, cols] - m_new).astype(BF16))
            pv = _dot(vt_ref[hd * V_ROWS:(hd + 1) * V_ROWS, :], p)
            acc_ref[hd, :, cols] = alpha * acc_ref[hd, :, cols] + pv
            m_ref[hd, :, cols] = m_new

        for hd in range(N_HEADS):
            for c in range(nchunk):
                softmax_pv(hd, c)
                scores(hd, c)

    is_diag = diag_n_ref[step] == 1

    @pl.when(jnp.logical_not(is_diag))
    def _():
        body(False)

    @pl.when(is_diag)
    def _():
        body(True)

    @pl.when(last_c_ref[step] == 1)
    def _():
        for hd in range(N_HEADS):
            acc = acc_ref[hd]
            o_ref[:, hd * V_DIM:(hd + 1) * V_DIM] = (
                acc[:V_DIM] / acc[V_DIM:V_DIM + 1]).T.astype(o_ref.dtype)


def _out_kernel(o_ref, ga_ref, yc_ref, x_ref, p_ref, g_oa_ref, w_o_ref, g_pl_ref, w_plg_ref, w_pl_ref,
                out_ref):
    tm = x_ref.shape[0]

    def residual(r0):
        rows = slice(r0, r0 + OUT_ROWS)
        ya = o_ref[rows, :].astype(F32) * ga_ref[rows, :].astype(F32)
        ya_n = (ya * _rms_scale(ya, ATTN_WIDTH) * g_oa_ref[...]).astype(BF16)
        x1 = (x_ref[rows, :] + _dot(ya_n, w_o_ref[:ATTN_WIDTH, :])
              + _dot(yc_ref[rows, :], w_o_ref[ATTN_WIDTH:, :]))
        out_ref[rows, :] = x1
        hn = (x1 * _rms_scale(x1, D_MODEL) * g_pl_ref[...]).astype(BF16)
        return r0, hn

    def gated(r0, hn):
        rows = slice(r0, r0 + OUT_ROWS)
        gate = 1.0 / (1.0 + jnp.exp(-_dot(hn, w_plg_ref[...])))
        ple = _dot(p_ref[rows, :].astype(BF16), w_pl_ref[...])
        out_ref[rows, :] = out_ref[rows, :] + gate * ple

    prev = None
    for r0 in range(0, tm, OUT_ROWS):
        cur = residual(r0)
        if prev is not None:
            gated(*prev)
        prev = cur
    gated(*prev)


def _row_spec(tile, width):
    return pl.BlockSpec((tile, width), lambda i: (i, 0))


def _const_spec(shape):
    return pl.BlockSpec(shape, lambda i: (0,) * len(shape))


def _triangle_schedule(batch, n_blk):
    pairs = [(b * n_blk + i, b * n_blk + j, j == 0, j == i)
             for b in range(batch) for i in range(n_blk) for j in range(i + 1)]
    nxt = pairs + [pairs[-1]]
    cur = [pairs[0]] + pairs
    cols = ([p[0] for p in nxt], [p[1] for p in nxt], [p[1] for p in cur], [p[0] for p in cur],
            [int(p[3]) for p in nxt],
            [0] + [int(p[2]) for p in pairs], [0] + [int(p[3]) for p in pairs])
    return tuple(jnp.asarray(np.asarray(c, np.int32)) for c in cols)


def _prep_in_kernel(wt_lat_ref, wt_ref, g_ref, lat_ref, wide_ref):
    g = g_ref[...]

    @pl.when(pl.program_id(0) == 0)
    def _():
        pad = jnp.zeros((LAT_PAD - LAT_SRC, D_MODEL), F32)
        lat_ref[...] = jnp.concatenate([wt_lat_ref[...] * g, pad], axis=0).T.astype(BF16)

    wide_ref[...] = (wt_ref[...] * g).T.astype(BF16)


def _prepare_w_in(w_in, g_in):
    def rows_at(nrows, start):
        return pl.BlockSpec((pl.Element(nrows), pl.Element(D_MODEL)), start)

    return pl.pallas_call(
        _prep_in_kernel,
        grid=(WIDE // PREP_COLS,),
        in_specs=[rows_at(LAT_SRC, lambda j: (0, 0)),
                  rows_at(PREP_COLS, lambda j: (pl.multiple_of(LAT_SRC + j * PREP_COLS, SUBLANES), 0)),
                  _const_spec((1, D_MODEL))],
        out_specs=[_const_spec((D_MODEL, LAT_PAD)), pl.BlockSpec((D_MODEL, PREP_COLS), lambda j: (0, j))],
        out_shape=[jax.ShapeDtypeStruct((D_MODEL, LAT_PAD), BF16),
                   jax.ShapeDtypeStruct((D_MODEL, WIDE), BF16)],
        compiler_params=pltpu.CompilerParams(dimension_semantics=("arbitrary",),
                                             vmem_limit_bytes=VMEM_LIMIT),
        name="mla_conv_w_in_prep",
    )(w_in.T, w_in.T, g_in.reshape(1, -1))


def _layer(x2, p2, posc, invf, batch, seq, g_in, w_in, g_cq, w_uq, g_ckv, w_ukv, g_q, g_k,
           conv_w, g_oa, g_oc, w_o, w_pl, w_plg, g_pl):
    tokens = batch * seq
    tm = PROJ_TILE
    assert seq % tm == 0 and seq % ATTN_TILE == 0

    w_lat, w_wide = _prepare_w_in(w_in, g_in)
    w_uq_p = jnp.pad(w_uq.reshape(Q_LORA, N_HEADS, QK_DIM),
                     ((0, 0), (0, 0), (0, HEAD_PAD - QK_DIM))).reshape(Q_LORA, N_HEADS * HEAD_PAD).astype(BF16)
    g_q_p = jnp.pad(g_q, (0, HEAD_PAD - QK_DIM)).reshape(1, HEAD_PAD)
    g_kn = g_k[:NOPE_DIM].reshape(1, NOPE_DIM)
    g_kr = jnp.pad(g_k[NOPE_DIM:], (0, KPE_PAD - ROPE_DIM)).reshape(1, KPE_PAD)

    cparams = pltpu.CompilerParams(dimension_semantics=("arbitrary",), vmem_limit_bytes=VMEM_LIMIT)

    def col_spec(rows, tile):
        return pl.BlockSpec((rows, tile), lambda i: (0, i))

    qt, k, vt, ga, yc = pl.pallas_call(
        functools.partial(_proj_kernel, seq // tm),
        grid=(tokens // tm,),
        in_specs=[
            _row_spec(tm, D_MODEL), _row_spec(tm // TOKENS_PER_ROW, LANES), _const_spec((1, LANES)),
            _const_spec((D_MODEL, LAT_PAD)), _const_spec((D_MODEL, WIDE)),
            _const_spec((1, Q_LORA)), _const_spec((Q_LORA, N_HEADS * HEAD_PAD)),
            _const_spec((1, KV_LORA)), _const_spec((KV_LORA, N_HEADS * (NOPE_DIM + V_DIM))),
            _const_spec((1, HEAD_PAD)), _const_spec((1, NOPE_DIM)), _const_spec((1, KPE_PAD)),
            _const_spec((CONV_K, CONV_WIDTH)), _const_spec((1, CONV_WIDTH)),
        ],
        out_specs=[
            col_spec(N_HEADS * HEAD_PAD, tm), _row_spec(tm, N_HEADS * HEAD_PAD),
            col_spec(N_HEADS * V_ROWS, tm), _row_spec(tm, ATTN_WIDTH), _row_spec(tm, CONV_WIDTH),
        ],
        out_shape=[
            jax.ShapeDtypeStruct((N_HEADS * HEAD_PAD, tokens), BF16),
            jax.ShapeDtypeStruct((tokens, N_HEADS * HEAD_PAD), BF16),
            jax.ShapeDtypeStruct((N_HEADS * V_ROWS, tokens), BF16),
            jax.ShapeDtypeStruct((tokens, ATTN_WIDTH), BF16),
            jax.ShapeDtypeStruct((tokens, CONV_WIDTH), BF16),
        ],
        scratch_shapes=[pltpu.VMEM((tm + SUBLANES, CONV_WIDTH), F32)],
        compiler_params=cparams,
        name="mla_conv_proj",
    )(x2, posc, invf, w_lat, w_wide, g_cq.reshape(1, -1), w_uq_p,
      g_ckv.reshape(1, -1), w_ukv.astype(BF16), g_q_p, g_kn, g_kr, conv_w, g_oc.reshape(1, -1))

    ta = ATTN_TILE
    sched = _triangle_schedule(batch, seq // ta)
    o = pl.pallas_call(
        _attn_kernel,
        grid_spec=pltpu.PrefetchScalarGridSpec(
            num_scalar_prefetch=len(sched),
            grid=(int(sched[0].shape[0]),),
            in_specs=[
                pl.BlockSpec((N_HEADS * HEAD_PAD, ta), lambda s, qn, kn, kc, qc, *_: (0, qn[s])),
                pl.BlockSpec((ta, N_HEADS * HEAD_PAD), lambda s, qn, kn, kc, qc, *_: (kn[s], 0)),
                pl.BlockSpec((N_HEADS * V_ROWS, ta), lambda s, qn, kn, kc, qc, *_: (0, kc[s])),
            ],
            out_specs=pl.BlockSpec((ta, ATTN_WIDTH), lambda s, qn, kn, kc, qc, *_: (qc[s], 0)),
            scratch_shapes=[pltpu.VMEM((N_HEADS, ta, ta), F32),
                            pltpu.VMEM((N_HEADS, 1, ta), F32),
                            pltpu.VMEM((N_HEADS, 1, ta), F32),
                            pltpu.VMEM((N_HEADS, V_ROWS, ta), F32)],
        ),
        out_shape=jax.ShapeDtypeStruct((tokens, ATTN_WIDTH), BF16),
        compiler_params=cparams,
        name="mla_flash_attn",
    )(*sched, qt, k, vt)

    return pl.pallas_call(
        _out_kernel,
        grid=(tokens // tm,),
        in_specs=[
            _row_spec(tm, ATTN_WIDTH), _row_spec(tm, ATTN_WIDTH), _row_spec(tm, CONV_WIDTH),
            _row_spec(tm, D_MODEL), _row_spec(tm, PLE_DIM),
            _const_spec((1, ATTN_WIDTH)), _const_spec((D_MODEL, D_MODEL)), _const_spec((1, D_MODEL)),
            _const_spec((D_MODEL, D_MODEL)), _const_spec((PLE_DIM, D_MODEL)),
        ],
        out_specs=_row_spec(tm, D_MODEL),
        out_shape=jax.ShapeDtypeStruct((tokens, D_MODEL), F32),
        compiler_params=cparams,
        name="mla_conv_out",
    )(o, ga, yc, x2, p2, g_oa.reshape(1, -1), w_o.astype(BF16),
      g_pl.reshape(1, -1), w_plg.astype(BF16), w_pl.astype(BF16))


def kernel(x, p, positions, g_in, w_in, g_cq, w_uq, g_ckv, w_ukv, g_q, g_k, conv_w, g_oa, g_oc,
           w_o, w_pl, w_plg, g_pl):
    batch, seq, d_model = x.shape
    depth = p.shape[0]
    tokens = batch * seq
    rows = PROJ_TILE // TOKENS_PER_ROW
    posc = jnp.repeat(positions.astype(F32).reshape(tokens // PROJ_TILE, TOKENS_PER_ROW, rows)
                      .transpose(0, 2, 1), HALF_ROPE, axis=-1).reshape(tokens // TOKENS_PER_ROW, LANES)
    inv_freq = 1.0 / (ROPE_THETA ** (jnp.arange(0, ROPE_DIM, 2, dtype=F32) / ROPE_DIM))
    invf = jnp.tile(inv_freq, LANES // HALF_ROPE).reshape(1, LANES)
    h = x.reshape(tokens, d_model)
    for i in range(depth):
        h = _layer(h, p[i].reshape(tokens, PLE_DIM), posc, invf, batch, seq,
                   g_in[i], w_in[i], g_cq[i], w_uq[i], g_ckv[i], w_ukv[i], g_q[i], g_k[i],
                   conv_w[i], g_oa[i], g_oc[i], w_o[i], w_pl[i], w_plg[i], g_pl[i])
    return h.reshape(batch, seq, d_model).astype(x.dtype)
```

```python
import functools
import math

import jax
import jax.numpy as jnp
import numpy as np
from jax import lax
from jax.experimental import pallas as pl
from jax.experimental.pallas import tpu as pltpu

D_MODEL = 1024
PLE_DIM = 256
N_HEADS = 4
NOPE_DIM = 128
ROPE_DIM = 64
HALF_ROPE = ROPE_DIM // 2
V_DIM = 128
QK_DIM = NOPE_DIM + ROPE_DIM
Q_LORA = 256
KV_LORA = 128
ATTN_WIDTH = N_HEADS * V_DIM
CONV_WIDTH = D_MODEL - ATTN_WIDTH
CONV_K = 3
ROPE_THETA = 10000.0
RMS_EPS = 1e-6
NEG_INF = -1e30

LANES = 128
SUBLANES = 8
HEAD_PAD = 2 * LANES
KPE_PAD = LANES
BF16_ROWS = 16
V_ROWS = V_DIM + BF16_ROWS
TOKENS_PER_ROW = LANES // HALF_ROPE
OFF_CQ = 0
OFF_CKV = OFF_CQ + Q_LORA
OFF_KPE = OFF_CKV + KV_LORA
LAT_SRC = OFF_KPE + ROPE_DIM
LAT_PAD = OFF_KPE + KPE_PAD
OFF_ZA = 0
OFF_CB = OFF_ZA + ATTN_WIDTH
OFF_CC = OFF_CB + CONV_WIDTH
OFF_CX = OFF_CC + CONV_WIDTH
OFF_ZC = OFF_CX + CONV_WIDTH
WIDE = OFF_ZC + CONV_WIDTH

PROJ_TILE = 1024
PROJ_ROWS = 256
OUT_ROWS = 256
PREP_COLS = 512
ATTN_TILE = 1024
ATTN_Q_CHUNK = 256
VMEM_LIMIT = 48 * 1024 * 1024

BF16 = jnp.bfloat16
F32 = jnp.float32


def _rms_scale(v, width):
    return lax.rsqrt(jnp.sum(v * v, axis=-1, keepdims=True) * (1.0 / width) + RMS_EPS)


def _silu(z):
    h = 0.5 * z
    return h + h * jnp.tanh(h)


def _dot(a, b):
    return jnp.dot(a, b, preferred_element_type=F32)


def _rope_tables(posc, invf):
    ang = posc * invf
    cosc = jnp.cos(ang)
    sinc = jnp.sin(ang)
    lane = lax.broadcasted_iota(jnp.int32, (1, LANES), 1)
    lo = lane < HALF_ROPE
    mid = (lane >= HALF_ROPE) & (lane < ROPE_DIM)
    quarters = []
    for a in range(TOKENS_PER_ROW):
        c = pltpu.roll(cosc, LANES - HALF_ROPE * a, 1) if a else cosc
        s = pltpu.roll(sinc, LANES - HALF_ROPE * a, 1) if a else sinc
        quarters.append((jnp.where(lo, c, jnp.where(mid, pltpu.roll(c, HALF_ROPE, 1), 0.0)),
                         jnp.where(lo, -s, jnp.where(mid, pltpu.roll(s, HALF_ROPE, 1), 0.0))))
    return quarters, lo


def _rope(t, c, s, lo):
    swapped = jnp.where(lo, pltpu.roll(t, LANES - HALF_ROPE, 1), pltpu.roll(t, HALF_ROPE, 1))
    return t * c + swapped * s


def _proj_kernel(tiles_per_seq,
                 x_ref, posc_ref, invf_ref, w_lat_ref, w_in_ref, g_cq_ref, w_uq_ref,
                 g_ckv_ref, w_ukv_ref, g_q_ref, g_kn_ref, g_kr_ref, conv_w_ref, g_oc_ref,
                 qt_ref, k_ref, vt_ref, ga_ref, yc_ref, carry_ref):
    tm = x_ref.shape[0]
    quarter_rows = tm // TOKENS_PER_ROW
    tables = []
    q_scale = math.log2(math.e) / math.sqrt(QK_DIM)

    @pl.when(pl.program_id(0) % tiles_per_seq == 0)
    def _():
        carry_ref[0:SUBLANES, :] = jnp.zeros((SUBLANES, CONV_WIDTH), F32)

    def rows_block(r0, nr):
        x = x_ref[r0:r0 + nr, :]
        h = (x * _rms_scale(x, D_MODEL)).astype(BF16)

        def proj(off, width):
            return _dot(h, w_in_ref[:, off:off + width])

        lat = _dot(h, w_lat_ref[...])
        c_q = lat[:, OFF_CQ:OFF_CKV]
        c_kv = lat[:, OFF_CKV:OFF_KPE]
        kpe = lat[:, OFF_KPE:LAT_PAD]
        cv = proj(OFF_CC, CONV_WIDTH) * proj(OFF_CX, CONV_WIDTH)
        cqn = (c_q * _rms_scale(c_q, Q_LORA) * g_cq_ref[...]).astype(BF16)
        qf = _dot(cqn, w_uq_ref[...])
        cb = proj(OFF_CB, CONV_WIDTH)
        ckvn = (c_kv * _rms_scale(c_kv, KV_LORA) * g_ckv_ref[...]).astype(BF16)
        kv = _dot(ckvn, w_ukv_ref[...])
        z_c = proj(OFF_ZC, CONV_WIDTH)
        z_a = proj(OFF_ZA, ATTN_WIDTH)

        if not tables:
            tables.extend(_rope_tables(posc_ref[...], invf_ref[...]))
        quarters, lo = tables

        def table_rows(which):
            pieces, r = [], r0
            while r < r0 + nr:
                qa, off = divmod(r, quarter_rows)
                n = min(quarter_rows - off, r0 + nr - r)
                pieces.append(quarters[qa][which][off:off + n])
                r += n
            return pieces[0] if len(pieces) == 1 else jnp.concatenate(pieces, axis=0)

        cos_t, sin_t = table_rows(0), table_rows(1)

        for hd in range(N_HEADS):
            qh = qf[:, hd * HEAD_PAD:(hd + 1) * HEAD_PAD]
            qn = qh * (_rms_scale(qh, QK_DIM) * q_scale) * g_q_ref[...]
            qt_ref[hd * HEAD_PAD:hd * HEAD_PAD + LANES, r0:r0 + nr] = qn[:, :LANES].astype(BF16).T
            qt_ref[hd * HEAD_PAD + LANES:(hd + 1) * HEAD_PAD, r0:r0 + nr] = (
                _rope(qn[:, LANES:], cos_t, sin_t, lo).astype(BF16).T)

        ss_kpe = jnp.sum(kpe * kpe, axis=-1, keepdims=True)
        k_rot = _rope(kpe * g_kr_ref[...], cos_t, sin_t, lo)
        for hd in range(N_HEADS):
            kn = kv[:, hd * (NOPE_DIM + V_DIM):hd * (NOPE_DIM + V_DIM) + NOPE_DIM]
            vh = kv[:, hd * (NOPE_DIM + V_DIM) + NOPE_DIM:(hd + 1) * (NOPE_DIM + V_DIM)]
            ss = jnp.sum(kn * kn, axis=-1, keepdims=True) + ss_kpe
            rs = lax.rsqrt(ss * (1.0 / QK_DIM) + RMS_EPS)
            k_ref[r0:r0 + nr, hd * HEAD_PAD:hd * HEAD_PAD + LANES] = (kn * rs * g_kn_ref[...]).astype(BF16)
            k_ref[r0:r0 + nr, hd * HEAD_PAD + LANES:(hd + 1) * HEAD_PAD] = (k_rot * rs).astype(BF16)
            vt_ref[hd * V_ROWS:hd * V_ROWS + V_DIM, r0:r0 + nr] = vh.astype(BF16).T
            vt_ref[hd * V_ROWS + V_DIM:(hd + 1) * V_ROWS, r0:r0 + nr] = jnp.ones((BF16_ROWS, nr), BF16)

        base = SUBLANES + r0
        prev = carry_ref[base - SUBLANES:base, :]
        carry_ref[base + nr - SUBLANES:base + nr, :] = cv[nr - SUBLANES:, :]
        row = lax.broadcasted_iota(jnp.int32, (SUBLANES, CONV_WIDTH), 0)

        def delayed(d):
            sh = pltpu.roll(cv, d, 0)
            head = jnp.where(row < d, pltpu.roll(prev, d, 0), sh[0:SUBLANES])
            return jnp.concatenate([head, sh[SUBLANES:]], axis=0)

        u = (conv_w_ref[2:3, :] * cv + conv_w_ref[1:2, :] * delayed(1) + conv_w_ref[0:1, :] * delayed(2))
        yc = cb * u * _silu(z_c)
        yc_ref[r0:r0 + nr, :] = (yc * _rms_scale(yc, CONV_WIDTH) * g_oc_ref[...]).astype(BF16)

        ga_ref[r0:r0 + nr, :] = _silu(z_a).astype(BF16)

    for blk in range(tm // PROJ_ROWS):
        rows_block(blk * PROJ_ROWS, PROJ_ROWS)
    carry_ref[0:SUBLANES, :] = carry_ref[tm:tm + SUBLANES, :]


def _attn_kernel(qn_ref, kn_ref, kc_ref, qc_ref, diag_n_ref, first_c_ref, last_c_ref,
                 qt_ref, k_ref, vt_ref, o_ref, s_ref, mx_ref, m_ref, acc_ref):
    step = pl.program_id(0)
    tq = qt_ref.shape[1]
    tk = k_ref.shape[0]

    @pl.when(step == 0)
    def _():
        s_ref[...] = jnp.zeros(s_ref.shape, F32)
        mx_ref[...] = jnp.zeros(mx_ref.shape, F32)

    @pl.when((first_c_ref[step] == 1) | (step == 0))
    def _():
        m_ref[...] = jnp.full(m_ref.shape, NEG_INF, F32)
        acc_ref[...] = jnp.zeros(acc_ref.shape, F32)

    chunk = ATTN_Q_CHUNK
    nchunk = tq // chunk

    def body(diag_cur, diag_next):
        if diag_next:
            tri = (lax.broadcasted_iota(jnp.int32, (chunk, chunk), 0)
                   <= lax.broadcasted_iota(jnp.int32, (chunk, chunk), 1))

        def scores(hd, c):
            cols = slice(c * chunk, (c + 1) * chunk)
            rows = chunk * (c + 1) if diag_next else tk
            s = _dot(k_ref[0:rows, hd * HEAD_PAD:(hd + 1) * HEAD_PAD],
                     qt_ref[hd * HEAD_PAD:(hd + 1) * HEAD_PAD, cols])
            if not diag_next:
                s_ref[hd, :, cols] = s
                mx_ref[hd, :, cols] = jnp.max(s, axis=0, keepdims=True)
                return
            last = jnp.where(tri, s[rows - chunk:rows], NEG_INF)
            s_ref[hd, rows - chunk:rows, cols] = last
            mx = jnp.max(last, axis=0, keepdims=True)
            if rows > chunk:
                s_ref[hd, 0:rows - chunk, cols] = s[0:rows - chunk]
                mx = jnp.maximum(mx, jnp.max(s[0:rows - chunk], axis=0, keepdims=True))
            mx_ref[hd, :, cols] = mx

        def softmax_pv(hd, c):
            cols = slice(c * chunk, (c + 1) * chunk)
            rows = chunk * (c + 1) if diag_cur else tk
            m_prev = m_ref[hd, :, cols]
            m_new = jnp.maximum(m_prev, mx_ref[hd, :, cols])
            alpha = jnp.exp2(m_prev - m_new)
            p = jnp.exp2((s_ref[hd, 0:rows, cols] - m_new).astype(BF16))
            pv = _dot(vt_ref[hd * V_ROWS:(hd + 1) * V_ROWS, 0:rows], p)
            acc_ref[hd, :, cols] = alpha * acc_ref[hd, :, cols] + pv
            m_ref[hd, :, cols] = m_new

        for hd in range(N_HEADS):
            for c in range(nchunk):
                softmax_pv(hd, c)
                scores(hd, c)

    diag_cur = last_c_ref[step] == 1
    diag_next = diag_n_ref[step] == 1
    for cur_flag in (False, True):
        for next_flag in (False, True):
            @pl.when((diag_cur == cur_flag) & (diag_next == next_flag))
            def _(cur_flag=cur_flag, next_flag=next_flag):
                body(cur_flag, next_flag)


    @pl.when(last_c_ref[step] == 1)
    def _():
        for hd in range(N_HEADS):
            acc = acc_ref[hd]
            o_ref[:, hd * V_DIM:(hd + 1) * V_DIM] = (
                acc[:V_DIM] / acc[V_DIM:V_DIM + 1]).T.astype(o_ref.dtype)


def _out_kernel(o_ref, ga_ref, yc_ref, x_ref, p_ref, g_oa_ref, w_o_ref, g_pl_ref, w_plg_ref, w_pl_ref,
                out_ref):
    tm = x_ref.shape[0]

    def residual(r0):
        rows = slice(r0, r0 + OUT_ROWS)
        ya = o_ref[rows, :].astype(F32) * ga_ref[rows, :].astype(F32)
        ya_n = (ya * _rms_scale(ya, ATTN_WIDTH) * g_oa_ref[...]).astype(BF16)
        x1 = (x_ref[rows, :] + _dot(ya_n, w_o_ref[:ATTN_WIDTH, :])
              + _dot(yc_ref[rows, :], w_o_ref[ATTN_WIDTH:, :]))
        out_ref[rows, :] = x1
        hn = (x1 * _rms_scale(x1, D_MODEL) * g_pl_ref[...]).astype(BF16)
        return r0, hn

    def gated(r0, hn):
        rows = slice(r0, r0 + OUT_ROWS)
        gate = 1.0 / (1.0 + jnp.exp(-_dot(hn, w_plg_ref[...])))
        ple = _dot(p_ref[rows, :].astype(BF16), w_pl_ref[...])
        out_ref[rows, :] = out_ref[rows, :] + gate * ple

    prev = None
    for r0 in range(0, tm, OUT_ROWS):
        cur = residual(r0)
        if prev is not None:
            gated(*prev)
        prev = cur
    gated(*prev)


def _row_spec(tile, width):
    return pl.BlockSpec((tile, width), lambda i: (i, 0))


def _const_spec(shape):
    return pl.BlockSpec(shape, lambda i: (0,) * len(shape))


def _triangle_schedule(batch, n_blk):
    pairs = [(b * n_blk + i, b * n_blk + j, j == 0, j == i)
             for b in range(batch) for i in range(n_blk) for j in range(i + 1)]
    nxt = pairs + [pairs[-1]]
    cur = [pairs[0]] + pairs
    cols = ([p[0] for p in nxt], [p[1] for p in nxt], [p[1] for p in cur], [p[0] for p in cur],
            [int(p[3]) for p in nxt],
            [0] + [int(p[2]) for p in pairs], [0] + [int(p[3]) for p in pairs])
    return tuple(jnp.asarray(np.asarray(c, np.int32)) for c in cols)


def _prep_in_kernel(wt_lat_ref, wt_ref, g_ref, lat_ref, wide_ref):
    g = g_ref[...]

    @pl.when(pl.program_id(0) == 0)
    def _():
        pad = jnp.zeros((LAT_PAD - LAT_SRC, D_MODEL), F32)
        lat_ref[...] = jnp.concatenate([wt_lat_ref[...] * g, pad], axis=0).T.astype(BF16)

    wide_ref[...] = (wt_ref[...] * g).T.astype(BF16)


def _prepare_w_in(w_in, g_in):
    def rows_at(nrows, start):
        return pl.BlockSpec((pl.Element(nrows), pl.Element(D_MODEL)), start)

    return pl.pallas_call(
        _prep_in_kernel,
        grid=(WIDE // PREP_COLS,),
        in_specs=[rows_at(LAT_SRC, lambda j: (0, 0)),
                  rows_at(PREP_COLS, lambda j: (pl.multiple_of(LAT_SRC + j * PREP_COLS, SUBLANES), 0)),
                  _const_spec((1, D_MODEL))],
        out_specs=[_const_spec((D_MODEL, LAT_PAD)), pl.BlockSpec((D_MODEL, PREP_COLS), lambda j: (0, j))],
        out_shape=[jax.ShapeDtypeStruct((D_MODEL, LAT_PAD), BF16),
                   jax.ShapeDtypeStruct((D_MODEL, WIDE), BF16)],
        compiler_params=pltpu.CompilerParams(dimension_semantics=("arbitrary",),
                                             vmem_limit_bytes=VMEM_LIMIT),
        name="mla_conv_w_in_prep",
    )(w_in.T, w_in.T, g_in.reshape(1, -1))


def _layer(x2, p2, posc, invf, batch, seq, g_in, w_in, g_cq, w_uq, g_ckv, w_ukv, g_q, g_k,
           conv_w, g_oa, g_oc, w_o, w_pl, w_plg, g_pl):
    tokens = batch * seq
    tm = PROJ_TILE
    assert seq % tm == 0 and seq % ATTN_TILE == 0

    w_lat, w_wide = _prepare_w_in(w_in, g_in)
    w_uq_p = jnp.pad(w_uq.reshape(Q_LORA, N_HEADS, QK_DIM),
                     ((0, 0), (0, 0), (0, HEAD_PAD - QK_DIM))).reshape(Q_LORA, N_HEADS * HEAD_PAD).astype(BF16)
    g_q_p = jnp.pad(g_q, (0, HEAD_PAD - QK_DIM)).reshape(1, HEAD_PAD)
    g_kn = g_k[:NOPE_DIM].reshape(1, NOPE_DIM)
    g_kr = jnp.pad(g_k[NOPE_DIM:], (0, KPE_PAD - ROPE_DIM)).reshape(1, KPE_PAD)

    cparams = pltpu.CompilerParams(dimension_semantics=("arbitrary",), vmem_limit_bytes=VMEM_LIMIT)

    def col_spec(rows, tile):
        return pl.BlockSpec((rows, tile), lambda i: (0, i))

    qt, k, vt, ga, yc = pl.pallas_call(
        functools.partial(_proj_kernel, seq // tm),
        grid=(tokens // tm,),
        in_specs=[
            _row_spec(tm, D_MODEL), _row_spec(tm // TOKENS_PER_ROW, LANES), _const_spec((1, LANES)),
            _const_spec((D_MODEL, LAT_PAD)), _const_spec((D_MODEL, WIDE)),
            _const_spec((1, Q_LORA)), _const_spec((Q_LORA, N_HEADS * HEAD_PAD)),
            _const_spec((1, KV_LORA)), _const_spec((KV_LORA, N_HEADS * (NOPE_DIM + V_DIM))),
            _const_spec((1, HEAD_PAD)), _const_spec((1, NOPE_DIM)), _const_spec((1, KPE_PAD)),
            _const_spec((CONV_K, CONV_WIDTH)), _const_spec((1, CONV_WIDTH)),
        ],
        out_specs=[
            col_spec(N_HEADS * HEAD_PAD, tm), _row_spec(tm, N_HEADS * HEAD_PAD),
            col_spec(N_HEADS * V_ROWS, tm), _row_spec(tm, ATTN_WIDTH), _row_spec(tm, CONV_WIDTH),
        ],
        out_shape=[
            jax.ShapeDtypeStruct((N_HEADS * HEAD_PAD, tokens), BF16),
            jax.ShapeDtypeStruct((tokens, N_HEADS * HEAD_PAD), BF16),
            jax.ShapeDtypeStruct((N_HEADS * V_ROWS, tokens), BF16),
            jax.ShapeDtypeStruct((tokens, ATTN_WIDTH), BF16),
            jax.ShapeDtypeStruct((tokens, CONV_WIDTH), BF16),
        ],
        scratch_shapes=[pltpu.VMEM((tm + SUBLANES, CONV_WIDTH), F32)],
        compiler_params=cparams,
        name="mla_conv_proj",
    )(x2, posc, invf, w_lat, w_wide, g_cq.reshape(1, -1), w_uq_p,
      g_ckv.reshape(1, -1), w_ukv.astype(BF16), g_q_p, g_kn, g_kr, conv_w, g_oc.reshape(1, -1))

    ta = ATTN_TILE
    sched = _triangle_schedule(batch, seq // ta)
    o = pl.pallas_call(
        _attn_kernel,
        grid_spec=pltpu.PrefetchScalarGridSpec(
            num_scalar_prefetch=len(sched),
            grid=(int(sched[0].shape[0]),),
            in_specs=[
                pl.BlockSpec((N_HEADS * HEAD_PAD, ta), lambda s, qn, kn, kc, qc, *_: (0, qn[s])),
                pl.BlockSpec((ta, N_HEADS * HEAD_PAD), lambda s, qn, kn, kc, qc, *_: (kn[s], 0)),
                pl.BlockSpec((N_HEADS * V_ROWS, ta), lambda s, qn, kn, kc, qc, *_: (0, kc[s])),
            ],
            out_specs=pl.BlockSpec((ta, ATTN_WIDTH), lambda s, qn, kn, kc, qc, *_: (qc[s], 0)),
            scratch_shapes=[pltpu.VMEM((N_HEADS, ta, ta), F32),
                            pltpu.VMEM((N_HEADS, 1, ta), F32),
                            pltpu.VMEM((N_HEADS, 1, ta), F32),
                            pltpu.VMEM((N_HEADS, V_ROWS, ta), F32)],
        ),
        out_shape=jax.ShapeDtypeStruct((tokens, ATTN_WIDTH), BF16),
        compiler_params=cparams,
        name="mla_flash_attn",
    )(*sched, qt, k, vt)

    return pl.pallas_call(
        _out_kernel,
        grid=(tokens // tm,),
        in_specs=[
            _row_spec(tm, ATTN_WIDTH), _row_spec(tm, ATTN_WIDTH), _row_spec(tm, CONV_WIDTH),
            _row_spec(tm, D_MODEL), _row_spec(tm, PLE_DIM),
            _const_spec((1, ATTN_WIDTH)), _const_spec((D_MODEL, D_MODEL)), _const_spec((1, D_MODEL)),
            _const_spec((D_MODEL, D_MODEL)), _const_spec((PLE_DIM, D_MODEL)),
        ],
        out_specs=_row_spec(tm, D_MODEL),
        out_shape=jax.ShapeDtypeStruct((tokens, D_MODEL), F32),
        compiler_params=cparams,
        name="mla_conv_out",
    )(o, ga, yc, x2, p2, g_oa.reshape(1, -1), w_o.astype(BF16),
      g_pl.reshape(1, -1), w_plg.astype(BF16), w_pl.astype(BF16))


def kernel(x, p, positions, g_in, w_in, g_cq, w_uq, g_ckv, w_ukv, g_q, g_k, conv_w, g_oa, g_oc,
           w_o, w_pl, w_plg, g_pl):
    batch, seq, d_model = x.shape
    depth = p.shape[0]
    tokens = batch * seq
    rows = PROJ_TILE // TOKENS_PER_ROW
    posc = jnp.repeat(positions.astype(F32).reshape(tokens // PROJ_TILE, TOKENS_PER_ROW, rows)
                      .transpose(0, 2, 1), HALF_ROPE, axis=-1).reshape(tokens // TOKENS_PER_ROW, LANES)
    inv_freq = 1.0 / (ROPE_THETA ** (jnp.arange(0, ROPE_DIM, 2, dtype=F32) / ROPE_DIM))
    invf = jnp.tile(inv_freq, LANES // HALF_ROPE).reshape(1, LANES)
    h = x.reshape(tokens, d_model)
    for i in range(depth):
        h = _layer(h, p[i].reshape(tokens, PLE_DIM), posc, invf, batch, seq,
                   g_in[i], w_in[i], g_cq[i], w_uq[i], g_ckv[i], w_ukv[i], g_q[i], g_k[i],
                   conv_w[i], g_oa[i], g_oc[i], w_o[i], w_pl[i], w_plg[i], g_pl[i])
    return h.reshape(batch, seq, d_model).astype(x.dtype)
```

```python
import functools
import math

import jax
import jax.numpy as jnp
import numpy as np
from jax import lax
from jax.experimental import pallas as pl
from jax.experimental.pallas import tpu as pltpu

D_MODEL = 1024
PLE_DIM = 256
N_HEADS = 4
NOPE_DIM = 128
ROPE_DIM = 64
HALF_ROPE = ROPE_DIM // 2
V_DIM = 128
QK_DIM = NOPE_DIM + ROPE_DIM
Q_LORA = 256
KV_LORA = 128
ATTN_WIDTH = N_HEADS * V_DIM
CONV_WIDTH = D_MODEL - ATTN_WIDTH
CONV_K = 3
ROPE_THETA = 10000.0
RMS_EPS = 1e-6
NEG_INF = -1e30

LANES = 128
SUBLANES = 8
HEAD_PAD = 2 * LANES
KPE_PAD = LANES
BF16_ROWS = 16
V_ROWS = V_DIM + BF16_ROWS
TOKENS_PER_ROW = LANES // HALF_ROPE
OFF_CQ = 0
OFF_CKV = OFF_CQ + Q_LORA
OFF_KPE = OFF_CKV + KV_LORA
LAT_SRC = OFF_KPE + ROPE_DIM
LAT_PAD = OFF_KPE + KPE_PAD
SRC_ZA = 0
SRC_CB = SRC_ZA + ATTN_WIDTH
SRC_CC = SRC_CB + CONV_WIDTH
SRC_CX = SRC_CC + CONV_WIDTH
SRC_ZC = SRC_CX + CONV_WIDTH
WIDE = SRC_ZC + CONV_WIDTH
PREP_COLS = 512
PREP_HALF = PREP_COLS // 2
PREP_PAIRS = ((SRC_ZA, SRC_ZA + PREP_HALF), (SRC_CC, SRC_CX), (SRC_CC + PREP_HALF, SRC_CX + PREP_HALF),
              (SRC_CB, SRC_ZC), (SRC_CB + PREP_HALF, SRC_ZC + PREP_HALF))
OFF_ZA = 0
OFF_CCX = OFF_ZA + PREP_COLS
OFF_CBZ = OFF_CCX + 2 * PREP_COLS

PROJ_TILE = 1024
PROJ_ROWS = 256
OUT_ROWS = 256
ATTN_TILE = 1024
ATTN_Q_CHUNK = 256
VMEM_LIMIT = 48 * 1024 * 1024

BF16 = jnp.bfloat16
F32 = jnp.float32


def _rms_scale(v, width):
    return lax.rsqrt(jnp.sum(v * v, axis=-1, keepdims=True) * (1.0 / width) + RMS_EPS)


def _silu(z):
    h = 0.5 * z
    return h + h * jnp.tanh(h)


def _dot(a, b):
    return jnp.dot(a, b, preferred_element_type=F32)


def _rope_tables(posc, invf):
    ang = posc * invf
    cosc = jnp.cos(ang)
    sinc = jnp.sin(ang)
    lane = lax.broadcasted_iota(jnp.int32, (1, LANES), 1)
    lo = lane < HALF_ROPE
    mid = (lane >= HALF_ROPE) & (lane < ROPE_DIM)
    quarters = []
    for a in range(TOKENS_PER_ROW):
        c = pltpu.roll(cosc, LANES - HALF_ROPE * a, 1) if a else cosc
        s = pltpu.roll(sinc, LANES - HALF_ROPE * a, 1) if a else sinc
        quarters.append((jnp.where(lo, c, jnp.where(mid, pltpu.roll(c, HALF_ROPE, 1), 0.0)),
                         jnp.where(lo, -s, jnp.where(mid, pltpu.roll(s, HALF_ROPE, 1), 0.0))))
    return quarters, lo


def _rope(t, c, s, lo):
    swapped = jnp.where(lo, pltpu.roll(t, LANES - HALF_ROPE, 1), pltpu.roll(t, HALF_ROPE, 1))
    return t * c + swapped * s


def _proj_kernel(tiles_per_seq,
                 x_ref, posc_ref, invf_ref, w_lat_ref, w_in_ref, g_cq_ref, w_uq_ref,
                 g_ckv_ref, w_ukv_ref, g_q_ref, g_kn_ref, g_kr_ref, conv_w_ref, g_oc_ref,
                 qt_ref, k_ref, vt_ref, ga_ref, yc_ref, carry_ref):
    tm = x_ref.shape[0]
    quarter_rows = tm // TOKENS_PER_ROW
    tables = []
    q_scale = math.log2(math.e) / math.sqrt(QK_DIM)

    @pl.when(pl.program_id(0) % tiles_per_seq == 0)
    def _():
        carry_ref[0:SUBLANES, :] = jnp.zeros((SUBLANES, CONV_WIDTH), F32)

    def rows_block(r0, nr):
        x = x_ref[r0:r0 + nr, :]
        h = (x * _rms_scale(x, D_MODEL)).astype(BF16)

        def proj(off, width):
            return _dot(h, w_in_ref[:, off:off + width])

        lat = _dot(h, w_lat_ref[...])
        c_q = lat[:, OFF_CQ:OFF_CKV]
        c_kv = lat[:, OFF_CKV:OFF_KPE]
        kpe = lat[:, OFF_KPE:LAT_PAD]
        def tile_pairs(r):
            return [(r[:, i:i + LANES], r[:, i + LANES:i + 2 * LANES]) for i in range(0, r.shape[1], 2 * LANES)]

        cv = jnp.concatenate([c * xin for c, xin in tile_pairs(proj(OFF_CCX, 2 * CONV_WIDTH))], axis=1)
        cqn = (c_q * _rms_scale(c_q, Q_LORA) * g_cq_ref[...]).astype(BF16)
        qf = _dot(cqn, w_uq_ref[...])
        gated_b = jnp.concatenate([b * _silu(z) for b, z in tile_pairs(proj(OFF_CBZ, 2 * CONV_WIDTH))], axis=1)
        ckvn = (c_kv * _rms_scale(c_kv, KV_LORA) * g_ckv_ref[...]).astype(BF16)
        kv = _dot(ckvn, w_ukv_ref[...])
        z_a = proj(OFF_ZA, ATTN_WIDTH)

        if not tables:
            tables.extend(_rope_tables(posc_ref[...], invf_ref[...]))
        quarters, lo = tables

        def table_rows(which):
            pieces, r = [], r0
            while r < r0 + nr:
                qa, off = divmod(r, quarter_rows)
                n = min(quarter_rows - off, r0 + nr - r)
                pieces.append(quarters[qa][which][off:off + n])
                r += n
            return pieces[0] if len(pieces) == 1 else jnp.concatenate(pieces, axis=0)

        cos_t, sin_t = table_rows(0), table_rows(1)

        for hd in range(N_HEADS):
            qh = qf[:, hd * HEAD_PAD:(hd + 1) * HEAD_PAD]
            qn = qh * (_rms_scale(qh, QK_DIM) * q_scale) * g_q_ref[...]
            qt_ref[hd * HEAD_PAD:hd * HEAD_PAD + LANES, r0:r0 + nr] = qn[:, :LANES].astype(BF16).T
            qt_ref[hd * HEAD_PAD + LANES:(hd + 1) * HEAD_PAD, r0:r0 + nr] = (
                _rope(qn[:, LANES:], cos_t, sin_t, lo).astype(BF16).T)

        ss_kpe = jnp.sum(kpe * kpe, axis=-1, keepdims=True)
        k_rot = _rope(kpe * g_kr_ref[...], cos_t, sin_t, lo)
        for hd in range(N_HEADS):
            kn = kv[:, hd * (NOPE_DIM + V_DIM):hd * (NOPE_DIM + V_DIM) + NOPE_DIM]
            vh = kv[:, hd * (NOPE_DIM + V_DIM) + NOPE_DIM:(hd + 1) * (NOPE_DIM + V_DIM)]
            ss = jnp.sum(kn * kn, axis=-1, keepdims=True) + ss_kpe
            rs = lax.rsqrt(ss * (1.0 / QK_DIM) + RMS_EPS)
            k_ref[r0:r0 + nr, hd * HEAD_PAD:hd * HEAD_PAD + LANES] = (kn * rs * g_kn_ref[...]).astype(BF16)
            k_ref[r0:r0 + nr, hd * HEAD_PAD + LANES:(hd + 1) * HEAD_PAD] = (k_rot * rs).astype(BF16)
            vt_ref[hd * V_ROWS:hd * V_ROWS + V_DIM, r0:r0 + nr] = vh.astype(BF16).T
            vt_ref[hd * V_ROWS + V_DIM:(hd + 1) * V_ROWS, r0:r0 + nr] = jnp.ones((BF16_ROWS, nr), BF16)

        base = SUBLANES + r0
        prev = carry_ref[base - SUBLANES:base, :]
        carry_ref[base + nr - SUBLANES:base + nr, :] = cv[nr - SUBLANES:, :]
        row = lax.broadcasted_iota(jnp.int32, (SUBLANES, CONV_WIDTH), 0)

        def delayed(d):
            sh = pltpu.roll(cv, d, 0)
            head = jnp.where(row < d, pltpu.roll(prev, d, 0), sh[0:SUBLANES])
            return jnp.concatenate([head, sh[SUBLANES:]], axis=0)

        u = (conv_w_ref[2:3, :] * cv + conv_w_ref[1:2, :] * delayed(1) + conv_w_ref[0:1, :] * delayed(2))
        yc = gated_b * u
        yc_ref[r0:r0 + nr, :] = (yc * _rms_scale(yc, CONV_WIDTH) * g_oc_ref[...]).astype(BF16)

        for i, (za_lo, za_hi) in enumerate(tile_pairs(z_a)):
            ga_ref[r0:r0 + nr, i * LANES:(i + 1) * LANES] = _silu(za_lo).astype(BF16)
            ga_ref[r0:r0 + nr, PREP_HALF + i * LANES:PREP_HALF + (i + 1) * LANES] = _silu(za_hi).astype(BF16)

    for blk in range(tm // PROJ_ROWS):
        rows_block(blk * PROJ_ROWS, PROJ_ROWS)
    carry_ref[0:SUBLANES, :] = carry_ref[tm:tm + SUBLANES, :]


def _attn_kernel(qn_ref, kn_ref, kc_ref, qc_ref, diag_n_ref, first_c_ref, last_c_ref,
                 qt_ref, k_ref, vt_ref, o_ref, s_ref, mx_ref, m_ref, acc_ref):
    step = pl.program_id(0)
    tq = qt_ref.shape[1]
    tk = k_ref.shape[0]

    @pl.when(step == 0)
    def _():
        s_ref[...] = jnp.zeros(s_ref.shape, F32)
        mx_ref[...] = jnp.zeros(mx_ref.shape, F32)

    @pl.when((first_c_ref[step] == 1) | (step == 0))
    def _():
        m_ref[...] = jnp.full(m_ref.shape, NEG_INF, F32)
        acc_ref[...] = jnp.zeros(acc_ref.shape, F32)

    chunk = ATTN_Q_CHUNK
    nchunk = tq // chunk

    def body(diag_cur, diag_next):
        if diag_next:
            tri = (lax.broadcasted_iota(jnp.int32, (chunk, chunk), 0)
                   <= lax.broadcasted_iota(jnp.int32, (chunk, chunk), 1))

        def scores(hd, c):
            cols = slice(c * chunk, (c + 1) * chunk)
            rows = chunk * (c + 1) if diag_next else tk
            s = _dot(k_ref[0:rows, hd * HEAD_PAD:(hd + 1) * HEAD_PAD],
                     qt_ref[hd * HEAD_PAD:(hd + 1) * HEAD_PAD, cols])
            if not diag_next:
                s_ref[hd, :, cols] = s
                mx_ref[hd, :, cols] = jnp.max(s, axis=0, keepdims=True)
                return
            last = jnp.where(tri, s[rows - chunk:rows], NEG_INF)
            s_ref[hd, rows - chunk:rows, cols] = last
            mx = jnp.max(last, axis=0, keepdims=True)
            if rows > chunk:
                s_ref[hd, 0:rows - chunk, cols] = s[0:rows - chunk]
                mx = jnp.maximum(mx, jnp.max(s[0:rows - chunk], axis=0, keepdims=True))
            mx_ref[hd, :, cols] = mx

        def softmax_pv(hd, c):
            cols = slice(c * chunk, (c + 1) * chunk)
            rows = chunk * (c + 1) if diag_cur else tk
            m_prev = m_ref[hd, :, cols]
            m_new = jnp.maximum(m_prev, mx_ref[hd, :, cols])
            alpha = jnp.exp2(m_prev - m_new)
            p = jnp.exp2((s_ref[hd, 0:rows, cols] - m_new).astype(BF16))
            pv = _dot(vt_ref[hd * V_ROWS:(hd + 1) * V_ROWS, 0:rows], p)
            acc_ref[hd, :, cols] = alpha * acc_ref[hd, :, cols] + pv
            m_ref[hd, :, cols] = m_new

        for hd in range(N_HEADS):
            for c in range(nchunk):
                softmax_pv(hd, c)
                scores(hd, c)

    diag_cur = last_c_ref[step] == 1
    diag_next = diag_n_ref[step] == 1
    for cur_flag in (False, True):
        for next_flag in (False, True):
            @pl.when((diag_cur == cur_flag) & (diag_next == next_flag))
            def _(cur_flag=cur_flag, next_flag=next_flag):
                body(cur_flag, next_flag)


    @pl.when(last_c_ref[step] == 1)
    def _():
        for hd in range(N_HEADS):
            acc = acc_ref[hd]
            o_ref[:, hd * V_DIM:(hd + 1) * V_DIM] = (
                acc[:V_DIM] / acc[V_DIM:V_DIM + 1]).T.astype(o_ref.dtype)


def _out_kernel(o_ref, ga_ref, yc_ref, x_ref, p_ref, g_oa_ref, w_o_ref, g_pl_ref, w_plg_ref, w_pl_ref,
                out_ref):
    tm = x_ref.shape[0]

    def residual(r0):
        rows = slice(r0, r0 + OUT_ROWS)
        ya = o_ref[rows, :].astype(F32) * ga_ref[rows, :].astype(F32)
        ya_n = (ya * _rms_scale(ya, ATTN_WIDTH) * g_oa_ref[...]).astype(BF16)
        x1 = (x_ref[rows, :] + _dot(ya_n, w_o_ref[:ATTN_WIDTH, :])
              + _dot(yc_ref[rows, :], w_o_ref[ATTN_WIDTH:, :]))
        out_ref[rows, :] = x1
        hn = (x1 * _rms_scale(x1, D_MODEL) * g_pl_ref[...]).astype(BF16)
        return r0, hn

    def gated(r0, hn):
        rows = slice(r0, r0 + OUT_ROWS)
        gate = 1.0 / (1.0 + jnp.exp(-_dot(hn, w_plg_ref[...])))
        ple = _dot(p_ref[rows, :].astype(BF16), w_pl_ref[...])
        out_ref[rows, :] = out_ref[rows, :] + gate * ple

    prev = None
    for r0 in range(0, tm, OUT_ROWS):
        cur = residual(r0)
        if prev is not None:
            gated(*prev)
        prev = cur
    gated(*prev)


def _row_spec(tile, width):
    return pl.BlockSpec((tile, width), lambda i: (i, 0))


def _const_spec(shape):
    return pl.BlockSpec(shape, lambda i: (0,) * len(shape))


def _triangle_schedule(batch, n_blk):
    pairs = [(b * n_blk + i, b * n_blk + j, j == 0, j == i)
             for b in range(batch) for i in range(n_blk) for j in range(i + 1)]
    nxt = pairs + [pairs[-1]]
    cur = [pairs[0]] + pairs
    cols = ([p[0] for p in nxt], [p[1] for p in nxt], [p[1] for p in cur], [p[0] for p in cur],
            [int(p[3]) for p in nxt],
            [0] + [int(p[2]) for p in pairs], [0] + [int(p[3]) for p in pairs])
    return tuple(jnp.asarray(np.asarray(c, np.int32)) for c in cols)


def _prep_in_kernel(rows_a_ref, rows_b_ref, wt_lat_ref, wt_a_ref, wt_b_ref, g_ref, lat_ref, wide_ref):
    g = g_ref[...]

    @pl.when(pl.program_id(0) == 0)
    def _():
        pad = jnp.zeros((LAT_PAD - LAT_SRC, D_MODEL), F32)
        lat_ref[...] = jnp.concatenate([wt_lat_ref[...] * g, pad], axis=0).T.astype(BF16)

    a = (wt_a_ref[...] * g).T.astype(BF16)
    b = (wt_b_ref[...] * g).T.astype(BF16)
    for i in range(PREP_HALF // LANES):
        wide_ref[:, 2 * i * LANES:(2 * i + 1) * LANES] = a[:, i * LANES:(i + 1) * LANES]
        wide_ref[:, (2 * i + 1) * LANES:(2 * i + 2) * LANES] = b[:, i * LANES:(i + 1) * LANES]


def _prepare_w_in(w_in, g_in):
    def rows_at(nrows, start):
        return pl.BlockSpec((pl.Element(nrows), pl.Element(D_MODEL)), start)

    rows_a = jnp.asarray(np.asarray([LAT_SRC + a for a, _ in PREP_PAIRS], np.int32))
    rows_b = jnp.asarray(np.asarray([LAT_SRC + b for _, b in PREP_PAIRS], np.int32))
    return pl.pallas_call(
        _prep_in_kernel,
        grid_spec=pltpu.PrefetchScalarGridSpec(
            num_scalar_prefetch=2,
            grid=(len(PREP_PAIRS),),
            in_specs=[rows_at(LAT_SRC, lambda j, ra, rb: (0, 0)),
                      rows_at(PREP_HALF, lambda j, ra, rb: (pl.multiple_of(ra[j], SUBLANES), 0)),
                      rows_at(PREP_HALF, lambda j, ra, rb: (pl.multiple_of(rb[j], SUBLANES), 0)),
                      pl.BlockSpec((1, D_MODEL), lambda j, ra, rb: (0, 0))],
            out_specs=[pl.BlockSpec((D_MODEL, LAT_PAD), lambda j, ra, rb: (0, 0)),
                       pl.BlockSpec((D_MODEL, PREP_COLS), lambda j, ra, rb: (0, j))],
        ),
        out_shape=[jax.ShapeDtypeStruct((D_MODEL, LAT_PAD), BF16),
                   jax.ShapeDtypeStruct((D_MODEL, WIDE), BF16)],
        compiler_params=pltpu.CompilerParams(dimension_semantics=("arbitrary",),
                                             vmem_limit_bytes=VMEM_LIMIT),
        name="mla_conv_w_in_prep",
    )(rows_a, rows_b, w_in.T, w_in.T, w_in.T, g_in.reshape(1, -1))


def _layer(x2, p2, posc, invf, batch, seq, g_in, w_in, g_cq, w_uq, g_ckv, w_ukv, g_q, g_k,
           conv_w, g_oa, g_oc, w_o, w_pl, w_plg, g_pl):
    tokens = batch * seq
    tm = PROJ_TILE
    assert seq % tm == 0 and seq % ATTN_TILE == 0

    w_lat, w_wide = _prepare_w_in(w_in, g_in)
    w_uq_p = jnp.pad(w_uq.reshape(Q_LORA, N_HEADS, QK_DIM),
                     ((0, 0), (0, 0), (0, HEAD_PAD - QK_DIM))).reshape(Q_LORA, N_HEADS * HEAD_PAD).astype(BF16)
    g_q_p = jnp.pad(g_q, (0, HEAD_PAD - QK_DIM)).reshape(1, HEAD_PAD)
    g_kn = g_k[:NOPE_DIM].reshape(1, NOPE_DIM)
    g_kr = jnp.pad(g_k[NOPE_DIM:], (0, KPE_PAD - ROPE_DIM)).reshape(1, KPE_PAD)

    cparams = pltpu.CompilerParams(dimension_semantics=("arbitrary",), vmem_limit_bytes=VMEM_LIMIT)

    def col_spec(rows, tile):
        return pl.BlockSpec((rows, tile), lambda i: (0, i))

    qt, k, vt, ga, yc = pl.pallas_call(
        functools.partial(_proj_kernel, seq // tm),
        grid=(tokens // tm,),
        in_specs=[
            _row_spec(tm, D_MODEL), _row_spec(tm // TOKENS_PER_ROW, LANES), _const_spec((1, LANES)),
            _const_spec((D_MODEL, LAT_PAD)), _const_spec((D_MODEL, WIDE)),
            _const_spec((1, Q_LORA)), _const_spec((Q_LORA, N_HEADS * HEAD_PAD)),
            _const_spec((1, KV_LORA)), _const_spec((KV_LORA, N_HEADS * (NOPE_DIM + V_DIM))),
            _const_spec((1, HEAD_PAD)), _const_spec((1, NOPE_DIM)), _const_spec((1, KPE_PAD)),
            _const_spec((CONV_K, CONV_WIDTH)), _const_spec((1, CONV_WIDTH)),
        ],
        out_specs=[
            col_spec(N_HEADS * HEAD_PAD, tm), _row_spec(tm, N_HEADS * HEAD_PAD),
            col_spec(N_HEADS * V_ROWS, tm), _row_spec(tm, ATTN_WIDTH), _row_spec(tm, CONV_WIDTH),
        ],
        out_shape=[
            jax.ShapeDtypeStruct((N_HEADS * HEAD_PAD, tokens), BF16),
            jax.ShapeDtypeStruct((tokens, N_HEADS * HEAD_PAD), BF16),
            jax.ShapeDtypeStruct((N_HEADS * V_ROWS, tokens), BF16),
            jax.ShapeDtypeStruct((tokens, ATTN_WIDTH), BF16),
            jax.ShapeDtypeStruct((tokens, CONV_WIDTH), BF16),
        ],
        scratch_shapes=[pltpu.VMEM((tm + SUBLANES, CONV_WIDTH), F32)],
        compiler_params=cparams,
        name="mla_conv_proj",
    )(x2, posc, invf, w_lat, w_wide, g_cq.reshape(1, -1), w_uq_p,
      g_ckv.reshape(1, -1), w_ukv.astype(BF16), g_q_p, g_kn, g_kr, conv_w, g_oc.reshape(1, -1))

    ta = ATTN_TILE
    sched = _triangle_schedule(batch, seq // ta)
    o = pl.pallas_call(
        _attn_kernel,
        grid_spec=pltpu.PrefetchScalarGridSpec(
            num_scalar_prefetch=len(sched),
            grid=(int(sched[0].shape[0]),),
            in_specs=[
                pl.BlockSpec((N_HEADS * HEAD_PAD, ta), lambda s, qn, kn, kc, qc, *_: (0, qn[s])),
                pl.BlockSpec((ta, N_HEADS * HEAD_PAD), lambda s, qn, kn, kc, qc, *_: (kn[s], 0)),
                pl.BlockSpec((N_HEADS * V_ROWS, ta), lambda s, qn, kn, kc, qc, *_: (0, kc[s])),
            ],
            out_specs=pl.BlockSpec((ta, ATTN_WIDTH), lambda s, qn, kn, kc, qc, *_: (qc[s], 0)),
            scratch_shapes=[pltpu.VMEM((N_HEADS, ta, ta), F32),
                            pltpu.VMEM((N_HEADS, 1, ta), F32),
                            pltpu.VMEM((N_HEADS, 1, ta), F32),
                            pltpu.VMEM((N_HEADS, V_ROWS, ta), F32)],
        ),
        out_shape=jax.ShapeDtypeStruct((tokens, ATTN_WIDTH), BF16),
        compiler_params=cparams,
        name="mla_flash_attn",
    )(*sched, qt, k, vt)

    return pl.pallas_call(
        _out_kernel,
        grid=(tokens // tm,),
        in_specs=[
            _row_spec(tm, ATTN_WIDTH), _row_spec(tm, ATTN_WIDTH), _row_spec(tm, CONV_WIDTH),
            _row_spec(tm, D_MODEL), _row_spec(tm, PLE_DIM),
            _const_spec((1, ATTN_WIDTH)), _const_spec((D_MODEL, D_MODEL)), _const_spec((1, D_MODEL)),
            _const_spec((D_MODEL, D_MODEL)), _const_spec((PLE_DIM, D_MODEL)),
        ],
        out_specs=_row_spec(tm, D_MODEL),
        out_shape=jax.ShapeDtypeStruct((tokens, D_MODEL), F32),
        compiler_params=cparams,
        name="mla_conv_out",
    )(o, ga, yc, x2, p2, g_oa.reshape(1, -1), w_o.astype(BF16),
      g_pl.reshape(1, -1), w_plg.astype(BF16), w_pl.astype(BF16))


def kernel(x, p, positions, g_in, w_in, g_cq, w_uq, g_ckv, w_ukv, g_q, g_k, conv_w, g_oa, g_oc,
           w_o, w_pl, w_plg, g_pl):
    batch, seq, d_model = x.shape
    depth = p.shape[0]
    tokens = batch * seq
    rows = PROJ_TILE // TOKENS_PER_ROW
    posc = jnp.repeat(positions.astype(F32).reshape(tokens // PROJ_TILE, TOKENS_PER_ROW, rows)
                      .transpose(0, 2, 1), HALF_ROPE, axis=-1).reshape(tokens // TOKENS_PER_ROW, LANES)
    inv_freq = 1.0 / (ROPE_THETA ** (jnp.arange(0, ROPE_DIM, 2, dtype=F32) / ROPE_DIM))
    invf = jnp.tile(inv_freq, LANES // HALF_ROPE).reshape(1, LANES)
    h = x.reshape(tokens, d_model)
    for i in range(depth):
        h = _layer(h, p[i].reshape(tokens, PLE_DIM), posc, invf, batch, seq,
                   g_in[i], w_in[i], g_cq[i], w_uq[i], g_ckv[i], w_ukv[i], g_q[i], g_k[i],
                   conv_w[i], g_oa[i], g_oc[i], w_o[i], w_pl[i], w_plg[i], g_pl[i])
    return h.reshape(batch, seq, d_model).astype(x.dtype)
```

```python
import functools
import math

import jax
import jax.numpy as jnp
import numpy as np
from jax import lax
from jax.experimental import pallas as pl
from jax.experimental.pallas import tpu as pltpu

D_MODEL = 1024
PLE_DIM = 256
N_HEADS = 4
NOPE_DIM = 128
ROPE_DIM = 64
HALF_ROPE = ROPE_DIM // 2
V_DIM = 128
QK_DIM = NOPE_DIM + ROPE_DIM
Q_LORA = 256
KV_LORA = 128
ATTN_WIDTH = N_HEADS * V_DIM
CONV_WIDTH = D_MODEL - ATTN_WIDTH
CONV_K = 3
ROPE_THETA = 10000.0
RMS_EPS = 1e-6
NEG_INF = -1e30

LANES = 128
SUBLANES = 8
HEAD_PAD = 2 * LANES
KPE_PAD = LANES
BF16_ROWS = 16
V_ROWS = V_DIM + BF16_ROWS
TOKENS_PER_ROW = LANES // HALF_ROPE
OFF_CQ = 0
OFF_CKV = OFF_CQ + Q_LORA
OFF_KPE = OFF_CKV + KV_LORA
LAT_SRC = OFF_KPE + ROPE_DIM
LAT_PAD = OFF_KPE + KPE_PAD
SRC_ZA = 0
SRC_CB = SRC_ZA + ATTN_WIDTH
SRC_CC = SRC_CB + CONV_WIDTH
SRC_CX = SRC_CC + CONV_WIDTH
SRC_ZC = SRC_CX + CONV_WIDTH
WIDE = SRC_ZC + CONV_WIDTH
PREP_COLS = 512
PREP_HALF = PREP_COLS // 2
PREP_PAIRS = ((SRC_ZA, SRC_ZA + PREP_HALF), (SRC_CC, SRC_CX), (SRC_CC + PREP_HALF, SRC_CX + PREP_HALF),
              (SRC_CB, SRC_ZC), (SRC_CB + PREP_HALF, SRC_ZC + PREP_HALF))
OFF_ZA = 0
OFF_CCX = OFF_ZA + PREP_COLS
OFF_CBZ = OFF_CCX + 2 * PREP_COLS

PROJ_TILE = 1024
PROJ_ROWS = 256
OUT_ROWS = 256
ATTN_TILE = 1024
ATTN_Q_CHUNK = 256
VMEM_LIMIT = 48 * 1024 * 1024

BF16 = jnp.bfloat16
F32 = jnp.float32


def _rms_scale(v, width):
    return lax.rsqrt(jnp.sum(v * v, axis=-1, keepdims=True) * (1.0 / width) + RMS_EPS)


def _silu(z):
    h = 0.5 * z
    return h + h * jnp.tanh(h)


def _dot(a, b):
    return jnp.dot(a, b, preferred_element_type=F32)


def _rope_angles(posc, invf):
    ang = posc * invf
    return jnp.cos(ang), jnp.sin(ang)


def _rope_quarter(cosc, sinc, a):
    lane = lax.broadcasted_iota(jnp.int32, (1, LANES), 1)
    lo = lane < HALF_ROPE
    mid = (lane >= HALF_ROPE) & (lane < ROPE_DIM)
    c = pltpu.roll(cosc, LANES - HALF_ROPE * a, 1) if a else cosc
    s = pltpu.roll(sinc, LANES - HALF_ROPE * a, 1) if a else sinc
    return (jnp.where(lo, c, jnp.where(mid, pltpu.roll(c, HALF_ROPE, 1), 0.0)),
            jnp.where(lo, -s, jnp.where(mid, pltpu.roll(s, HALF_ROPE, 1), 0.0)), lo)


def _rope(t, c, s, lo):
    swapped = jnp.where(lo, pltpu.roll(t, LANES - HALF_ROPE, 1), pltpu.roll(t, HALF_ROPE, 1))
    return t * c + swapped * s


def _proj_kernel(tiles_per_seq,
                 x_ref, posc_ref, invf_ref, w_lat_ref, w_in_ref, g_cq_ref, w_uq_ref,
                 g_ckv_ref, w_ukv_ref, g_q_ref, g_kn_ref, g_kr_ref, conv_w_ref, g_oc_ref,
                 qt_ref, k_ref, vt_ref, ga_ref, yc_ref, carry_ref):
    tm = x_ref.shape[0]
    quarter_rows = tm // TOKENS_PER_ROW
    tables = []
    q_scale = math.log2(math.e) / math.sqrt(QK_DIM)

    @pl.when(pl.program_id(0) % tiles_per_seq == 0)
    def _():
        carry_ref[0:SUBLANES, :] = jnp.zeros((SUBLANES, CONV_WIDTH), F32)

    def rows_block(r0, nr):
        x = x_ref[r0:r0 + nr, :]
        h = (x * _rms_scale(x, D_MODEL)).astype(BF16)

        def proj(off, width):
            return _dot(h, w_in_ref[:, off:off + width])

        lat = _dot(h, w_lat_ref[...])
        c_q = lat[:, OFF_CQ:OFF_CKV]
        c_kv = lat[:, OFF_CKV:OFF_KPE]
        kpe = lat[:, OFF_KPE:LAT_PAD]
        def tile_pairs(r):
            return [(r[:, i:i + LANES], r[:, i + LANES:i + 2 * LANES]) for i in range(0, r.shape[1], 2 * LANES)]

        cv = jnp.concatenate([c * xin for c, xin in tile_pairs(proj(OFF_CCX, 2 * CONV_WIDTH))], axis=1)
        cqn = (c_q * _rms_scale(c_q, Q_LORA) * g_cq_ref[...]).astype(BF16)
        qf = _dot(cqn, w_uq_ref[...])
        gated_b = jnp.concatenate([b * _silu(z) for b, z in tile_pairs(proj(OFF_CBZ, 2 * CONV_WIDTH))], axis=1)
        ckvn = (c_kv * _rms_scale(c_kv, KV_LORA) * g_ckv_ref[...]).astype(BF16)
        kv = _dot(ckvn, w_ukv_ref[...])
        z_a = proj(OFF_ZA, ATTN_WIDTH)

        if not tables:
            tables.extend(_rope_angles(posc_ref[...], invf_ref[...]))
        assert nr == quarter_rows
        cos_t, sin_t, lo = _rope_quarter(*tables, r0 // quarter_rows)

        for hd in range(N_HEADS):
            qh = qf[:, hd * HEAD_PAD:(hd + 1) * HEAD_PAD]
            qn = qh * (_rms_scale(qh, QK_DIM) * q_scale) * g_q_ref[...]
            qt_ref[hd * HEAD_PAD:hd * HEAD_PAD + LANES, r0:r0 + nr] = qn[:, :LANES].astype(BF16).T
            qt_ref[hd * HEAD_PAD + LANES:(hd + 1) * HEAD_PAD, r0:r0 + nr] = (
                _rope(qn[:, LANES:], cos_t, sin_t, lo).astype(BF16).T)

        ss_kpe = jnp.sum(kpe * kpe, axis=-1, keepdims=True)
        k_rot = _rope(kpe * g_kr_ref[...], cos_t, sin_t, lo)
        for hd in range(N_HEADS):
            kn = kv[:, hd * (NOPE_DIM + V_DIM):hd * (NOPE_DIM + V_DIM) + NOPE_DIM]
            vh = kv[:, hd * (NOPE_DIM + V_DIM) + NOPE_DIM:(hd + 1) * (NOPE_DIM + V_DIM)]
            ss = jnp.sum(kn * kn, axis=-1, keepdims=True) + ss_kpe
            rs = lax.rsqrt(ss * (1.0 / QK_DIM) + RMS_EPS)
            k_ref[r0:r0 + nr, hd * HEAD_PAD:hd * HEAD_PAD + LANES] = (kn * rs * g_kn_ref[...]).astype(BF16)
            k_ref[r0:r0 + nr, hd * HEAD_PAD + LANES:(hd + 1) * HEAD_PAD] = (k_rot * rs).astype(BF16)
            vt_ref[hd * V_ROWS:hd * V_ROWS + V_DIM, r0:r0 + nr] = vh.astype(BF16).T
            vt_ref[hd * V_ROWS + V_DIM:(hd + 1) * V_ROWS, r0:r0 + nr] = jnp.ones((BF16_ROWS, nr), BF16)

        base = SUBLANES + r0
        prev = carry_ref[base - SUBLANES:base, :]
        carry_ref[base + nr - SUBLANES:base + nr, :] = cv[nr - SUBLANES:, :]
        row = lax.broadcasted_iota(jnp.int32, (SUBLANES, CONV_WIDTH), 0)

        def delayed(d):
            sh = pltpu.roll(cv, d, 0)
            head = jnp.where(row < d, pltpu.roll(prev, d, 0), sh[0:SUBLANES])
            return jnp.concatenate([head, sh[SUBLANES:]], axis=0)

        u = (conv_w_ref[2:3, :] * cv + conv_w_ref[1:2, :] * delayed(1) + conv_w_ref[0:1, :] * delayed(2))
        yc = gated_b * u
        yc_ref[r0:r0 + nr, :] = (yc * _rms_scale(yc, CONV_WIDTH) * g_oc_ref[...]).astype(BF16)

        for i, (za_lo, za_hi) in enumerate(tile_pairs(z_a)):
            ga_ref[r0:r0 + nr, i * LANES:(i + 1) * LANES] = _silu(za_lo).astype(BF16)
            ga_ref[r0:r0 + nr, PREP_HALF + i * LANES:PREP_HALF + (i + 1) * LANES] = _silu(za_hi).astype(BF16)

    for blk in range(tm // PROJ_ROWS):
        rows_block(blk * PROJ_ROWS, PROJ_ROWS)
    carry_ref[0:SUBLANES, :] = carry_ref[tm:tm + SUBLANES, :]


def _attn_kernel(qn_ref, kn_ref, kc_ref, qc_ref, diag_n_ref, first_c_ref, last_c_ref,
                 qt_ref, k_ref, vt_ref, o_ref, s_ref, mx_ref, m_ref, acc_ref):
    step = pl.program_id(0)
    tq = qt_ref.shape[1]
    tk = k_ref.shape[0]

    @pl.when(step == 0)
    def _():
        s_ref[...] = jnp.zeros(s_ref.shape, F32)
        mx_ref[...] = jnp.zeros(mx_ref.shape, F32)

    @pl.when((first_c_ref[step] == 1) | (step == 0))
    def _():
        m_ref[...] = jnp.full(m_ref.shape, NEG_INF, F32)
        acc_ref[...] = jnp.zeros(acc_ref.shape, F32)

    chunk = ATTN_Q_CHUNK
    nchunk = tq // chunk

    def body(diag_cur, diag_next):
        if diag_next:
            tri = (lax.broadcasted_iota(jnp.int32, (chunk, chunk), 0)
                   <= lax.broadcasted_iota(jnp.int32, (chunk, chunk), 1))

        def scores(hd, c):
            cols = slice(c * chunk, (c + 1) * chunk)
            rows = chunk * (c + 1) if diag_next else tk
            s = _dot(k_ref[0:rows, hd * HEAD_PAD:(hd + 1) * HEAD_PAD],
                     qt_ref[hd * HEAD_PAD:(hd + 1) * HEAD_PAD, cols])
            if not diag_next:
                s_ref[hd, :, cols] = s
                mx_ref[hd, :, cols] = jnp.max(s, axis=0, keepdims=True)
                return
            last = jnp.where(tri, s[rows - chunk:rows], NEG_INF)
            s_ref[hd, rows - chunk:rows, cols] = last
            mx = jnp.max(last, axis=0, keepdims=True)
            if rows > chunk:
                s_ref[hd, 0:rows - chunk, cols] = s[0:rows - chunk]
                mx = jnp.maximum(mx, jnp.max(s[0:rows - chunk], axis=0, keepdims=True))
            mx_ref[hd, :, cols] = mx

        def softmax_pv(hd, c):
            cols = slice(c * chunk, (c + 1) * chunk)
            rows = chunk * (c + 1) if diag_cur else tk
            m_prev = m_ref[hd, :, cols]
            m_new = jnp.maximum(m_prev, mx_ref[hd, :, cols])
            alpha = jnp.exp2(m_prev - m_new)
            p = jnp.exp2((s_ref[hd, 0:rows, cols] - m_new).astype(BF16))
            pv = _dot(vt_ref[hd * V_ROWS:(hd + 1) * V_ROWS, 0:rows], p)
            acc_ref[hd, :, cols] = alpha * acc_ref[hd, :, cols] + pv
            m_ref[hd, :, cols] = m_new

        for hd in range(N_HEADS):
            for c in range(nchunk):
                softmax_pv(hd, c)
                scores(hd, c)

    diag_cur = last_c_ref[step] == 1
    diag_next = diag_n_ref[step] == 1
    for cur_flag in (False, True):
        for next_flag in (False, True):
            @pl.when((diag_cur == cur_flag) & (diag_next == next_flag))
            def _(cur_flag=cur_flag, next_flag=next_flag):
                body(cur_flag, next_flag)


    @pl.when(last_c_ref[step] == 1)
    def _():
        for hd in range(N_HEADS):
            acc = acc_ref[hd]
            o_ref[:, hd * V_DIM:(hd + 1) * V_DIM] = (
                acc[:V_DIM] / acc[V_DIM:V_DIM + 1]).T.astype(o_ref.dtype)


def _out_kernel(o_ref, ga_ref, yc_ref, x_ref, p_ref, g_oa_ref, w_o_ref, g_pl_ref, w_plg_ref, w_pl_ref,
                out_ref):
    tm = x_ref.shape[0]

    def residual(r0):
        rows = slice(r0, r0 + OUT_ROWS)
        ya = o_ref[rows, :].astype(F32) * ga_ref[rows, :].astype(F32)
        ya_n = (ya * _rms_scale(ya, ATTN_WIDTH) * g_oa_ref[...]).astype(BF16)
        y = jnp.concatenate([ya_n, yc_ref[rows, :]], axis=1)
        x1 = x_ref[rows, :] + _dot(y, w_o_ref[...])
        out_ref[rows, :] = x1
        hn = (x1 * _rms_scale(x1, D_MODEL) * g_pl_ref[...]).astype(BF16)
        return r0, hn

    def gated(r0, hn):
        rows = slice(r0, r0 + OUT_ROWS)
        gate = 1.0 / (1.0 + jnp.exp(-_dot(hn, w_plg_ref[...])))
        ple = _dot(p_ref[rows, :].astype(BF16), w_pl_ref[...])
        out_ref[rows, :] = out_ref[rows, :] + gate * ple

    prev = None
    for r0 in range(0, tm, OUT_ROWS):
        cur = residual(r0)
        if prev is not None:
            gated(*prev)
        prev = cur
    gated(*prev)


def _row_spec(tile, width):
    return pl.BlockSpec((tile, width), lambda i: (i, 0))


def _const_spec(shape):
    return pl.BlockSpec(shape, lambda i: (0,) * len(shape))


def _triangle_schedule(batch, n_blk):
    pairs = [(b * n_blk + i, b * n_blk + j, j == 0, j == i)
             for b in range(batch) for i in range(n_blk) for j in range(i + 1)]
    nxt = pairs + [pairs[-1]]
    cur = [pairs[0]] + pairs
    cols = ([p[0] for p in nxt], [p[1] for p in nxt], [p[1] for p in cur], [p[0] for p in cur],
            [int(p[3]) for p in nxt],
            [0] + [int(p[2]) for p in pairs], [0] + [int(p[3]) for p in pairs])
    return tuple(jnp.asarray(np.asarray(c, np.int32)) for c in cols)


def _prep_in_kernel(rows_a_ref, rows_b_ref, wt_lat_ref, wt_a_ref, wt_b_ref, g_ref, lat_ref, wide_ref):
    g = g_ref[...]

    @pl.when(pl.program_id(0) == 0)
    def _():
        pad = jnp.zeros((LAT_PAD - LAT_SRC, D_MODEL), F32)
        lat_ref[...] = jnp.concatenate([wt_lat_ref[...] * g, pad], axis=0).T.astype(BF16)

    a = (wt_a_ref[...] * g).T.astype(BF16)
    b = (wt_b_ref[...] * g).T.astype(BF16)
    for i in range(PREP_HALF // LANES):
        wide_ref[:, 2 * i * LANES:(2 * i + 1) * LANES] = a[:, i * LANES:(i + 1) * LANES]
        wide_ref[:, (2 * i + 1) * LANES:(2 * i + 2) * LANES] = b[:, i * LANES:(i + 1) * LANES]


def _prepare_w_in(w_in, g_in):
    def rows_at(nrows, start):
        return pl.BlockSpec((pl.Element(nrows), pl.Element(D_MODEL)), start)

    rows_a = jnp.asarray(np.asarray([LAT_SRC + a for a, _ in PREP_PAIRS], np.int32))
    rows_b = jnp.asarray(np.asarray([LAT_SRC + b for _, b in PREP_PAIRS], np.int32))
    return pl.pallas_call(
        _prep_in_kernel,
        grid_spec=pltpu.PrefetchScalarGridSpec(
            num_scalar_prefetch=2,
            grid=(len(PREP_PAIRS),),
            in_specs=[rows_at(LAT_SRC, lambda j, ra, rb: (0, 0)),
                      rows_at(PREP_HALF, lambda j, ra, rb: (pl.multiple_of(ra[j], SUBLANES), 0)),
                      rows_at(PREP_HALF, lambda j, ra, rb: (pl.multiple_of(rb[j], SUBLANES), 0)),
                      pl.BlockSpec((1, D_MODEL), lambda j, ra, rb: (0, 0))],
            out_specs=[pl.BlockSpec((D_MODEL, LAT_PAD), lambda j, ra, rb: (0, 0)),
                       pl.BlockSpec((D_MODEL, PREP_COLS), lambda j, ra, rb: (0, j))],
        ),
        out_shape=[jax.ShapeDtypeStruct((D_MODEL, LAT_PAD), BF16),
                   jax.ShapeDtypeStruct((D_MODEL, WIDE), BF16)],
        compiler_params=pltpu.CompilerParams(dimension_semantics=("arbitrary",),
                                             vmem_limit_bytes=VMEM_LIMIT),
        name="mla_conv_w_in_prep",
    )(rows_a, rows_b, w_in.T, w_in.T, w_in.T, g_in.reshape(1, -1))


def _layer(x2, p2, posc, invf, batch, seq, g_in, w_in, g_cq, w_uq, g_ckv, w_ukv, g_q, g_k,
           conv_w, g_oa, g_oc, w_o, w_pl, w_plg, g_pl):
    tokens = batch * seq
    tm = PROJ_TILE
    assert seq % tm == 0 and seq % ATTN_TILE == 0

    w_lat, w_wide = _prepare_w_in(w_in, g_in)
    w_uq_p = jnp.pad(w_uq.reshape(Q_LORA, N_HEADS, QK_DIM),
                     ((0, 0), (0, 0), (0, HEAD_PAD - QK_DIM))).reshape(Q_LORA, N_HEADS * HEAD_PAD).astype(BF16)
    g_q_p = jnp.pad(g_q, (0, HEAD_PAD - QK_DIM)).reshape(1, HEAD_PAD)
    g_kn = g_k[:NOPE_DIM].reshape(1, NOPE_DIM)
    g_kr = jnp.pad(g_k[NOPE_DIM:], (0, KPE_PAD - ROPE_DIM)).reshape(1, KPE_PAD)

    cparams = pltpu.CompilerParams(dimension_semantics=("arbitrary",), vmem_limit_bytes=VMEM_LIMIT)

    def col_spec(rows, tile):
        return pl.BlockSpec((rows, tile), lambda i: (0, i))

    qt, k, vt, ga, yc = pl.pallas_call(
        functools.partial(_proj_kernel, seq // tm),
        grid=(tokens // tm,),
        in_specs=[
            _row_spec(tm, D_MODEL), _row_spec(tm // TOKENS_PER_ROW, LANES), _const_spec((1, LANES)),
            _const_spec((D_MODEL, LAT_PAD)), _const_spec((D_MODEL, WIDE)),
            _const_spec((1, Q_LORA)), _const_spec((Q_LORA, N_HEADS * HEAD_PAD)),
            _const_spec((1, KV_LORA)), _const_spec((KV_LORA, N_HEADS * (NOPE_DIM + V_DIM))),
            _const_spec((1, HEAD_PAD)), _const_spec((1, NOPE_DIM)), _const_spec((1, KPE_PAD)),
            _const_spec((CONV_K, CONV_WIDTH)), _const_spec((1, CONV_WIDTH)),
        ],
        out_specs=[
            col_spec(N_HEADS * HEAD_PAD, tm), _row_spec(tm, N_HEADS * HEAD_PAD),
            col_spec(N_HEADS * V_ROWS, tm), _row_spec(tm, ATTN_WIDTH), _row_spec(tm, CONV_WIDTH),
        ],
        out_shape=[
            jax.ShapeDtypeStruct((N_HEADS * HEAD_PAD, tokens), BF16),
            jax.ShapeDtypeStruct((tokens, N_HEADS * HEAD_PAD), BF16),
            jax.ShapeDtypeStruct((N_HEADS * V_ROWS, tokens), BF16),
            jax.ShapeDtypeStruct((tokens, ATTN_WIDTH), BF16),
            jax.ShapeDtypeStruct((tokens, CONV_WIDTH), BF16),
        ],
        scratch_shapes=[pltpu.VMEM((tm + SUBLANES, CONV_WIDTH), F32)],
        compiler_params=cparams,
        name="mla_conv_proj",
    )(x2, posc, invf, w_lat, w_wide, g_cq.reshape(1, -1), w_uq_p,
      g_ckv.reshape(1, -1), w_ukv.astype(BF16), g_q_p, g_kn, g_kr, conv_w, g_oc.reshape(1, -1))

    ta = ATTN_TILE
    sched = _triangle_schedule(batch, seq // ta)
    o = pl.pallas_call(
        _attn_kernel,
        grid_spec=pltpu.PrefetchScalarGridSpec(
            num_scalar_prefetch=len(sched),
            grid=(int(sched[0].shape[0]),),
            in_specs=[
                pl.BlockSpec((N_HEADS * HEAD_PAD, ta), lambda s, qn, kn, kc, qc, *_: (0, qn[s])),
                pl.BlockSpec((ta, N_HEADS * HEAD_PAD), lambda s, qn, kn, kc, qc, *_: (kn[s], 0)),
                pl.BlockSpec((N_HEADS * V_ROWS, ta), lambda s, qn, kn, kc, qc, *_: (0, kc[s])),
            ],
            out_specs=pl.BlockSpec((ta, ATTN_WIDTH), lambda s, qn, kn, kc, qc, *_: (qc[s], 0)),
            scratch_shapes=[pltpu.VMEM((N_HEADS, ta, ta), F32),
                            pltpu.VMEM((N_HEADS, 1, ta), F32),
                            pltpu.VMEM((N_HEADS, 1, ta), F32),
                            pltpu.VMEM((N_HEADS, V_ROWS, ta), F32)],
        ),
        out_shape=jax.ShapeDtypeStruct((tokens, ATTN_WIDTH), BF16),
        compiler_params=cparams,
        name="mla_flash_attn",
    )(*sched, qt, k, vt)

    return pl.pallas_call(
        _out_kernel,
        grid=(tokens // tm,),
        in_specs=[
            _row_spec(tm, ATTN_WIDTH), _row_spec(tm, ATTN_WIDTH), _row_spec(tm, CONV_WIDTH),
            _row_spec(tm, D_MODEL), _row_spec(tm, PLE_DIM),
            _const_spec((1, ATTN_WIDTH)), _const_spec((D_MODEL, D_MODEL)), _const_spec((1, D_MODEL)),
            _const_spec((D_MODEL, D_MODEL)), _const_spec((PLE_DIM, D_MODEL)),
        ],
        out_specs=_row_spec(tm, D_MODEL),
        out_shape=jax.ShapeDtypeStruct((tokens, D_MODEL), F32),
        compiler_params=cparams,
        name="mla_conv_out",
    )(o, ga, yc, x2, p2, g_oa.reshape(1, -1), w_o.astype(BF16),
      g_pl.reshape(1, -1), w_plg.astype(BF16), w_pl.astype(BF16))


def kernel(x, p, positions, g_in, w_in, g_cq, w_uq, g_ckv, w_ukv, g_q, g_k, conv_w, g_oa, g_oc,
           w_o, w_pl, w_plg, g_pl):
    batch, seq, d_model = x.shape
    depth = p.shape[0]
    tokens = batch * seq
    rows = PROJ_TILE // TOKENS_PER_ROW
    posc = jnp.repeat(positions.astype(F32).reshape(tokens // PROJ_TILE, TOKENS_PER_ROW, rows)
                      .transpose(0, 2, 1), HALF_ROPE, axis=-1).reshape(tokens // TOKENS_PER_ROW, LANES)
    inv_freq = 1.0 / (ROPE_THETA ** (jnp.arange(0, ROPE_DIM, 2, dtype=F32) / ROPE_DIM))
    invf = jnp.tile(inv_freq, LANES // HALF_ROPE).reshape(1, LANES)
    h = x.reshape(tokens, d_model)
    for i in range(depth):
        h = _layer(h, p[i].reshape(tokens, PLE_DIM), posc, invf, batch, seq,
                   g_in[i], w_in[i], g_cq[i], w_uq[i], g_ckv[i], w_ukv[i], g_q[i], g_k[i],
                   conv_w[i], g_oa[i], g_oc[i], w_o[i], w_pl[i], w_plg[i], g_pl[i])
    return h.reshape(batch, seq, d_model).astype(x.dtype)
```

```python
import functools
import math

import jax
import jax.numpy as jnp
import numpy as np
from jax import lax
from jax.experimental import pallas as pl
from jax.experimental.pallas import tpu as pltpu

D_MODEL = 1024
PLE_DIM = 256
N_HEADS = 4
NOPE_DIM = 128
ROPE_DIM = 64
HALF_ROPE = ROPE_DIM // 2
V_DIM = 128
QK_DIM = NOPE_DIM + ROPE_DIM
Q_LORA = 256
KV_LORA = 128
ATTN_WIDTH = N_HEADS * V_DIM
CONV_WIDTH = D_MODEL - ATTN_WIDTH
CONV_K = 3
ROPE_THETA = 10000.0
RMS_EPS = 1e-6
NEG_INF = -1e30

LANES = 128
SUBLANES = 8
HEAD_PAD = 2 * LANES
KPE_PAD = LANES
BF16_ROWS = 16
V_ROWS = V_DIM + BF16_ROWS
TOKENS_PER_ROW = LANES // HALF_ROPE
OFF_CQ = 0
OFF_CKV = OFF_CQ + Q_LORA
OFF_KPE = OFF_CKV + KV_LORA
LAT_SRC = OFF_KPE + ROPE_DIM
LAT_PAD = OFF_KPE + KPE_PAD
SRC_ZA = 0
SRC_CB = SRC_ZA + ATTN_WIDTH
SRC_CC = SRC_CB + CONV_WIDTH
SRC_CX = SRC_CC + CONV_WIDTH
SRC_ZC = SRC_CX + CONV_WIDTH
WIDE = SRC_ZC + CONV_WIDTH
PREP_COLS = 512
PREP_HALF = PREP_COLS // 2
PREP_PAIRS = ((SRC_ZA, SRC_ZA + PREP_HALF), (SRC_CC, SRC_CX), (SRC_CC + PREP_HALF, SRC_CX + PREP_HALF),
              (SRC_CB, SRC_ZC), (SRC_CB + PREP_HALF, SRC_ZC + PREP_HALF))
OFF_ZA = 0
OFF_CCX = OFF_ZA + PREP_COLS
OFF_CBZ = OFF_CCX + 2 * PREP_COLS

PROJ_TILE = 1024
PROJ_ROWS = PROJ_TILE // TOKENS_PER_ROW
OUT_ROWS = 256
ATTN_TILE = 1024
ATTN_Q_CHUNK = 256
VMEM_LIMIT = 48 * 1024 * 1024

BF16 = jnp.bfloat16
F32 = jnp.float32


def _rms_scale(v, width):
    return lax.rsqrt(jnp.sum(v * v, axis=-1, keepdims=True) * (1.0 / width) + RMS_EPS)


def _silu(z):
    h = 0.5 * z
    return h + h * jnp.tanh(h)


def _dot(a, b):
    return jnp.dot(a, b, preferred_element_type=F32)


def _rope_angles(posc, invf):
    ang = posc * invf
    return jnp.cos(ang), jnp.sin(ang)


def _rope_quarter(cosc, sinc, a):
    lane = lax.broadcasted_iota(jnp.int32, (1, LANES), 1)
    lo = lane < HALF_ROPE
    mid = (lane >= HALF_ROPE) & (lane < ROPE_DIM)
    c = pltpu.roll(cosc, LANES - HALF_ROPE * a, 1) if a else cosc
    s = pltpu.roll(sinc, LANES - HALF_ROPE * a, 1) if a else sinc
    return (jnp.where(lo, c, jnp.where(mid, pltpu.roll(c, HALF_ROPE, 1), 0.0)),
            jnp.where(lo, -s, jnp.where(mid, pltpu.roll(s, HALF_ROPE, 1), 0.0)), lo)


def _rope(t, c, s, lo):
    swapped = jnp.where(lo, pltpu.roll(t, LANES - HALF_ROPE, 1), pltpu.roll(t, HALF_ROPE, 1))
    return t * c + swapped * s


def _proj_kernel(tiles_per_seq,
                 x_ref, posc_ref, invf_ref, w_lat_ref, w_in_ref, g_cq_ref, w_uq_ref,
                 g_ckv_ref, w_ukv_ref, g_q_ref, g_kn_ref, g_kr_ref, conv_w_ref, g_oc_ref,
                 qt_ref, k_ref, vt_ref, ga_ref, yc_ref, carry_ref):
    tm = x_ref.shape[0]
    quarter_rows = tm // TOKENS_PER_ROW
    tables = []
    q_scale = math.log2(math.e) / math.sqrt(QK_DIM)

    @pl.when(pl.program_id(0) % tiles_per_seq == 0)
    def _():
        carry_ref[0:SUBLANES, :] = jnp.zeros((SUBLANES, CONV_WIDTH), F32)

    def rows_block(r0, nr):
        x = x_ref[r0:r0 + nr, :]
        h = (x * _rms_scale(x, D_MODEL)).astype(BF16)

        def proj(off, width):
            return _dot(h, w_in_ref[:, off:off + width])

        lat = _dot(h, w_lat_ref[...])
        c_q = lat[:, OFF_CQ:OFF_CKV]
        c_kv = lat[:, OFF_CKV:OFF_KPE]
        kpe = lat[:, OFF_KPE:LAT_PAD]
        def tile_pairs(r):
            return [(r[:, i:i + LANES], r[:, i + LANES:i + 2 * LANES]) for i in range(0, r.shape[1], 2 * LANES)]

        cv = jnp.concatenate([c * xin for c, xin in tile_pairs(proj(OFF_CCX, 2 * CONV_WIDTH))], axis=1)
        cqn = (c_q * _rms_scale(c_q, Q_LORA) * g_cq_ref[...]).astype(BF16)
        qf = _dot(cqn, w_uq_ref[...])
        gated_b = jnp.concatenate([b * _silu(z) for b, z in tile_pairs(proj(OFF_CBZ, 2 * CONV_WIDTH))], axis=1)
        ckvn = (c_kv * _rms_scale(c_kv, KV_LORA) * g_ckv_ref[...]).astype(BF16)
        kv = _dot(ckvn, w_ukv_ref[...])
        z_a = proj(OFF_ZA, ATTN_WIDTH)

        if not tables:
            tables.extend(_rope_angles(posc_ref[...], invf_ref[...]))
        assert nr == quarter_rows
        cos_t, sin_t, lo = _rope_quarter(*tables, r0 // quarter_rows)

        for hd in range(N_HEADS):
            qh = qf[:, hd * HEAD_PAD:(hd + 1) * HEAD_PAD]
            qn = qh * (_rms_scale(qh, QK_DIM) * q_scale) * g_q_ref[...]
            qt_ref[hd * HEAD_PAD:hd * HEAD_PAD + LANES, r0:r0 + nr] = qn[:, :LANES].astype(BF16).T
            qt_ref[hd * HEAD_PAD + LANES:(hd + 1) * HEAD_PAD, r0:r0 + nr] = (
                _rope(qn[:, LANES:], cos_t, sin_t, lo).astype(BF16).T)

        ss_kpe = jnp.sum(kpe * kpe, axis=-1, keepdims=True)
        k_rot = _rope(kpe * g_kr_ref[...], cos_t, sin_t, lo)
        for hd in range(N_HEADS):
            kn = kv[:, hd * (NOPE_DIM + V_DIM):hd * (NOPE_DIM + V_DIM) + NOPE_DIM]
            vh = kv[:, hd * (NOPE_DIM + V_DIM) + NOPE_DIM:(hd + 1) * (NOPE_DIM + V_DIM)]
            ss = jnp.sum(kn * kn, axis=-1, keepdims=True) + ss_kpe
            rs = lax.rsqrt(ss * (1.0 / QK_DIM) + RMS_EPS)
            k_ref[r0:r0 + nr, hd * HEAD_PAD:hd * HEAD_PAD + LANES] = (kn * rs * g_kn_ref[...]).astype(BF16)
            k_ref[r0:r0 + nr, hd * HEAD_PAD + LANES:(hd + 1) * HEAD_PAD] = (k_rot * rs).astype(BF16)
            vt_ref[hd * V_ROWS:hd * V_ROWS + V_DIM, r0:r0 + nr] = vh.astype(BF16).T
            vt_ref[hd * V_ROWS + V_DIM:(hd + 1) * V_ROWS, r0:r0 + nr] = jnp.ones((BF16_ROWS, nr), BF16)

        base = SUBLANES + r0
        prev = carry_ref[base - SUBLANES:base, :]
        carry_ref[base + nr - SUBLANES:base + nr, :] = cv[nr - SUBLANES:, :]
        row = lax.broadcasted_iota(jnp.int32, (SUBLANES, CONV_WIDTH), 0)

        def delayed(d):
            sh = pltpu.roll(cv, d, 0)
            head = jnp.where(row < d, pltpu.roll(prev, d, 0), sh[0:SUBLANES])
            return jnp.concatenate([head, sh[SUBLANES:]], axis=0)

        u = (conv_w_ref[2:3, :] * cv + conv_w_ref[1:2, :] * delayed(1) + conv_w_ref[0:1, :] * delayed(2))
        yc = gated_b * u
        yc_ref[r0:r0 + nr, :] = (yc * _rms_scale(yc, CONV_WIDTH) * g_oc_ref[...]).astype(BF16)

        for i, (za_lo, za_hi) in enumerate(tile_pairs(z_a)):
            ga_ref[r0:r0 + nr, i * LANES:(i + 1) * LANES] = _silu(za_lo).astype(BF16)
            ga_ref[r0:r0 + nr, PREP_HALF + i * LANES:PREP_HALF + (i + 1) * LANES] = _silu(za_hi).astype(BF16)

    for blk in range(tm // PROJ_ROWS):
        rows_block(blk * PROJ_ROWS, PROJ_ROWS)
    carry_ref[0:SUBLANES, :] = carry_ref[tm:tm + SUBLANES, :]


def _attn_kernel(qn_ref, kn_ref, kc_ref, qc_ref, diag_n_ref, first_c_ref, last_c_ref,
                 qt_ref, k_ref, vt_ref, o_ref, s_ref, mx_ref, m_ref, acc_ref):
    step = pl.program_id(0)
    tq = qt_ref.shape[1]
    tk = k_ref.shape[0]

    @pl.when(step == 0)
    def _():
        s_ref[...] = jnp.zeros(s_ref.shape, F32)
        mx_ref[...] = jnp.zeros(mx_ref.shape, F32)
        acc_ref[...] = jnp.zeros(acc_ref.shape, F32)

    @pl.when((first_c_ref[step] == 1) | (step == 0))
    def _():
        m_ref[...] = jnp.full(m_ref.shape, NEG_INF, F32)

    chunk = ATTN_Q_CHUNK
    nchunk = tq // chunk

    def body(diag_cur, diag_next):
        if diag_next:
            tri = (lax.broadcasted_iota(jnp.int32, (chunk, chunk), 0)
                   <= lax.broadcasted_iota(jnp.int32, (chunk, chunk), 1))

        def scores(hd, c):
            cols = slice(c * chunk, (c + 1) * chunk)
            rows = chunk * (c + 1) if diag_next else tk
            s = _dot(k_ref[0:rows, hd * HEAD_PAD:(hd + 1) * HEAD_PAD],
                     qt_ref[hd * HEAD_PAD:(hd + 1) * HEAD_PAD, cols])
            if not diag_next:
                s_ref[hd, :, cols] = s
                mx_ref[hd, :, cols] = jnp.max(s, axis=0, keepdims=True)
                return
            last = jnp.where(tri, s[rows - chunk:rows], NEG_INF)
            s_ref[hd, rows - chunk:rows, cols] = last
            mx = jnp.max(last, axis=0, keepdims=True)
            if rows > chunk:
                s_ref[hd, 0:rows - chunk, cols] = s[0:rows - chunk]
                mx = jnp.maximum(mx, jnp.max(s[0:rows - chunk], axis=0, keepdims=True))
            mx_ref[hd, :, cols] = mx

        def softmax_pv(hd, c):
            cols = slice(c * chunk, (c + 1) * chunk)
            rows = chunk * (c + 1) if diag_cur else tk
            m_prev = m_ref[hd, :, cols]
            m_new = jnp.maximum(m_prev, mx_ref[hd, :, cols])
            alpha = jnp.exp2(m_prev - m_new)
            p = jnp.exp2((s_ref[hd, 0:rows, cols] - m_new).astype(BF16))
            pv = _dot(vt_ref[hd * V_ROWS:(hd + 1) * V_ROWS, 0:rows], p)
            acc_ref[hd, :, cols] = alpha * acc_ref[hd, :, cols] + pv
            m_ref[hd, :, cols] = m_new

        for hd in range(N_HEADS):
            for c in range(nchunk):
                softmax_pv(hd, c)
                scores(hd, c)

    diag_cur = last_c_ref[step] == 1
    diag_next = diag_n_ref[step] == 1
    for cur_flag in (False, True):
        for next_flag in (False, True):
            @pl.when((diag_cur == cur_flag) & (diag_next == next_flag))
            def _(cur_flag=cur_flag, next_flag=next_flag):
                body(cur_flag, next_flag)


    @pl.when(last_c_ref[step] == 1)
    def _():
        for hd in range(N_HEADS):
            acc = acc_ref[hd]
            o_ref[:, hd * V_DIM:(hd + 1) * V_DIM] = (
                acc[:V_DIM] / acc[V_DIM:V_DIM + 1]).T.astype(o_ref.dtype)


def _out_kernel(o_ref, ga_ref, yc_ref, x_ref, p_ref, g_oa_ref, w_o_ref, g_pl_ref, w_plg_ref, w_pl_ref,
                out_ref):
    tm = x_ref.shape[0]

    def residual(r0):
        rows = slice(r0, r0 + OUT_ROWS)
        ya = o_ref[rows, :].astype(F32) * ga_ref[rows, :].astype(F32)
        ya_n = (ya * _rms_scale(ya, ATTN_WIDTH) * g_oa_ref[...]).astype(BF16)
        y = jnp.concatenate([ya_n, yc_ref[rows, :]], axis=1)
        x1 = x_ref[rows, :] + _dot(y, w_o_ref[...])
        out_ref[rows, :] = x1
        hn = (x1 * _rms_scale(x1, D_MODEL) * g_pl_ref[...]).astype(BF16)
        return r0, hn

    def gated(r0, hn):
        rows = slice(r0, r0 + OUT_ROWS)
        gate = 1.0 / (1.0 + jnp.exp(-_dot(hn, w_plg_ref[...])))
        ple = _dot(p_ref[rows, :].astype(BF16), w_pl_ref[...])
        out_ref[rows, :] = out_ref[rows, :] + gate * ple

    prev = None
    for r0 in range(0, tm, OUT_ROWS):
        cur = residual(r0)
        if prev is not None:
            gated(*prev)
        prev = cur
    gated(*prev)


def _row_spec(tile, width):
    return pl.BlockSpec((tile, width), lambda i: (i, 0))


def _const_spec(shape):
    return pl.BlockSpec(shape, lambda i: (0,) * len(shape))


def _triangle_schedule(batch, n_blk):
    pairs = [(b * n_blk + i, b * n_blk + j, j == 0, j == i)
             for b in range(batch) for i in range(n_blk) for j in range(i + 1)]
    nxt = pairs + [pairs[-1]]
    cur = [pairs[0]] + pairs
    cols = ([p[0] for p in nxt], [p[1] for p in nxt], [p[1] for p in cur], [p[0] for p in cur],
            [int(p[3]) for p in nxt],
            [0] + [int(p[2]) for p in pairs], [0] + [int(p[3]) for p in pairs])
    return tuple(jnp.asarray(np.asarray(c, np.int32)) for c in cols)


def _prep_in_kernel(rows_a_ref, rows_b_ref, wt_lat_ref, wt_a_ref, wt_b_ref, g_ref, lat_ref, wide_ref):
    g = g_ref[...]

    @pl.when(pl.program_id(0) == 0)
    def _():
        pad = jnp.zeros((LAT_PAD - LAT_SRC, D_MODEL), F32)
        lat_ref[...] = jnp.concatenate([wt_lat_ref[...] * g, pad], axis=0).T.astype(BF16)

    a = (wt_a_ref[...] * g).T.astype(BF16)
    b = (wt_b_ref[...] * g).T.astype(BF16)
    for i in range(PREP_HALF // LANES):
        wide_ref[:, 2 * i * LANES:(2 * i + 1) * LANES] = a[:, i * LANES:(i + 1) * LANES]
        wide_ref[:, (2 * i + 1) * LANES:(2 * i + 2) * LANES] = b[:, i * LANES:(i + 1) * LANES]


def _prepare_w_in(w_in, g_in):
    def rows_at(nrows, start):
        return pl.BlockSpec((pl.Element(nrows), pl.Element(D_MODEL)), start)

    rows_a = jnp.asarray(np.asarray([LAT_SRC + a for a, _ in PREP_PAIRS], np.int32))
    rows_b = jnp.asarray(np.asarray([LAT_SRC + b for _, b in PREP_PAIRS], np.int32))
    return pl.pallas_call(
        _prep_in_kernel,
        grid_spec=pltpu.PrefetchScalarGridSpec(
            num_scalar_prefetch=2,
            grid=(len(PREP_PAIRS),),
            in_specs=[rows_at(LAT_SRC, lambda j, ra, rb: (0, 0)),
                      rows_at(PREP_HALF, lambda j, ra, rb: (pl.multiple_of(ra[j], SUBLANES), 0)),
                      rows_at(PREP_HALF, lambda j, ra, rb: (pl.multiple_of(rb[j], SUBLANES), 0)),
                      pl.BlockSpec((1, D_MODEL), lambda j, ra, rb: (0, 0))],
            out_specs=[pl.BlockSpec((D_MODEL, LAT_PAD), lambda j, ra, rb: (0, 0)),
                       pl.BlockSpec((D_MODEL, PREP_COLS), lambda j, ra, rb: (0, j))],
        ),
        out_shape=[jax.ShapeDtypeStruct((D_MODEL, LAT_PAD), BF16),
                   jax.ShapeDtypeStruct((D_MODEL, WIDE), BF16)],
        compiler_params=pltpu.CompilerParams(dimension_semantics=("arbitrary",),
                                             vmem_limit_bytes=VMEM_LIMIT),
        name="mla_conv_w_in_prep",
    )(rows_a, rows_b, w_in.T, w_in.T, w_in.T, g_in.reshape(1, -1))


def _layer(x2, p2, posc, invf, batch, seq, g_in, w_in, g_cq, w_uq, g_ckv, w_ukv, g_q, g_k,
           conv_w, g_oa, g_oc, w_o, w_pl, w_plg, g_pl):
    tokens = batch * seq
    tm = PROJ_TILE
    assert seq % tm == 0 and seq % ATTN_TILE == 0

    w_lat, w_wide = _prepare_w_in(w_in, g_in)
    w_uq_p = jnp.pad(w_uq.reshape(Q_LORA, N_HEADS, QK_DIM),
                     ((0, 0), (0, 0), (0, HEAD_PAD - QK_DIM))).reshape(Q_LORA, N_HEADS * HEAD_PAD).astype(BF16)
    g_q_p = jnp.pad(g_q, (0, HEAD_PAD - QK_DIM)).reshape(1, HEAD_PAD)
    g_kn = g_k[:NOPE_DIM].reshape(1, NOPE_DIM)
    g_kr = jnp.pad(g_k[NOPE_DIM:], (0, KPE_PAD - ROPE_DIM)).reshape(1, KPE_PAD)

    cparams = pltpu.CompilerParams(dimension_semantics=("arbitrary",), vmem_limit_bytes=VMEM_LIMIT)

    def col_spec(rows, tile):
        return pl.BlockSpec((rows, tile), lambda i: (0, i))

    qt, k, vt, ga, yc = pl.pallas_call(
        functools.partial(_proj_kernel, seq // tm),
        grid=(tokens // tm,),
        in_specs=[
            _row_spec(tm, D_MODEL), _row_spec(tm // TOKENS_PER_ROW, LANES), _const_spec((1, LANES)),
            _const_spec((D_MODEL, LAT_PAD)), _const_spec((D_MODEL, WIDE)),
            _const_spec((1, Q_LORA)), _const_spec((Q_LORA, N_HEADS * HEAD_PAD)),
            _const_spec((1, KV_LORA)), _const_spec((KV_LORA, N_HEADS * (NOPE_DIM + V_DIM))),
            _const_spec((1, HEAD_PAD)), _const_spec((1, NOPE_DIM)), _const_spec((1, KPE_PAD)),
            _const_spec((CONV_K, CONV_WIDTH)), _const_spec((1, CONV_WIDTH)),
        ],
        out_specs=[
            col_spec(N_HEADS * HEAD_PAD, tm), _row_spec(tm, N_HEADS * HEAD_PAD),
            col_spec(N_HEADS * V_ROWS, tm), _row_spec(tm, ATTN_WIDTH), _row_spec(tm, CONV_WIDTH),
        ],
        out_shape=[
            jax.ShapeDtypeStruct((N_HEADS * HEAD_PAD, tokens), BF16),
            jax.ShapeDtypeStruct((tokens, N_HEADS * HEAD_PAD), BF16),
            jax.ShapeDtypeStruct((N_HEADS * V_ROWS, tokens), BF16),
            jax.ShapeDtypeStruct((tokens, ATTN_WIDTH), BF16),
            jax.ShapeDtypeStruct((tokens, CONV_WIDTH), BF16),
        ],
        scratch_shapes=[pltpu.VMEM((tm + SUBLANES, CONV_WIDTH), F32)],
        compiler_params=cparams,
        name="mla_conv_proj",
    )(x2, posc, invf, w_lat, w_wide, g_cq.reshape(1, -1), w_uq_p,
      g_ckv.reshape(1, -1), w_ukv.astype(BF16), g_q_p, g_kn, g_kr, conv_w, g_oc.reshape(1, -1))

    ta = ATTN_TILE
    sched = _triangle_schedule(batch, seq // ta)
    o = pl.pallas_call(
        _attn_kernel,
        grid_spec=pltpu.PrefetchScalarGridSpec(
            num_scalar_prefetch=len(sched),
            grid=(int(sched[0].shape[0]),),
            in_specs=[
                pl.BlockSpec((N_HEADS * HEAD_PAD, ta), lambda s, qn, kn, kc, qc, *_: (0, qn[s])),
                pl.BlockSpec((ta, N_HEADS * HEAD_PAD), lambda s, qn, kn, kc, qc, *_: (kn[s], 0)),
                pl.BlockSpec((N_HEADS * V_ROWS, ta), lambda s, qn, kn, kc, qc, *_: (0, kc[s])),
            ],
            out_specs=pl.BlockSpec((ta, ATTN_WIDTH), lambda s, qn, kn, kc, qc, *_: (qc[s], 0)),
            scratch_shapes=[pltpu.VMEM((N_HEADS, ta, ta), F32),
                            pltpu.VMEM((N_HEADS, 1, ta), F32),
                            pltpu.VMEM((N_HEADS, 1, ta), F32),
                            pltpu.VMEM((N_HEADS, V_ROWS, ta), F32)],
        ),
        out_shape=jax.ShapeDtypeStruct((tokens, ATTN_WIDTH), BF16),
        compiler_params=cparams,
        name="mla_flash_attn",
    )(*sched, qt, k, vt)

    return pl.pallas_call(
        _out_kernel,
        grid=(tokens // tm,),
        in_specs=[
            _row_spec(tm, ATTN_WIDTH), _row_spec(tm, ATTN_WIDTH), _row_spec(tm, CONV_WIDTH),
            _row_spec(tm, D_MODEL), _row_spec(tm, PLE_DIM),
            _const_spec((1, ATTN_WIDTH)), _const_spec((D_MODEL, D_MODEL)), _const_spec((1, D_MODEL)),
            _const_spec((D_MODEL, D_MODEL)), _const_spec((PLE_DIM, D_MODEL)),
        ],
        out_specs=_row_spec(tm, D_MODEL),
        out_shape=jax.ShapeDtypeStruct((tokens, D_MODEL), F32),
        compiler_params=cparams,
        name="mla_conv_out",
    )(o, ga, yc, x2, p2, g_oa.reshape(1, -1), w_o.astype(BF16),
      g_pl.reshape(1, -1), w_plg.astype(BF16), w_pl.astype(BF16))


def kernel(x, p, positions, g_in, w_in, g_cq, w_uq, g_ckv, w_ukv, g_q, g_k, conv_w, g_oa, g_oc,
           w_o, w_pl, w_plg, g_pl):
    batch, seq, d_model = x.shape
    depth = p.shape[0]
    tokens = batch * seq
    rows = PROJ_TILE // TOKENS_PER_ROW
    posc = jnp.repeat(positions.astype(F32).reshape(tokens // PROJ_TILE, TOKENS_PER_ROW, rows)
                      .transpose(0, 2, 1), HALF_ROPE, axis=-1).reshape(tokens // TOKENS_PER_ROW, LANES)
    inv_freq = 1.0 / (ROPE_THETA ** (jnp.arange(0, ROPE_DIM, 2, dtype=F32) / ROPE_DIM))
    invf = jnp.tile(inv_freq, LANES // HALF_ROPE).reshape(1, LANES)
    h = x.reshape(tokens, d_model)
    for i in range(depth):
        h = _layer(h, p[i].reshape(tokens, PLE_DIM), posc, invf, batch, seq,
                   g_in[i], w_in[i], g_cq[i], w_uq[i], g_ckv[i], w_ukv[i], g_q[i], g_k[i],
                   conv_w[i], g_oa[i], g_oc[i], w_o[i], w_pl[i], w_plg[i], g_pl[i])
    return h.reshape(batch, seq, d_model).astype(x.dtype)
```

```python
import functools
import math

import jax
import jax.numpy as jnp
import numpy as np
from jax import lax
from jax.experimental import pallas as pl
from jax.experimental.pallas import tpu as pltpu

D_MODEL = 1024
PLE_DIM = 256
N_HEADS = 4
NOPE_DIM = 128
ROPE_DIM = 64
HALF_ROPE = ROPE_DIM // 2
V_DIM = 128
QK_DIM = NOPE_DIM + ROPE_DIM
Q_LORA = 256
KV_LORA = 128
ATTN_WIDTH = N_HEADS * V_DIM
CONV_WIDTH = D_MODEL - ATTN_WIDTH
CONV_K = 3
ROPE_THETA = 10000.0
RMS_EPS = 1e-6
NEG_INF = -1e30

LANES = 128
SUBLANES = 8
HEAD_PAD = 2 * LANES
KPE_PAD = LANES
BF16_ROWS = 16
V_ROWS = V_DIM + BF16_ROWS
TOKENS_PER_ROW = LANES // HALF_ROPE
OFF_CQ = 0
OFF_CKV = OFF_CQ + Q_LORA
OFF_KPE = OFF_CKV + KV_LORA
LAT_SRC = OFF_KPE + ROPE_DIM
LAT_PAD = OFF_KPE + KPE_PAD
SRC_ZA = 0
SRC_CB = SRC_ZA + ATTN_WIDTH
SRC_CC = SRC_CB + CONV_WIDTH
SRC_CX = SRC_CC + CONV_WIDTH
SRC_ZC = SRC_CX + CONV_WIDTH
WIDE = SRC_ZC + CONV_WIDTH
PREP_COLS = 512
PREP_HALF = PREP_COLS // 2
PREP_PAIRS = ((SRC_ZA, SRC_ZA + PREP_HALF), (SRC_CC, SRC_CX), (SRC_CC + PREP_HALF, SRC_CX + PREP_HALF),
              (SRC_CB, SRC_ZC), (SRC_CB + PREP_HALF, SRC_ZC + PREP_HALF))
OFF_ZA = 0
OFF_CCX = OFF_ZA + PREP_COLS
OFF_CBZ = OFF_CCX + 2 * PREP_COLS

PROJ_TILE = 1024
PROJ_ROWS = PROJ_TILE // TOKENS_PER_ROW
OUT_ROWS = 256
ATTN_TILE = 1024
ATTN_Q_CHUNK = 256
VMEM_LIMIT = 48 * 1024 * 1024

BF16 = jnp.bfloat16
F32 = jnp.float32


def _rms_scale(v, width):
    return lax.rsqrt(jnp.sum(v * v, axis=-1, keepdims=True) * (1.0 / width) + RMS_EPS)


def _silu(z):
    h = 0.5 * z
    return h + h * jnp.tanh(h)


def _dot(a, b):
    return jnp.dot(a, b, preferred_element_type=F32)


def _rope_angles(posc, invf):
    ang = posc * invf
    return jnp.cos(ang), jnp.sin(ang)


def _rope_quarter(cosc, sinc, a):
    lane = lax.broadcasted_iota(jnp.int32, (1, LANES), 1)
    lo = lane < HALF_ROPE
    mid = (lane >= HALF_ROPE) & (lane < ROPE_DIM)
    c = pltpu.roll(cosc, LANES - HALF_ROPE * a, 1) if a else cosc
    s = pltpu.roll(sinc, LANES - HALF_ROPE * a, 1) if a else sinc
    return (jnp.where(lo, c, jnp.where(mid, pltpu.roll(c, HALF_ROPE, 1), 0.0)),
            jnp.where(lo, -s, jnp.where(mid, pltpu.roll(s, HALF_ROPE, 1), 0.0)), lo)


def _rope(t, c, s, lo):
    swapped = jnp.where(lo, pltpu.roll(t, LANES - HALF_ROPE, 1), pltpu.roll(t, HALF_ROPE, 1))
    return t * c + swapped * s


def _proj_kernel(tiles_per_seq,
                 x_ref, posc_ref, invf_ref, w_lat_ref, w_in_ref, g_cq_ref, w_uq_ref,
                 g_ckv_ref, w_ukv_ref, g_q_ref, g_kn_ref, g_kr_ref, conv_w_ref, g_oc_ref,
                 qt_ref, k_ref, vt_ref, ga_ref, yc_ref, carry_ref):
    tm = x_ref.shape[0]
    quarter_rows = tm // TOKENS_PER_ROW
    tables = []
    q_scale = math.log2(math.e) / math.sqrt(QK_DIM)

    @pl.when(pl.program_id(0) % tiles_per_seq == 0)
    def _():
        carry_ref[0:SUBLANES, :] = jnp.zeros((SUBLANES, CONV_WIDTH), F32)

    def rows_block(r0, nr):
        x = x_ref[r0:r0 + nr, :]
        h = (x * _rms_scale(x, D_MODEL)).astype(BF16)

        def proj(off, width):
            return _dot(h, w_in_ref[:, off:off + width])

        lat = _dot(h, w_lat_ref[...])
        c_q = lat[:, OFF_CQ:OFF_CKV]
        c_kv = lat[:, OFF_CKV:OFF_KPE]
        kpe = lat[:, OFF_KPE:LAT_PAD]
        def tile_pairs(r):
            return [(r[:, i:i + LANES], r[:, i + LANES:i + 2 * LANES]) for i in range(0, r.shape[1], 2 * LANES)]

        cv = jnp.concatenate([c * xin for c, xin in tile_pairs(proj(OFF_CCX, 2 * CONV_WIDTH))], axis=1)
        cqn = (c_q * _rms_scale(c_q, Q_LORA) * g_cq_ref[...]).astype(BF16)
        qf = _dot(cqn, w_uq_ref[...])
        gated_b = jnp.concatenate([b * _silu(z) for b, z in tile_pairs(proj(OFF_CBZ, 2 * CONV_WIDTH))], axis=1)
        z_a = proj(OFF_ZA, ATTN_WIDTH)
        ckvn = (c_kv * _rms_scale(c_kv, KV_LORA) * g_ckv_ref[...]).astype(BF16)
        kv = _dot(ckvn, w_ukv_ref[...])

        if not tables:
            tables.extend(_rope_angles(posc_ref[...], invf_ref[...]))
        assert nr == quarter_rows
        cos_t, sin_t, lo = _rope_quarter(*tables, r0 // quarter_rows)

        for hd in range(N_HEADS):
            qh = qf[:, hd * HEAD_PAD:(hd + 1) * HEAD_PAD]
            qn = qh * (_rms_scale(qh, QK_DIM) * q_scale) * g_q_ref[...]
            qt_ref[hd * HEAD_PAD:hd * HEAD_PAD + LANES, r0:r0 + nr] = qn[:, :LANES].astype(BF16).T
            qt_ref[hd * HEAD_PAD + LANES:(hd + 1) * HEAD_PAD, r0:r0 + nr] = (
                _rope(qn[:, LANES:], cos_t, sin_t, lo).astype(BF16).T)

        ss_kpe = jnp.sum(kpe * kpe, axis=-1, keepdims=True)
        k_rot = _rope(kpe * g_kr_ref[...], cos_t, sin_t, lo)
        for hd in range(N_HEADS):
            kn = kv[:, hd * (NOPE_DIM + V_DIM):hd * (NOPE_DIM + V_DIM) + NOPE_DIM]
            vh = kv[:, hd * (NOPE_DIM + V_DIM) + NOPE_DIM:(hd + 1) * (NOPE_DIM + V_DIM)]
            ss = jnp.sum(kn * kn, axis=-1, keepdims=True) + ss_kpe
            rs = lax.rsqrt(ss * (1.0 / QK_DIM) + RMS_EPS)
            k_ref[r0:r0 + nr, hd * HEAD_PAD:hd * HEAD_PAD + LANES] = (kn * rs * g_kn_ref[...]).astype(BF16)
            k_ref[r0:r0 + nr, hd * HEAD_PAD + LANES:(hd + 1) * HEAD_PAD] = (k_rot * rs).astype(BF16)
            vt_ref[hd * V_ROWS:hd * V_ROWS + V_DIM, r0:r0 + nr] = vh.astype(BF16).T
            vt_ref[hd * V_ROWS + V_DIM:(hd + 1) * V_ROWS, r0:r0 + nr] = jnp.ones((BF16_ROWS, nr), BF16)

        base = SUBLANES + r0
        prev = carry_ref[base - SUBLANES:base, :]
        carry_ref[base + nr - SUBLANES:base + nr, :] = cv[nr - SUBLANES:, :]
        row = lax.broadcasted_iota(jnp.int32, (SUBLANES, CONV_WIDTH), 0)

        def delayed(d):
            sh = pltpu.roll(cv, d, 0)
            head = jnp.where(row < d, pltpu.roll(prev, d, 0), sh[0:SUBLANES])
            return jnp.concatenate([head, sh[SUBLANES:]], axis=0)

        u = (conv_w_ref[2:3, :] * cv + conv_w_ref[1:2, :] * delayed(1) + conv_w_ref[0:1, :] * delayed(2))
        yc = gated_b * u
        yc_ref[r0:r0 + nr, :] = (yc * _rms_scale(yc, CONV_WIDTH) * g_oc_ref[...]).astype(BF16)

        for i, (za_lo, za_hi) in enumerate(tile_pairs(z_a)):
            ga_ref[r0:r0 + nr, i * LANES:(i + 1) * LANES] = _silu(za_lo).astype(BF16)
            ga_ref[r0:r0 + nr, PREP_HALF + i * LANES:PREP_HALF + (i + 1) * LANES] = _silu(za_hi).astype(BF16)

    for blk in range(tm // PROJ_ROWS):
        rows_block(blk * PROJ_ROWS, PROJ_ROWS)
    carry_ref[0:SUBLANES, :] = carry_ref[tm:tm + SUBLANES, :]


def _attn_kernel(qn_ref, kn_ref, kc_ref, qc_ref, diag_n_ref, first_c_ref, last_c_ref,
                 qt_ref, k_ref, vt_ref, o_ref, s_ref, mx_ref, m_ref, acc_ref):
    step = pl.program_id(0)
    tq = qt_ref.shape[1]
    tk = k_ref.shape[0]

    @pl.when(step == 0)
    def _():
        s_ref[...] = jnp.zeros(s_ref.shape, F32)
        mx_ref[...] = jnp.zeros(mx_ref.shape, F32)
        acc_ref[...] = jnp.zeros(acc_ref.shape, F32)

    @pl.when((first_c_ref[step] == 1) | (step == 0))
    def _():
        m_ref[...] = jnp.full(m_ref.shape, NEG_INF, F32)

    chunk = ATTN_Q_CHUNK
    nchunk = tq // chunk

    def body(diag_cur, diag_next):
        if diag_next:
            tri = (lax.broadcasted_iota(jnp.int32, (chunk, chunk), 0)
                   <= lax.broadcasted_iota(jnp.int32, (chunk, chunk), 1))

        def scores(hd, c):
            cols = slice(c * chunk, (c + 1) * chunk)
            rows = chunk * (c + 1) if diag_next else tk
            s = _dot(k_ref[0:rows, hd * HEAD_PAD:(hd + 1) * HEAD_PAD],
                     qt_ref[hd * HEAD_PAD:(hd + 1) * HEAD_PAD, cols])
            if not diag_next:
                s_ref[hd, :, cols] = s
                mx_ref[hd, :, cols] = jnp.max(s, axis=0, keepdims=True)
                return
            last = jnp.where(tri, s[rows - chunk:rows], NEG_INF)
            s_ref[hd, rows - chunk:rows, cols] = last
            mx = jnp.max(last, axis=0, keepdims=True)
            if rows > chunk:
                s_ref[hd, 0:rows - chunk, cols] = s[0:rows - chunk]
                mx = jnp.maximum(mx, jnp.max(s[0:rows - chunk], axis=0, keepdims=True))
            mx_ref[hd, :, cols] = mx

        def softmax_pv(hd, c):
            cols = slice(c * chunk, (c + 1) * chunk)
            rows = chunk * (c + 1) if diag_cur else tk
            m_prev = m_ref[hd, :, cols]
            m_new = jnp.maximum(m_prev, mx_ref[hd, :, cols])
            alpha = jnp.exp2(m_prev - m_new)
            p = jnp.exp2((s_ref[hd, 0:rows, cols] - m_new).astype(BF16))
            pv = _dot(vt_ref[hd * V_ROWS:(hd + 1) * V_ROWS, 0:rows], p)
            acc_ref[hd, :, cols] = alpha * acc_ref[hd, :, cols] + pv
            m_ref[hd, :, cols] = m_new

        for hd in range(N_HEADS):
            for c in range(nchunk):
                softmax_pv(hd, c)
                scores(hd, c)

    diag_cur = last_c_ref[step] == 1
    diag_next = diag_n_ref[step] == 1
    for cur_flag in (False, True):
        for next_flag in (False, True):
            @pl.when((diag_cur == cur_flag) & (diag_next == next_flag))
            def _(cur_flag=cur_flag, next_flag=next_flag):
                body(cur_flag, next_flag)


    @pl.when(last_c_ref[step] == 1)
    def _():
        for hd in range(N_HEADS):
            acc = acc_ref[hd]
            o_ref[:, hd * V_DIM:(hd + 1) * V_DIM] = (
                acc[:V_DIM] / acc[V_DIM:V_DIM + 1]).T.astype(o_ref.dtype)


def _out_kernel(o_ref, ga_ref, yc_ref, x_ref, p_ref, g_oa_ref, w_o_ref, g_pl_ref, w_plg_ref, w_pl_ref,
                out_ref):
    tm = x_ref.shape[0]

    def residual(r0):
        rows = slice(r0, r0 + OUT_ROWS)
        ya = o_ref[rows, :].astype(F32) * ga_ref[rows, :].astype(F32)
        ya_n = (ya * _rms_scale(ya, ATTN_WIDTH) * g_oa_ref[...]).astype(BF16)
        y = jnp.concatenate([ya_n, yc_ref[rows, :]], axis=1)
        x1 = x_ref[rows, :] + _dot(y, w_o_ref[...])
        out_ref[rows, :] = x1
        hn = (x1 * _rms_scale(x1, D_MODEL) * g_pl_ref[...]).astype(BF16)
        return r0, hn

    def gated(r0, hn):
        rows = slice(r0, r0 + OUT_ROWS)
        gate = 1.0 / (1.0 + jnp.exp(-_dot(hn, w_plg_ref[...])))
        ple = _dot(p_ref[rows, :].astype(BF16), w_pl_ref[...])
        out_ref[rows, :] = out_ref[rows, :] + gate * ple

    prev = None
    for r0 in range(0, tm, OUT_ROWS):
        cur = residual(r0)
        if prev is not None:
            gated(*prev)
        prev = cur
    gated(*prev)


def _row_spec(tile, width):
    return pl.BlockSpec((tile, width), lambda i: (i, 0))


def _const_spec(shape):
    return pl.BlockSpec(shape, lambda i: (0,) * len(shape))


def _triangle_schedule(batch, n_blk):
    pairs = [(b * n_blk + i, b * n_blk + j, j == 0, j == i)
             for b in range(batch) for i in range(n_blk) for j in range(i + 1)]
    nxt = pairs + [pairs[-1]]
    cur = [pairs[0]] + pairs
    cols = ([p[0] for p in nxt], [p[1] for p in nxt], [p[1] for p in cur], [p[0] for p in cur],
            [int(p[3]) for p in nxt],
            [0] + [int(p[2]) for p in pairs], [0] + [int(p[3]) for p in pairs])
    return tuple(jnp.asarray(np.asarray(c, np.int32)) for c in cols)


def _prep_in_kernel(rows_a_ref, rows_b_ref, wt_lat_ref, wt_a_ref, wt_b_ref, g_ref, lat_ref, wide_ref):
    g = g_ref[...]

    @pl.when(pl.program_id(0) == 0)
    def _():
        pad = jnp.zeros((LAT_PAD - LAT_SRC, D_MODEL), F32)
        lat_ref[...] = jnp.concatenate([wt_lat_ref[...] * g, pad], axis=0).T.astype(BF16)

    a = (wt_a_ref[...] * g).T.astype(BF16)
    b = (wt_b_ref[...] * g).T.astype(BF16)
    for i in range(PREP_HALF // LANES):
        wide_ref[:, 2 * i * LANES:(2 * i + 1) * LANES] = a[:, i * LANES:(i + 1) * LANES]
        wide_ref[:, (2 * i + 1) * LANES:(2 * i + 2) * LANES] = b[:, i * LANES:(i + 1) * LANES]


def _prepare_w_in(w_in, g_in):
    def rows_at(nrows, start):
        return pl.BlockSpec((pl.Element(nrows), pl.Element(D_MODEL)), start)

    rows_a = jnp.asarray(np.asarray([LAT_SRC + a for a, _ in PREP_PAIRS], np.int32))
    rows_b = jnp.asarray(np.asarray([LAT_SRC + b for _, b in PREP_PAIRS], np.int32))
    return pl.pallas_call(
        _prep_in_kernel,
        grid_spec=pltpu.PrefetchScalarGridSpec(
            num_scalar_prefetch=2,
            grid=(len(PREP_PAIRS),),
            in_specs=[rows_at(LAT_SRC, lambda j, ra, rb: (0, 0)),
                      rows_at(PREP_HALF, lambda j, ra, rb: (pl.multiple_of(ra[j], SUBLANES), 0)),
                      rows_at(PREP_HALF, lambda j, ra, rb: (pl.multiple_of(rb[j], SUBLANES), 0)),
                      pl.BlockSpec((1, D_MODEL), lambda j, ra, rb: (0, 0))],
            out_specs=[pl.BlockSpec((D_MODEL, LAT_PAD), lambda j, ra, rb: (0, 0)),
                       pl.BlockSpec((D_MODEL, PREP_COLS), lambda j, ra, rb: (0, j))],
        ),
        out_shape=[jax.ShapeDtypeStruct((D_MODEL, LAT_PAD), BF16),
                   jax.ShapeDtypeStruct((D_MODEL, WIDE), BF16)],
        compiler_params=pltpu.CompilerParams(dimension_semantics=("arbitrary",),
                                             vmem_limit_bytes=VMEM_LIMIT),
        name="mla_conv_w_in_prep",
    )(rows_a, rows_b, w_in.T, w_in.T, w_in.T, g_in.reshape(1, -1))


def _layer(x2, p2, posc, invf, batch, seq, g_in, w_in, g_cq, w_uq, g_ckv, w_ukv, g_q, g_k,
           conv_w, g_oa, g_oc, w_o, w_pl, w_plg, g_pl):
    tokens = batch * seq
    tm = PROJ_TILE
    assert seq % tm == 0 and seq % ATTN_TILE == 0

    w_lat, w_wide = _prepare_w_in(w_in, g_in)
    w_uq_p = jnp.pad(w_uq.reshape(Q_LORA, N_HEADS, QK_DIM),
                     ((0, 0), (0, 0), (0, HEAD_PAD - QK_DIM))).reshape(Q_LORA, N_HEADS * HEAD_PAD).astype(BF16)
    g_q_p = jnp.pad(g_q, (0, HEAD_PAD - QK_DIM)).reshape(1, HEAD_PAD)
    g_kn = g_k[:NOPE_DIM].reshape(1, NOPE_DIM)
    g_kr = jnp.pad(g_k[NOPE_DIM:], (0, KPE_PAD - ROPE_DIM)).reshape(1, KPE_PAD)

    cparams = pltpu.CompilerParams(dimension_semantics=("arbitrary",), vmem_limit_bytes=VMEM_LIMIT)

    def col_spec(rows, tile):
        return pl.BlockSpec((rows, tile), lambda i: (0, i))

    qt, k, vt, ga, yc = pl.pallas_call(
        functools.partial(_proj_kernel, seq // tm),
        grid=(tokens // tm,),
        in_specs=[
            _row_spec(tm, D_MODEL), _row_spec(tm // TOKENS_PER_ROW, LANES), _const_spec((1, LANES)),
            _const_spec((D_MODEL, LAT_PAD)), _const_spec((D_MODEL, WIDE)),
            _const_spec((1, Q_LORA)), _const_spec((Q_LORA, N_HEADS * HEAD_PAD)),
            _const_spec((1, KV_LORA)), _const_spec((KV_LORA, N_HEADS * (NOPE_DIM + V_DIM))),
            _const_spec((1, HEAD_PAD)), _const_spec((1, NOPE_DIM)), _const_spec((1, KPE_PAD)),
            _const_spec((CONV_K, CONV_WIDTH)), _const_spec((1, CONV_WIDTH)),
        ],
        out_specs=[
            col_spec(N_HEADS * HEAD_PAD, tm), _row_spec(tm, N_HEADS * HEAD_PAD),
            col_spec(N_HEADS * V_ROWS, tm), _row_spec(tm, ATTN_WIDTH), _row_spec(tm, CONV_WIDTH),
        ],
        out_shape=[
            jax.ShapeDtypeStruct((N_HEADS * HEAD_PAD, tokens), BF16),
            jax.ShapeDtypeStruct((tokens, N_HEADS * HEAD_PAD), BF16),
            jax.ShapeDtypeStruct((N_HEADS * V_ROWS, tokens), BF16),
            jax.ShapeDtypeStruct((tokens, ATTN_WIDTH), BF16),
            jax.ShapeDtypeStruct((tokens, CONV_WIDTH), BF16),
        ],
        scratch_shapes=[pltpu.VMEM((tm + SUBLANES, CONV_WIDTH), F32)],
        compiler_params=cparams,
        name="mla_conv_proj",
    )(x2, posc, invf, w_lat, w_wide, g_cq.reshape(1, -1), w_uq_p,
      g_ckv.reshape(1, -1), w_ukv.astype(BF16), g_q_p, g_kn, g_kr, conv_w, g_oc.reshape(1, -1))

    ta = ATTN_TILE
    sched = _triangle_schedule(batch, seq // ta)
    o = pl.pallas_call(
        _attn_kernel,
        grid_spec=pltpu.PrefetchScalarGridSpec(
            num_scalar_prefetch=len(sched),
            grid=(int(sched[0].shape[0]),),
            in_specs=[
                pl.BlockSpec((N_HEADS * HEAD_PAD, ta), lambda s, qn, kn, kc, qc, *_: (0, qn[s])),
                pl.BlockSpec((ta, N_HEADS * HEAD_PAD), lambda s, qn, kn, kc, qc, *_: (kn[s], 0)),
                pl.BlockSpec((N_HEADS * V_ROWS, ta), lambda s, qn, kn, kc, qc, *_: (0, kc[s])),
            ],
            out_specs=pl.BlockSpec((ta, ATTN_WIDTH), lambda s, qn, kn, kc, qc, *_: (qc[s], 0)),
            scratch_shapes=[pltpu.VMEM((N_HEADS, ta, ta), F32),
                            pltpu.VMEM((N_HEADS, 1, ta), F32),
                            pltpu.VMEM((N_HEADS, 1, ta), F32),
                            pltpu.VMEM((N_HEADS, V_ROWS, ta), F32)],
        ),
        out_shape=jax.ShapeDtypeStruct((tokens, ATTN_WIDTH), BF16),
        compiler_params=cparams,
        name="mla_flash_attn",
    )(*sched, qt, k, vt)

    return pl.pallas_call(
        _out_kernel,
        grid=(tokens // tm,),
        in_specs=[
            _row_spec(tm, ATTN_WIDTH), _row_spec(tm, ATTN_WIDTH), _row_spec(tm, CONV_WIDTH),
            _row_spec(tm, D_MODEL), _row_spec(tm, PLE_DIM),
            _const_spec((1, ATTN_WIDTH)), _const_spec((D_MODEL, D_MODEL)), _const_spec((1, D_MODEL)),
            _const_spec((D_MODEL, D_MODEL)), _const_spec((PLE_DIM, D_MODEL)),
        ],
        out_specs=_row_spec(tm, D_MODEL),
        out_shape=jax.ShapeDtypeStruct((tokens, D_MODEL), F32),
        compiler_params=cparams,
        name="mla_conv_out",
    )(o, ga, yc, x2, p2, g_oa.reshape(1, -1), w_o.astype(BF16),
      g_pl.reshape(1, -1), w_plg.astype(BF16), w_pl.astype(BF16))


def kernel(x, p, positions, g_in, w_in, g_cq, w_uq, g_ckv, w_ukv, g_q, g_k, conv_w, g_oa, g_oc,
           w_o, w_pl, w_plg, g_pl):
    batch, seq, d_model = x.shape
    depth = p.shape[0]
    tokens = batch * seq
    rows = PROJ_TILE // TOKENS_PER_ROW
    posc = jnp.repeat(positions.astype(F32).reshape(tokens // PROJ_TILE, TOKENS_PER_ROW, rows)
                      .transpose(0, 2, 1), HALF_ROPE, axis=-1).reshape(tokens // TOKENS_PER_ROW, LANES)
    inv_freq = 1.0 / (ROPE_THETA ** (jnp.arange(0, ROPE_DIM, 2, dtype=F32) / ROPE_DIM))
    invf = jnp.tile(inv_freq, LANES // HALF_ROPE).reshape(1, LANES)
    h = x.reshape(tokens, d_model)
    for i in range(depth):
        h = _layer(h, p[i].reshape(tokens, PLE_DIM), posc, invf, batch, seq,
                   g_in[i], w_in[i], g_cq[i], w_uq[i], g_ckv[i], w_ukv[i], g_q[i], g_k[i],
                   conv_w[i], g_oa[i], g_oc[i], w_o[i], w_pl[i], w_plg[i], g_pl[i])
    return h.reshape(batch, seq, d_model).astype(x.dtype)
```

```python
import functools
import math

import jax
import jax.numpy as jnp
import numpy as np
from jax import lax
from jax.experimental import pallas as pl
from jax.experimental.pallas import tpu as pltpu

D_MODEL = 1024
PLE_DIM = 256
N_HEADS = 4
NOPE_DIM = 128
ROPE_DIM = 64
HALF_ROPE = ROPE_DIM // 2
V_DIM = 128
QK_DIM = NOPE_DIM + ROPE_DIM
Q_LORA = 256
KV_LORA = 128
ATTN_WIDTH = N_HEADS * V_DIM
CONV_WIDTH = D_MODEL - ATTN_WIDTH
CONV_K = 3
ROPE_THETA = 10000.0
RMS_EPS = 1e-6
NEG_INF = -1e30

LANES = 128
SUBLANES = 8
HEAD_PAD = 2 * LANES
KPE_PAD = LANES
BF16_ROWS = 16
V_ROWS = V_DIM + BF16_ROWS
TOKENS_PER_ROW = LANES // HALF_ROPE
OFF_CQ = 0
OFF_CKV = OFF_CQ + Q_LORA
OFF_KPE = OFF_CKV + KV_LORA
LAT_SRC = OFF_KPE + ROPE_DIM
LAT_PAD = OFF_KPE + KPE_PAD
SRC_ZA = 0
SRC_CB = SRC_ZA + ATTN_WIDTH
SRC_CC = SRC_CB + CONV_WIDTH
SRC_CX = SRC_CC + CONV_WIDTH
SRC_ZC = SRC_CX + CONV_WIDTH
WIDE = SRC_ZC + CONV_WIDTH
PREP_COLS = 512
PREP_HALF = PREP_COLS // 2
PREP_PAIRS = ((SRC_ZA, SRC_ZA + PREP_HALF), (SRC_CC, SRC_CX), (SRC_CC + PREP_HALF, SRC_CX + PREP_HALF),
              (SRC_CB, SRC_ZC), (SRC_CB + PREP_HALF, SRC_ZC + PREP_HALF))
OFF_ZA = 0
OFF_CCX = OFF_ZA + PREP_COLS
OFF_CBZ = OFF_CCX + 2 * PREP_COLS

PROJ_TILE = 1024
PROJ_ROWS = PROJ_TILE // TOKENS_PER_ROW
OUT_ROWS = 256
ATTN_TILE = 1024
ATTN_Q_CHUNK = 256
VMEM_LIMIT = 48 * 1024 * 1024

BF16 = jnp.bfloat16
F32 = jnp.float32


def _rms_scale(v, width):
    return lax.rsqrt(jnp.sum(v * v, axis=-1, keepdims=True) * (1.0 / width) + RMS_EPS)


def _silu(z):
    h = 0.5 * z
    return h + h * jnp.tanh(h)


def _dot(a, b):
    return jnp.dot(a, b, preferred_element_type=F32)


def _rope_angles(posc, invf):
    ang = posc * invf
    return jnp.cos(ang), jnp.sin(ang)


def _rope_quarter(cosc, sinc, a):
    lane = lax.broadcasted_iota(jnp.int32, (1, LANES), 1)
    lo = lane < HALF_ROPE
    mid = (lane >= HALF_ROPE) & (lane < ROPE_DIM)
    c = pltpu.roll(cosc, LANES - HALF_ROPE * a, 1) if a else cosc
    s = pltpu.roll(sinc, LANES - HALF_ROPE * a, 1) if a else sinc
    return (jnp.where(lo, c, jnp.where(mid, pltpu.roll(c, HALF_ROPE, 1), 0.0)),
            jnp.where(lo, -s, jnp.where(mid, pltpu.roll(s, HALF_ROPE, 1), 0.0)), lo)


def _rope(t, c, s, lo):
    swapped = jnp.where(lo, pltpu.roll(t, LANES - HALF_ROPE, 1), pltpu.roll(t, HALF_ROPE, 1))
    return t * c + swapped * s


def _proj_kernel(tiles_per_seq,
                 x_ref, posc_ref, invf_ref, w_lat_ref, w_in_ref, g_cq_ref, w_uq_ref,
                 g_ckv_ref, w_ukv_ref, g_q_ref, g_kn_ref, g_kr_ref, conv_w_ref, g_oc_ref,
                 qt_ref, k_ref, vt_ref, ga_ref, yc_ref, carry_ref):
    tm = x_ref.shape[0]
    quarter_rows = tm // TOKENS_PER_ROW
    tables = []
    q_scale = math.log2(math.e) / math.sqrt(QK_DIM)

    @pl.when(pl.program_id(0) % tiles_per_seq == 0)
    def _():
        carry_ref[0:SUBLANES, :] = jnp.zeros((SUBLANES, CONV_WIDTH), F32)

    def rows_block(r0, nr):
        x = x_ref[r0:r0 + nr, :]
        h = (x * _rms_scale(x, D_MODEL)).astype(BF16)

        def proj(off, width):
            return _dot(h, w_in_ref[:, off:off + width])

        def tile_pairs(r):
            return [(r[:, i:i + LANES], r[:, i + LANES:i + 2 * LANES]) for i in range(0, r.shape[1], 2 * LANES)]

        lat = _dot(h, w_lat_ref[...])
        c_q = lat[:, OFF_CQ:OFF_CKV]
        c_kv = lat[:, OFF_CKV:OFF_KPE]
        kpe = lat[:, OFF_KPE:LAT_PAD]
        cv =jnp.concatenate([c * xin for c, xin in tile_pairs(proj(OFF_CCX, 2 * CONV_WIDTH))], axis=1)
        cqn = (c_q * _rms_scale(c_q, Q_LORA) * g_cq_ref[...]).astype(BF16)
        qf = _dot(cqn, w_uq_ref[...])
        gated_b = jnp.concatenate([b * _silu(z) for b, z in tile_pairs(proj(OFF_CBZ, 2 * CONV_WIDTH))], axis=1)
        ckvn = (c_kv * _rms_scale(c_kv, KV_LORA) * g_ckv_ref[...]).astype(BF16)
        kv = _dot(ckvn, w_ukv_ref[...])
        z_a = proj(OFF_ZA, ATTN_WIDTH)

        if not tables:
            tables.extend(_rope_angles(posc_ref[...], invf_ref[...]))
        assert nr == quarter_rows
        cos_t, sin_t, lo = _rope_quarter(*tables, r0 // quarter_rows)

        for hd in range(N_HEADS):
            qh = qf[:, hd * HEAD_PAD:(hd + 1) * HEAD_PAD]
            qn = qh * (_rms_scale(qh, QK_DIM) * q_scale) * g_q_ref[...]
            qt_ref[hd * HEAD_PAD:hd * HEAD_PAD + LANES, r0:r0 + nr] = qn[:, :LANES].astype(BF16).T
            qt_ref[hd * HEAD_PAD + LANES:(hd + 1) * HEAD_PAD, r0:r0 + nr] = (
                _rope(qn[:, LANES:], cos_t, sin_t, lo).astype(BF16).T)

        ss_kpe = jnp.sum(kpe * kpe, axis=-1, keepdims=True)
        k_rot = _rope(kpe * g_kr_ref[...], cos_t, sin_t, lo)
        for hd in range(N_HEADS):
            kn = kv[:, hd * (NOPE_DIM + V_DIM):hd * (NOPE_DIM + V_DIM) + NOPE_DIM]
            vh = kv[:, hd * (NOPE_DIM + V_DIM) + NOPE_DIM:(hd + 1) * (NOPE_DIM + V_DIM)]
            ss = jnp.sum(kn * kn, axis=-1, keepdims=True) + ss_kpe
            rs = lax.rsqrt(ss * (1.0 / QK_DIM) + RMS_EPS)
            k_ref[r0:r0 + nr, hd * HEAD_PAD:hd * HEAD_PAD + LANES] = (kn * rs * g_kn_ref[...]).astype(BF16)
            k_ref[r0:r0 + nr, hd * HEAD_PAD + LANES:(hd + 1) * HEAD_PAD] = (k_rot * rs).astype(BF16)
            vt_ref[hd * V_ROWS:hd * V_ROWS + V_DIM, r0:r0 + nr] = vh.astype(BF16).T
            vt_ref[hd * V_ROWS + V_DIM:(hd + 1) * V_ROWS, r0:r0 + nr] = jnp.ones((BF16_ROWS, nr), BF16)

        base = SUBLANES + r0
        prev = carry_ref[base - SUBLANES:base, :]
        carry_ref[base + nr - SUBLANES:base + nr, :] = cv[nr - SUBLANES:, :]
        row = lax.broadcasted_iota(jnp.int32, (SUBLANES, CONV_WIDTH), 0)

        def delayed(d):
            sh = pltpu.roll(cv, d, 0)
            head = jnp.where(row < d, pltpu.roll(prev, d, 0), sh[0:SUBLANES])
            return jnp.concatenate([head, sh[SUBLANES:]], axis=0)

        u = (conv_w_ref[2:3, :] * cv + conv_w_ref[1:2, :] * delayed(1) + conv_w_ref[0:1, :] * delayed(2))
        yc = gated_b * u
        yc_ref[r0:r0 + nr, :] = (yc * _rms_scale(yc, CONV_WIDTH) * g_oc_ref[...]).astype(BF16)

        for i, (za_lo, za_hi) in enumerate(tile_pairs(z_a)):
            ga_ref[r0:r0 + nr, i * LANES:(i + 1) * LANES] = _silu(za_lo).astype(BF16)
            ga_ref[r0:r0 + nr, PREP_HALF + i * LANES:PREP_HALF + (i + 1) * LANES] = _silu(za_hi).astype(BF16)

    for blk in range(tm // PROJ_ROWS):
        rows_block(blk * PROJ_ROWS, PROJ_ROWS)
    carry_ref[0:SUBLANES, :] = carry_ref[tm:tm + SUBLANES, :]


def _attn_kernel(qn_ref, kn_ref, kc_ref, qc_ref, diag_n_ref, first_c_ref, last_c_ref,
                 qt_ref, k_ref, vt_ref, o_ref, s_ref, mx_ref, m_ref, acc_ref):
    step = pl.program_id(0)
    tq = qt_ref.shape[1]
    tk = k_ref.shape[0]

    @pl.when(step == 0)
    def _():
        s_ref[...] = jnp.zeros(s_ref.shape, F32)
        mx_ref[...] = jnp.zeros(mx_ref.shape, F32)
        acc_ref[...] = jnp.zeros(acc_ref.shape, F32)

    @pl.when((first_c_ref[step] == 1) | (step == 0))
    def _():
        m_ref[...] = jnp.full(m_ref.shape, NEG_INF, F32)

    chunk = ATTN_Q_CHUNK
    nchunk = tq // chunk

    def body(diag_cur, diag_next):
        if diag_next:
            tri = (lax.broadcasted_iota(jnp.int32, (chunk, chunk), 0)
                   <= lax.broadcasted_iota(jnp.int32, (chunk, chunk), 1))

        def scores(hd, c):
            cols = slice(c * chunk, (c + 1) * chunk)
            rows = chunk * (c + 1) if diag_next else tk
            s = _dot(k_ref[0:rows, hd * HEAD_PAD:(hd + 1) * HEAD_PAD],
                     qt_ref[hd * HEAD_PAD:(hd + 1) * HEAD_PAD, cols])
            if not diag_next:
                s_ref[hd, :, cols] = s
                mx_ref[hd, :, cols] = jnp.max(s, axis=0, keepdims=True)
                return
            last = jnp.where(tri, s[rows - chunk:rows], NEG_INF)
            s_ref[hd, rows - chunk:rows, cols] = last
            mx = jnp.max(last, axis=0, keepdims=True)
            if rows > chunk:
                s_ref[hd, 0:rows - chunk, cols] = s[0:rows - chunk]
                mx = jnp.maximum(mx, jnp.max(s[0:rows - chunk], axis=0, keepdims=True))
            mx_ref[hd, :, cols] = mx

        def softmax_pv(hd, c):
            cols = slice(c * chunk, (c + 1) * chunk)
            rows = chunk * (c + 1) if diag_cur else tk
            m_prev = m_ref[hd, :, cols]
            m_new = jnp.maximum(m_prev, mx_ref[hd, :, cols])
            alpha = jnp.exp2(m_prev - m_new)
            p = jnp.exp2((s_ref[hd, 0:rows, cols] - m_new).astype(BF16))
            pv = _dot(vt_ref[hd * V_ROWS:(hd + 1) * V_ROWS, 0:rows], p)
            acc_ref[hd, :, cols] = alpha * acc_ref[hd, :, cols] + pv
            m_ref[hd, :, cols] = m_new

        for hd in range(N_HEADS):
            for c in range(nchunk):
                softmax_pv(hd, c)
                scores(hd, c)

    diag_cur = last_c_ref[step] == 1
    diag_next = diag_n_ref[step] == 1
    for cur_flag in (False, True):
        for next_flag in (False, True):
            @pl.when((diag_cur == cur_flag) & (diag_next == next_flag))
            def _(cur_flag=cur_flag, next_flag=next_flag):
                body(cur_flag, next_flag)


    @pl.when(last_c_ref[step] == 1)
    def _():
        for hd in range(N_HEADS):
            acc = acc_ref[hd]
            o_ref[:, hd * V_DIM:(hd + 1) * V_DIM] = (
                acc[:V_DIM] / acc[V_DIM:V_DIM + 1]).T.astype(o_ref.dtype)


def _out_kernel(o_ref, ga_ref, yc_ref, x_ref, p_ref, g_oa_ref, w_o_ref, g_pl_ref, w_plg_ref, w_pl_ref,
                out_ref):
    tm = x_ref.shape[0]

    def residual(r0):
        rows = slice(r0, r0 + OUT_ROWS)
        ya = o_ref[rows, :].astype(F32) * ga_ref[rows, :].astype(F32)
        ya_n = (ya * _rms_scale(ya, ATTN_WIDTH) * g_oa_ref[...]).astype(BF16)
        y = jnp.concatenate([ya_n, yc_ref[rows, :]], axis=1)
        x1 = x_ref[rows, :] + _dot(y, w_o_ref[...])
        out_ref[rows, :] = x1
        hn = (x1 * _rms_scale(x1, D_MODEL) * g_pl_ref[...]).astype(BF16)
        return r0, hn

    def gated(r0, hn):
        rows = slice(r0, r0 + OUT_ROWS)
        gate = 1.0 / (1.0 + jnp.exp(-_dot(hn, w_plg_ref[...])))
        ple = _dot(p_ref[rows, :].astype(BF16), w_pl_ref[...])
        out_ref[rows, :] = out_ref[rows, :] + gate * ple

    prev = None
    for r0 in range(0, tm, OUT_ROWS):
        cur = residual(r0)
        if prev is not None:
            gated(*prev)
        prev = cur
    gated(*prev)


def _row_spec(tile, width):
    return pl.BlockSpec((tile, width), lambda i: (i, 0))


def _const_spec(shape):
    return pl.BlockSpec(shape, lambda i: (0,) * len(shape))


def _triangle_schedule(batch, n_blk):
    pairs = [(b * n_blk + i, b * n_blk + j, j == 0, j == i)
             for b in range(batch) for i in range(n_blk) for j in range(i + 1)]
    nxt = pairs + [pairs[-1]]
    cur = [pairs[0]] + pairs
    cols = ([p[0] for p in nxt], [p[1] for p in nxt], [p[1] for p in cur], [p[0] for p in cur],
            [int(p[3]) for p in nxt],
            [0] + [int(p[2]) for p in pairs], [0] + [int(p[3]) for p in pairs])
    return tuple(jnp.asarray(np.asarray(c, np.int32)) for c in cols)


def _prep_in_kernel(rows_a_ref, rows_b_ref, wt_lat_ref, wt_a_ref, wt_b_ref, g_ref, lat_ref, wide_ref):
    g = g_ref[...]

    @pl.when(pl.program_id(0) == 0)
    def _():
        pad = jnp.zeros((LAT_PAD - LAT_SRC, D_MODEL), F32)
        lat_ref[...] = jnp.concatenate([wt_lat_ref[...] * g, pad], axis=0).T.astype(BF16)

    a = (wt_a_ref[...] * g).T.astype(BF16)
    b = (wt_b_ref[...] * g).T.astype(BF16)
    for i in range(PREP_HALF // LANES):
        wide_ref[:, 2 * i * LANES:(2 * i + 1) * LANES] = a[:, i * LANES:(i + 1) * LANES]
        wide_ref[:, (2 * i + 1) * LANES:(2 * i + 2) * LANES] = b[:, i * LANES:(i + 1) * LANES]


def _prepare_w_in(w_in, g_in):
    def rows_at(nrows, start):
        return pl.BlockSpec((pl.Element(nrows), pl.Element(D_MODEL)), start)

    rows_a = jnp.asarray(np.asarray([LAT_SRC + a for a, _ in PREP_PAIRS], np.int32))
    rows_b = jnp.asarray(np.asarray([LAT_SRC + b for _, b in PREP_PAIRS], np.int32))
    return pl.pallas_call(
        _prep_in_kernel,
        grid_spec=pltpu.PrefetchScalarGridSpec(
            num_scalar_prefetch=2,
            grid=(len(PREP_PAIRS),),
            in_specs=[rows_at(LAT_SRC, lambda j, ra, rb: (0, 0)),
                      rows_at(PREP_HALF, lambda j, ra, rb: (pl.multiple_of(ra[j], SUBLANES), 0)),
                      rows_at(PREP_HALF, lambda j, ra, rb: (pl.multiple_of(rb[j], SUBLANES), 0)),
                      pl.BlockSpec((1, D_MODEL), lambda j, ra, rb: (0, 0))],
            out_specs=[pl.BlockSpec((D_MODEL, LAT_PAD), lambda j, ra, rb: (0, 0)),
                       pl.BlockSpec((D_MODEL, PREP_COLS), lambda j, ra, rb: (0, j))],
        ),
        out_shape=[jax.ShapeDtypeStruct((D_MODEL, LAT_PAD), BF16),
                   jax.ShapeDtypeStruct((D_MODEL, WIDE), BF16)],
        compiler_params=pltpu.CompilerParams(dimension_semantics=("arbitrary",),
                                             vmem_limit_bytes=VMEM_LIMIT),
        name="mla_conv_w_in_prep",
    )(rows_a, rows_b, w_in.T, w_in.T, w_in.T, g_in.reshape(1, -1))


def _layer(x2, p2, posc, invf, batch, seq, g_in, w_in, g_cq, w_uq, g_ckv, w_ukv, g_q, g_k,
           conv_w, g_oa, g_oc, w_o, w_pl, w_plg, g_pl):
    tokens = batch * seq
    tm = PROJ_TILE
    assert seq % tm == 0 and seq % ATTN_TILE == 0

    w_lat, w_wide = _prepare_w_in(w_in, g_in)
    w_uq_p = jnp.pad(w_uq.reshape(Q_LORA, N_HEADS, QK_DIM),
                     ((0, 0), (0, 0), (0, HEAD_PAD - QK_DIM))).reshape(Q_LORA, N_HEADS * HEAD_PAD).astype(BF16)
    g_q_p = jnp.pad(g_q, (0, HEAD_PAD - QK_DIM)).reshape(1, HEAD_PAD)
    g_kn = g_k[:NOPE_DIM].reshape(1, NOPE_DIM)
    g_kr = jnp.pad(g_k[NOPE_DIM:], (0, KPE_PAD - ROPE_DIM)).reshape(1, KPE_PAD)

    cparams = pltpu.CompilerParams(dimension_semantics=("arbitrary",), vmem_limit_bytes=VMEM_LIMIT)

    def col_spec(rows, tile):
        return pl.BlockSpec((rows, tile), lambda i: (0, i))

    qt, k, vt, ga, yc = pl.pallas_call(
        functools.partial(_proj_kernel, seq // tm),
        grid=(tokens // tm,),
        in_specs=[
            _row_spec(tm, D_MODEL), _row_spec(tm // TOKENS_PER_ROW, LANES), _const_spec((1, LANES)),
            _const_spec((D_MODEL, LAT_PAD)), _const_spec((D_MODEL, WIDE)),
            _const_spec((1, Q_LORA)), _const_spec((Q_LORA, N_HEADS * HEAD_PAD)),
            _const_spec((1, KV_LORA)), _const_spec((KV_LORA, N_HEADS * (NOPE_DIM + V_DIM))),
            _const_spec((1, HEAD_PAD)), _const_spec((1, NOPE_DIM)), _const_spec((1, KPE_PAD)),
            _const_spec((CONV_K, CONV_WIDTH)), _const_spec((1, CONV_WIDTH)),
        ],
        out_specs=[
            col_spec(N_HEADS * HEAD_PAD, tm), _row_spec(tm, N_HEADS * HEAD_PAD),
            col_spec(N_HEADS * V_ROWS, tm), _row_spec(tm, ATTN_WIDTH), _row_spec(tm, CONV_WIDTH),
        ],
        out_shape=[
            jax.ShapeDtypeStruct((N_HEADS * HEAD_PAD, tokens), BF16),
            jax.ShapeDtypeStruct((tokens, N_HEADS * HEAD_PAD), BF16),
            jax.ShapeDtypeStruct((N_HEADS * V_ROWS, tokens), BF16),
            jax.ShapeDtypeStruct((tokens, ATTN_WIDTH), BF16),
            jax.ShapeDtypeStruct((tokens, CONV_WIDTH), BF16),
        ],
        scratch_shapes=[pltpu.VMEM((tm + SUBLANES, CONV_WIDTH), F32)],
        compiler_params=cparams,
        name="mla_conv_proj",
    )(x2, posc, invf, w_lat, w_wide, g_cq.reshape(1, -1), w_uq_p,
      g_ckv.reshape(1, -1), w_ukv.astype(BF16), g_q_p, g_kn, g_kr, conv_w, g_oc.reshape(1, -1))

    ta = ATTN_TILE
    sched = _triangle_schedule(batch, seq // ta)
    o = pl.pallas_call(
        _attn_kernel,
        grid_spec=pltpu.PrefetchScalarGridSpec(
            num_scalar_prefetch=len(sched),
            grid=(int(sched[0].shape[0]),),
            in_specs=[
                pl.BlockSpec((N_HEADS * HEAD_PAD, ta), lambda s, qn, kn, kc, qc, *_: (0, qn[s])),
                pl.BlockSpec((ta, N_HEADS * HEAD_PAD), lambda s, qn, kn, kc, qc, *_: (kn[s], 0)),
                pl.BlockSpec((N_HEADS * V_ROWS, ta), lambda s, qn, kn, kc, qc, *_: (0, kc[s])),
            ],
            out_specs=pl.BlockSpec((ta, ATTN_WIDTH), lambda s, qn, kn, kc, qc, *_: (qc[s], 0)),
            scratch_shapes=[pltpu.VMEM((N_HEADS, ta, ta), F32),
                            pltpu.VMEM((N_HEADS, 1, ta), F32),
                            pltpu.VMEM((N_HEADS, 1, ta), F32),
                            pltpu.VMEM((N_HEADS, V_ROWS, ta), F32)],
        ),
        out_shape=jax.ShapeDtypeStruct((tokens, ATTN_WIDTH), BF16),
        compiler_params=cparams,
        name="mla_flash_attn",
    )(*sched, qt, k, vt)

    return pl.pallas_call(
        _out_kernel,
        grid=(tokens // tm,),
        in_specs=[
            _row_spec(tm, ATTN_WIDTH), _row_spec(tm, ATTN_WIDTH), _row_spec(tm, CONV_WIDTH),
            _row_spec(tm, D_MODEL), _row_spec(tm, PLE_DIM),
            _const_spec((1, ATTN_WIDTH)), _const_spec((D_MODEL, D_MODEL)), _const_spec((1, D_MODEL)),
            _const_spec((D_MODEL, D_MODEL)), _const_spec((PLE_DIM, D_MODEL)),
        ],
        out_specs=_row_spec(tm, D_MODEL),
        out_shape=jax.ShapeDtypeStruct((tokens, D_MODEL), F32),
        compiler_params=cparams,
        name="mla_conv_out",
    )(o, ga, yc, x2, p2, g_oa.reshape(1, -1), w_o.astype(BF16),
      g_pl.reshape(1, -1), w_plg.astype(BF16), w_pl.astype(BF16))


def kernel(x, p, positions, g_in, w_in, g_cq, w_uq, g_ckv, w_ukv, g_q, g_k, conv_w, g_oa, g_oc,
           w_o, w_pl, w_plg, g_pl):
    batch, seq, d_model = x.shape
    depth = p.shape[0]
    tokens = batch * seq
    rows = PROJ_TILE // TOKENS_PER_ROW
    posc = jnp.repeat(positions.astype(F32).reshape(tokens // PROJ_TILE, TOKENS_PER_ROW, rows)
                      .transpose(0, 2, 1), HALF_ROPE, axis=-1).reshape(tokens // TOKENS_PER_ROW, LANES)
    inv_freq = 1.0 / (ROPE_THETA ** (jnp.arange(0, ROPE_DIM, 2, dtype=F32) / ROPE_DIM))
    invf = jnp.tile(inv_freq, LANES // HALF_ROPE).reshape(1, LANES)
    h = x.reshape(tokens, d_model)
    for i in range(depth):
        h = _layer(h, p[i].reshape(tokens, PLE_DIM), posc, invf, batch, seq,
                   g_in[i], w_in[i], g_cq[i], w_uq[i], g_ckv[i], w_ukv[i], g_q[i], g_k[i],
                   conv_w[i], g_oa[i], g_oc[i], w_o[i], w_pl[i], w_plg[i], g_pl[i])
    return h.reshape(batch, seq, d_model).astype(x.dtype)
```

```python
import functools
import math

import jax
import jax.numpy as jnp
import numpy as np
from jax import lax
from jax.experimental import pallas as pl
from jax.experimental.pallas import tpu as pltpu

D_MODEL = 1024
PLE_DIM = 256
N_HEADS = 4
NOPE_DIM = 128
ROPE_DIM = 64
HALF_ROPE = ROPE_DIM // 2
V_DIM = 128
QK_DIM = NOPE_DIM + ROPE_DIM
Q_LORA = 256
KV_LORA = 128
ATTN_WIDTH = N_HEADS * V_DIM
CONV_WIDTH = D_MODEL - ATTN_WIDTH
CONV_K = 3
ROPE_THETA = 10000.0
RMS_EPS = 1e-6
NEG_INF = -1e30

LANES = 128
SUBLANES = 8
HEAD_PAD = 2 * LANES
KPE_PAD = LANES
BF16_ROWS = 16
V_ROWS = V_DIM + BF16_ROWS
TOKENS_PER_ROW = LANES // HALF_ROPE
OFF_CQ = 0
OFF_CKV = OFF_CQ + Q_LORA
OFF_KPE = OFF_CKV + KV_LORA
LAT_SRC = OFF_KPE + ROPE_DIM
LAT_PAD = OFF_KPE + KPE_PAD
SRC_ZA = 0
SRC_CB = SRC_ZA + ATTN_WIDTH
SRC_CC = SRC_CB + CONV_WIDTH
SRC_CX = SRC_CC + CONV_WIDTH
SRC_ZC = SRC_CX + CONV_WIDTH
WIDE = SRC_ZC + CONV_WIDTH
PREP_COLS = 512
PREP_HALF = PREP_COLS // 2
PREP_PAIRS = ((SRC_ZA, SRC_ZA + PREP_HALF), (SRC_CC, SRC_CX), (SRC_CC + PREP_HALF, SRC_CX + PREP_HALF),
              (SRC_CB, SRC_ZC), (SRC_CB + PREP_HALF, SRC_ZC + PREP_HALF))
OFF_ZA = 0
OFF_CCX = OFF_ZA + PREP_COLS
OFF_CBZ = OFF_CCX + 2 * PREP_COLS

PROJ_TILE = 1024
PROJ_ROWS = PROJ_TILE // TOKENS_PER_ROW
OUT_ROWS = 256
ATTN_TILE = 1024
ATTN_Q_CHUNK = 256
VMEM_LIMIT = 48 * 1024 * 1024

BF16 = jnp.bfloat16
F32 = jnp.float32


def _rms_scale(v, width):
    return lax.rsqrt(jnp.sum(v * v, axis=-1, keepdims=True) * (1.0 / width) + RMS_EPS)


def _silu(z):
    h = 0.5 * z
    return h + h * jnp.tanh(h)


def _dot(a, b):
    return jnp.dot(a, b, preferred_element_type=F32)


def _rope_angles(posc, invf):
    ang = posc * invf
    return jnp.cos(ang), jnp.sin(ang)


def _rope_quarter(cosc, sinc, a):
    lane = lax.broadcasted_iota(jnp.int32, (1, LANES), 1)
    lo = lane < HALF_ROPE
    mid = (lane >= HALF_ROPE) & (lane < ROPE_DIM)
    c = pltpu.roll(cosc, LANES - HALF_ROPE * a, 1) if a else cosc
    s = pltpu.roll(sinc, LANES - HALF_ROPE * a, 1) if a else sinc
    return (jnp.where(lo, c, jnp.where(mid, pltpu.roll(c, HALF_ROPE, 1), 0.0)),
            jnp.where(lo, -s, jnp.where(mid, pltpu.roll(s, HALF_ROPE, 1), 0.0)), lo)


def _rope(t, c, s, lo):
    swapped = jnp.where(lo, pltpu.roll(t, LANES - HALF_ROPE, 1), pltpu.roll(t, HALF_ROPE, 1))
    return t * c + swapped * s


def _proj_kernel(tiles_per_seq,
                 x_ref, posc_ref, invf_ref, w_lat_ref, w_in_ref, g_cq_ref, w_uq_ref,
                 g_ckv_ref, w_ukv_ref, g_q_ref, g_kn_ref, g_kr_ref, conv_w_ref, g_oc_ref,
                 qt_ref, k_ref, vt_ref, ga_ref, yc_ref, carry_ref):
    tm = x_ref.shape[0]
    quarter_rows = tm // TOKENS_PER_ROW
    tables = []
    q_scale = math.log2(math.e) / math.sqrt(QK_DIM)

    @pl.when(pl.program_id(0) % tiles_per_seq == 0)
    def _():
        carry_ref[0:SUBLANES, :] = jnp.zeros((SUBLANES, CONV_WIDTH), F32)

    def rows_block(r0, nr):
        x = x_ref[r0:r0 + nr, :]
        h = (x * _rms_scale(x, D_MODEL)).astype(BF16)

        def proj(off, width):
            return _dot(h, w_in_ref[:, off:off + width])

        def tile_pairs(r):
            return [(r[:, i:i + LANES], r[:, i + LANES:i + 2 * LANES]) for i in range(0, r.shape[1], 2 * LANES)]

        lat = _dot(h, w_lat_ref[...])
        c_q = lat[:, OFF_CQ:OFF_CKV]
        c_kv = lat[:, OFF_CKV:OFF_KPE]
        kpe = lat[:, OFF_KPE:LAT_PAD]
        cv =jnp.concatenate([c * xin for c, xin in tile_pairs(proj(OFF_CCX, 2 * CONV_WIDTH))], axis=1)
        cqn = (c_q * _rms_scale(c_q, Q_LORA) * g_cq_ref[...]).astype(BF16)
        qf = _dot(cqn, w_uq_ref[...])
        gated_b = jnp.concatenate([b * _silu(z) for b, z in tile_pairs(proj(OFF_CBZ, 2 * CONV_WIDTH))], axis=1)
        ckvn = (c_kv * _rms_scale(c_kv, KV_LORA) * g_ckv_ref[...]).astype(BF16)
        kv = _dot(ckvn, w_ukv_ref[...])
        z_a = proj(OFF_ZA, ATTN_WIDTH)

        if not tables:
            tables.extend(_rope_angles(posc_ref[...], invf_ref[...]))
        assert nr == quarter_rows
        cos_t, sin_t, lo = _rope_quarter(*tables, r0 // quarter_rows)

        for hd in range(N_HEADS):
            qh = qf[:, hd * HEAD_PAD:(hd + 1) * HEAD_PAD]
            qn = qh * (_rms_scale(qh, QK_DIM) * q_scale) * g_q_ref[...]
            qt_ref[hd * HEAD_PAD:hd * HEAD_PAD + LANES, r0:r0 + nr] = qn[:, :LANES].astype(BF16).T
            qt_ref[hd * HEAD_PAD + LANES:(hd + 1) * HEAD_PAD, r0:r0 + nr] = (
                _rope(qn[:, LANES:], cos_t, sin_t, lo).astype(BF16).T)

        ss_kpe = jnp.sum(kpe * kpe, axis=-1, keepdims=True)
        k_rot = _rope(kpe * g_kr_ref[...], cos_t, sin_t, lo)
        for hd in range(N_HEADS):
            kn = kv[:, hd * (NOPE_DIM + V_DIM):hd * (NOPE_DIM + V_DIM) + NOPE_DIM]
            vh = kv[:, hd * (NOPE_DIM + V_DIM) + NOPE_DIM:(hd + 1) * (NOPE_DIM + V_DIM)]
            ss = jnp.sum(kn * kn, axis=-1, keepdims=True) + ss_kpe
            rs = lax.rsqrt(ss * (1.0 / QK_DIM) + RMS_EPS)
            k_ref[r0:r0 + nr, hd * HEAD_PAD:hd * HEAD_PAD + LANES] = (kn * rs * g_kn_ref[...]).astype(BF16)
            k_ref[r0:r0 + nr, hd * HEAD_PAD + LANES:(hd + 1) * HEAD_PAD] = (k_rot * rs).astype(BF16)
            vt_ref[hd * V_ROWS:hd * V_ROWS + V_DIM, r0:r0 + nr] = vh.astype(BF16).T
            vt_ref[hd * V_ROWS + V_DIM:(hd + 1) * V_ROWS, r0:r0 + nr] = jnp.ones((BF16_ROWS, nr), BF16)

        base = SUBLANES + r0
        prev = carry_ref[base - SUBLANES:base, :]
        carry_ref[base + nr - SUBLANES:base + nr, :] = cv[nr - SUBLANES:, :]
        row = lax.broadcasted_iota(jnp.int32, (SUBLANES, CONV_WIDTH), 0)

        def delayed(d):
            sh = pltpu.roll(cv, d, 0)
            head = jnp.where(row < d, pltpu.roll(prev, d, 0), sh[0:SUBLANES])
            return jnp.concatenate([head, sh[SUBLANES:]], axis=0)

        u = (conv_w_ref[2:3, :] * cv + conv_w_ref[1:2, :] * delayed(1) + conv_w_ref[0:1, :] * delayed(2))
        yc = gated_b * u
        yc_ref[r0:r0 + nr, :] = (yc * _rms_scale(yc, CONV_WIDTH) * g_oc_ref[...]).astype(BF16)

        for i, (za_lo, za_hi) in enumerate(tile_pairs(z_a)):
            ga_ref[r0:r0 + nr, i * LANES:(i + 1) * LANES] = _silu(za_lo).astype(BF16)
            ga_ref[r0:r0 + nr, PREP_HALF + i * LANES:PREP_HALF + (i + 1) * LANES] = _silu(za_hi).astype(BF16)

    for blk in range(tm // PROJ_ROWS):
        rows_block(blk * PROJ_ROWS, PROJ_ROWS)
    carry_ref[0:SUBLANES, :] = carry_ref[tm:tm + SUBLANES, :]


def _attn_kernel(qn_ref, kn_ref, kc_ref, qc_ref, diag_n_ref, first_c_ref, last_c_ref,
                 qt_ref, k_ref, vt_ref, o_ref, s_ref, mx_ref, m_ref, acc_ref):
    step = pl.program_id(0)
    tq = qt_ref.shape[1]
    tk = k_ref.shape[0]

    @pl.when(step == 0)
    def _():
        acc_ref[...] = jnp.zeros(acc_ref.shape, F32)

    @pl.when((first_c_ref[step] == 1) | (step == 0))
    def _():
        m_ref[...] = jnp.full(m_ref.shape, NEG_INF, F32)

    chunk = ATTN_Q_CHUNK
    nchunk = tq // chunk

    def body(diag_cur, diag_next, consume=True, produce=True):
        if diag_next:
            tri = (lax.broadcasted_iota(jnp.int32, (chunk, chunk), 0)
                   <= lax.broadcasted_iota(jnp.int32, (chunk, chunk), 1))

        def scores(hd, c):
            cols = slice(c * chunk, (c + 1) * chunk)
            rows = chunk * (c + 1) if diag_next else tk
            s = _dot(k_ref[0:rows, hd * HEAD_PAD:(hd + 1) * HEAD_PAD],
                     qt_ref[hd * HEAD_PAD:(hd + 1) * HEAD_PAD, cols])
            if not diag_next:
                s_ref[hd, :, cols] = s
                mx_ref[hd, :, cols] = jnp.max(s, axis=0, keepdims=True)
                return
            last = jnp.where(tri, s[rows - chunk:rows], NEG_INF)
            s_ref[hd, rows - chunk:rows, cols] = last
            mx = jnp.max(last, axis=0, keepdims=True)
            if rows > chunk:
                s_ref[hd, 0:rows - chunk, cols] = s[0:rows - chunk]
                mx = jnp.maximum(mx, jnp.max(s[0:rows - chunk], axis=0, keepdims=True))
            mx_ref[hd, :, cols] = mx

        def softmax_pv(hd, c):
            cols = slice(c * chunk, (c + 1) * chunk)
            rows = chunk * (c + 1) if diag_cur else tk
            m_prev = m_ref[hd, :, cols]
            m_new = jnp.maximum(m_prev, mx_ref[hd, :, cols])
            alpha = jnp.exp2(m_prev - m_new)
            p = jnp.exp2((s_ref[hd, 0:rows, cols] - m_new).astype(BF16))
            pv = _dot(vt_ref[hd * V_ROWS:(hd + 1) * V_ROWS, 0:rows], p)
            acc_ref[hd, :, cols] = alpha * acc_ref[hd, :, cols] + pv
            m_ref[hd, :, cols] = m_new

        for hd in range(N_HEADS):
            for c in range(nchunk):
                if consume:
                    softmax_pv(hd, c)
                if produce:
                    scores(hd, c)

    is_first = step == 0
    is_last = step == pl.num_programs(0) - 1
    inner = jnp.logical_not(is_first | is_last)
    diag_cur = last_c_ref[step] == 1
    diag_next = diag_n_ref[step] == 1
    for cur_flag in (False, True):
        for next_flag in (False, True):
            @pl.when(inner & (diag_cur == cur_flag) & (diag_next == next_flag))
            def _(cur_flag=cur_flag, next_flag=next_flag):
                body(cur_flag, next_flag)

    @pl.when(is_first)
    def _():
        body(False, True, consume=False)

    @pl.when(is_last)
    def _():
        body(True, False, produce=False)

    @pl.when(last_c_ref[step] == 1)
    def _():
        for hd in range(N_HEADS):
            acc = acc_ref[hd]
            o_ref[:, hd * V_DIM:(hd + 1) * V_DIM] = (
                acc[:V_DIM] / acc[V_DIM:V_DIM + 1]).T.astype(o_ref.dtype)


def _out_kernel(o_ref, ga_ref, yc_ref, x_ref, p_ref, g_oa_ref, w_o_ref, g_pl_ref, w_plg_ref, w_pl_ref,
                out_ref):
    tm = x_ref.shape[0]

    def residual(r0):
        rows = slice(r0, r0 + OUT_ROWS)
        ya = o_ref[rows, :].astype(F32) * ga_ref[rows, :].astype(F32)
        ya_n = (ya * _rms_scale(ya, ATTN_WIDTH) * g_oa_ref[...]).astype(BF16)
        y = jnp.concatenate([ya_n, yc_ref[rows, :]], axis=1)
        x1 = x_ref[rows, :] + _dot(y, w_o_ref[...])
        out_ref[rows, :] = x1
        hn = (x1 * _rms_scale(x1, D_MODEL) * g_pl_ref[...]).astype(BF16)
        return r0, hn

    def gated(r0, hn):
        rows = slice(r0, r0 + OUT_ROWS)
        gate = 1.0 / (1.0 + jnp.exp(-_dot(hn, w_plg_ref[...])))
        ple = _dot(p_ref[rows, :].astype(BF16), w_pl_ref[...])
        out_ref[rows, :] = out_ref[rows, :] + gate * ple

    prev = None
    for r0 in range(0, tm, OUT_ROWS):
        cur = residual(r0)
        if prev is not None:
            gated(*prev)
        prev = cur
    gated(*prev)


def _row_spec(tile, width):
    return pl.BlockSpec((tile, width), lambda i: (i, 0))


def _const_spec(shape):
    return pl.BlockSpec(shape, lambda i: (0,) * len(shape))


def _triangle_schedule(batch, n_blk):
    pairs = [(b * n_blk + i, b * n_blk + j, j == 0, j == i)
             for b in range(batch) for i in range(n_blk) for j in range(i + 1)]
    nxt = pairs + [pairs[-1]]
    cur = [pairs[0]] + pairs
    cols = ([p[0] for p in nxt], [p[1] for p in nxt], [p[1] for p in cur], [p[0] for p in cur],
            [int(p[3]) for p in nxt],
            [0] + [int(p[2]) for p in pairs], [0] + [int(p[3]) for p in pairs])
    return tuple(jnp.asarray(np.asarray(c, np.int32)) for c in cols)


def _prep_in_kernel(rows_a_ref, rows_b_ref, wt_lat_ref, wt_a_ref, wt_b_ref, g_ref, lat_ref, wide_ref):
    g = g_ref[...]

    @pl.when(pl.program_id(0) == 0)
    def _():
        pad = jnp.zeros((LAT_PAD - LAT_SRC, D_MODEL), F32)
        lat_ref[...] = jnp.concatenate([wt_lat_ref[...] * g, pad], axis=0).T.astype(BF16)

    a = (wt_a_ref[...] * g).T.astype(BF16)
    b = (wt_b_ref[...] * g).T.astype(BF16)
    for i in range(PREP_HALF // LANES):
        wide_ref[:, 2 * i * LANES:(2 * i + 1) * LANES] = a[:, i * LANES:(i + 1) * LANES]
        wide_ref[:, (2 * i + 1) * LANES:(2 * i + 2) * LANES] = b[:, i * LANES:(i + 1) * LANES]


def _prepare_w_in(w_in, g_in):
    def rows_at(nrows, start):
        return pl.BlockSpec((pl.Element(nrows), pl.Element(D_MODEL)), start)

    rows_a = jnp.asarray(np.asarray([LAT_SRC + a for a, _ in PREP_PAIRS], np.int32))
    rows_b = jnp.asarray(np.asarray([LAT_SRC + b for _, b in PREP_PAIRS], np.int32))
    return pl.pallas_call(
        _prep_in_kernel,
        grid_spec=pltpu.PrefetchScalarGridSpec(
            num_scalar_prefetch=2,
            grid=(len(PREP_PAIRS),),
            in_specs=[rows_at(LAT_SRC, lambda j, ra, rb: (0, 0)),
                      rows_at(PREP_HALF, lambda j, ra, rb: (pl.multiple_of(ra[j], SUBLANES), 0)),
                      rows_at(PREP_HALF, lambda j, ra, rb: (pl.multiple_of(rb[j], SUBLANES), 0)),
                      pl.BlockSpec((1, D_MODEL), lambda j, ra, rb: (0, 0))],
            out_specs=[pl.BlockSpec((D_MODEL, LAT_PAD), lambda j, ra, rb: (0, 0)),
                       pl.BlockSpec((D_MODEL, PREP_COLS), lambda j, ra, rb: (0, j))],
        ),
        out_shape=[jax.ShapeDtypeStruct((D_MODEL, LAT_PAD), BF16),
                   jax.ShapeDtypeStruct((D_MODEL, WIDE), BF16)],
        compiler_params=pltpu.CompilerParams(dimension_semantics=("arbitrary",),
                                             vmem_limit_bytes=VMEM_LIMIT),
        name="mla_conv_w_in_prep",
    )(rows_a, rows_b, w_in.T, w_in.T, w_in.T, g_in.reshape(1, -1))


def _layer(x2, p2, posc, invf, batch, seq, g_in, w_in, g_cq, w_uq, g_ckv, w_ukv, g_q, g_k,
           conv_w, g_oa, g_oc, w_o, w_pl, w_plg, g_pl):
    tokens = batch * seq
    tm = PROJ_TILE
    assert seq % tm == 0 and seq % ATTN_TILE == 0

    w_lat, w_wide = _prepare_w_in(w_in, g_in)
    w_uq_p = jnp.pad(w_uq.reshape(Q_LORA, N_HEADS, QK_DIM),
                     ((0, 0), (0, 0), (0, HEAD_PAD - QK_DIM))).reshape(Q_LORA, N_HEADS * HEAD_PAD).astype(BF16)
    g_q_p = jnp.pad(g_q, (0, HEAD_PAD - QK_DIM)).reshape(1, HEAD_PAD)
    g_kn = g_k[:NOPE_DIM].reshape(1, NOPE_DIM)
    g_kr = jnp.pad(g_k[NOPE_DIM:], (0, KPE_PAD - ROPE_DIM)).reshape(1, KPE_PAD)

    cparams = pltpu.CompilerParams(dimension_semantics=("arbitrary",), vmem_limit_bytes=VMEM_LIMIT)

    def col_spec(rows, tile):
        return pl.BlockSpec((rows, tile), lambda i: (0, i))

    qt, k, vt, ga, yc = pl.pallas_call(
        functools.partial(_proj_kernel, seq // tm),
        grid=(tokens // tm,),
        in_specs=[
            _row_spec(tm, D_MODEL), _row_spec(tm // TOKENS_PER_ROW, LANES), _const_spec((1, LANES)),
            _const_spec((D_MODEL, LAT_PAD)), _const_spec((D_MODEL, WIDE)),
            _const_spec((1, Q_LORA)), _const_spec((Q_LORA, N_HEADS * HEAD_PAD)),
            _const_spec((1, KV_LORA)), _const_spec((KV_LORA, N_HEADS * (NOPE_DIM + V_DIM))),
            _const_spec((1, HEAD_PAD)), _const_spec((1, NOPE_DIM)), _const_spec((1, KPE_PAD)),
            _const_spec((CONV_K, CONV_WIDTH)), _const_spec((1, CONV_WIDTH)),
        ],
        out_specs=[
            col_spec(N_HEADS * HEAD_PAD, tm), _row_spec(tm, N_HEADS * HEAD_PAD),
            col_spec(N_HEADS * V_ROWS, tm), _row_spec(tm, ATTN_WIDTH), _row_spec(tm, CONV_WIDTH),
        ],
        out_shape=[
            jax.ShapeDtypeStruct((N_HEADS * HEAD_PAD, tokens), BF16),
            jax.ShapeDtypeStruct((tokens, N_HEADS * HEAD_PAD), BF16),
            jax.ShapeDtypeStruct((N_HEADS * V_ROWS, tokens), BF16),
            jax.ShapeDtypeStruct((tokens, ATTN_WIDTH), BF16),
            jax.ShapeDtypeStruct((tokens, CONV_WIDTH), BF16),
        ],
        scratch_shapes=[pltpu.VMEM((tm + SUBLANES, CONV_WIDTH), F32)],
        compiler_params=cparams,
        name="mla_conv_proj",
    )(x2, posc, invf, w_lat, w_wide, g_cq.reshape(1, -1), w_uq_p,
      g_ckv.reshape(1, -1), w_ukv.astype(BF16), g_q_p, g_kn, g_kr, conv_w, g_oc.reshape(1, -1))

    ta = ATTN_TILE
    sched = _triangle_schedule(batch, seq // ta)
    o = pl.pallas_call(
        _attn_kernel,
        grid_spec=pltpu.PrefetchScalarGridSpec(
            num_scalar_prefetch=len(sched),
            grid=(int(sched[0].shape[0]),),
            in_specs=[
                pl.BlockSpec((N_HEADS * HEAD_PAD, ta), lambda s, qn, kn, kc, qc, *_: (0, qn[s])),
                pl.BlockSpec((ta, N_HEADS * HEAD_PAD), lambda s, qn, kn, kc, qc, *_: (kn[s], 0)),
                pl.BlockSpec((N_HEADS * V_ROWS, ta), lambda s, qn, kn, kc, qc, *_: (0, kc[s])),
            ],
            out_specs=pl.BlockSpec((ta, ATTN_WIDTH), lambda s, qn, kn, kc, qc, *_: (qc[s], 0)),
            scratch_shapes=[pltpu.VMEM((N_HEADS, ta, ta), F32),
                            pltpu.VMEM((N_HEADS, 1, ta), F32),
                            pltpu.VMEM((N_HEADS, 1, ta), F32),
                            pltpu.VMEM((N_HEADS, V_ROWS, ta), F32)],
        ),
        out_shape=jax.ShapeDtypeStruct((tokens, ATTN_WIDTH), BF16),
        compiler_params=cparams,
        name="mla_flash_attn",
    )(*sched, qt, k, vt)

    return pl.pallas_call(
        _out_kernel,
        grid=(tokens // tm,),
        in_specs=[
            _row_spec(tm, ATTN_WIDTH), _row_spec(tm, ATTN_WIDTH), _row_spec(tm, CONV_WIDTH),
            _row_spec(tm, D_MODEL), _row_spec(tm, PLE_DIM),
            _const_spec((1, ATTN_WIDTH)), _const_spec((D_MODEL, D_MODEL)), _const_spec((1, D_MODEL)),
            _const_spec((D_MODEL, D_MODEL)), _const_spec((PLE_DIM, D_MODEL)),
        ],
        out_specs=_row_spec(tm, D_MODEL),
        out_shape=jax.ShapeDtypeStruct((tokens, D_MODEL), F32),
        compiler_params=cparams,
        name="mla_conv_out",
    )(o, ga, yc, x2, p2, g_oa.reshape(1, -1), w_o.astype(BF16),
      g_pl.reshape(1, -1), w_plg.astype(BF16), w_pl.astype(BF16))


def kernel(x, p, positions, g_in, w_in, g_cq, w_uq, g_ckv, w_ukv, g_q, g_k, conv_w, g_oa, g_oc,
           w_o, w_pl, w_plg, g_pl):
    batch, seq, d_model = x.shape
    depth = p.shape[0]
    tokens = batch * seq
    rows = PROJ_TILE // TOKENS_PER_ROW
    posc = jnp.repeat(positions.astype(F32).reshape(tokens // PROJ_TILE, TOKENS_PER_ROW, rows)
                      .transpose(0, 2, 1), HALF_ROPE, axis=-1).reshape(tokens // TOKENS_PER_ROW, LANES)
    inv_freq = 1.0 / (ROPE_THETA ** (jnp.arange(0, ROPE_DIM, 2, dtype=F32) / ROPE_DIM))
    invf = jnp.tile(inv_freq, LANES // HALF_ROPE).reshape(1, LANES)
    h = x.reshape(tokens, d_model)
    for i in range(depth):
        h = _layer(h, p[i].reshape(tokens, PLE_DIM), posc, invf, batch, seq,
                   g_in[i], w_in[i], g_cq[i], w_uq[i], g_ckv[i], w_ukv[i], g_q[i], g_k[i],
                   conv_w[i], g_oa[i], g_oc[i], w_o[i], w_pl[i], w_plg[i], g_pl[i])
    return h.reshape(batch, seq, d_model).astype(x.dtype)
```

```python
import functools
import math

import jax
import jax.numpy as jnp
import numpy as np
from jax import lax
from jax.experimental import pallas as pl
from jax.experimental.pallas import tpu as pltpu

D_MODEL = 1024
PLE_DIM = 256
N_HEADS = 4
NOPE_DIM = 128
ROPE_DIM = 64
HALF_ROPE = ROPE_DIM // 2
V_DIM = 128
QK_DIM = NOPE_DIM + ROPE_DIM
Q_LORA = 256
KV_LORA = 128
ATTN_WIDTH = N_HEADS * V_DIM
CONV_WIDTH = D_MODEL - ATTN_WIDTH
CONV_K = 3
ROPE_THETA = 10000.0
RMS_EPS = 1e-6
NEG_INF = -1e30

LANES = 128
SUBLANES = 8
HEAD_PAD = 2 * LANES
KPE_PAD = LANES
BF16_ROWS = 16
V_ROWS = V_DIM + BF16_ROWS
TOKENS_PER_ROW = LANES // HALF_ROPE
OFF_CQ = 0
OFF_CKV = OFF_CQ + Q_LORA
OFF_KPE = OFF_CKV + KV_LORA
LAT_SRC = OFF_KPE + ROPE_DIM
LAT_PAD = OFF_KPE + KPE_PAD
SRC_ZA = 0
SRC_CB = SRC_ZA + ATTN_WIDTH
SRC_CC = SRC_CB + CONV_WIDTH
SRC_CX = SRC_CC + CONV_WIDTH
SRC_ZC = SRC_CX + CONV_WIDTH
WIDE = SRC_ZC + CONV_WIDTH
PREP_COLS = 512
PREP_HALF = PREP_COLS // 2
PREP_PAIRS = ((SRC_ZA, SRC_ZA + PREP_HALF), (SRC_CC, SRC_CX), (SRC_CC + PREP_HALF, SRC_CX + PREP_HALF),
              (SRC_CB, SRC_ZC), (SRC_CB + PREP_HALF, SRC_ZC + PREP_HALF))
OFF_ZA = 0
OFF_CCX = OFF_ZA + PREP_COLS
OFF_CBZ = OFF_CCX + 2 * PREP_COLS

PROJ_TILE = 1024
PROJ_ROWS = PROJ_TILE // TOKENS_PER_ROW
OUT_ROWS = 256
ATTN_TILE = 1024
ATTN_Q_CHUNK = 256
VMEM_LIMIT = 48 * 1024 * 1024

BF16 = jnp.bfloat16
F32 = jnp.float32


def _rms_scale(v, width):
    return lax.rsqrt(jnp.sum(v * v, axis=-1, keepdims=True) * (1.0 / width) + RMS_EPS)


def _silu(z):
    h = 0.5 * z
    return h + h * jnp.tanh(h)


def _dot(a, b):
    return jnp.dot(a, b, preferred_element_type=F32)


def _rope_angles(posc, invf):
    ang = posc * invf
    return jnp.cos(ang), jnp.sin(ang)


def _rope_quarter(cosc, sinc, a):
    lane = lax.broadcasted_iota(jnp.int32, (1, LANES), 1)
    lo = lane < HALF_ROPE
    mid = (lane >= HALF_ROPE) & (lane < ROPE_DIM)
    c = pltpu.roll(cosc, LANES - HALF_ROPE * a, 1) if a else cosc
    s = pltpu.roll(sinc, LANES - HALF_ROPE * a, 1) if a else sinc
    return (jnp.where(lo, c, jnp.where(mid, pltpu.roll(c, HALF_ROPE, 1), 0.0)),
            jnp.where(lo, -s, jnp.where(mid, pltpu.roll(s, HALF_ROPE, 1), 0.0)), lo)


def _rope(t, c, s, lo):
    swapped = jnp.where(lo, pltpu.roll(t, LANES - HALF_ROPE, 1), pltpu.roll(t, HALF_ROPE, 1))
    return t * c + swapped * s


def _proj_kernel(tiles_per_seq,
                 x_ref, posc_ref, invf_ref, w_lat_ref, w_in_ref, g_cq_ref, w_uq_ref,
                 g_ckv_ref, w_ukv_ref, g_q_ref, g_kn_ref, g_kr_ref, conv_w_ref, g_oc_ref,
                 qt_ref, k_ref, vt_ref, ga_ref, yc_ref, carry_ref):
    tm = x_ref.shape[0]
    quarter_rows = tm // TOKENS_PER_ROW
    tables = []
    q_scale = math.log2(math.e) / math.sqrt(QK_DIM)

    @pl.when(pl.program_id(0) % tiles_per_seq == 0)
    def _():
        carry_ref[0:SUBLANES, :] = jnp.zeros((SUBLANES, CONV_WIDTH), F32)

    def rows_block(r0, nr):
        x = x_ref[r0:r0 + nr, :]
        h = (x * _rms_scale(x, D_MODEL)).astype(BF16)

        def proj(off, width):
            return _dot(h, w_in_ref[:, off:off + width])

        def tile_pairs(r):
            return [(r[:, i:i + LANES], r[:, i + LANES:i + 2 * LANES]) for i in range(0, r.shape[1], 2 * LANES)]

        lat = _dot(h, w_lat_ref[...])
        c_q = lat[:, OFF_CQ:OFF_CKV]
        c_kv = lat[:, OFF_CKV:OFF_KPE]
        kpe = lat[:, OFF_KPE:LAT_PAD]
        cv =jnp.concatenate([c * xin for c, xin in tile_pairs(proj(OFF_CCX, 2 * CONV_WIDTH))], axis=1)
        cqn = (c_q * _rms_scale(c_q, Q_LORA) * g_cq_ref[...]).astype(BF16)
        qf = _dot(cqn, w_uq_ref[...])
        gated_b = jnp.concatenate([b * _silu(z) for b, z in tile_pairs(proj(OFF_CBZ, 2 * CONV_WIDTH))], axis=1)
        ckvn = (c_kv * _rms_scale(c_kv, KV_LORA) * g_ckv_ref[...]).astype(BF16)
        kv = _dot(ckvn, w_ukv_ref[...])
        z_a = proj(OFF_ZA, ATTN_WIDTH)

        if not tables:
            tables.extend(_rope_angles(posc_ref[...], invf_ref[...]))
        assert nr == quarter_rows
        cos_t, sin_t, lo = _rope_quarter(*tables, r0 // quarter_rows)

        for hd in range(N_HEADS):
            qh = qf[:, hd * HEAD_PAD:(hd + 1) * HEAD_PAD]
            qn = qh * (_rms_scale(qh, QK_DIM) * q_scale) * g_q_ref[...]
            qt_ref[hd * HEAD_PAD:hd * HEAD_PAD + LANES, r0:r0 + nr] = qn[:, :LANES].astype(BF16).T
            qt_ref[hd * HEAD_PAD + LANES:(hd + 1) * HEAD_PAD, r0:r0 + nr] = (
                _rope(qn[:, LANES:], cos_t, sin_t, lo).astype(BF16).T)

        ss_kpe = jnp.sum(kpe * kpe, axis=-1, keepdims=True)
        k_rot = _rope(kpe * g_kr_ref[...], cos_t, sin_t, lo)
        for hd in range(N_HEADS):
            kn = kv[:, hd * (NOPE_DIM + V_DIM):hd * (NOPE_DIM + V_DIM) + NOPE_DIM]
            vh = kv[:, hd * (NOPE_DIM + V_DIM) + NOPE_DIM:(hd + 1) * (NOPE_DIM + V_DIM)]
            ss = jnp.sum(kn * kn, axis=-1, keepdims=True) + ss_kpe
            rs = lax.rsqrt(ss * (1.0 / QK_DIM) + RMS_EPS)
            k_ref[r0:r0 + nr, hd * HEAD_PAD:hd * HEAD_PAD + LANES] = (kn * rs * g_kn_ref[...]).astype(BF16)
            k_ref[r0:r0 + nr, hd * HEAD_PAD + LANES:(hd + 1) * HEAD_PAD] = (k_rot * rs).astype(BF16)
            vt_ref[hd * V_ROWS:hd * V_ROWS + V_DIM, r0:r0 + nr] = vh.astype(BF16).T
            vt_ref[hd * V_ROWS + V_DIM:(hd + 1) * V_ROWS, r0:r0 + nr] = jnp.ones((BF16_ROWS, nr), BF16)

        base = SUBLANES + r0
        prev = carry_ref[base - SUBLANES:base, :]
        carry_ref[base + nr - SUBLANES:base + nr, :] = cv[nr - SUBLANES:, :]
        row = lax.broadcasted_iota(jnp.int32, (SUBLANES, CONV_WIDTH), 0)

        def delayed(d):
            sh = pltpu.roll(cv, d, 0)
            head = jnp.where(row < d, pltpu.roll(prev, d, 0), sh[0:SUBLANES])
            return jnp.concatenate([head, sh[SUBLANES:]], axis=0)

        u = (conv_w_ref[2:3, :] * cv + conv_w_ref[1:2, :] * delayed(1) + conv_w_ref[0:1, :] * delayed(2))
        yc = gated_b * u
        yc_ref[r0:r0 + nr, :] = (yc * _rms_scale(yc, CONV_WIDTH) * g_oc_ref[...]).astype(BF16)

        for i, (za_lo, za_hi) in enumerate(tile_pairs(z_a)):
            ga_ref[r0:r0 + nr, i * LANES:(i + 1) * LANES] = _silu(za_lo).astype(BF16)
            ga_ref[r0:r0 + nr, PREP_HALF + i * LANES:PREP_HALF + (i + 1) * LANES] = _silu(za_hi).astype(BF16)

    for blk in range(tm // PROJ_ROWS):
        rows_block(blk * PROJ_ROWS, PROJ_ROWS)
    carry_ref[0:SUBLANES, :] = carry_ref[tm:tm + SUBLANES, :]


def _attn_kernel(qn_ref, kn_ref, kc_ref, qc_ref, diag_n_ref, first_c_ref, last_c_ref,
                 qt_ref, k_ref, vt_ref, o_ref, s_ref, mx_ref, m_ref, acc_ref):
    step = pl.program_id(0)
    tq = qt_ref.shape[1]
    tk = k_ref.shape[0]

    @pl.when(step == 0)
    def _():
        s_ref[...] = jnp.zeros(s_ref.shape, F32)
        mx_ref[...] = jnp.zeros(mx_ref.shape, F32)
        acc_ref[...] = jnp.zeros(acc_ref.shape, F32)

    @pl.when((first_c_ref[step] == 1) | (step == 0))
    def _():
        m_ref[...] = jnp.full(m_ref.shape, NEG_INF, F32)

    chunk = ATTN_Q_CHUNK
    nchunk = tq // chunk

    def body(diag_cur, diag_next):
        if diag_next:
            tri = (lax.broadcasted_iota(jnp.int32, (chunk, chunk), 0)
                   <= lax.broadcasted_iota(jnp.int32, (chunk, chunk), 1))

        def scores(hd, c):
            cols = slice(c * chunk, (c + 1) * chunk)
            rows = chunk * (c + 1) if diag_next else tk
            s = _dot(k_ref[0:rows, hd * HEAD_PAD:(hd + 1) * HEAD_PAD],
                     qt_ref[hd * HEAD_PAD:(hd + 1) * HEAD_PAD, cols])
            if not diag_next:
                s_ref[hd, :, cols] = s
                mx_ref[hd, :, cols] = jnp.max(s, axis=0, keepdims=True)
                return
            last = jnp.where(tri, s[rows - chunk:rows], NEG_INF)
            s_ref[hd, rows - chunk:rows, cols] = last
            mx = jnp.max(last, axis=0, keepdims=True)
            if rows > chunk:
                s_ref[hd, 0:rows - chunk, cols] = s[0:rows - chunk]
                mx = jnp.maximum(mx, jnp.max(s[0:rows - chunk], axis=0, keepdims=True))
            mx_ref[hd, :, cols] = mx

        def probabilities(hd, c):
            cols = slice(c * chunk, (c + 1) * chunk)
            rows = chunk * (c + 1) if diag_cur else tk
            m_prev = m_ref[hd, :, cols]
            m_new = jnp.maximum(m_prev, mx_ref[hd, :, cols])
            m_ref[hd, :, cols] = m_new
            return jnp.exp2(m_prev - m_new), jnp.exp2((s_ref[hd, 0:rows, cols] - m_new).astype(BF16))

        def accumulate(hd, c, alpha, p):
            cols = slice(c * chunk, (c + 1) * chunk)
            pv = _dot(vt_ref[hd * V_ROWS:(hd + 1) * V_ROWS, 0:p.shape[0]], p)
            acc_ref[hd, :, cols] = alpha * acc_ref[hd, :, cols] + pv

        for hd in range(N_HEADS):
            for c in range(nchunk):
                alpha, p = probabilities(hd, c)
                scores(hd, c)
                accumulate(hd, c, alpha, p)

    diag_cur = last_c_ref[step] == 1
    diag_next = diag_n_ref[step] == 1
    for cur_flag in (False, True):
        for next_flag in (False, True):
            @pl.when((diag_cur == cur_flag) & (diag_next == next_flag))
            def _(cur_flag=cur_flag, next_flag=next_flag):
                body(cur_flag, next_flag)


    @pl.when(last_c_ref[step] == 1)
    def _():
        for hd in range(N_HEADS):
            acc = acc_ref[hd]
            o_ref[:, hd * V_DIM:(hd + 1) * V_DIM] = (
                acc[:V_DIM] / acc[V_DIM:V_DIM + 1]).T.astype(o_ref.dtype)


def _out_kernel(o_ref, ga_ref, yc_ref, x_ref, p_ref, g_oa_ref, w_o_ref, g_pl_ref, w_plg_ref, w_pl_ref,
                out_ref):
    tm = x_ref.shape[0]

    def residual(r0):
        rows = slice(r0, r0 + OUT_ROWS)
        ya = o_ref[rows, :].astype(F32) * ga_ref[rows, :].astype(F32)
        ya_n = (ya * _rms_scale(ya, ATTN_WIDTH) * g_oa_ref[...]).astype(BF16)
        y = jnp.concatenate([ya_n, yc_ref[rows, :]], axis=1)
        x1 = x_ref[rows, :] + _dot(y, w_o_ref[...])
        out_ref[rows, :] = x1
        hn = (x1 * _rms_scale(x1, D_MODEL) * g_pl_ref[...]).astype(BF16)
        return r0, hn

    def gated(r0, hn):
        rows = slice(r0, r0 + OUT_ROWS)
        gate = 1.0 / (1.0 + jnp.exp(-_dot(hn, w_plg_ref[...])))
        ple = _dot(p_ref[rows, :].astype(BF16), w_pl_ref[...])
        out_ref[rows, :] = out_ref[rows, :] + gate * ple

    prev = None
    for r0 in range(0, tm, OUT_ROWS):
        cur = residual(r0)
        if prev is not None:
            gated(*prev)
        prev = cur
    gated(*prev)


def _row_spec(tile, width):
    return pl.BlockSpec((tile, width), lambda i: (i, 0))


def _const_spec(shape):
    return pl.BlockSpec(shape, lambda i: (0,) * len(shape))


def _triangle_schedule(batch, n_blk):
    pairs = [(b * n_blk + i, b * n_blk + j, j == 0, j == i)
             for b in range(batch) for i in range(n_blk) for j in range(i + 1)]
    nxt = pairs + [pairs[-1]]
    cur = [pairs[0]] + pairs
    cols = ([p[0] for p in nxt], [p[1] for p in nxt], [p[1] for p in cur], [p[0] for p in cur],
            [int(p[3]) for p in nxt],
            [0] + [int(p[2]) for p in pairs], [0] + [int(p[3]) for p in pairs])
    return tuple(jnp.asarray(np.asarray(c, np.int32)) for c in cols)


def _prep_in_kernel(rows_a_ref, rows_b_ref, wt_lat_ref, wt_a_ref, wt_b_ref, g_ref, lat_ref, wide_ref):
    g = g_ref[...]

    @pl.when(pl.program_id(0) == 0)
    def _():
        pad = jnp.zeros((LAT_PAD - LAT_SRC, D_MODEL), F32)
        lat_ref[...] = jnp.concatenate([wt_lat_ref[...] * g, pad], axis=0).T.astype(BF16)

    a = (wt_a_ref[...] * g).T.astype(BF16)
    b = (wt_b_ref[...] * g).T.astype(BF16)
    for i in range(PREP_HALF // LANES):
        wide_ref[:, 2 * i * LANES:(2 * i + 1) * LANES] = a[:, i * LANES:(i + 1) * LANES]
        wide_ref[:, (2 * i + 1) * LANES:(2 * i + 2) * LANES] = b[:, i * LANES:(i + 1) * LANES]


def _prepare_w_in(w_in, g_in):
    def rows_at(nrows, start):
        return pl.BlockSpec((pl.Element(nrows), pl.Element(D_MODEL)), start)

    rows_a = jnp.asarray(np.asarray([LAT_SRC + a for a, _ in PREP_PAIRS], np.int32))
    rows_b = jnp.asarray(np.asarray([LAT_SRC + b for _, b in PREP_PAIRS], np.int32))
    return pl.pallas_call(
        _prep_in_kernel,
        grid_spec=pltpu.PrefetchScalarGridSpec(
            num_scalar_prefetch=2,
            grid=(len(PREP_PAIRS),),
            in_specs=[rows_at(LAT_SRC, lambda j, ra, rb: (0, 0)),
                      rows_at(PREP_HALF, lambda j, ra, rb: (pl.multiple_of(ra[j], SUBLANES), 0)),
                      rows_at(PREP_HALF, lambda j, ra, rb: (pl.multiple_of(rb[j], SUBLANES), 0)),
                      pl.BlockSpec((1, D_MODEL), lambda j, ra, rb: (0, 0))],
            out_specs=[pl.BlockSpec((D_MODEL, LAT_PAD), lambda j, ra, rb: (0, 0)),
                       pl.BlockSpec((D_MODEL, PREP_COLS), lambda j, ra, rb: (0, j))],
        ),
        out_shape=[jax.ShapeDtypeStruct((D_MODEL, LAT_PAD), BF16),
                   jax.ShapeDtypeStruct((D_MODEL, WIDE), BF16)],
        compiler_params=pltpu.CompilerParams(dimension_semantics=("arbitrary",),
                                             vmem_limit_bytes=VMEM_LIMIT),
        name="mla_conv_w_in_prep",
    )(rows_a, rows_b, w_in.T, w_in.T, w_in.T, g_in.reshape(1, -1))


def _layer(x2, p2, posc, invf, batch, seq, g_in, w_in, g_cq, w_uq, g_ckv, w_ukv, g_q, g_k,
           conv_w, g_oa, g_oc, w_o, w_pl, w_plg, g_pl):
    tokens = batch * seq
    tm = PROJ_TILE
    assert seq % tm == 0 and seq % ATTN_TILE == 0

    w_lat, w_wide = _prepare_w_in(w_in, g_in)
    w_uq_p = jnp.pad(w_uq.reshape(Q_LORA, N_HEADS, QK_DIM),
                     ((0, 0), (0, 0), (0, HEAD_PAD - QK_DIM))).reshape(Q_LORA, N_HEADS * HEAD_PAD).astype(BF16)
    g_q_p = jnp.pad(g_q, (0, HEAD_PAD - QK_DIM)).reshape(1, HEAD_PAD)
    g_kn = g_k[:NOPE_DIM].reshape(1, NOPE_DIM)
    g_kr = jnp.pad(g_k[NOPE_DIM:], (0, KPE_PAD - ROPE_DIM)).reshape(1, KPE_PAD)

    cparams = pltpu.CompilerParams(dimension_semantics=("arbitrary",), vmem_limit_bytes=VMEM_LIMIT)

    def col_spec(rows, tile):
        return pl.BlockSpec((rows, tile), lambda i: (0, i))

    qt, k, vt, ga, yc = pl.pallas_call(
        functools.partial(_proj_kernel, seq // tm),
        grid=(tokens // tm,),
        in_specs=[
            _row_spec(tm, D_MODEL), _row_spec(tm // TOKENS_PER_ROW, LANES), _const_spec((1, LANES)),
            _const_spec((D_MODEL, LAT_PAD)), _const_spec((D_MODEL, WIDE)),
            _const_spec((1, Q_LORA)), _const_spec((Q_LORA, N_HEADS * HEAD_PAD)),
            _const_spec((1, KV_LORA)), _const_spec((KV_LORA, N_HEADS * (NOPE_DIM + V_DIM))),
            _const_spec((1, HEAD_PAD)), _const_spec((1, NOPE_DIM)), _const_spec((1, KPE_PAD)),
            _const_spec((CONV_K, CONV_WIDTH)), _const_spec((1, CONV_WIDTH)),
        ],
        out_specs=[
            col_spec(N_HEADS * HEAD_PAD, tm), _row_spec(tm, N_HEADS * HEAD_PAD),
            col_spec(N_HEADS * V_ROWS, tm), _row_spec(tm, ATTN_WIDTH), _row_spec(tm, CONV_WIDTH),
        ],
        out_shape=[
            jax.ShapeDtypeStruct((N_HEADS * HEAD_PAD, tokens), BF16),
            jax.ShapeDtypeStruct((tokens, N_HEADS * HEAD_PAD), BF16),
            jax.ShapeDtypeStruct((N_HEADS * V_ROWS, tokens), BF16),
            jax.ShapeDtypeStruct((tokens, ATTN_WIDTH), BF16),
            jax.ShapeDtypeStruct((tokens, CONV_WIDTH), BF16),
        ],
        scratch_shapes=[pltpu.VMEM((tm + SUBLANES, CONV_WIDTH), F32)],
        compiler_params=cparams,
        name="mla_conv_proj",
    )(x2, posc, invf, w_lat, w_wide, g_cq.reshape(1, -1), w_uq_p,
      g_ckv.reshape(1, -1), w_ukv.astype(BF16), g_q_p, g_kn, g_kr, conv_w, g_oc.reshape(1, -1))

    ta = ATTN_TILE
    sched = _triangle_schedule(batch, seq // ta)
    o = pl.pallas_call(
        _attn_kernel,
        grid_spec=pltpu.PrefetchScalarGridSpec(
            num_scalar_prefetch=len(sched),
            grid=(int(sched[0].shape[0]),),
            in_specs=[
                pl.BlockSpec((N_HEADS * HEAD_PAD, ta), lambda s, qn, kn, kc, qc, *_: (0, qn[s])),
                pl.BlockSpec((ta, N_HEADS * HEAD_PAD), lambda s, qn, kn, kc, qc, *_: (kn[s], 0)),
                pl.BlockSpec((N_HEADS * V_ROWS, ta), lambda s, qn, kn, kc, qc, *_: (0, kc[s])),
            ],
            out_specs=pl.BlockSpec((ta, ATTN_WIDTH), lambda s, qn, kn, kc, qc, *_: (qc[s], 0)),
            scratch_shapes=[pltpu.VMEM((N_HEADS, ta, ta), F32),
                            pltpu.VMEM((N_HEADS, 1, ta), F32),
                            pltpu.VMEM((N_HEADS, 1, ta), F32),
                            pltpu.VMEM((N_HEADS, V_ROWS, ta), F32)],
        ),
        out_shape=jax.ShapeDtypeStruct((tokens, ATTN_WIDTH), BF16),
        compiler_params=cparams,
        name="mla_flash_attn",
    )(*sched, qt, k, vt)

    return pl.pallas_call(
        _out_kernel,
        grid=(tokens // tm,),
        in_specs=[
            _row_spec(tm, ATTN_WIDTH), _row_spec(tm, ATTN_WIDTH), _row_spec(tm, CONV_WIDTH),
            _row_spec(tm, D_MODEL), _row_spec(tm, PLE_DIM),
            _const_spec((1, ATTN_WIDTH)), _const_spec((D_MODEL, D_MODEL)), _const_spec((1, D_MODEL)),
            _const_spec((D_MODEL, D_MODEL)), _const_spec((PLE_DIM, D_MODEL)),
        ],
        out_specs=_row_spec(tm, D_MODEL),
        out_shape=jax.ShapeDtypeStruct((tokens, D_MODEL), F32),
        compiler_params=cparams,
        name="mla_conv_out",
    )(o, ga, yc, x2, p2, g_oa.reshape(1, -1), w_o.astype(BF16),
      g_pl.reshape(1, -1), w_plg.astype(BF16), w_pl.astype(BF16))


def kernel(x, p, positions, g_in, w_in, g_cq, w_uq, g_ckv, w_ukv, g_q, g_k, conv_w, g_oa, g_oc,
           w_o, w_pl, w_plg, g_pl):
    batch, seq, d_model = x.shape
    depth = p.shape[0]
    tokens = batch * seq
    rows = PROJ_TILE // TOKENS_PER_ROW
    posc = jnp.repeat(positions.astype(F32).reshape(tokens // PROJ_TILE, TOKENS_PER_ROW, rows)
                      .transpose(0, 2, 1), HALF_ROPE, axis=-1).reshape(tokens // TOKENS_PER_ROW, LANES)
    inv_freq = 1.0 / (ROPE_THETA ** (jnp.arange(0, ROPE_DIM, 2, dtype=F32) / ROPE_DIM))
    invf = jnp.tile(inv_freq, LANES // HALF_ROPE).reshape(1, LANES)
    h = x.reshape(tokens, d_model)
    for i in range(depth):
        h = _layer(h, p[i].reshape(tokens, PLE_DIM), posc, invf, batch, seq,
                   g_in[i], w_in[i], g_cq[i], w_uq[i], g_ckv[i], w_ukv[i], g_q[i], g_k[i],
                   conv_w[i], g_oa[i], g_oc[i], w_o[i], w_pl[i], w_plg[i], g_pl[i])
    return h.reshape(batch, seq, d_model).astype(x.dtype)
```

```python
import functools
import math

import jax
import jax.numpy as jnp
import numpy as np
from jax import lax
from jax.experimental import pallas as pl
from jax.experimental.pallas import tpu as pltpu

D_MODEL = 1024
PLE_DIM = 256
N_HEADS = 4
NOPE_DIM = 128
ROPE_DIM = 64
HALF_ROPE = ROPE_DIM // 2
V_DIM = 128
QK_DIM = NOPE_DIM + ROPE_DIM
Q_LORA = 256
KV_LORA = 128
ATTN_WIDTH = N_HEADS * V_DIM
CONV_WIDTH = D_MODEL - ATTN_WIDTH
CONV_K = 3
ROPE_THETA = 10000.0
RMS_EPS = 1e-6
NEG_INF = -1e30

LANES = 128
SUBLANES = 8
HEAD_PAD = 2 * LANES
KPE_PAD = LANES
BF16_ROWS = 16
V_ROWS = V_DIM + BF16_ROWS
TOKENS_PER_ROW = LANES // HALF_ROPE
OFF_CQ = 0
OFF_CKV = OFF_CQ + Q_LORA
OFF_KPE = OFF_CKV + KV_LORA
LAT_SRC = OFF_KPE + ROPE_DIM
LAT_PAD = OFF_KPE + KPE_PAD
SRC_ZA = 0
SRC_CB = SRC_ZA + ATTN_WIDTH
SRC_CC = SRC_CB + CONV_WIDTH
SRC_CX = SRC_CC + CONV_WIDTH
SRC_ZC = SRC_CX + CONV_WIDTH
WIDE = SRC_ZC + CONV_WIDTH
PREP_COLS = 512
PREP_HALF = PREP_COLS // 2
PREP_PAIRS = ((SRC_ZA, SRC_ZA + PREP_HALF), (SRC_CC, SRC_CX), (SRC_CC + PREP_HALF, SRC_CX + PREP_HALF),
              (SRC_CB, SRC_ZC), (SRC_CB + PREP_HALF, SRC_ZC + PREP_HALF))
OFF_ZA = 0
OFF_CCX = OFF_ZA + PREP_COLS
OFF_CBZ = OFF_CCX + 2 * PREP_COLS

PROJ_TILE = 1024
PROJ_ROWS = PROJ_TILE // TOKENS_PER_ROW
OUT_ROWS = 256
ATTN_TILE = 1024
ATTN_Q_CHUNK = 256
VMEM_LIMIT = 48 * 1024 * 1024

BF16 = jnp.bfloat16
F32 = jnp.float32


def _rms_scale(v, width):
    return lax.rsqrt(jnp.sum(v * v, axis=-1, keepdims=True) * (1.0 / width) + RMS_EPS)


def _silu(z):
    h = 0.5 * z
    return h + h * jnp.tanh(h)


def _dot(a, b):
    return jnp.dot(a, b, preferred_element_type=F32)


def _rope_angles(posc, invf):
    ang = posc * invf
    return jnp.cos(ang), jnp.sin(ang)


def _rope_quarter(cosc, sinc, a):
    lane = lax.broadcasted_iota(jnp.int32, (1, LANES), 1)
    lo = lane < HALF_ROPE
    mid = (lane >= HALF_ROPE) & (lane < ROPE_DIM)
    c = pltpu.roll(cosc, LANES - HALF_ROPE * a, 1) if a else cosc
    s = pltpu.roll(sinc, LANES - HALF_ROPE * a, 1) if a else sinc
    return (jnp.where(lo, c, jnp.where(mid, pltpu.roll(c, HALF_ROPE, 1), 0.0)),
            jnp.where(lo, -s, jnp.where(mid, pltpu.roll(s, HALF_ROPE, 1), 0.0)), lo)


def _rope(t, c, s, lo):
    swapped = jnp.where(lo, pltpu.roll(t, LANES - HALF_ROPE, 1), pltpu.roll(t, HALF_ROPE, 1))
    return t * c + swapped * s


def _proj_kernel(tiles_per_seq,
                 x_ref, posc_ref, invf_ref, w_lat_ref, w_in_ref, g_cq_ref, w_uq_ref,
                 g_ckv_ref, w_ukv_ref, g_q_ref, g_kn_ref, g_kr_ref, conv_w_ref, g_oc_ref,
                 qt_ref, k_ref, vt_ref, ga_ref, yc_ref, carry_ref):
    tm = x_ref.shape[0]
    quarter_rows = tm // TOKENS_PER_ROW
    tables = []
    q_scale = math.log2(math.e) / math.sqrt(QK_DIM)

    @pl.when(pl.program_id(0) % tiles_per_seq == 0)
    def _():
        carry_ref[0:SUBLANES, :] = jnp.zeros((SUBLANES, CONV_WIDTH), F32)

    def rows_block(r0, nr):
        x = x_ref[r0:r0 + nr, :]
        h = (x * _rms_scale(x, D_MODEL)).astype(BF16)

        def proj(off, width):
            return _dot(h, w_in_ref[:, off:off + width])

        def tile_pairs(r):
            return [(r[:, i:i + LANES], r[:, i + LANES:i + 2 * LANES]) for i in range(0, r.shape[1], 2 * LANES)]

        lat = _dot(h, w_lat_ref[...])
        c_q = lat[:, OFF_CQ:OFF_CKV]
        c_kv = lat[:, OFF_CKV:OFF_KPE]
        kpe = lat[:, OFF_KPE:LAT_PAD]
        cv =jnp.concatenate([c * xin for c, xin in tile_pairs(proj(OFF_CCX, 2 * CONV_WIDTH))], axis=1)
        cqn = (c_q * _rms_scale(c_q, Q_LORA) * g_cq_ref[...]).astype(BF16)
        qf = _dot(cqn, w_uq_ref[...])
        gated_b = jnp.concatenate([b * _silu(z) for b, z in tile_pairs(proj(OFF_CBZ, 2 * CONV_WIDTH))], axis=1)
        ckvn = (c_kv * _rms_scale(c_kv, KV_LORA) * g_ckv_ref[...]).astype(BF16)
        kv = _dot(ckvn, w_ukv_ref[...])
        z_a = proj(OFF_ZA, ATTN_WIDTH)

        if not tables:
            tables.extend(_rope_angles(posc_ref[...], invf_ref[...]))
        assert nr == quarter_rows
        cos_t, sin_t, lo = _rope_quarter(*tables, r0 // quarter_rows)

        for hd in range(N_HEADS):
            qh = qf[:, hd * HEAD_PAD:(hd + 1) * HEAD_PAD]
            qn = qh * (_rms_scale(qh, QK_DIM) * q_scale) * g_q_ref[...]
            qt_ref[hd * HEAD_PAD:hd * HEAD_PAD + LANES, r0:r0 + nr] = qn[:, :LANES].astype(BF16).T
            qt_ref[hd * HEAD_PAD + LANES:(hd + 1) * HEAD_PAD, r0:r0 + nr] = (
                _rope(qn[:, LANES:], cos_t, sin_t, lo).astype(BF16).T)

        ss_kpe = jnp.sum(kpe * kpe, axis=-1, keepdims=True)
        k_rot = _rope(kpe * g_kr_ref[...], cos_t, sin_t, lo)
        for hd in range(N_HEADS):
            kn = kv[:, hd * (NOPE_DIM + V_DIM):hd * (NOPE_DIM + V_DIM) + NOPE_DIM]
            vh = kv[:, hd * (NOPE_DIM + V_DIM) + NOPE_DIM:(hd + 1) * (NOPE_DIM + V_DIM)]
            ss = jnp.sum(kn * kn, axis=-1, keepdims=True) + ss_kpe
            rs = lax.rsqrt(ss * (1.0 / QK_DIM) + RMS_EPS)
            k_ref[r0:r0 + nr, hd * HEAD_PAD:hd * HEAD_PAD + LANES] = (kn * rs * g_kn_ref[...]).astype(BF16)
            k_ref[r0:r0 + nr, hd * HEAD_PAD + LANES:(hd + 1) * HEAD_PAD] = (k_rot * rs).astype(BF16)
            vt_ref[hd * V_ROWS:hd * V_ROWS + V_DIM, r0:r0 + nr] = vh.astype(BF16).T
            vt_ref[hd * V_ROWS + V_DIM:(hd + 1) * V_ROWS, r0:r0 + nr] = jnp.ones((BF16_ROWS, nr), BF16)

        base = SUBLANES + r0
        prev = carry_ref[base - SUBLANES:base, :]
        carry_ref[base + nr - SUBLANES:base + nr, :] = cv[nr - SUBLANES:, :]
        row = lax.broadcasted_iota(jnp.int32, (SUBLANES, CONV_WIDTH), 0)

        def delayed(d):
            sh = pltpu.roll(cv, d, 0)
            head = jnp.where(row < d, pltpu.roll(prev, d, 0), sh[0:SUBLANES])
            return jnp.concatenate([head, sh[SUBLANES:]], axis=0)

        u = (conv_w_ref[2:3, :] * cv + conv_w_ref[1:2, :] * delayed(1) + conv_w_ref[0:1, :] * delayed(2))
        yc = gated_b * u
        yc_ref[r0:r0 + nr, :] = (yc * _rms_scale(yc, CONV_WIDTH) * g_oc_ref[...]).astype(BF16)

        for i, (za_lo, za_hi) in enumerate(tile_pairs(z_a)):
            ga_ref[r0:r0 + nr, i * LANES:(i + 1) * LANES] = _silu(za_lo).astype(BF16)
            ga_ref[r0:r0 + nr, PREP_HALF + i * LANES:PREP_HALF + (i + 1) * LANES] = _silu(za_hi).astype(BF16)

    for blk in range(tm // PROJ_ROWS):
        rows_block(blk * PROJ_ROWS, PROJ_ROWS)
    carry_ref[0:SUBLANES, :] = carry_ref[tm:tm + SUBLANES, :]


def _attn_kernel(qn_ref, kn_ref, kc_ref, qc_ref, diag_n_ref, first_c_ref, last_c_ref,
                 qt_ref, k_ref, vt_ref, o_ref, s_ref, mx_ref, m_ref, acc_ref):
    step = pl.program_id(0)
    tq = qt_ref.shape[1]
    tk = k_ref.shape[0]

    @pl.when(step == 0)
    def _():
        s_ref[...] = jnp.zeros(s_ref.shape, F32)
        mx_ref[...] = jnp.zeros(mx_ref.shape, F32)
        acc_ref[...] = jnp.zeros(acc_ref.shape, F32)

    @pl.when((first_c_ref[step] == 1) | (step == 0))
    def _():
        m_ref[...] = jnp.full(m_ref.shape, NEG_INF, F32)

    chunk = ATTN_Q_CHUNK
    nchunk = tq // chunk

    def body(diag_cur, diag_next):
        if diag_next:
            tri = (lax.broadcasted_iota(jnp.int32, (chunk, chunk), 0)
                   <= lax.broadcasted_iota(jnp.int32, (chunk, chunk), 1))

        def scores(hd, c):
            cols = slice(c * chunk, (c + 1) * chunk)
            rows = chunk * (c + 1) if diag_next else tk
            s = _dot(k_ref[0:rows, hd * HEAD_PAD:(hd + 1) * HEAD_PAD],
                     qt_ref[hd * HEAD_PAD:(hd + 1) * HEAD_PAD, cols])
            if not diag_next:
                s_ref[hd, :, cols] = s
                mx_ref[hd, :, cols] = jnp.max(s, axis=0, keepdims=True)
                return
            last = jnp.where(tri, s[rows - chunk:rows], NEG_INF)
            s_ref[hd, rows - chunk:rows, cols] = last
            mx = jnp.max(last, axis=0, keepdims=True)
            if rows > chunk:
                s_ref[hd, 0:rows - chunk, cols] = s[0:rows - chunk]
                mx = jnp.maximum(mx, jnp.max(s[0:rows - chunk], axis=0, keepdims=True))
            mx_ref[hd, :, cols] = mx

        def probabilities(hd, c):
            cols = slice(c * chunk, (c + 1) * chunk)
            rows = chunk * (c + 1) if diag_cur else tk
            m_prev = m_ref[hd, :, cols]
            m_new = jnp.maximum(m_prev, mx_ref[hd, :, cols])
            m_ref[hd, :, cols] = m_new
            return jnp.exp2(m_prev - m_new), jnp.exp2((s_ref[hd, 0:rows, cols] - m_new).astype(BF16))

        def accumulate(hd, c, alpha, p):
            cols = slice(c * chunk, (c + 1) * chunk)
            pv = _dot(vt_ref[hd * V_ROWS:(hd + 1) * V_ROWS, 0:p.shape[0]], p)
            acc_ref[hd, :, cols] = alpha * acc_ref[hd, :, cols] + pv

        for hd in range(N_HEADS):
            for c in range(nchunk):
                alpha, p = probabilities(hd, c)
                scores(hd, c)
                accumulate(hd, c, alpha, p)
            if diag_cur:
                acc = acc_ref[hd]
                o_ref[:, hd * V_DIM:(hd + 1) * V_DIM] = (
                    acc[:V_DIM] / acc[V_DIM:V_DIM + 1]).T.astype(o_ref.dtype)

    diag_cur = last_c_ref[step] == 1
    diag_next = diag_n_ref[step] == 1
    for cur_flag in (False, True):
        for next_flag in (False, True):
            @pl.when((diag_cur == cur_flag) & (diag_next == next_flag))
            def _(cur_flag=cur_flag, next_flag=next_flag):
                body(cur_flag, next_flag)


def _out_kernel(o_ref, ga_ref, yc_ref, x_ref, p_ref, g_oa_ref, w_o_ref, g_pl_ref, w_plg_ref, w_pl_ref,
                out_ref):
    tm = x_ref.shape[0]

    def residual(r0):
        rows = slice(r0, r0 + OUT_ROWS)
        ya = o_ref[rows, :].astype(F32) * ga_ref[rows, :].astype(F32)
        ya_n = (ya * _rms_scale(ya, ATTN_WIDTH) * g_oa_ref[...]).astype(BF16)
        y = jnp.concatenate([ya_n, yc_ref[rows, :]], axis=1)
        x1 = x_ref[rows, :] + _dot(y, w_o_ref[...])
        out_ref[rows, :] = x1
        hn = (x1 * _rms_scale(x1, D_MODEL) * g_pl_ref[...]).astype(BF16)
        return r0, hn

    def gated(r0, hn):
        rows = slice(r0, r0 + OUT_ROWS)
        gate = 1.0 / (1.0 + jnp.exp(-_dot(hn, w_plg_ref[...])))
        ple = _dot(p_ref[rows, :].astype(BF16), w_pl_ref[...])
        out_ref[rows, :] = out_ref[rows, :] + gate * ple

    prev = None
    for r0 in range(0, tm, OUT_ROWS):
        cur = residual(r0)
        if prev is not None:
            gated(*prev)
        prev = cur
    gated(*prev)


def _row_spec(tile, width):
    return pl.BlockSpec((tile, width), lambda i: (i, 0))


def _const_spec(shape):
    return pl.BlockSpec(shape, lambda i: (0,) * len(shape))


def _triangle_schedule(batch, n_blk):
    pairs = [(b * n_blk + i, b * n_blk + j, j == 0, j == i)
             for b in range(batch) for i in range(n_blk) for j in range(i + 1)]
    nxt = pairs + [pairs[-1]]
    cur = [pairs[0]] + pairs
    cols = ([p[0] for p in nxt], [p[1] for p in nxt], [p[1] for p in cur], [p[0] for p in cur],
            [int(p[3]) for p in nxt],
            [0] + [int(p[2]) for p in pairs], [0] + [int(p[3]) for p in pairs])
    return tuple(jnp.asarray(np.asarray(c, np.int32)) for c in cols)


def _prep_in_kernel(rows_a_ref, rows_b_ref, wt_lat_ref, wt_a_ref, wt_b_ref, g_ref, lat_ref, wide_ref):
    g = g_ref[...]

    @pl.when(pl.program_id(0) == 0)
    def _():
        pad = jnp.zeros((LAT_PAD - LAT_SRC, D_MODEL), F32)
        lat_ref[...] = jnp.concatenate([wt_lat_ref[...] * g, pad], axis=0).T.astype(BF16)

    a = (wt_a_ref[...] * g).T.astype(BF16)
    b = (wt_b_ref[...] * g).T.astype(BF16)
    for i in range(PREP_HALF // LANES):
        wide_ref[:, 2 * i * LANES:(2 * i + 1) * LANES] = a[:, i * LANES:(i + 1) * LANES]
        wide_ref[:, (2 * i + 1) * LANES:(2 * i + 2) * LANES] = b[:, i * LANES:(i + 1) * LANES]


def _prepare_w_in(w_in, g_in):
    def rows_at(nrows, start):
        return pl.BlockSpec((pl.Element(nrows), pl.Element(D_MODEL)), start)

    rows_a = jnp.asarray(np.asarray([LAT_SRC + a for a, _ in PREP_PAIRS], np.int32))
    rows_b = jnp.asarray(np.asarray([LAT_SRC + b for _, b in PREP_PAIRS], np.int32))
    return pl.pallas_call(
        _prep_in_kernel,
        grid_spec=pltpu.PrefetchScalarGridSpec(
            num_scalar_prefetch=2,
            grid=(len(PREP_PAIRS),),
            in_specs=[rows_at(LAT_SRC, lambda j, ra, rb: (0, 0)),
                      rows_at(PREP_HALF, lambda j, ra, rb: (pl.multiple_of(ra[j], SUBLANES), 0)),
                      rows_at(PREP_HALF, lambda j, ra, rb: (pl.multiple_of(rb[j], SUBLANES), 0)),
                      pl.BlockSpec((1, D_MODEL), lambda j, ra, rb: (0, 0))],
            out_specs=[pl.BlockSpec((D_MODEL, LAT_PAD), lambda j, ra, rb: (0, 0)),
                       pl.BlockSpec((D_MODEL, PREP_COLS), lambda j, ra, rb: (0, j))],
        ),
        out_shape=[jax.ShapeDtypeStruct((D_MODEL, LAT_PAD), BF16),
                   jax.ShapeDtypeStruct((D_MODEL, WIDE), BF16)],
        compiler_params=pltpu.CompilerParams(dimension_semantics=("arbitrary",),
                                             vmem_limit_bytes=VMEM_LIMIT),
        name="mla_conv_w_in_prep",
    )(rows_a, rows_b, w_in.T, w_in.T, w_in.T, g_in.reshape(1, -1))


def _layer(x2, p2, posc, invf, batch, seq, g_in, w_in, g_cq, w_uq, g_ckv, w_ukv, g_q, g_k,
           conv_w, g_oa, g_oc, w_o, w_pl, w_plg, g_pl):
    tokens = batch * seq
    tm = PROJ_TILE
    assert seq % tm == 0 and seq % ATTN_TILE == 0

    w_lat, w_wide = _prepare_w_in(w_in, g_in)
    w_uq_p = jnp.pad(w_uq.reshape(Q_LORA, N_HEADS, QK_DIM),
                     ((0, 0), (0, 0), (0, HEAD_PAD - QK_DIM))).reshape(Q_LORA, N_HEADS * HEAD_PAD).astype(BF16)
    g_q_p = jnp.pad(g_q, (0, HEAD_PAD - QK_DIM)).reshape(1, HEAD_PAD)
    g_kn = g_k[:NOPE_DIM].reshape(1, NOPE_DIM)
    g_kr = jnp.pad(g_k[NOPE_DIM:], (0, KPE_PAD - ROPE_DIM)).reshape(1, KPE_PAD)

    cparams = pltpu.CompilerParams(dimension_semantics=("arbitrary",), vmem_limit_bytes=VMEM_LIMIT)

    def col_spec(rows, tile):
        return pl.BlockSpec((rows, tile), lambda i: (0, i))

    qt, k, vt, ga, yc = pl.pallas_call(
        functools.partial(_proj_kernel, seq // tm),
        grid=(tokens // tm,),
        in_specs=[
            _row_spec(tm, D_MODEL), _row_spec(tm // TOKENS_PER_ROW, LANES), _const_spec((1, LANES)),
            _const_spec((D_MODEL, LAT_PAD)), _const_spec((D_MODEL, WIDE)),
            _const_spec((1, Q_LORA)), _const_spec((Q_LORA, N_HEADS * HEAD_PAD)),
            _const_spec((1, KV_LORA)), _const_spec((KV_LORA, N_HEADS * (NOPE_DIM + V_DIM))),
            _const_spec((1, HEAD_PAD)), _const_spec((1, NOPE_DIM)), _const_spec((1, KPE_PAD)),
            _const_spec((CONV_K, CONV_WIDTH)), _const_spec((1, CONV_WIDTH)),
        ],
        out_specs=[
            col_spec(N_HEADS * HEAD_PAD, tm), _row_spec(tm, N_HEADS * HEAD_PAD),
            col_spec(N_HEADS * V_ROWS, tm), _row_spec(tm, ATTN_WIDTH), _row_spec(tm, CONV_WIDTH),
        ],
        out_shape=[
            jax.ShapeDtypeStruct((N_HEADS * HEAD_PAD, tokens), BF16),
            jax.ShapeDtypeStruct((tokens, N_HEADS * HEAD_PAD), BF16),
            jax.ShapeDtypeStruct((N_HEADS * V_ROWS, tokens), BF16),
            jax.ShapeDtypeStruct((tokens, ATTN_WIDTH), BF16),
            jax.ShapeDtypeStruct((tokens, CONV_WIDTH), BF16),
        ],
        scratch_shapes=[pltpu.VMEM((tm + SUBLANES, CONV_WIDTH), F32)],
        compiler_params=cparams,
        name="mla_conv_proj",
    )(x2, posc, invf, w_lat, w_wide, g_cq.reshape(1, -1), w_uq_p,
      g_ckv.reshape(1, -1), w_ukv.astype(BF16), g_q_p, g_kn, g_kr, conv_w, g_oc.reshape(1, -1))

    ta = ATTN_TILE
    sched = _triangle_schedule(batch, seq // ta)
    o = pl.pallas_call(
        _attn_kernel,
        grid_spec=pltpu.PrefetchScalarGridSpec(
            num_scalar_prefetch=len(sched),
            grid=(int(sched[0].shape[0]),),
            in_specs=[
                pl.BlockSpec((N_HEADS * HEAD_PAD, ta), lambda s, qn, kn, kc, qc, *_: (0, qn[s])),
                pl.BlockSpec((ta, N_HEADS * HEAD_PAD), lambda s, qn, kn, kc, qc, *_: (kn[s], 0)),
                pl.BlockSpec((N_HEADS * V_ROWS, ta), lambda s, qn, kn, kc, qc, *_: (0, kc[s])),
            ],
            out_specs=pl.BlockSpec((ta, ATTN_WIDTH), lambda s, qn, kn, kc, qc, *_: (qc[s], 0)),
            scratch_shapes=[pltpu.VMEM((N_HEADS, ta, ta), F32),
                            pltpu.VMEM((N_HEADS, 1, ta), F32),
                            pltpu.VMEM((N_HEADS, 1, ta), F32),
                            pltpu.VMEM((N_HEADS, V_ROWS, ta), F32)],
        ),
        out_shape=jax.ShapeDtypeStruct((tokens, ATTN_WIDTH), BF16),
        compiler_params=cparams,
        name="mla_flash_attn",
    )(*sched, qt, k, vt)

    return pl.pallas_call(
        _out_kernel,
        grid=(tokens // tm,),
        in_specs=[
            _row_spec(tm, ATTN_WIDTH), _row_spec(tm, ATTN_WIDTH), _row_spec(tm, CONV_WIDTH),
            _row_spec(tm, D_MODEL), _row_spec(tm, PLE_DIM),
            _const_spec((1, ATTN_WIDTH)), _const_spec((D_MODEL, D_MODEL)), _const_spec((1, D_MODEL)),
            _const_spec((D_MODEL, D_MODEL)), _const_spec((PLE_DIM, D_MODEL)),
        ],
        out_specs=_row_spec(tm, D_MODEL),
        out_shape=jax.ShapeDtypeStruct((tokens, D_MODEL), F32),
        compiler_params=cparams,
        name="mla_conv_out",
    )(o, ga, yc, x2, p2, g_oa.reshape(1, -1), w_o.astype(BF16),
      g_pl.reshape(1, -1), w_plg.astype(BF16), w_pl.astype(BF16))


def kernel(x, p, positions, g_in, w_in, g_cq, w_uq, g_ckv, w_ukv, g_q, g_k, conv_w, g_oa, g_oc,
           w_o, w_pl, w_plg, g_pl):
    batch, seq, d_model = x.shape
    depth = p.shape[0]
    tokens = batch * seq
    rows = PROJ_TILE // TOKENS_PER_ROW
    posc = jnp.repeat(positions.astype(F32).reshape(tokens // PROJ_TILE, TOKENS_PER_ROW, rows)
                      .transpose(0, 2, 1), HALF_ROPE, axis=-1).reshape(tokens // TOKENS_PER_ROW, LANES)
    inv_freq = 1.0 / (ROPE_THETA ** (jnp.arange(0, ROPE_DIM, 2, dtype=F32) / ROPE_DIM))
    invf = jnp.tile(inv_freq, LANES // HALF_ROPE).reshape(1, LANES)
    h = x.reshape(tokens, d_model)
    for i in range(depth):
        h = _layer(h, p[i].reshape(tokens, PLE_DIM), posc, invf, batch, seq,
                   g_in[i], w_in[i], g_cq[i], w_uq[i], g_ckv[i], w_ukv[i], g_q[i], g_k[i],
                   conv_w[i], g_oa[i], g_oc[i], w_o[i], w_pl[i], w_plg[i], g_pl[i])
    return h.reshape(batch, seq, d_model).astype(x.dtype)
```

```python
import functools
import math

import jax
import jax.numpy as jnp
import numpy as np
from jax import lax
from jax.experimental import pallas as pl
from jax.experimental.pallas import tpu as pltpu

D_MODEL = 1024
PLE_DIM = 256
N_HEADS = 4
NOPE_DIM = 128
ROPE_DIM = 64
HALF_ROPE = ROPE_DIM // 2
V_DIM = 128
QK_DIM = NOPE_DIM + ROPE_DIM
Q_LORA = 256
KV_LORA = 128
ATTN_WIDTH = N_HEADS * V_DIM
CONV_WIDTH = D_MODEL - ATTN_WIDTH
CONV_K = 3
ROPE_THETA = 10000.0
RMS_EPS = 1e-6
NEG_INF = -1e30

LANES = 128
SUBLANES = 8
HEAD_PAD = 2 * LANES
KPE_PAD = LANES
BF16_ROWS = 16
V_ROWS = V_DIM + BF16_ROWS
TOKENS_PER_ROW = LANES // HALF_ROPE
OFF_CQ = 0
OFF_CKV = OFF_CQ + Q_LORA
OFF_KPE = OFF_CKV + KV_LORA
LAT_SRC = OFF_KPE + ROPE_DIM
LAT_PAD = OFF_KPE + KPE_PAD
SRC_ZA = 0
SRC_CB = SRC_ZA + ATTN_WIDTH
SRC_CC = SRC_CB + CONV_WIDTH
SRC_CX = SRC_CC + CONV_WIDTH
SRC_ZC = SRC_CX + CONV_WIDTH
WIDE = SRC_ZC + CONV_WIDTH
PREP_COLS = 512
PREP_HALF = PREP_COLS // 2
PREP_PAIRS = ((SRC_ZA, SRC_ZA + PREP_HALF), (SRC_CC, SRC_CX), (SRC_CC + PREP_HALF, SRC_CX + PREP_HALF),
              (SRC_CB, SRC_ZC), (SRC_CB + PREP_HALF, SRC_ZC + PREP_HALF))
OFF_ZA = 0
OFF_CCX = OFF_ZA + PREP_COLS
OFF_CBZ = OFF_CCX + 2 * PREP_COLS

PROJ_TILE = 1024
PROJ_ROWS = PROJ_TILE // TOKENS_PER_ROW
OUT_ROWS = 256
ATTN_TILE = 1024
ATTN_Q_CHUNK = 256
VMEM_LIMIT = 48 * 1024 * 1024

BF16 = jnp.bfloat16
F32 = jnp.float32


def _rms_scale(v, width):
    return lax.rsqrt(jnp.sum(v * v, axis=-1, keepdims=True) * (1.0 / width) + RMS_EPS)


def _silu(z):
    h = 0.5 * z
    return h + h * jnp.tanh(h)


def _dot(a, b):
    return jnp.dot(a, b, preferred_element_type=F32)


def _rope_angles(posc, invf):
    ang = posc * invf
    return jnp.cos(ang), jnp.sin(ang)


def _rope_quarter(cosc, sinc, a):
    lane = lax.broadcasted_iota(jnp.int32, (1, LANES), 1)
    lo = lane < HALF_ROPE
    mid = (lane >= HALF_ROPE) & (lane < ROPE_DIM)
    c = pltpu.roll(cosc, LANES - HALF_ROPE * a, 1) if a else cosc
    s = pltpu.roll(sinc, LANES - HALF_ROPE * a, 1) if a else sinc
    return (jnp.where(lo, c, jnp.where(mid, pltpu.roll(c, HALF_ROPE, 1), 0.0)),
            jnp.where(lo, -s, jnp.where(mid, pltpu.roll(s, HALF_ROPE, 1), 0.0)), lo)


def _rope(t, c, s, lo):
    swapped = jnp.where(lo, pltpu.roll(t, LANES - HALF_ROPE, 1), pltpu.roll(t, HALF_ROPE, 1))
    return t * c + swapped * s


def _proj_kernel(tiles_per_seq,
                 x_ref, posc_ref, invf_ref, w_lat_ref, w_in_ref, g_cq_ref, w_uq_ref,
                 g_ckv_ref, w_ukv_ref, g_q_ref, g_kn_ref, g_kr_ref, conv_w_ref, g_oc_ref,
                 qt_ref, k_ref, vt_ref, ga_ref, yc_ref, carry_ref):
    tm = x_ref.shape[0]
    quarter_rows = tm // TOKENS_PER_ROW
    tables = []
    q_scale = math.log2(math.e) / math.sqrt(QK_DIM)

    @pl.when(pl.program_id(0) % tiles_per_seq == 0)
    def _():
        carry_ref[0:SUBLANES, :] = jnp.zeros((SUBLANES, CONV_WIDTH), F32)

    def rows_block(r0, nr):
        x = x_ref[r0:r0 + nr, :]
        h = (x * _rms_scale(x, D_MODEL)).astype(BF16)

        def proj(off, width):
            return _dot(h, w_in_ref[:, off:off + width])

        def tile_pairs(r):
            return [(r[:, i:i + LANES], r[:, i + LANES:i + 2 * LANES]) for i in range(0, r.shape[1], 2 * LANES)]

        lat = _dot(h, w_lat_ref[...])
        c_q = lat[:, OFF_CQ:OFF_CKV]
        c_kv = lat[:, OFF_CKV:OFF_KPE]
        kpe = lat[:, OFF_KPE:LAT_PAD]
        cv =jnp.concatenate([c * xin for c, xin in tile_pairs(proj(OFF_CCX, 2 * CONV_WIDTH))], axis=1)
        cqn = (c_q * _rms_scale(c_q, Q_LORA) * g_cq_ref[...]).astype(BF16)
        qf = _dot(cqn, w_uq_ref[...])
        gated_b = jnp.concatenate([b * _silu(z) for b, z in tile_pairs(proj(OFF_CBZ, 2 * CONV_WIDTH))], axis=1)
        ckvn = (c_kv * _rms_scale(c_kv, KV_LORA) * g_ckv_ref[...]).astype(BF16)
        kv = _dot(ckvn, w_ukv_ref[...])
        z_a = proj(OFF_ZA, ATTN_WIDTH)

        if not tables:
            tables.extend(_rope_angles(posc_ref[...], invf_ref[...]))
        assert nr == quarter_rows
        cos_t, sin_t, lo = _rope_quarter(*tables, r0 // quarter_rows)

        for hd in range(N_HEADS):
            qh = qf[:, hd * HEAD_PAD:(hd + 1) * HEAD_PAD]
            qn = qh * (_rms_scale(qh, QK_DIM) * q_scale) * g_q_ref[...]
            qt_ref[hd * HEAD_PAD:hd * HEAD_PAD + LANES, r0:r0 + nr] = qn[:, :LANES].astype(BF16).T
            qt_ref[hd * HEAD_PAD + LANES:(hd + 1) * HEAD_PAD, r0:r0 + nr] = (
                _rope(qn[:, LANES:], cos_t, sin_t, lo).astype(BF16).T)

        ss_kpe = jnp.sum(kpe * kpe, axis=-1, keepdims=True)
        k_rot = _rope(kpe * g_kr_ref[...], cos_t, sin_t, lo)
        for hd in range(N_HEADS):
            kn = kv[:, hd * (NOPE_DIM + V_DIM):hd * (NOPE_DIM + V_DIM) + NOPE_DIM]
            vh = kv[:, hd * (NOPE_DIM + V_DIM) + NOPE_DIM:(hd + 1) * (NOPE_DIM + V_DIM)]
            ss = jnp.sum(kn * kn, axis=-1, keepdims=True) + ss_kpe
            rs = lax.rsqrt(ss * (1.0 / QK_DIM) + RMS_EPS)
            k_ref[r0:r0 + nr, hd * HEAD_PAD:hd * HEAD_PAD + LANES] = (kn * rs * g_kn_ref[...]).astype(BF16)
            k_ref[r0:r0 + nr, hd * HEAD_PAD + LANES:(hd + 1) * HEAD_PAD] = (k_rot * rs).astype(BF16)
            vt_ref[hd * V_ROWS:hd * V_ROWS + V_DIM, r0:r0 + nr] = vh.astype(BF16).T
            vt_ref[hd * V_ROWS + V_DIM:(hd + 1) * V_ROWS, r0:r0 + nr] = jnp.ones((BF16_ROWS, nr), BF16)

        base = SUBLANES + r0
        prev = carry_ref[base - SUBLANES:base, :]
        carry_ref[base + nr - SUBLANES:base + nr, :] = cv[nr - SUBLANES:, :]
        row = lax.broadcasted_iota(jnp.int32, (SUBLANES, CONV_WIDTH), 0)

        def delayed(d):
            sh = pltpu.roll(cv, d, 0)
            head = jnp.where(row < d, pltpu.roll(prev, d, 0), sh[0:SUBLANES])
            return jnp.concatenate([head, sh[SUBLANES:]], axis=0)

        u = (conv_w_ref[2:3, :] * cv + conv_w_ref[1:2, :] * delayed(1) + conv_w_ref[0:1, :] * delayed(2))
        yc = gated_b * u
        yc_ref[r0:r0 + nr, :] = (yc * _rms_scale(yc, CONV_WIDTH) * g_oc_ref[...]).astype(BF16)

        for i, (za_lo, za_hi) in enumerate(tile_pairs(z_a)):
            ga_ref[r0:r0 + nr, i * LANES:(i + 1) * LANES] = _silu(za_lo).astype(BF16)
            ga_ref[r0:r0 + nr, PREP_HALF + i * LANES:PREP_HALF + (i + 1) * LANES] = _silu(za_hi).astype(BF16)

    for blk in range(tm // PROJ_ROWS):
        rows_block(blk * PROJ_ROWS, PROJ_ROWS)
    carry_ref[0:SUBLANES, :] = carry_ref[tm:tm + SUBLANES, :]


def _attn_kernel(qn_ref, kn_ref, kc_ref, qc_ref, diag_n_ref, first_c_ref, last_c_ref,
                 qt_ref, k_ref, vt_ref, o_ref, s_ref, mx_ref, m_ref, acc_ref):
    step = pl.program_id(0)
    tq = qt_ref.shape[1]
    tk = k_ref.shape[0]

    @pl.when(step == 0)
    def _():
        s_ref[...] = jnp.zeros(s_ref.shape, F32)
        mx_ref[...] = jnp.zeros(mx_ref.shape, F32)
        acc_ref[...] = jnp.zeros(acc_ref.shape, F32)

    @pl.when((first_c_ref[step] == 1) | (step == 0))
    def _():
        m_ref[...] = jnp.full(m_ref.shape, NEG_INF, F32)

    chunk = ATTN_Q_CHUNK
    nchunk = tq // chunk

    def body(diag_cur, diag_next):
        if diag_next:
            tri = (lax.broadcasted_iota(jnp.int32, (chunk, chunk), 0)
                   <= lax.broadcasted_iota(jnp.int32, (chunk, chunk), 1))

        def scores(hd, c):
            cols = slice(c * chunk, (c + 1) * chunk)
            rows = chunk * (c + 1) if diag_next else tk
            s = _dot(k_ref[0:rows, hd * HEAD_PAD:(hd + 1) * HEAD_PAD],
                     qt_ref[hd * HEAD_PAD:(hd + 1) * HEAD_PAD, cols])
            if not diag_next:
                s_ref[hd, :, cols] = s
                mx_ref[hd, :, cols] = jnp.max(s, axis=0, keepdims=True)
                return
            last = jnp.where(tri, s[rows - chunk:rows], NEG_INF)
            s_ref[hd, rows - chunk:rows, cols] = last
            mx = jnp.max(last, axis=0, keepdims=True)
            if rows > chunk:
                s_ref[hd, 0:rows - chunk, cols] = s[0:rows - chunk]
                mx = jnp.maximum(mx, jnp.max(s[0:rows - chunk], axis=0, keepdims=True))
            mx_ref[hd, :, cols] = mx

        def probabilities(hd, c):
            cols = slice(c * chunk, (c + 1) * chunk)
            rows = chunk * (c + 1) if diag_cur else tk
            m_prev = m_ref[hd, :, cols]
            m_new = jnp.maximum(m_prev, mx_ref[hd, :, cols])
            m_ref[hd, :, cols] = m_new
            return jnp.exp2(m_prev - m_new), jnp.exp2((s_ref[hd, 0:rows, cols] - m_new).astype(BF16))

        def accumulate(hd, c, alpha, p):
            cols = slice(c * chunk, (c + 1) * chunk)
            pv = _dot(vt_ref[hd * V_ROWS:(hd + 1) * V_ROWS, 0:p.shape[0]], p)
            acc_ref[hd, :, cols] = alpha * acc_ref[hd, :, cols] + pv

        for hd in range(N_HEADS):
            for c in range(nchunk):
                alpha, p = probabilities(hd, c)
                scores(hd, c)
                accumulate(hd, c, alpha, p)
            if diag_cur:
                acc = acc_ref[hd]
                o_ref[:, hd * V_DIM:(hd + 1) * V_DIM] = (
                    acc[:V_DIM] / acc[V_DIM:V_DIM + 1]).T.astype(o_ref.dtype)

    diag_cur = last_c_ref[step] == 1
    diag_next = diag_n_ref[step] == 1
    for cur_flag in (False, True):
        for next_flag in (False, True):
            @pl.when((diag_cur == cur_flag) & (diag_next == next_flag))
            def _(cur_flag=cur_flag, next_flag=next_flag):
                body(cur_flag, next_flag)


def _out_kernel(o_ref, ga_ref, yc_ref, x_ref, p_ref, g_oa_ref, w_o_ref, g_pl_ref, w_plg_ref, w_pl_ref,
                out_ref):
    tm = x_ref.shape[0]

    def residual(r0):
        rows = slice(r0, r0 + OUT_ROWS)
        ya = o_ref[rows, :].astype(F32) * ga_ref[rows, :].astype(F32)
        ya_n = (ya * _rms_scale(ya, ATTN_WIDTH) * g_oa_ref[...]).astype(BF16)
        y = jnp.concatenate([ya_n, yc_ref[rows, :]], axis=1)
        x1 = x_ref[rows, :] + _dot(y, w_o_ref[...])
        out_ref[rows, :] = x1
        hn = (x1 * _rms_scale(x1, D_MODEL) * g_pl_ref[...]).astype(BF16)
        return r0, hn

    def gated(r0, hn):
        rows = slice(r0, r0 + OUT_ROWS)
        gate = 1.0 / (1.0 + jnp.exp(-_dot(hn, w_plg_ref[...])))
        ple = _dot(p_ref[rows, :].astype(BF16), w_pl_ref[...])
        out_ref[rows, :] = out_ref[rows, :] + gate * ple

    prev = None
    for r0 in range(0, tm, OUT_ROWS):
        cur = residual(r0)
        if prev is not None:
            gated(*prev)
        prev = cur
    gated(*prev)


def _row_spec(tile, width):
    return pl.BlockSpec((tile, width), lambda i: (i, 0))


def _const_spec(shape):
    return pl.BlockSpec(shape, lambda i: (0,) * len(shape))


def _triangle_schedule(batch, n_blk):
    pairs = [(b * n_blk + i, b * n_blk + j, j == 0, j == i)
             for b in range(batch) for i in range(n_blk) for j in range(i + 1)]
    nxt = pairs + [pairs[-1]]
    cur = [pairs[0]] + pairs
    cols = ([p[0] for p in nxt], [p[1] for p in nxt], [p[1] for p in cur], [p[0] for p in cur],
            [int(p[3]) for p in nxt],
            [0] + [int(p[2]) for p in pairs], [0] + [int(p[3]) for p in pairs])
    return tuple(jnp.asarray(np.asarray(c, np.int32)) for c in cols)


def _prep_in_kernel(rows_a_ref, rows_b_ref, wt_lat_ref, wt_a_ref, wt_b_ref, g_ref, lat_ref, wide_ref):
    g = g_ref[...]

    @pl.when(pl.program_id(0) == 0)
    def _():
        pad = jnp.zeros((LAT_PAD - LAT_SRC, D_MODEL), F32)
        lat_ref[...] = jnp.concatenate([wt_lat_ref[...] * g, pad], axis=0).T.astype(BF16)

    a = (wt_a_ref[...] * g).T.astype(BF16)
    b = (wt_b_ref[...] * g).T.astype(BF16)
    for i in range(PREP_HALF // LANES):
        wide_ref[:, 2 * i * LANES:(2 * i + 1) * LANES] = a[:, i * LANES:(i + 1) * LANES]
        wide_ref[:, (2 * i + 1) * LANES:(2 * i + 2) * LANES] = b[:, i * LANES:(i + 1) * LANES]


def _prepare_w_in(w_in, g_in):
    def rows_at(nrows, start):
        return pl.BlockSpec((pl.Element(nrows), pl.Element(D_MODEL)), start)

    rows_a = jnp.asarray(np.asarray([LAT_SRC + a for a, _ in PREP_PAIRS], np.int32))
    rows_b = jnp.asarray(np.asarray([LAT_SRC + b for _, b in PREP_PAIRS], np.int32))
    return pl.pallas_call(
        _prep_in_kernel,
        grid_spec=pltpu.PrefetchScalarGridSpec(
            num_scalar_prefetch=2,
            grid=(len(PREP_PAIRS),),
            in_specs=[rows_at(LAT_SRC, lambda j, ra, rb: (0, 0)),
                      rows_at(PREP_HALF, lambda j, ra, rb: (pl.multiple_of(ra[j], SUBLANES), 0)),
                      rows_at(PREP_HALF, lambda j, ra, rb: (pl.multiple_of(rb[j], SUBLANES), 0)),
                      pl.BlockSpec((1, D_MODEL), lambda j, ra, rb: (0, 0))],
            out_specs=[pl.BlockSpec((D_MODEL, LAT_PAD), lambda j, ra, rb: (0, 0)),
                       pl.BlockSpec((D_MODEL, PREP_COLS), lambda j, ra, rb: (0, j))],
        ),
        out_shape=[jax.ShapeDtypeStruct((D_MODEL, LAT_PAD), BF16),
                   jax.ShapeDtypeStruct((D_MODEL, WIDE), BF16)],
        compiler_params=pltpu.CompilerParams(dimension_semantics=("arbitrary",),
                                             vmem_limit_bytes=VMEM_LIMIT),
        name="mla_conv_w_in_prep",
    )(rows_a, rows_b, w_in.T, w_in.T, w_in.T, g_in.reshape(1, -1))


def _layer(x2, p2, posc, invf, batch, seq, g_in, w_in, g_cq, w_uq, g_ckv, w_ukv, g_q, g_k,
           conv_w, g_oa, g_oc, w_o, w_pl, w_plg, g_pl):
    tokens = batch * seq
    tm = PROJ_TILE
    assert seq % tm == 0 and seq % ATTN_TILE == 0

    w_lat, w_wide = _prepare_w_in(w_in, g_in)
    w_uq_p = jnp.pad(w_uq.reshape(Q_LORA, N_HEADS, QK_DIM),
                     ((0, 0), (0, 0), (0, HEAD_PAD - QK_DIM))).reshape(Q_LORA, N_HEADS * HEAD_PAD).astype(BF16)
    g_q_p = jnp.pad(g_q, (0, HEAD_PAD - QK_DIM)).reshape(1, HEAD_PAD)
    g_kn = g_k[:NOPE_DIM].reshape(1, NOPE_DIM)
    g_kr = jnp.pad(g_k[NOPE_DIM:], (0, KPE_PAD - ROPE_DIM)).reshape(1, KPE_PAD)

    cparams = pltpu.CompilerParams(dimension_semantics=("arbitrary",), vmem_limit_bytes=VMEM_LIMIT)

    def col_spec(rows, tile):
        return pl.BlockSpec((rows, tile), lambda i: (0, i))

    qt, k, vt, ga, yc = pl.pallas_call(
        functools.partial(_proj_kernel, seq // tm),
        grid=(tokens // tm,),
        in_specs=[
            _row_spec(tm, D_MODEL), _row_spec(tm // TOKENS_PER_ROW, LANES), _const_spec((1, LANES)),
            _const_spec((D_MODEL, LAT_PAD)), _const_spec((D_MODEL, WIDE)),
            _const_spec((1, Q_LORA)), _const_spec((Q_LORA, N_HEADS * HEAD_PAD)),
            _const_spec((1, KV_LORA)), _const_spec((KV_LORA, N_HEADS * (NOPE_DIM + V_DIM))),
            _const_spec((1, HEAD_PAD)), _const_spec((1, NOPE_DIM)), _const_spec((1, KPE_PAD)),
            _const_spec((CONV_K, CONV_WIDTH)), _const_spec((1, CONV_WIDTH)),
        ],
        out_specs=[
            col_spec(N_HEADS * HEAD_PAD, tm), _row_spec(tm, N_HEADS * HEAD_PAD),
            col_spec(N_HEADS * V_ROWS, tm), _row_spec(tm, ATTN_WIDTH), _row_spec(tm, CONV_WIDTH),
        ],
        out_shape=[
            jax.ShapeDtypeStruct((N_HEADS * HEAD_PAD, tokens), BF16),
            jax.ShapeDtypeStruct((tokens, N_HEADS * HEAD_PAD), BF16),
            jax.ShapeDtypeStruct((N_HEADS * V_ROWS, tokens), BF16),
            jax.ShapeDtypeStruct((tokens, ATTN_WIDTH), BF16),
            jax.ShapeDtypeStruct((tokens, CONV_WIDTH), BF16),
        ],
        scratch_shapes=[pltpu.VMEM((tm + SUBLANES, CONV_WIDTH), F32)],
        compiler_params=cparams,
        name="mla_conv_proj",
    )(x2, posc, invf, w_lat, w_wide, g_cq.reshape(1, -1), w_uq_p,
      g_ckv.reshape(1, -1), w_ukv.astype(BF16), g_q_p, g_kn, g_kr, conv_w, g_oc.reshape(1, -1))

    ta = ATTN_TILE
    sched = _triangle_schedule(batch, seq // ta)
    o = pl.pallas_call(
        _attn_kernel,
        grid_spec=pltpu.PrefetchScalarGridSpec(
            num_scalar_prefetch=len(sched),
            grid=(int(sched[0].shape[0]),),
            in_specs=[
                pl.BlockSpec((N_HEADS * HEAD_PAD, ta), lambda s, qn, kn, kc, qc, *_: (0, qn[s])),
                pl.BlockSpec((ta, N_HEADS * HEAD_PAD), lambda s, qn, kn, kc, qc, *_: (kn[s], 0)),
                pl.BlockSpec((N_HEADS * V_ROWS, ta), lambda s, qn, kn, kc, qc, *_: (0, kc[s])),
            ],
            out_specs=pl.BlockSpec((ta, ATTN_WIDTH), lambda s, qn, kn, kc, qc, *_: (qc[s], 0)),
            scratch_shapes=[pltpu.VMEM((N_HEADS, ta, ta), F32),
                            pltpu.VMEM((N_HEADS, 1, ta), F32),
                            pltpu.VMEM((N_HEADS, 1, ta), F32),
                            pltpu.VMEM((N_HEADS, V_ROWS, ta), F32)],
        ),
        out_shape=jax.ShapeDtypeStruct((tokens, ATTN_WIDTH), BF16),
        compiler_params=cparams,
        name="mla_flash_attn",
    )(*sched, qt, k, vt)

    out_in_specs = [
        _row_spec(tm, ATTN_WIDTH), _row_spec(tm, ATTN_WIDTH), _row_spec(tm, CONV_WIDTH),
        _row_spec(tm, D_MODEL), _row_spec(tm, PLE_DIM),
        _const_spec((1, ATTN_WIDTH)), _const_spec((D_MODEL, D_MODEL)), _const_spec((1, D_MODEL)),
        _const_spec((D_MODEL, D_MODEL)), _const_spec((PLE_DIM, D_MODEL)),
    ]

    def out_body(*refs):
        pltpu.emit_pipeline(_out_kernel, grid=(tokens // tm,), in_specs=out_in_specs,
                            out_specs=[_row_spec(tm, D_MODEL)])(*refs)

    return pl.pallas_call(
        out_body,
        in_specs=[pl.BlockSpec(memory_space=pl.ANY)] * len(out_in_specs),
        out_specs=pl.BlockSpec(memory_space=pl.ANY),
        out_shape=jax.ShapeDtypeStruct((tokens, D_MODEL), F32),
        compiler_params=pltpu.CompilerParams(vmem_limit_bytes=VMEM_LIMIT),
        name="mla_conv_out",
    )(o, ga, yc, x2, p2, g_oa.reshape(1, -1), w_o.astype(BF16),
      g_pl.reshape(1, -1), w_plg.astype(BF16), w_pl.astype(BF16))


def kernel(x, p, positions, g_in, w_in, g_cq, w_uq, g_ckv, w_ukv, g_q, g_k, conv_w, g_oa, g_oc,
           w_o, w_pl, w_plg, g_pl):
    batch, seq, d_model = x.shape
    depth = p.shape[0]
    tokens = batch * seq
    rows = PROJ_TILE // TOKENS_PER_ROW
    posc = jnp.repeat(positions.astype(F32).reshape(tokens // PROJ_TILE, TOKENS_PER_ROW, rows)
                      .transpose(0, 2, 1), HALF_ROPE, axis=-1).reshape(tokens // TOKENS_PER_ROW, LANES)
    inv_freq = 1.0 / (ROPE_THETA ** (jnp.arange(0, ROPE_DIM, 2, dtype=F32) / ROPE_DIM))
    invf = jnp.tile(inv_freq, LANES // HALF_ROPE).reshape(1, LANES)
    h = x.reshape(tokens, d_model)
    for i in range(depth):
        h = _layer(h, p[i].reshape(tokens, PLE_DIM), posc, invf, batch, seq,
                   g_in[i], w_in[i], g_cq[i], w_uq[i], g_ckv[i], w_ukv[i], g_q[i], g_k[i],
                   conv_w[i], g_oa[i], g_oc[i], w_o[i], w_pl[i], w_plg[i], g_pl[i])
    return h.reshape(batch, seq, d_model).astype(x.dtype)
```

```python
import functools
import math

import jax
import jax.numpy as jnp
import numpy as np
from jax import lax
from jax.experimental import pallas as pl
from jax.experimental.pallas import tpu as pltpu

D_MODEL = 1024
PLE_DIM = 256
N_HEADS = 4
NOPE_DIM = 128
ROPE_DIM = 64
HALF_ROPE = ROPE_DIM // 2
V_DIM = 128
QK_DIM = NOPE_DIM + ROPE_DIM
Q_LORA = 256
KV_LORA = 128
ATTN_WIDTH = N_HEADS * V_DIM
CONV_WIDTH = D_MODEL - ATTN_WIDTH
CONV_K = 3
ROPE_THETA = 10000.0
RMS_EPS = 1e-6
NEG_INF = -1e30

LANES = 128
SUBLANES = 8
HEAD_PAD = 2 * LANES
KPE_PAD = LANES
BF16_ROWS = 16
V_ROWS = V_DIM + BF16_ROWS
TOKENS_PER_ROW = LANES // HALF_ROPE
OFF_CQ = 0
OFF_CKV = OFF_CQ + Q_LORA
OFF_KPE = OFF_CKV + KV_LORA
LAT_SRC = OFF_KPE + ROPE_DIM
LAT_PAD = OFF_KPE + KPE_PAD
SRC_ZA = 0
SRC_CB = SRC_ZA + ATTN_WIDTH
SRC_CC = SRC_CB + CONV_WIDTH
SRC_CX = SRC_CC + CONV_WIDTH
SRC_ZC = SRC_CX + CONV_WIDTH
WIDE = SRC_ZC + CONV_WIDTH
PREP_COLS = 512
PREP_HALF = PREP_COLS // 2
PREP_PAIRS = ((SRC_ZA, SRC_ZA + PREP_HALF), (SRC_CC, SRC_CX), (SRC_CC + PREP_HALF, SRC_CX + PREP_HALF),
              (SRC_CB, SRC_ZC), (SRC_CB + PREP_HALF, SRC_ZC + PREP_HALF))
OFF_ZA = 0
OFF_CCX = OFF_ZA + PREP_COLS
OFF_CBZ = OFF_CCX + 2 * PREP_COLS

PROJ_TILE = 1024
PROJ_ROWS = PROJ_TILE // TOKENS_PER_ROW
OUT_ROWS = 256
ATTN_TILE = 1024
ATTN_Q_CHUNK = 256
VMEM_LIMIT = 48 * 1024 * 1024

BF16 = jnp.bfloat16
F32 = jnp.float32


def _rms_scale(v, width):
    return lax.rsqrt(jnp.sum(v * v, axis=-1, keepdims=True) * (1.0 / width) + RMS_EPS)


def _silu(z):
    h = 0.5 * z
    return h + h * jnp.tanh(h)


def _dot(a, b):
    return jnp.dot(a, b, preferred_element_type=F32)


def _rope_angles(posc, invf):
    ang = posc * invf
    return jnp.cos(ang), jnp.sin(ang)


def _rope_quarter(cosc, sinc, a):
    lane = lax.broadcasted_iota(jnp.int32, (1, LANES), 1)
    lo = lane < HALF_ROPE
    mid = (lane >= HALF_ROPE) & (lane < ROPE_DIM)
    c = pltpu.roll(cosc, LANES - HALF_ROPE * a, 1) if a else cosc
    s = pltpu.roll(sinc, LANES - HALF_ROPE * a, 1) if a else sinc
    return (jnp.where(lo, c, jnp.where(mid, pltpu.roll(c, HALF_ROPE, 1), 0.0)),
            jnp.where(lo, -s, jnp.where(mid, pltpu.roll(s, HALF_ROPE, 1), 0.0)), lo)


def _rope(t, c, s, lo):
    swapped = jnp.where(lo, pltpu.roll(t, LANES - HALF_ROPE, 1), pltpu.roll(t, HALF_ROPE, 1))
    return t * c + swapped * s


def _proj_kernel(tiles_per_seq,
                 x_ref, posc_ref, invf_ref, w_lat_ref, w_in_ref, g_cq_ref, w_uq_ref,
                 g_ckv_ref, w_ukv_ref, g_q_ref, g_kn_ref, g_kr_ref, conv_w_ref, g_oc_ref,
                 qt_ref, k_ref, vt_ref, ga_ref, yc_ref, carry_ref):
    tm = x_ref.shape[0]
    quarter_rows = tm // TOKENS_PER_ROW
    tables = []
    q_scale = math.log2(math.e) / math.sqrt(QK_DIM)

    @pl.when(pl.program_id(0) % tiles_per_seq == 0)
    def _():
        carry_ref[0:SUBLANES, :] = jnp.zeros((SUBLANES, CONV_WIDTH), F32)

    def rows_block(r0, nr):
        x = x_ref[r0:r0 + nr, :]
        h = (x * _rms_scale(x, D_MODEL)).astype(BF16)

        def proj(off, width):
            return _dot(h, w_in_ref[:, off:off + width])

        def tile_pairs(r):
            return [(r[:, i:i + LANES], r[:, i + LANES:i + 2 * LANES]) for i in range(0, r.shape[1], 2 * LANES)]

        lat = _dot(h, w_lat_ref[...])
        c_q = lat[:, OFF_CQ:OFF_CKV]
        c_kv = lat[:, OFF_CKV:OFF_KPE]
        kpe = lat[:, OFF_KPE:LAT_PAD]
        cv =jnp.concatenate([c * xin for c, xin in tile_pairs(proj(OFF_CCX, 2 * CONV_WIDTH))], axis=1)
        cqn = (c_q * _rms_scale(c_q, Q_LORA) * g_cq_ref[...]).astype(BF16)
        qf = _dot(cqn, w_uq_ref[...])
        gated_b = jnp.concatenate([b * _silu(z) for b, z in tile_pairs(proj(OFF_CBZ, 2 * CONV_WIDTH))], axis=1)
        ckvn = (c_kv * _rms_scale(c_kv, KV_LORA) * g_ckv_ref[...]).astype(BF16)
        kv = _dot(ckvn, w_ukv_ref[...])
        z_a = proj(OFF_ZA, ATTN_WIDTH)

        if not tables:
            tables.extend(_rope_angles(posc_ref[...], invf_ref[...]))
        assert nr == quarter_rows
        cos_t, sin_t, lo = _rope_quarter(*tables, r0 // quarter_rows)

        for hd in range(N_HEADS):
            qh = qf[:, hd * HEAD_PAD:(hd + 1) * HEAD_PAD]
            qn = qh * (_rms_scale(qh, QK_DIM) * q_scale) * g_q_ref[...]
            qt_ref[hd * HEAD_PAD:hd * HEAD_PAD + LANES, r0:r0 + nr] = qn[:, :LANES].astype(BF16).T
            qt_ref[hd * HEAD_PAD + LANES:(hd + 1) * HEAD_PAD, r0:r0 + nr] = (
                _rope(qn[:, LANES:], cos_t, sin_t, lo).astype(BF16).T)

        ss_kpe = jnp.sum(kpe * kpe, axis=-1, keepdims=True)
        k_rot = _rope(kpe * g_kr_ref[...], cos_t, sin_t, lo)
        for hd in range(N_HEADS):
            kn = kv[:, hd * (NOPE_DIM + V_DIM):hd * (NOPE_DIM + V_DIM) + NOPE_DIM]
            vh = kv[:, hd * (NOPE_DIM + V_DIM) + NOPE_DIM:(hd + 1) * (NOPE_DIM + V_DIM)]
            ss = jnp.sum(kn * kn, axis=-1, keepdims=True) + ss_kpe
            rs = lax.rsqrt(ss * (1.0 / QK_DIM) + RMS_EPS)
            k_ref[r0:r0 + nr, hd * HEAD_PAD:hd * HEAD_PAD + LANES] = (kn * rs * g_kn_ref[...]).astype(BF16)
            k_ref[r0:r0 + nr, hd * HEAD_PAD + LANES:(hd + 1) * HEAD_PAD] = (k_rot * rs).astype(BF16)
            vt_ref[hd * V_ROWS:hd * V_ROWS + V_DIM, r0:r0 + nr] = vh.astype(BF16).T
            vt_ref[hd * V_ROWS + V_DIM:(hd + 1) * V_ROWS, r0:r0 + nr] = jnp.ones((BF16_ROWS, nr), BF16)

        base = SUBLANES + r0
        prev = carry_ref[base - SUBLANES:base, :]
        carry_ref[base + nr - SUBLANES:base + nr, :] = cv[nr - SUBLANES:, :]
        row = lax.broadcasted_iota(jnp.int32, (SUBLANES, CONV_WIDTH), 0)

        def delayed(d):
            sh = pltpu.roll(cv, d, 0)
            head = jnp.where(row < d, pltpu.roll(prev, d, 0), sh[0:SUBLANES])
            return jnp.concatenate([head, sh[SUBLANES:]], axis=0)

        u = (conv_w_ref[2:3, :] * cv + conv_w_ref[1:2, :] * delayed(1) + conv_w_ref[0:1, :] * delayed(2))
        yc = gated_b * u
        yc_ref[r0:r0 + nr, :] = (yc * _rms_scale(yc, CONV_WIDTH) * g_oc_ref[...]).astype(BF16)

        for i, (za_lo, za_hi) in enumerate(tile_pairs(z_a)):
            ga_ref[r0:r0 + nr, i * LANES:(i + 1) * LANES] = _silu(za_lo).astype(BF16)
            ga_ref[r0:r0 + nr, PREP_HALF + i * LANES:PREP_HALF + (i + 1) * LANES] = _silu(za_hi).astype(BF16)

    for blk in range(tm // PROJ_ROWS):
        rows_block(blk * PROJ_ROWS, PROJ_ROWS)
    carry_ref[0:SUBLANES, :] = carry_ref[tm:tm + SUBLANES, :]


def _attn_kernel(qn_ref, kn_ref, kc_ref, qc_ref, diag_n_ref, first_c_ref, last_c_ref,
                 qt_ref, k_ref, vt_ref, o_ref, s_ref, mx_ref, m_ref, acc_ref):
    step = pl.program_id(0)
    tq = qt_ref.shape[1]
    tk = k_ref.shape[0]

    @pl.when(step == 0)
    def _():
        s_ref[...] = jnp.zeros(s_ref.shape, F32)
        mx_ref[...] = jnp.zeros(mx_ref.shape, F32)
        acc_ref[...] = jnp.zeros(acc_ref.shape, F32)

    @pl.when((first_c_ref[step] == 1) | (step == 0))
    def _():
        m_ref[...] = jnp.full(m_ref.shape, NEG_INF, F32)

    chunk = ATTN_Q_CHUNK
    nchunk = tq // chunk

    def body(diag_cur, diag_next):
        if diag_next:
            tri = (lax.broadcasted_iota(jnp.int32, (chunk, chunk), 0)
                   <= lax.broadcasted_iota(jnp.int32, (chunk, chunk), 1))

        def scores(hd, c):
            cols = slice(c * chunk, (c + 1) * chunk)
            rows = chunk * (c + 1) if diag_next else tk
            s = _dot(k_ref[0:rows, hd * HEAD_PAD:(hd + 1) * HEAD_PAD],
                     qt_ref[hd * HEAD_PAD:(hd + 1) * HEAD_PAD, cols])
            if not diag_next:
                s_ref[hd, :, cols] = s
                mx_ref[hd, :, cols] = jnp.max(s, axis=0, keepdims=True)
                return
            last = jnp.where(tri, s[rows - chunk:rows], NEG_INF)
            s_ref[hd, rows - chunk:rows, cols] = last
            mx = jnp.max(last, axis=0, keepdims=True)
            if rows > chunk:
                s_ref[hd, 0:rows - chunk, cols] = s[0:rows - chunk]
                mx = jnp.maximum(mx, jnp.max(s[0:rows - chunk], axis=0, keepdims=True))
            mx_ref[hd, :, cols] = mx

        def probabilities(hd, c):
            cols = slice(c * chunk, (c + 1) * chunk)
            rows = chunk * (c + 1) if diag_cur else tk
            m_prev = m_ref[hd, :, cols]
            m_new = jnp.maximum(m_prev, mx_ref[hd, :, cols])
            m_ref[hd, :, cols] = m_new
            return jnp.exp2(m_prev - m_new), jnp.exp2((s_ref[hd, 0:rows, cols] - m_new).astype(BF16))

        def accumulate(hd, c, alpha, p):
            cols = slice(c * chunk, (c + 1) * chunk)
            pv = _dot(vt_ref[hd * V_ROWS:(hd + 1) * V_ROWS, 0:p.shape[0]], p)
            acc_ref[hd, :, cols] = alpha * acc_ref[hd, :, cols] + pv

        for hd in range(N_HEADS):
            for c in range(nchunk):
                alpha, p = probabilities(hd, c)
                scores(hd, c)
                accumulate(hd, c, alpha, p)
            if diag_cur:
                acc = acc_ref[hd]
                o_ref[:, hd * V_DIM:(hd + 1) * V_DIM] = (
                    acc[:V_DIM] / acc[V_DIM:V_DIM + 1]).T.astype(o_ref.dtype)

    diag_cur = last_c_ref[step] == 1
    diag_next = diag_n_ref[step] == 1
    for cur_flag in (False, True):
        for next_flag in (False, True):
            @pl.when((diag_cur == cur_flag) & (diag_next == next_flag))
            def _(cur_flag=cur_flag, next_flag=next_flag):
                body(cur_flag, next_flag)


def _out_kernel(o_ref, ga_ref, yc_ref, x_ref, p_ref, g_oa_ref, w_o_ref, g_pl_ref, w_plg_ref, w_pl_ref,
                out_ref):
    tm = x_ref.shape[0]

    def residual(r0):
        rows = slice(r0, r0 + OUT_ROWS)
        ya = o_ref[rows, :].astype(F32) * ga_ref[rows, :].astype(F32)
        ya_n = (ya * _rms_scale(ya, ATTN_WIDTH) * g_oa_ref[...]).astype(BF16)
        y = jnp.concatenate([ya_n, yc_ref[rows, :]], axis=1)
        x1 = x_ref[rows, :] + _dot(y, w_o_ref[...])
        out_ref[rows, :] = x1
        hn = (x1 * _rms_scale(x1, D_MODEL) * g_pl_ref[...]).astype(BF16)
        return r0, hn

    def gated(r0, hn):
        rows = slice(r0, r0 + OUT_ROWS)
        gate = 1.0 / (1.0 + jnp.exp(-_dot(hn, w_plg_ref[...])))
        ple = _dot(p_ref[rows, :].astype(BF16), w_pl_ref[...])
        out_ref[rows, :] = out_ref[rows, :] + gate * ple

    prev = None
    for r0 in range(0, tm, OUT_ROWS):
        cur = residual(r0)
        if prev is not None:
            gated(*prev)
        prev = cur
    gated(*prev)


def _row_spec(tile, width):
    return pl.BlockSpec((tile, width), lambda i: (i, 0))


def _const_spec(shape):
    return pl.BlockSpec(shape, lambda i: (0,) * len(shape))


def _triangle_schedule(batch, n_blk):
    pairs = [(b * n_blk + i, b * n_blk + j, j == 0, j == i)
             for b in range(batch) for i in range(n_blk) for j in range(i + 1)]
    nxt = pairs + [pairs[-1]]
    cur = [pairs[0]] + pairs
    cols = ([p[0] for p in nxt], [p[1] for p in nxt], [p[1] for p in cur], [p[0] for p in cur],
            [int(p[3]) for p in nxt],
            [0] + [int(p[2]) for p in pairs], [0] + [int(p[3]) for p in pairs])
    return tuple(jnp.asarray(np.asarray(c, np.int32)) for c in cols)


def _prep_in_kernel(rows_a_ref, rows_b_ref, wt_lat_ref, wt_a_ref, wt_b_ref, g_ref, lat_ref, wide_ref):
    g = g_ref[...]

    @pl.when(pl.program_id(0) == 0)
    def _():
        pad = jnp.zeros((LAT_PAD - LAT_SRC, D_MODEL), F32)
        lat_ref[...] = jnp.concatenate([wt_lat_ref[...] * g, pad], axis=0).T.astype(BF16)

    a = (wt_a_ref[...] * g).T.astype(BF16)
    b = (wt_b_ref[...] * g).T.astype(BF16)
    for i in range(PREP_HALF // LANES):
        wide_ref[:, 2 * i * LANES:(2 * i + 1) * LANES] = a[:, i * LANES:(i + 1) * LANES]
        wide_ref[:, (2 * i + 1) * LANES:(2 * i + 2) * LANES] = b[:, i * LANES:(i + 1) * LANES]


def _prepare_w_in(w_in, g_in):
    def rows_at(nrows, start):
        return pl.BlockSpec((pl.Element(nrows), pl.Element(D_MODEL)), start)

    rows_a = jnp.asarray(np.asarray([LAT_SRC + a for a, _ in PREP_PAIRS], np.int32))
    rows_b = jnp.asarray(np.asarray([LAT_SRC + b for _, b in PREP_PAIRS], np.int32))
    return pl.pallas_call(
        _prep_in_kernel,
        grid_spec=pltpu.PrefetchScalarGridSpec(
            num_scalar_prefetch=2,
            grid=(len(PREP_PAIRS),),
            in_specs=[rows_at(LAT_SRC, lambda j, ra, rb: (0, 0)),
                      rows_at(PREP_HALF, lambda j, ra, rb: (pl.multiple_of(ra[j], SUBLANES), 0)),
                      rows_at(PREP_HALF, lambda j, ra, rb: (pl.multiple_of(rb[j], SUBLANES), 0)),
                      pl.BlockSpec((1, D_MODEL), lambda j, ra, rb: (0, 0))],
            out_specs=[pl.BlockSpec((D_MODEL, LAT_PAD), lambda j, ra, rb: (0, 0)),
                       pl.BlockSpec((D_MODEL, PREP_COLS), lambda j, ra, rb: (0, j))],
        ),
        out_shape=[jax.ShapeDtypeStruct((D_MODEL, LAT_PAD), BF16),
                   jax.ShapeDtypeStruct((D_MODEL, WIDE), BF16)],
        compiler_params=pltpu.CompilerParams(dimension_semantics=("arbitrary",),
                                             vmem_limit_bytes=VMEM_LIMIT),
        name="mla_conv_w_in_prep",
    )(rows_a, rows_b, w_in.T, w_in.T, w_in.T, g_in.reshape(1, -1))


def _layer(x2, p2, posc, invf, batch, seq, g_in, w_in, g_cq, w_uq, g_ckv, w_ukv, g_q, g_k,
           conv_w, g_oa, g_oc, w_o, w_pl, w_plg, g_pl):
    tokens = batch * seq
    tm = PROJ_TILE
    assert seq % tm == 0 and seq % ATTN_TILE == 0

    w_lat, w_wide = _prepare_w_in(w_in, g_in)
    w_uq_p = jnp.pad(w_uq.reshape(Q_LORA, N_HEADS, QK_DIM),
                     ((0, 0), (0, 0), (0, HEAD_PAD - QK_DIM))).reshape(Q_LORA, N_HEADS * HEAD_PAD).astype(BF16)
    g_q_p = jnp.pad(g_q, (0, HEAD_PAD - QK_DIM)).reshape(1, HEAD_PAD)
    g_kn = g_k[:NOPE_DIM].reshape(1, NOPE_DIM)
    g_kr = jnp.pad(g_k[NOPE_DIM:], (0, KPE_PAD - ROPE_DIM)).reshape(1, KPE_PAD)

    cparams = pltpu.CompilerParams(dimension_semantics=("arbitrary",), vmem_limit_bytes=VMEM_LIMIT)

    def col_spec(rows, tile):
        return pl.BlockSpec((rows, tile), lambda i: (0, i))

    qt, k, vt, ga, yc = pl.pallas_call(
        functools.partial(_proj_kernel, seq // tm),
        grid=(tokens // tm,),
        in_specs=[
            _row_spec(tm, D_MODEL), _row_spec(tm // TOKENS_PER_ROW, LANES), _const_spec((1, LANES)),
            _const_spec((D_MODEL, LAT_PAD)), _const_spec((D_MODEL, WIDE)),
            _const_spec((1, Q_LORA)), _const_spec((Q_LORA, N_HEADS * HEAD_PAD)),
            _const_spec((1, KV_LORA)), _const_spec((KV_LORA, N_HEADS * (NOPE_DIM + V_DIM))),
            _const_spec((1, HEAD_PAD)), _const_spec((1, NOPE_DIM)), _const_spec((1, KPE_PAD)),
            _const_spec((CONV_K, CONV_WIDTH)), _const_spec((1, CONV_WIDTH)),
        ],
        out_specs=[
            col_spec(N_HEADS * HEAD_PAD, tm), _row_spec(tm, N_HEADS * HEAD_PAD),
            col_spec(N_HEADS * V_ROWS, tm), _row_spec(tm, ATTN_WIDTH), _row_spec(tm, CONV_WIDTH),
        ],
        out_shape=[
            jax.ShapeDtypeStruct((N_HEADS * HEAD_PAD, tokens), BF16),
            jax.ShapeDtypeStruct((tokens, N_HEADS * HEAD_PAD), BF16),
            jax.ShapeDtypeStruct((N_HEADS * V_ROWS, tokens), BF16),
            jax.ShapeDtypeStruct((tokens, ATTN_WIDTH), BF16),
            jax.ShapeDtypeStruct((tokens, CONV_WIDTH), BF16),
        ],
        scratch_shapes=[pltpu.VMEM((tm + SUBLANES, CONV_WIDTH), F32)],
        compiler_params=pltpu.CompilerParams(
            dimension_semantics=("arbitrary",), vmem_limit_bytes=VMEM_LIMIT,
            allow_input_fusion=[False, True, True, False, False] + [True] * 9),
        name="mla_conv_proj",
    )(x2, posc, invf, w_lat, w_wide, g_cq.reshape(1, -1), w_uq_p,
      g_ckv.reshape(1, -1), w_ukv.astype(BF16), g_q_p, g_kn, g_kr, conv_w, g_oc.reshape(1, -1))

    ta = ATTN_TILE
    sched = _triangle_schedule(batch, seq // ta)
    o = pl.pallas_call(
        _attn_kernel,
        grid_spec=pltpu.PrefetchScalarGridSpec(
            num_scalar_prefetch=len(sched),
            grid=(int(sched[0].shape[0]),),
            in_specs=[
                pl.BlockSpec((N_HEADS * HEAD_PAD, ta), lambda s, qn, kn, kc, qc, *_: (0, qn[s])),
                pl.BlockSpec((ta, N_HEADS * HEAD_PAD), lambda s, qn, kn, kc, qc, *_: (kn[s], 0)),
                pl.BlockSpec((N_HEADS * V_ROWS, ta), lambda s, qn, kn, kc, qc, *_: (0, kc[s])),
            ],
            out_specs=pl.BlockSpec((ta, ATTN_WIDTH), lambda s, qn, kn, kc, qc, *_: (qc[s], 0)),
            scratch_shapes=[pltpu.VMEM((N_HEADS, ta, ta), F32),
                            pltpu.VMEM((N_HEADS, 1, ta), F32),
                            pltpu.VMEM((N_HEADS, 1, ta), F32),
                            pltpu.VMEM((N_HEADS, V_ROWS, ta), F32)],
        ),
        out_shape=jax.ShapeDtypeStruct((tokens, ATTN_WIDTH), BF16),
        compiler_params=cparams,
        name="mla_flash_attn",
    )(*sched, qt, k, vt)

    return pl.pallas_call(
        _out_kernel,
        grid=(tokens // tm,),
        in_specs=[
            _row_spec(tm, ATTN_WIDTH), _row_spec(tm, ATTN_WIDTH), _row_spec(tm, CONV_WIDTH),
            _row_spec(tm, D_MODEL), _row_spec(tm, PLE_DIM),
            _const_spec((1, ATTN_WIDTH)), _const_spec((D_MODEL, D_MODEL)), _const_spec((1, D_MODEL)),
            _const_spec((D_MODEL, D_MODEL)), _const_spec((PLE_DIM, D_MODEL)),
        ],
        out_specs=_row_spec(tm, D_MODEL),
        out_shape=jax.ShapeDtypeStruct((tokens, D_MODEL), F32),
        compiler_params=pltpu.CompilerParams(
            dimension_semantics=("arbitrary",), vmem_limit_bytes=VMEM_LIMIT,
            allow_input_fusion=[False] * 5 + [True] * 5),
        name="mla_conv_out",
    )(o, ga, yc, x2, p2, g_oa.reshape(1, -1), w_o.astype(BF16),
      g_pl.reshape(1, -1), w_plg.astype(BF16), w_pl.astype(BF16))


def kernel(x, p, positions, g_in, w_in, g_cq, w_uq, g_ckv, w_ukv, g_q, g_k, conv_w, g_oa, g_oc,
           w_o, w_pl, w_plg, g_pl):
    batch, seq, d_model = x.shape
    depth = p.shape[0]
    tokens = batch * seq
    rows = PROJ_TILE // TOKENS_PER_ROW
    posc = jnp.repeat(positions.astype(F32).reshape(tokens // PROJ_TILE, TOKENS_PER_ROW, rows)
                      .transpose(0, 2, 1), HALF_ROPE, axis=-1).reshape(tokens // TOKENS_PER_ROW, LANES)
    inv_freq = 1.0 / (ROPE_THETA ** (jnp.arange(0, ROPE_DIM, 2, dtype=F32) / ROPE_DIM))
    invf = jnp.tile(inv_freq, LANES // HALF_ROPE).reshape(1, LANES)
    h = x.reshape(tokens, d_model)
    for i in range(depth):
        h = _layer(h, p[i].reshape(tokens, PLE_DIM), posc, invf, batch, seq,
                   g_in[i], w_in[i], g_cq[i], w_uq[i], g_ckv[i], w_ukv[i], g_q[i], g_k[i],
                   conv_w[i], g_oa[i], g_oc[i], w_o[i], w_pl[i], w_plg[i], g_pl[i])
    return h.reshape(batch, seq, d_model).astype(x.dtype)
```

```python
import functools
import math

import jax
import jax.numpy as jnp
import numpy as np
from jax import lax
from jax.experimental import pallas as pl
from jax.experimental.pallas import tpu as pltpu

D_MODEL = 1024
PLE_DIM = 256
N_HEADS = 4
NOPE_DIM = 128
ROPE_DIM = 64
HALF_ROPE = ROPE_DIM // 2
V_DIM = 128
QK_DIM = NOPE_DIM + ROPE_DIM
Q_LORA = 256
KV_LORA = 128
ATTN_WIDTH = N_HEADS * V_DIM
CONV_WIDTH = D_MODEL - ATTN_WIDTH
CONV_K = 3
ROPE_THETA = 10000.0
RMS_EPS = 1e-6
NEG_INF = -1e30

LANES = 128
SUBLANES = 8
HEAD_PAD = 2 * LANES
KPE_PAD = LANES
BF16_ROWS = 16
V_ROWS = V_DIM + BF16_ROWS
TOKENS_PER_ROW = LANES // HALF_ROPE
OFF_CQ = 0
OFF_CKV = OFF_CQ + Q_LORA
OFF_KPE = OFF_CKV + KV_LORA
LAT_SRC = OFF_KPE + ROPE_DIM
LAT_PAD = OFF_KPE + KPE_PAD
SRC_ZA = 0
SRC_CB = SRC_ZA + ATTN_WIDTH
SRC_CC = SRC_CB + CONV_WIDTH
SRC_CX = SRC_CC + CONV_WIDTH
SRC_ZC = SRC_CX + CONV_WIDTH
WIDE = SRC_ZC + CONV_WIDTH
PREP_COLS = 512
PREP_HALF = PREP_COLS // 2
PREP_PAIRS = ((SRC_ZA, SRC_ZA + PREP_HALF), (SRC_CC, SRC_CX), (SRC_CC + PREP_HALF, SRC_CX + PREP_HALF),
              (SRC_CB, SRC_ZC), (SRC_CB + PREP_HALF, SRC_ZC + PREP_HALF))
OFF_ZA = 0
OFF_CCX = OFF_ZA + PREP_COLS
OFF_CBZ = OFF_CCX + 2 * PREP_COLS

PROJ_TILE = 1024
PROJ_ROWS = PROJ_TILE // TOKENS_PER_ROW
OUT_ROWS = 256
ATTN_TILE = 1024
ATTN_Q_CHUNK = 256
VMEM_LIMIT = 48 * 1024 * 1024

BF16 = jnp.bfloat16
F32 = jnp.float32


def _rms_scale(v, width):
    return lax.rsqrt(jnp.sum(v * v, axis=-1, keepdims=True) * (1.0 / width) + RMS_EPS)


def _silu(z):
    h = 0.5 * z
    return h + h * jnp.tanh(h)


def _dot(a, b):
    return jnp.dot(a, b, preferred_element_type=F32)


def _rope_angles(posc, invf):
    ang = posc * invf
    return jnp.cos(ang), jnp.sin(ang)


def _rope_quarter(cosc, sinc, a):
    lane = lax.broadcasted_iota(jnp.int32, (1, LANES), 1)
    lo = lane < HALF_ROPE
    mid = (lane >= HALF_ROPE) & (lane < ROPE_DIM)
    c = pltpu.roll(cosc, LANES - HALF_ROPE * a, 1) if a else cosc
    s = pltpu.roll(sinc, LANES - HALF_ROPE * a, 1) if a else sinc
    return (jnp.where(lo, c, jnp.where(mid, pltpu.roll(c, HALF_ROPE, 1), 0.0)),
            jnp.where(lo, -s, jnp.where(mid, pltpu.roll(s, HALF_ROPE, 1), 0.0)), lo)


def _rope(t, c, s, lo):
    swapped = jnp.where(lo, pltpu.roll(t, LANES - HALF_ROPE, 1), pltpu.roll(t, HALF_ROPE, 1))
    return t * c + swapped * s


def _proj_kernel(tiles_per_seq,
                 x_ref, posc_ref, invf_ref, w_lat_ref, w_in_ref, g_cq_ref, w_uq_ref,
                 g_ckv_ref, w_ukv_ref, g_q_ref, g_kn_ref, g_kr_ref, conv_w_ref, g_oc_ref,
                 qt_ref, k_ref, vt_ref, ga_ref, yc_ref, carry_ref):
    tm = x_ref.shape[0]
    quarter_rows = tm // TOKENS_PER_ROW
    tables = []
    q_scale = math.log2(math.e) / math.sqrt(QK_DIM)

    @pl.when(pl.program_id(0) % tiles_per_seq == 0)
    def _():
        carry_ref[0:SUBLANES, :] = jnp.zeros((SUBLANES, CONV_WIDTH), F32)

    def rows_block(r0, nr):
        x = x_ref[r0:r0 + nr, :]
        h = (x * _rms_scale(x, D_MODEL)).astype(BF16)

        def proj(off, width):
            return _dot(h, w_in_ref[:, off:off + width])

        def tile_pairs(r):
            return [(r[:, i:i + LANES], r[:, i + LANES:i + 2 * LANES]) for i in range(0, r.shape[1], 2 * LANES)]

        lat = _dot(h, w_lat_ref[...])
        c_q = lat[:, OFF_CQ:OFF_CKV]
        c_kv = lat[:, OFF_CKV:OFF_KPE]
        kpe = lat[:, OFF_KPE:LAT_PAD]
        cv =jnp.concatenate([c * xin for c, xin in tile_pairs(proj(OFF_CCX, 2 * CONV_WIDTH))], axis=1)
        cqn = (c_q * _rms_scale(c_q, Q_LORA) * g_cq_ref[...]).astype(BF16)
        qf = _dot(cqn, w_uq_ref[...])
        gated_b = jnp.concatenate([b * _silu(z) for b, z in tile_pairs(proj(OFF_CBZ, 2 * CONV_WIDTH))], axis=1)
        ckvn = (c_kv * _rms_scale(c_kv, KV_LORA) * g_ckv_ref[...]).astype(BF16)
        kv = _dot(ckvn, w_ukv_ref[...])
        z_a = proj(OFF_ZA, ATTN_WIDTH)

        if not tables:
            tables.extend(_rope_angles(posc_ref[...], invf_ref[...]))
        assert nr == quarter_rows
        cos_t, sin_t, lo = _rope_quarter(*tables, r0 // quarter_rows)

        for hd in range(N_HEADS):
            qh = qf[:, hd * HEAD_PAD:(hd + 1) * HEAD_PAD]
            qn = qh * (_rms_scale(qh, QK_DIM) * q_scale) * g_q_ref[...]
            qt_ref[hd * HEAD_PAD:hd * HEAD_PAD + LANES, r0:r0 + nr] = qn[:, :LANES].astype(BF16).T
            qt_ref[hd * HEAD_PAD + LANES:(hd + 1) * HEAD_PAD, r0:r0 + nr] = (
                _rope(qn[:, LANES:], cos_t, sin_t, lo).astype(BF16).T)

        ss_kpe = jnp.sum(kpe * kpe, axis=-1, keepdims=True)
        k_rot = _rope(kpe * g_kr_ref[...], cos_t, sin_t, lo)
        for hd in range(N_HEADS):
            kn = kv[:, hd * (NOPE_DIM + V_DIM):hd * (NOPE_DIM + V_DIM) + NOPE_DIM]
            vh = kv[:, hd * (NOPE_DIM + V_DIM) + NOPE_DIM:(hd + 1) * (NOPE_DIM + V_DIM)]
            ss = jnp.sum(kn * kn, axis=-1, keepdims=True) + ss_kpe
            rs = lax.rsqrt(ss * (1.0 / QK_DIM) + RMS_EPS)
            k_ref[r0:r0 + nr, hd * HEAD_PAD:hd * HEAD_PAD + LANES] = (kn * rs * g_kn_ref[...]).astype(BF16)
            k_ref[r0:r0 + nr, hd * HEAD_PAD + LANES:(hd + 1) * HEAD_PAD] = (k_rot * rs).astype(BF16)
            vt_ref[hd * V_ROWS:hd * V_ROWS + V_DIM, r0:r0 + nr] = vh.astype(BF16).T
            vt_ref[hd * V_ROWS + V_DIM:(hd + 1) * V_ROWS, r0:r0 + nr] = jnp.ones((BF16_ROWS, nr), BF16)

        base = SUBLANES + r0
        prev = carry_ref[base - SUBLANES:base, :]
        carry_ref[base + nr - SUBLANES:base + nr, :] = cv[nr - SUBLANES:, :]
        row = lax.broadcasted_iota(jnp.int32, (SUBLANES, CONV_WIDTH), 0)

        def delayed(d):
            sh = pltpu.roll(cv, d, 0)
            head = jnp.where(row < d, pltpu.roll(prev, d, 0), sh[0:SUBLANES])
            return jnp.concatenate([head, sh[SUBLANES:]], axis=0)

        u = (conv_w_ref[2:3, :] * cv + conv_w_ref[1:2, :] * delayed(1) + conv_w_ref[0:1, :] * delayed(2))
        yc = gated_b * u
        yc_ref[r0:r0 + nr, :] = (yc * _rms_scale(yc, CONV_WIDTH) * g_oc_ref[...]).astype(BF16)

        for i, (za_lo, za_hi) in enumerate(tile_pairs(z_a)):
            ga_ref[r0:r0 + nr, i * LANES:(i + 1) * LANES] = _silu(za_lo).astype(BF16)
            ga_ref[r0:r0 + nr, PREP_HALF + i * LANES:PREP_HALF + (i + 1) * LANES] = _silu(za_hi).astype(BF16)

    for blk in range(tm // PROJ_ROWS):
        rows_block(blk * PROJ_ROWS, PROJ_ROWS)
    carry_ref[0:SUBLANES, :] = carry_ref[tm:tm + SUBLANES, :]


def _attn_kernel(qn_ref, kn_ref, kc_ref, qc_ref, diag_n_ref, first_c_ref, last_c_ref,
                 qt_ref, k_ref, vt_ref, o_ref, s_ref, mx_ref, m_ref, acc_ref):
    step = pl.program_id(0)
    tq = qt_ref.shape[1]
    tk = k_ref.shape[0]

    @pl.when(step == 0)
    def _():
        s_ref[...] = jnp.zeros(s_ref.shape, F32)
        mx_ref[...] = jnp.zeros(mx_ref.shape, F32)
        acc_ref[...] = jnp.zeros(acc_ref.shape, F32)

    @pl.when((first_c_ref[step] == 1) | (step == 0))
    def _():
        m_ref[...] = jnp.full(m_ref.shape, NEG_INF, F32)

    chunk = ATTN_Q_CHUNK
    nchunk = tq // chunk

    def body(diag_cur, diag_next):
        if diag_next:
            tri = (lax.broadcasted_iota(jnp.int32, (chunk, chunk), 0)
                   <= lax.broadcasted_iota(jnp.int32, (chunk, chunk), 1))

        def scores(hd, c):
            cols = slice(c * chunk, (c + 1) * chunk)
            rows = chunk * (c + 1) if diag_next else tk
            s = _dot(k_ref[0:rows, hd * HEAD_PAD:(hd + 1) * HEAD_PAD],
                     qt_ref[hd * HEAD_PAD:(hd + 1) * HEAD_PAD, cols])
            if not diag_next:
                s_ref[hd, :, cols] = s
                mx_ref[hd, :, cols] = jnp.max(s, axis=0, keepdims=True)
                return
            last = jnp.where(tri, s[rows - chunk:rows], NEG_INF)
            s_ref[hd, rows - chunk:rows, cols] = last
            mx = jnp.max(last, axis=0, keepdims=True)
            if rows > chunk:
                s_ref[hd, 0:rows - chunk, cols] = s[0:rows - chunk]
                mx = jnp.maximum(mx, jnp.max(s[0:rows - chunk], axis=0, keepdims=True))
            mx_ref[hd, :, cols] = mx

        def probabilities(hd, c):
            cols = slice(c * chunk, (c + 1) * chunk)
            rows = chunk * (c + 1) if diag_cur else tk
            m_prev = m_ref[hd, :, cols]
            m_new = jnp.maximum(m_prev, mx_ref[hd, :, cols])
            m_ref[hd, :, cols] = m_new
            return jnp.exp2(m_prev - m_new), jnp.exp2((s_ref[hd, 0:rows, cols] - m_new).astype(BF16))

        def accumulate(hd, c, alpha, p):
            cols = slice(c * chunk, (c + 1) * chunk)
            pv = _dot(vt_ref[hd * V_ROWS:(hd + 1) * V_ROWS, 0:p.shape[0]], p)
            acc_ref[hd, :, cols] = alpha * acc_ref[hd, :, cols] + pv

        for hd in range(N_HEADS):
            for c in range(nchunk):
                alpha, p = probabilities(hd, c)
                scores(hd, c)
                accumulate(hd, c, alpha, p)
            if diag_cur:
                acc = acc_ref[hd]
                o_ref[:, hd * V_DIM:(hd + 1) * V_DIM] = (
                    acc[:V_DIM] / acc[V_DIM:V_DIM + 1]).T.astype(o_ref.dtype)

    diag_cur = last_c_ref[step] == 1
    diag_next = diag_n_ref[step] == 1
    for cur_flag in (False, True):
        for next_flag in (False, True):
            @pl.when((diag_cur == cur_flag) & (diag_next == next_flag))
            def _(cur_flag=cur_flag, next_flag=next_flag):
                body(cur_flag, next_flag)


def _out_kernel(o_ref, ga_ref, yc_ref, x_ref, p_ref, g_oa_ref, w_o_ref, g_pl_ref, w_plg_ref, w_pl_ref,
                out_ref):
    tm = x_ref.shape[0]

    def residual(r0):
        rows = slice(r0, r0 + OUT_ROWS)
        ya = o_ref[rows, :].astype(F32) * ga_ref[rows, :].astype(F32)
        ya_n = (ya * _rms_scale(ya, ATTN_WIDTH) * g_oa_ref[...]).astype(BF16)
        y = jnp.concatenate([ya_n, yc_ref[rows, :]], axis=1)
        x1 = x_ref[rows, :] + _dot(y, w_o_ref[...])
        out_ref[rows, :] = x1
        hn = (x1 * _rms_scale(x1, D_MODEL) * g_pl_ref[...]).astype(BF16)
        return r0, hn

    def gated(r0, hn):
        rows = slice(r0, r0 + OUT_ROWS)
        gate = 1.0 / (1.0 + jnp.exp(-_dot(hn, w_plg_ref[...])))
        ple = _dot(p_ref[rows, :].astype(BF16), w_pl_ref[...])
        out_ref[rows, :] = out_ref[rows, :] + gate * ple

    prev = None
    for r0 in range(0, tm, OUT_ROWS):
        cur = residual(r0)
        if prev is not None:
            gated(*prev)
        prev = cur
    gated(*prev)


def _row_spec(tile, width):
    return pl.BlockSpec((tile, width), lambda i: (i, 0))


def _const_spec(shape):
    return pl.BlockSpec(shape, lambda i: (0,) * len(shape))


def _triangle_schedule(batch, n_blk):
    pairs = [(b * n_blk + i, b * n_blk + j, j == 0, j == i)
             for b in range(batch) for i in range(n_blk) for j in range(i + 1)]
    nxt = pairs + [pairs[-1]]
    cur = [pairs[0]] + pairs
    cols = ([p[0] for p in nxt], [p[1] for p in nxt], [p[1] for p in cur], [p[0] for p in cur],
            [int(p[3]) for p in nxt],
            [0] + [int(p[2]) for p in pairs], [0] + [int(p[3]) for p in pairs])
    return tuple(jnp.asarray(np.asarray(c, np.int32)) for c in cols)


def _prep_in_kernel(rows_a_ref, rows_b_ref, wt_lat_ref, wt_a_ref, wt_b_ref, g_ref, lat_ref, wide_ref):
    g = g_ref[...]

    @pl.when(pl.program_id(0) == 0)
    def _():
        pad = jnp.zeros((LAT_PAD - LAT_SRC, D_MODEL), F32)
        lat_ref[...] = jnp.concatenate([wt_lat_ref[...] * g, pad], axis=0).T.astype(BF16)

    a = (wt_a_ref[...] * g).T.astype(BF16)
    b = (wt_b_ref[...] * g).T.astype(BF16)
    for i in range(PREP_HALF // LANES):
        wide_ref[:, 2 * i * LANES:(2 * i + 1) * LANES] = a[:, i * LANES:(i + 1) * LANES]
        wide_ref[:, (2 * i + 1) * LANES:(2 * i + 2) * LANES] = b[:, i * LANES:(i + 1) * LANES]


def _prepare_w_in(w_in, g_in):
    def rows_at(nrows, start):
        return pl.BlockSpec((pl.Element(nrows), pl.Element(D_MODEL)), start)

    rows_a = jnp.asarray(np.asarray([LAT_SRC + a for a, _ in PREP_PAIRS], np.int32))
    rows_b = jnp.asarray(np.asarray([LAT_SRC + b for _, b in PREP_PAIRS], np.int32))
    return pl.pallas_call(
        _prep_in_kernel,
        grid_spec=pltpu.PrefetchScalarGridSpec(
            num_scalar_prefetch=2,
            grid=(len(PREP_PAIRS),),
            in_specs=[rows_at(LAT_SRC, lambda j, ra, rb: (0, 0)),
                      rows_at(PREP_HALF, lambda j, ra, rb: (pl.multiple_of(ra[j], SUBLANES), 0)),
                      rows_at(PREP_HALF, lambda j, ra, rb: (pl.multiple_of(rb[j], SUBLANES), 0)),
                      pl.BlockSpec((1, D_MODEL), lambda j, ra, rb: (0, 0))],
            out_specs=[pl.BlockSpec((D_MODEL, LAT_PAD), lambda j, ra, rb: (0, 0)),
                       pl.BlockSpec((D_MODEL, PREP_COLS), lambda j, ra, rb: (0, j))],
        ),
        out_shape=[jax.ShapeDtypeStruct((D_MODEL, LAT_PAD), BF16),
                   jax.ShapeDtypeStruct((D_MODEL, WIDE), BF16)],
        compiler_params=pltpu.CompilerParams(dimension_semantics=("arbitrary",),
                                             vmem_limit_bytes=VMEM_LIMIT),
        name="mla_conv_w_in_prep",
    )(rows_a, rows_b, w_in.T, w_in.T, w_in.T, g_in.reshape(1, -1))


def _layer(x2, p2, posc, invf, batch, seq, g_in, w_in, g_cq, w_uq, g_ckv, w_ukv, g_q, g_k,
           conv_w, g_oa, g_oc, w_o, w_pl, w_plg, g_pl):
    tokens = batch * seq
    tm = PROJ_TILE
    assert seq % tm == 0 and seq % ATTN_TILE == 0

    w_lat, w_wide = _prepare_w_in(w_in, g_in)
    w_uq_p = jnp.pad(w_uq.reshape(Q_LORA, N_HEADS, QK_DIM),
                     ((0, 0), (0, 0), (0, HEAD_PAD - QK_DIM))).reshape(Q_LORA, N_HEADS * HEAD_PAD).astype(BF16)
    g_q_p = jnp.pad(g_q, (0, HEAD_PAD - QK_DIM)).reshape(1, HEAD_PAD)
    g_kn = g_k[:NOPE_DIM].reshape(1, NOPE_DIM)
    g_kr = jnp.pad(g_k[NOPE_DIM:], (0, KPE_PAD - ROPE_DIM)).reshape(1, KPE_PAD)

    cparams = pltpu.CompilerParams(dimension_semantics=("arbitrary",), vmem_limit_bytes=VMEM_LIMIT)

    def col_spec(rows, tile):
        return pl.BlockSpec((rows, tile), lambda i: (0, i))

    proj_in_specs = [
        _row_spec(tm, D_MODEL), _row_spec(tm // TOKENS_PER_ROW, LANES), _const_spec((1, LANES)),
        _const_spec((D_MODEL, LAT_PAD)), _const_spec((D_MODEL, WIDE)),
        _const_spec((1, Q_LORA)), _const_spec((Q_LORA, N_HEADS * HEAD_PAD)),
        _const_spec((1, KV_LORA)), _const_spec((KV_LORA, N_HEADS * (NOPE_DIM + V_DIM))),
        _const_spec((1, HEAD_PAD)), _const_spec((1, NOPE_DIM)), _const_spec((1, KPE_PAD)),
        _const_spec((CONV_K, CONV_WIDTH)), _const_spec((1, CONV_WIDTH)),
    ]
    proj_out_specs = [
        col_spec(N_HEADS * HEAD_PAD, tm), _row_spec(tm, N_HEADS * HEAD_PAD),
        col_spec(N_HEADS * V_ROWS, tm), _row_spec(tm, ATTN_WIDTH), _row_spec(tm, CONV_WIDTH),
    ]

    def proj_body(*refs):
        *streamed, carry_ref = refs
        pltpu.emit_pipeline(
            lambda *blocks: _proj_kernel(seq // tm, *blocks, carry_ref),
            grid=(tokens // tm,), in_specs=proj_in_specs, out_specs=proj_out_specs)(*streamed)

    qt, k, vt, ga, yc = pl.pallas_call(
        proj_body,
        in_specs=[pl.BlockSpec(memory_space=pl.ANY)] * len(proj_in_specs),
        out_specs=[pl.BlockSpec(memory_space=pl.ANY)] * len(proj_out_specs),
        out_shape=[
            jax.ShapeDtypeStruct((N_HEADS * HEAD_PAD, tokens), BF16),
            jax.ShapeDtypeStruct((tokens, N_HEADS * HEAD_PAD), BF16),
            jax.ShapeDtypeStruct((N_HEADS * V_ROWS, tokens), BF16),
            jax.ShapeDtypeStruct((tokens, ATTN_WIDTH), BF16),
            jax.ShapeDtypeStruct((tokens, CONV_WIDTH), BF16),
        ],
        scratch_shapes=[pltpu.VMEM((tm + SUBLANES, CONV_WIDTH), F32)],
        compiler_params=pltpu.CompilerParams(vmem_limit_bytes=VMEM_LIMIT),
        name="mla_conv_proj",
    )(x2, posc, invf, w_lat, w_wide, g_cq.reshape(1, -1), w_uq_p,
      g_ckv.reshape(1, -1), w_ukv.astype(BF16), g_q_p, g_kn, g_kr, conv_w, g_oc.reshape(1, -1))

    ta = ATTN_TILE
    sched = _triangle_schedule(batch, seq // ta)
    o = pl.pallas_call(
        _attn_kernel,
        grid_spec=pltpu.PrefetchScalarGridSpec(
            num_scalar_prefetch=len(sched),
            grid=(int(sched[0].shape[0]),),
            in_specs=[
                pl.BlockSpec((N_HEADS * HEAD_PAD, ta), lambda s, qn, kn, kc, qc, *_: (0, qn[s])),
                pl.BlockSpec((ta, N_HEADS * HEAD_PAD), lambda s, qn, kn, kc, qc, *_: (kn[s], 0)),
                pl.BlockSpec((N_HEADS * V_ROWS, ta), lambda s, qn, kn, kc, qc, *_: (0, kc[s])),
            ],
            out_specs=pl.BlockSpec((ta, ATTN_WIDTH), lambda s, qn, kn, kc, qc, *_: (qc[s], 0)),
            scratch_shapes=[pltpu.VMEM((N_HEADS, ta, ta), F32),
                            pltpu.VMEM((N_HEADS, 1, ta), F32),
                            pltpu.VMEM((N_HEADS, 1, ta), F32),
                            pltpu.VMEM((N_HEADS, V_ROWS, ta), F32)],
        ),
        out_shape=jax.ShapeDtypeStruct((tokens, ATTN_WIDTH), BF16),
        compiler_params=cparams,
        name="mla_flash_attn",
    )(*sched, qt, k, vt)

    return pl.pallas_call(
        _out_kernel,
        grid=(tokens // tm,),
        in_specs=[
            _row_spec(tm, ATTN_WIDTH), _row_spec(tm, ATTN_WIDTH), _row_spec(tm, CONV_WIDTH),
            _row_spec(tm, D_MODEL), _row_spec(tm, PLE_DIM),
            _const_spec((1, ATTN_WIDTH)), _const_spec((D_MODEL, D_MODEL)), _const_spec((1, D_MODEL)),
            _const_spec((D_MODEL, D_MODEL)), _const_spec((PLE_DIM, D_MODEL)),
        ],
        out_specs=_row_spec(tm, D_MODEL),
        out_shape=jax.ShapeDtypeStruct((tokens, D_MODEL), F32),
        compiler_params=cparams,
        name="mla_conv_out",
    )(o, ga, yc, x2, p2, g_oa.reshape(1, -1), w_o.astype(BF16),
      g_pl.reshape(1, -1), w_plg.astype(BF16), w_pl.astype(BF16))


def kernel(x, p, positions, g_in, w_in, g_cq, w_uq, g_ckv, w_ukv, g_q, g_k, conv_w, g_oa, g_oc,
           w_o, w_pl, w_plg, g_pl):
    batch, seq, d_model = x.shape
    depth = p.shape[0]
    tokens = batch * seq
    rows = PROJ_TILE // TOKENS_PER_ROW
    posc = jnp.repeat(positions.astype(F32).reshape(tokens // PROJ_TILE, TOKENS_PER_ROW, rows)
                      .transpose(0, 2, 1), HALF_ROPE, axis=-1).reshape(tokens // TOKENS_PER_ROW, LANES)
    inv_freq = 1.0 / (ROPE_THETA ** (jnp.arange(0, ROPE_DIM, 2, dtype=F32) / ROPE_DIM))
    invf = jnp.tile(inv_freq, LANES // HALF_ROPE).reshape(1, LANES)
    h = x.reshape(tokens, d_model)
    for i in range(depth):
        h = _layer(h, p[i].reshape(tokens, PLE_DIM), posc, invf, batch, seq,
                   g_in[i], w_in[i], g_cq[i], w_uq[i], g_ckv[i], w_ukv[i], g_q[i], g_k[i],
                   conv_w[i], g_oa[i], g_oc[i], w_o[i], w_pl[i], w_plg[i], g_pl[i])
    return h.reshape(batch, seq, d_model).astype(x.dtype)
```

```python
import functools
import math

import jax
import jax.numpy as jnp
import numpy as np
from jax import lax
from jax.experimental import pallas as pl
from jax.experimental.pallas import tpu as pltpu

D_MODEL = 1024
PLE_DIM = 256
N_HEADS = 4
NOPE_DIM = 128
ROPE_DIM = 64
HALF_ROPE = ROPE_DIM // 2
V_DIM = 128
QK_DIM = NOPE_DIM + ROPE_DIM
Q_LORA = 256
KV_LORA = 128
ATTN_WIDTH = N_HEADS * V_DIM
CONV_WIDTH = D_MODEL - ATTN_WIDTH
CONV_K = 3
ROPE_THETA = 10000.0
RMS_EPS = 1e-6
NEG_INF = -1e30

LANES = 128
SUBLANES = 8
HEAD_PAD = 2 * LANES
KPE_PAD = LANES
BF16_ROWS = 16
V_ROWS = V_DIM + BF16_ROWS
TOKENS_PER_ROW = LANES // HALF_ROPE
OFF_CQ = 0
OFF_CKV = OFF_CQ + Q_LORA
OFF_KPE = OFF_CKV + KV_LORA
LAT_SRC = OFF_KPE + ROPE_DIM
LAT_PAD = OFF_KPE + KPE_PAD
SRC_ZA = 0
SRC_CB = SRC_ZA + ATTN_WIDTH
SRC_CC = SRC_CB + CONV_WIDTH
SRC_CX = SRC_CC + CONV_WIDTH
SRC_ZC = SRC_CX + CONV_WIDTH
WIDE = SRC_ZC + CONV_WIDTH
PREP_COLS = 512
PREP_HALF = PREP_COLS // 2
PREP_PAIRS = ((SRC_ZA, SRC_ZA + PREP_HALF), (SRC_CC, SRC_CX), (SRC_CC + PREP_HALF, SRC_CX + PREP_HALF),
              (SRC_CB, SRC_ZC), (SRC_CB + PREP_HALF, SRC_ZC + PREP_HALF))
OFF_ZA = 0
OFF_CCX = OFF_ZA + PREP_COLS
OFF_CBZ = OFF_CCX + 2 * PREP_COLS

PROJ_TILE = 1024
PROJ_ROWS = PROJ_TILE // TOKENS_PER_ROW
OUT_ROWS = 256
ATTN_TILE = 1024
ATTN_Q_CHUNK = 256
VMEM_LIMIT = 48 * 1024 * 1024

BF16 = jnp.bfloat16
F32 = jnp.float32


def _rms_scale(v, width):
    return lax.rsqrt(jnp.sum(v * v, axis=-1, keepdims=True) * (1.0 / width) + RMS_EPS)


def _silu(z):
    h = 0.5 * z
    return h + h * jnp.tanh(h)


def _dot(a, b):
    return jnp.dot(a, b, preferred_element_type=F32)


def _rope_angles(posc, invf):
    ang = posc * invf
    return jnp.cos(ang), jnp.sin(ang)


def _rope_quarter(cosc, sinc, a):
    lane = lax.broadcasted_iota(jnp.int32, (1, LANES), 1)
    lo = lane < HALF_ROPE
    mid = (lane >= HALF_ROPE) & (lane < ROPE_DIM)
    c = pltpu.roll(cosc, LANES - HALF_ROPE * a, 1) if a else cosc
    s = pltpu.roll(sinc, LANES - HALF_ROPE * a, 1) if a else sinc
    return (jnp.where(lo, c, jnp.where(mid, pltpu.roll(c, HALF_ROPE, 1), 0.0)),
            jnp.where(lo, -s, jnp.where(mid, pltpu.roll(s, HALF_ROPE, 1), 0.0)), lo)


def _rope(t, c, s, lo):
    swapped = jnp.where(lo, pltpu.roll(t, LANES - HALF_ROPE, 1), pltpu.roll(t, HALF_ROPE, 1))
    return t * c + swapped * s


def _proj_kernel(tiles_per_seq,
                 x_ref, posc_ref, invf_ref, w_lat_ref, w_in_ref, g_cq_ref, w_uq_ref,
                 g_ckv_ref, w_ukv_ref, g_q_ref, g_kn_ref, g_kr_ref, conv_w_ref, g_oc_ref,
                 qt_ref, k_ref, vt_ref, ga_ref, yc_ref, carry_ref):
    tm = x_ref.shape[0]
    quarter_rows = tm // TOKENS_PER_ROW
    tables = []
    q_scale = math.log2(math.e) / math.sqrt(QK_DIM)

    @pl.when(pl.program_id(0) % tiles_per_seq == 0)
    def _():
        carry_ref[0:SUBLANES, :] = jnp.zeros((SUBLANES, CONV_WIDTH), F32)

    def rows_block(r0, nr):
        x = x_ref[r0:r0 + nr, :]
        h = (x * _rms_scale(x, D_MODEL)).astype(BF16)

        def proj(off, width):
            return _dot(h, w_in_ref[:, off:off + width])

        def tile_pairs(r):
            return [(r[:, i:i + LANES], r[:, i + LANES:i + 2 * LANES]) for i in range(0, r.shape[1], 2 * LANES)]

        lat = _dot(h, w_lat_ref[...])
        c_q = lat[:, OFF_CQ:OFF_CKV]
        c_kv = lat[:, OFF_CKV:OFF_KPE]
        kpe = lat[:, OFF_KPE:LAT_PAD]
        cv =jnp.concatenate([c * xin for c, xin in tile_pairs(proj(OFF_CCX, 2 * CONV_WIDTH))], axis=1)
        cqn = (c_q * _rms_scale(c_q, Q_LORA) * g_cq_ref[...]).astype(BF16)
        qf = _dot(cqn, w_uq_ref[...])
        gated_b = jnp.concatenate([b * _silu(z) for b, z in tile_pairs(proj(OFF_CBZ, 2 * CONV_WIDTH))], axis=1)
        ckvn = (c_kv * _rms_scale(c_kv, KV_LORA) * g_ckv_ref[...]).astype(BF16)
        kv = _dot(ckvn, w_ukv_ref[...])
        z_a = proj(OFF_ZA, ATTN_WIDTH)

        if not tables:
            tables.extend(_rope_angles(posc_ref[...], invf_ref[...]))
        assert nr == quarter_rows
        cos_t, sin_t, lo = _rope_quarter(*tables, r0 // quarter_rows)

        for hd in range(N_HEADS):
            qh = qf[:, hd * HEAD_PAD:(hd + 1) * HEAD_PAD]
            qn = qh * (_rms_scale(qh, QK_DIM) * q_scale) * g_q_ref[...]
            qt_ref[hd * HEAD_PAD:hd * HEAD_PAD + LANES, r0:r0 + nr] = qn[:, :LANES].astype(BF16).T
            qt_ref[hd * HEAD_PAD + LANES:(hd + 1) * HEAD_PAD, r0:r0 + nr] = (
                _rope(qn[:, LANES:], cos_t, sin_t, lo).astype(BF16).T)

        ss_kpe = jnp.sum(kpe * kpe, axis=-1, keepdims=True)
        k_rot = _rope(kpe * g_kr_ref[...], cos_t, sin_t, lo)
        for hd in range(N_HEADS):
            kn = kv[:, hd * (NOPE_DIM + V_DIM):hd * (NOPE_DIM + V_DIM) + NOPE_DIM]
            vh = kv[:, hd * (NOPE_DIM + V_DIM) + NOPE_DIM:(hd + 1) * (NOPE_DIM + V_DIM)]
            ss = jnp.sum(kn * kn, axis=-1, keepdims=True) + ss_kpe
            rs = lax.rsqrt(ss * (1.0 / QK_DIM) + RMS_EPS)
            k_ref[r0:r0 + nr, hd * HEAD_PAD:hd * HEAD_PAD + LANES] = (kn * rs * g_kn_ref[...]).astype(BF16)
            k_ref[r0:r0 + nr, hd * HEAD_PAD + LANES:(hd + 1) * HEAD_PAD] = (k_rot * rs).astype(BF16)
            vt_ref[hd * V_ROWS:hd * V_ROWS + V_DIM, r0:r0 + nr] = vh.astype(BF16).T
            vt_ref[hd * V_ROWS + V_DIM:(hd + 1) * V_ROWS, r0:r0 + nr] = jnp.ones((BF16_ROWS, nr), BF16)

        base = SUBLANES + r0
        prev = carry_ref[base - SUBLANES:base, :]
        carry_ref[base + nr - SUBLANES:base + nr, :] = cv[nr - SUBLANES:, :]
        row = lax.broadcasted_iota(jnp.int32, (SUBLANES, CONV_WIDTH), 0)

        def delayed(d):
            sh = pltpu.roll(cv, d, 0)
            head = jnp.where(row < d, pltpu.roll(prev, d, 0), sh[0:SUBLANES])
            return jnp.concatenate([head, sh[SUBLANES:]], axis=0)

        u = (conv_w_ref[2:3, :] * cv + conv_w_ref[1:2, :] * delayed(1) + conv_w_ref[0:1, :] * delayed(2))
        yc = gated_b * u
        yc_ref[r0:r0 + nr, :] = (yc * _rms_scale(yc, CONV_WIDTH) * g_oc_ref[...]).astype(BF16)

        for i, (za_lo, za_hi) in enumerate(tile_pairs(z_a)):
            ga_ref[r0:r0 + nr, i * LANES:(i + 1) * LANES] = _silu(za_lo).astype(BF16)
            ga_ref[r0:r0 + nr, PREP_HALF + i * LANES:PREP_HALF + (i + 1) * LANES] = _silu(za_hi).astype(BF16)

    for blk in range(tm // PROJ_ROWS):
        rows_block(blk * PROJ_ROWS, PROJ_ROWS)
    carry_ref[0:SUBLANES, :] = carry_ref[tm:tm + SUBLANES, :]


def _attn_kernel(qn_ref, kn_ref, kc_ref, qc_ref, diag_n_ref, first_c_ref, last_c_ref,
                 qt_ref, k_ref, vt_ref, o_ref, s_ref, mx_ref, m_ref, acc_ref):
    step = pl.program_id(0)
    tq = qt_ref.shape[1]
    tk = k_ref.shape[0]

    @pl.when(step == 0)
    def _():
        s_ref[...] = jnp.zeros(s_ref.shape, F32)
        mx_ref[...] = jnp.zeros(mx_ref.shape, F32)
        acc_ref[...] = jnp.zeros(acc_ref.shape, F32)

    @pl.when((first_c_ref[step] == 1) | (step == 0))
    def _():
        m_ref[...] = jnp.full(m_ref.shape, NEG_INF, F32)

    chunk = ATTN_Q_CHUNK
    nchunk = tq // chunk

    def body(diag_cur, diag_next):
        if diag_next:
            tri = (lax.broadcasted_iota(jnp.int32, (chunk, chunk), 0)
                   <= lax.broadcasted_iota(jnp.int32, (chunk, chunk), 1))

        def scores(hd, c):
            cols = slice(c * chunk, (c + 1) * chunk)
            rows = chunk * (c + 1) if diag_next else tk
            s = _dot(k_ref[0:rows, hd * HEAD_PAD:(hd + 1) * HEAD_PAD],
                     qt_ref[hd * HEAD_PAD:(hd + 1) * HEAD_PAD, cols])
            if not diag_next:
                s_ref[hd, :, cols] = s
                mx_ref[hd, :, cols] = jnp.max(s, axis=0, keepdims=True)
                return
            last = jnp.where(tri, s[rows - chunk:rows], NEG_INF)
            s_ref[hd, rows - chunk:rows, cols] = last
            mx = jnp.max(last, axis=0, keepdims=True)
            if rows > chunk:
                s_ref[hd, 0:rows - chunk, cols] = s[0:rows - chunk]
                mx = jnp.maximum(mx, jnp.max(s[0:rows - chunk], axis=0, keepdims=True))
            mx_ref[hd, :, cols] = mx

        def probabilities(hd, c):
            cols = slice(c * chunk, (c + 1) * chunk)
            rows = chunk * (c + 1) if diag_cur else tk
            m_prev = m_ref[hd, :, cols]
            m_new = jnp.maximum(m_prev, mx_ref[hd, :, cols])
            m_ref[hd, :, cols] = m_new
            return jnp.exp2(m_prev - m_new), jnp.exp2((s_ref[hd, 0:rows, cols] - m_new).astype(BF16))

        def accumulate(hd, c, alpha, p):
            cols = slice(c * chunk, (c + 1) * chunk)
            pv = _dot(vt_ref[hd * V_ROWS:(hd + 1) * V_ROWS, 0:p.shape[0]], p)
            acc_ref[hd, :, cols] = alpha * acc_ref[hd, :, cols] + pv

        for hd in range(N_HEADS):
            for c in range(nchunk):
                alpha, p = probabilities(hd, c)
                scores(hd, c)
                accumulate(hd, c, alpha, p)
            if diag_cur:
                acc = acc_ref[hd]
                o_ref[:, hd * V_DIM:(hd + 1) * V_DIM] = (
                    acc[:V_DIM] / acc[V_DIM:V_DIM + 1]).T.astype(o_ref.dtype)

    diag_cur = last_c_ref[step] == 1
    diag_next = diag_n_ref[step] == 1
    for cur_flag in (False, True):
        for next_flag in (False, True):
            @pl.when((diag_cur == cur_flag) & (diag_next == next_flag))
            def _(cur_flag=cur_flag, next_flag=next_flag):
                body(cur_flag, next_flag)


def _out_kernel(o_ref, ga_ref, yc_ref, x_ref, p_ref, g_oa_ref, w_o_ref, g_pl_ref, w_plg_ref, w_pl_ref,
                out_ref):
    tm = x_ref.shape[0]

    def residual(r0):
        rows = slice(r0, r0 + OUT_ROWS)
        ya = o_ref[rows, :].astype(F32) * ga_ref[rows, :].astype(F32)
        ya_n = (ya * _rms_scale(ya, ATTN_WIDTH) * g_oa_ref[...]).astype(BF16)
        y = jnp.concatenate([ya_n, yc_ref[rows, :]], axis=1)
        x1 = x_ref[rows, :] + _dot(y, w_o_ref[...])
        out_ref[rows, :] = x1
        hn = (x1 * _rms_scale(x1, D_MODEL) * g_pl_ref[...]).astype(BF16)
        return r0, hn

    def gated(r0, hn):
        rows = slice(r0, r0 + OUT_ROWS)
        gate = 1.0 / (1.0 + jnp.exp(-_dot(hn, w_plg_ref[...])))
        ple = _dot(p_ref[rows, :].astype(BF16), w_pl_ref[...])
        out_ref[rows, :] = out_ref[rows, :] + gate * ple

    prev = None
    for r0 in range(0, tm, OUT_ROWS):
        cur = residual(r0)
        if prev is not None:
            gated(*prev)
        prev = cur
    gated(*prev)


def _row_spec(tile, width):
    return pl.BlockSpec((tile, width), lambda i: (i, 0))


def _const_spec(shape):
    return pl.BlockSpec(shape, lambda i: (0,) * len(shape))


def _triangle_schedule(batch, n_blk):
    pairs = [(b * n_blk + i, b * n_blk + j, j == 0, j == i)
             for b in range(batch) for i in range(n_blk) for j in range(i + 1)]
    nxt = pairs + [pairs[-1]]
    cur = [pairs[0]] + pairs
    cols = ([p[0] for p in nxt], [p[1] for p in nxt], [p[1] for p in cur], [p[0] for p in cur],
            [int(p[3]) for p in nxt],
            [0] + [int(p[2]) for p in pairs], [0] + [int(p[3]) for p in pairs])
    return tuple(jnp.asarray(np.asarray(c, np.int32)) for c in cols)


def _prep_in_kernel(rows_a_ref, rows_b_ref, wt_lat_ref, wt_a_ref, wt_b_ref, g_ref, lat_ref, wide_ref):
    g = g_ref[...]

    @pl.when(pl.program_id(0) == 0)
    def _():
        pad = jnp.zeros((LAT_PAD - LAT_SRC, D_MODEL), F32)
        lat_ref[...] = jnp.concatenate([wt_lat_ref[...] * g, pad], axis=0).T.astype(BF16)

    a = (wt_a_ref[...] * g).T.astype(BF16)
    b = (wt_b_ref[...] * g).T.astype(BF16)
    for i in range(PREP_HALF // LANES):
        wide_ref[:, 2 * i * LANES:(2 * i + 1) * LANES] = a[:, i * LANES:(i + 1) * LANES]
        wide_ref[:, (2 * i + 1) * LANES:(2 * i + 2) * LANES] = b[:, i * LANES:(i + 1) * LANES]


def _prepare_w_in(w_in, g_in):
    def rows_at(nrows, start):
        return pl.BlockSpec((pl.Element(nrows), pl.Element(D_MODEL)), start)

    rows_a = jnp.asarray(np.asarray([LAT_SRC + a for a, _ in PREP_PAIRS], np.int32))
    rows_b = jnp.asarray(np.asarray([LAT_SRC + b for _, b in PREP_PAIRS], np.int32))
    return pl.pallas_call(
        _prep_in_kernel,
        grid_spec=pltpu.PrefetchScalarGridSpec(
            num_scalar_prefetch=2,
            grid=(len(PREP_PAIRS),),
            in_specs=[rows_at(LAT_SRC, lambda j, ra, rb: (0, 0)),
                      rows_at(PREP_HALF, lambda j, ra, rb: (pl.multiple_of(ra[j], SUBLANES), 0)),
                      rows_at(PREP_HALF, lambda j, ra, rb: (pl.multiple_of(rb[j], SUBLANES), 0)),
                      pl.BlockSpec((1, D_MODEL), lambda j, ra, rb: (0, 0))],
            out_specs=[pl.BlockSpec((D_MODEL, LAT_PAD), lambda j, ra, rb: (0, 0)),
                       pl.BlockSpec((D_MODEL, PREP_COLS), lambda j, ra, rb: (0, j))],
        ),
        out_shape=[jax.ShapeDtypeStruct((D_MODEL, LAT_PAD), BF16),
                   jax.ShapeDtypeStruct((D_MODEL, WIDE), BF16)],
        compiler_params=pltpu.CompilerParams(dimension_semantics=("arbitrary",),
                                             vmem_limit_bytes=VMEM_LIMIT),
        name="mla_conv_w_in_prep",
    )(rows_a, rows_b, w_in.T, w_in.T, w_in.T, g_in.reshape(1, -1))


def _layer(x2, p2, posc, invf, batch, seq, g_in, w_in, g_cq, w_uq, g_ckv, w_ukv, g_q, g_k,
           conv_w, g_oa, g_oc, w_o, w_pl, w_plg, g_pl):
    tokens = batch * seq
    tm = PROJ_TILE
    assert seq % tm == 0 and seq % ATTN_TILE == 0

    w_lat, w_wide = _prepare_w_in(w_in, g_in)
    w_uq_p = jnp.pad(w_uq.reshape(Q_LORA, N_HEADS, QK_DIM),
                     ((0, 0), (0, 0), (0, HEAD_PAD - QK_DIM))).reshape(Q_LORA, N_HEADS * HEAD_PAD).astype(BF16)
    g_q_p = jnp.pad(g_q, (0, HEAD_PAD - QK_DIM)).reshape(1, HEAD_PAD)
    g_kn = g_k[:NOPE_DIM].reshape(1, NOPE_DIM)
    g_kr = jnp.pad(g_k[NOPE_DIM:], (0, KPE_PAD - ROPE_DIM)).reshape(1, KPE_PAD)

    cparams = pltpu.CompilerParams(dimension_semantics=("arbitrary",), vmem_limit_bytes=VMEM_LIMIT)

    def col_spec(rows, tile):
        return pl.BlockSpec((rows, tile), lambda i: (0, i))

    proj_in_specs = [
        _row_spec(tm, D_MODEL), _row_spec(tm // TOKENS_PER_ROW, LANES), _const_spec((1, LANES)),
        _const_spec((D_MODEL, LAT_PAD)), _const_spec((D_MODEL, WIDE)),
        _const_spec((1, Q_LORA)), _const_spec((Q_LORA, N_HEADS * HEAD_PAD)),
        _const_spec((1, KV_LORA)), _const_spec((KV_LORA, N_HEADS * (NOPE_DIM + V_DIM))),
        _const_spec((1, HEAD_PAD)), _const_spec((1, NOPE_DIM)), _const_spec((1, KPE_PAD)),
        _const_spec((CONV_K, CONV_WIDTH)), _const_spec((1, CONV_WIDTH)),
    ]
    proj_out_specs = [
        col_spec(N_HEADS * HEAD_PAD, tm), _row_spec(tm, N_HEADS * HEAD_PAD),
        col_spec(N_HEADS * V_ROWS, tm), _row_spec(tm, ATTN_WIDTH), _row_spec(tm, CONV_WIDTH),
    ]

    def proj_body(*refs):
        *streamed, carry_ref = refs
        pltpu.emit_pipeline(
            lambda *blocks: _proj_kernel(seq // tm, *blocks, carry_ref),
            grid=(tokens // tm,), in_specs=proj_in_specs, out_specs=proj_out_specs)(*streamed)

    qt, k, vt, ga, yc = pl.pallas_call(
        proj_body,
        in_specs=[pl.BlockSpec(memory_space=pl.ANY)] * len(proj_in_specs),
        out_specs=[pl.BlockSpec(memory_space=pl.ANY)] * len(proj_out_specs),
        out_shape=[
            jax.ShapeDtypeStruct((N_HEADS * HEAD_PAD, tokens), BF16),
            jax.ShapeDtypeStruct((tokens, N_HEADS * HEAD_PAD), BF16),
            jax.ShapeDtypeStruct((N_HEADS * V_ROWS, tokens), BF16),
            jax.ShapeDtypeStruct((tokens, ATTN_WIDTH), BF16),
            jax.ShapeDtypeStruct((tokens, CONV_WIDTH), BF16),
        ],
        scratch_shapes=[pltpu.VMEM((tm + SUBLANES, CONV_WIDTH), F32)],
        compiler_params=pltpu.CompilerParams(vmem_limit_bytes=VMEM_LIMIT),
        name="mla_conv_proj",
    )(x2, posc, invf, w_lat, w_wide, g_cq.reshape(1, -1), w_uq_p,
      g_ckv.reshape(1, -1), w_ukv.astype(BF16), g_q_p, g_kn, g_kr, conv_w, g_oc.reshape(1, -1))

    ta = ATTN_TILE
    sched = _triangle_schedule(batch, seq // ta)
    n_sched = len(sched)

    def attn_body(*refs):
        flags, streamed, scratch = refs[:n_sched], refs[n_sched:n_sched + 4], refs[n_sched + 4:]
        qn, kn, kc, qc = flags[:4]
        pltpu.emit_pipeline(
            lambda *blocks: _attn_kernel(*flags, *blocks, *scratch),
            grid=(int(sched[0].shape[0]),),
            in_specs=[
                pl.BlockSpec((N_HEADS * HEAD_PAD, ta), lambda s: (0, qn[s])),
                pl.BlockSpec((ta, N_HEADS * HEAD_PAD), lambda s: (kn[s], 0)),
                pl.BlockSpec((N_HEADS * V_ROWS, ta), lambda s: (0, kc[s])),
            ],
            out_specs=[pl.BlockSpec((ta, ATTN_WIDTH), lambda s: (qc[s], 0))])(*streamed)

    o = pl.pallas_call(
        attn_body,
        in_specs=[pl.BlockSpec(memory_space=pltpu.SMEM)] * n_sched + [pl.BlockSpec(memory_space=pl.ANY)] * 3,
        out_specs=pl.BlockSpec(memory_space=pl.ANY),
        scratch_shapes=[pltpu.VMEM((N_HEADS, ta, ta), F32),
                        pltpu.VMEM((N_HEADS, 1, ta), F32),
                        pltpu.VMEM((N_HEADS, 1, ta), F32),
                        pltpu.VMEM((N_HEADS, V_ROWS, ta), F32)],
        out_shape=jax.ShapeDtypeStruct((tokens, ATTN_WIDTH), BF16),
        compiler_params=pltpu.CompilerParams(vmem_limit_bytes=VMEM_LIMIT),
        name="mla_flash_attn",
    )(*sched, qt, k, vt)

    return pl.pallas_call(
        _out_kernel,
        grid=(tokens // tm,),
        in_specs=[
            _row_spec(tm, ATTN_WIDTH), _row_spec(tm, ATTN_WIDTH), _row_spec(tm, CONV_WIDTH),
            _row_spec(tm, D_MODEL), _row_spec(tm, PLE_DIM),
            _const_spec((1, ATTN_WIDTH)), _const_spec((D_MODEL, D_MODEL)), _const_spec((1, D_MODEL)),
            _const_spec((D_MODEL, D_MODEL)), _const_spec((PLE_DIM, D_MODEL)),
        ],
        out_specs=_row_spec(tm, D_MODEL),
        out_shape=jax.ShapeDtypeStruct((tokens, D_MODEL), F32),
        compiler_params=cparams,
        name="mla_conv_out",
    )(o, ga, yc, x2, p2, g_oa.reshape(1, -1), w_o.astype(BF16),
      g_pl.reshape(1, -1), w_plg.astype(BF16), w_pl.astype(BF16))


def kernel(x, p, positions, g_in, w_in, g_cq, w_uq, g_ckv, w_ukv, g_q, g_k, conv_w, g_oa, g_oc,
           w_o, w_pl, w_plg, g_pl):
    batch, seq, d_model = x.shape
    depth = p.shape[0]
    tokens = batch * seq
    rows = PROJ_TILE // TOKENS_PER_ROW
    posc = jnp.repeat(positions.astype(F32).reshape(tokens // PROJ_TILE, TOKENS_PER_ROW, rows)
                      .transpose(0, 2, 1), HALF_ROPE, axis=-1).reshape(tokens // TOKENS_PER_ROW, LANES)
    inv_freq = 1.0 / (ROPE_THETA ** (jnp.arange(0, ROPE_DIM, 2, dtype=F32) / ROPE_DIM))
    invf = jnp.tile(inv_freq, LANES // HALF_ROPE).reshape(1, LANES)
    h = x.reshape(tokens, d_model)
    for i in range(depth):
        h = _layer(h, p[i].reshape(tokens, PLE_DIM), posc, invf, batch, seq,
                   g_in[i], w_in[i], g_cq[i], w_uq[i], g_ckv[i], w_ukv[i], g_q[i], g_k[i],
                   conv_w[i], g_oa[i], g_oc[i], w_o[i], w_pl[i], w_plg[i], g_pl[i])
    return h.reshape(batch, seq, d_model).astype(x.dtype)
```
